```python
import jax, jax.numpy as jnp
from jax import lax
import numpy as np

D_MODEL = 1024
BATCH = 8
SEQ = 2048
DEPTH = 2
DEC_BATCH = 128
DEC_SEQ = 8
PAST_LEN = 16384
PAGE_SIZE = 128

SGU_HEADS = 8
SGU_HEAD_DIM = D_MODEL // 16
D_SGU = SGU_HEADS * SGU_HEAD_DIM
CHUNK = 128
D_LRU = D_MODEL
LRU_BLOCKS = 16
LRU_BLOCK_DIM = D_LRU // LRU_BLOCKS
CONV_W = 4
LRU_C = 8.0
D_MIX = D_SGU + D_LRU
D_IN = 2 * D_SGU + 2 * D_LRU
D_FF = 4 * D_MODEL
EPS = 1e-6

kernel_name = "hymba_sgu_rglru_decoder_step"


def rms_norm(x, g):
    x32 = x.astype(jnp.float32)
    y = x32 * lax.rsqrt(jnp.mean(x32 * x32, axis=-1, keepdims=True) + EPS)
    return (y * g.astype(jnp.float32)).astype(x.dtype)


def layer_norm(x, g, b):
    x32 = x.astype(jnp.float32)
    mu = jnp.mean(x32, axis=-1, keepdims=True)
    xc = x32 - mu
    y = xc * lax.rsqrt(jnp.mean(xc * xc, axis=-1, keepdims=True) + EPS)
    return (y * g.astype(jnp.float32) + b.astype(jnp.float32)).astype(x.dtype)


def causal_conv(x, buf, w, b):
    T = x.shape[1]
    xp = jnp.concatenate([buf.astype(x.dtype), x], axis=1)
    y = b + sum(xp[:, k:k + T] * w[k] for k in range(CONV_W))
    return y, xp[:, T:]


def spatial_gate(v, w_s, b_s):
    B, T, H, dh = v.shape
    L = min(T, CHUNK)
    nc = T // L
    mask = jnp.tril(jnp.ones((L, L), w_s.dtype))
    w = w_s[:, :L, :L] * mask
    vc = v.reshape(B, nc, L, H, dh)
    out = jnp.einsum('hts,bcshd->bcthd', w, vc) + b_s[:, :L].T[None, None, :, :, None]
    return out.reshape(B, T, H, dh)


def rg_lru(x, h0, gate_r_w, gate_r_b, gate_i_w, gate_i_b, lam):
    B, T, _ = x.shape
    x32 = x.astype(jnp.float32)
    xh = x32.reshape(B, T, LRU_BLOCKS, LRU_BLOCK_DIM)
    r = jax.nn.sigmoid(jnp.einsum('bthi,hij->bthj', xh, gate_r_w.astype(jnp.float32)).reshape(B, T, D_LRU) + gate_r_b.astype(jnp.float32))
    i = jax.nn.sigmoid(jnp.einsum('bthi,hij->bthj', xh, gate_i_w.astype(jnp.float32)).reshape(B, T, D_LRU) + gate_i_b.astype(jnp.float32))
    log_a = LRU_C * r * jax.nn.log_sigmoid(lam.astype(jnp.float32))
    a = jnp.exp(log_a)
    bt = jnp.sqrt(-jnp.expm1(2.0 * log_a)) * (i * x32)
    bt = bt.at[:, 0].add(a[:, 0] * h0.astype(jnp.float32))

    def combine(c1, c2):
        a1, b1 = c1
        a2, b2 = c2
        return a1 * a2, a2 * b1 + b2

    _, h = lax.associative_scan(combine, (a, bt), axis=1)
    return h.astype(x.dtype), h[:, -1].astype(h0.dtype)


def mixer(xn, h0, conv_buf, w_in, conv_w, conv_b, gate_r_w, gate_r_b, gate_i_w, gate_i_b,
          lru_lambda, sgu_norm_g, sgu_norm_b, sgu_w, sgu_b, w_out):
    B, T, _ = xn.shape
    proj = xn @ w_in
    u, v, xb, yb = jnp.split(proj, [D_SGU, 2 * D_SGU, 2 * D_SGU + D_LRU], axis=-1)
    u = jax.nn.gelu(u)
    v = layer_norm(jax.nn.gelu(v), sgu_norm_g, sgu_norm_b)
    gate = spatial_gate(v.reshape(B, T, SGU_HEADS, SGU_HEAD_DIM), sgu_w, sgu_b).reshape(B, T, D_SGU)
    out_a = u * gate
    xc, conv_new = causal_conv(xb, conv_buf, conv_w, conv_b)
    h, h_last = rg_lru(xc, h0, gate_r_w, gate_r_b, gate_i_w, gate_i_b, lru_lambda)
    out_b = h * jax.nn.gelu(yb)
    out = jnp.concatenate([out_a, out_b], axis=-1) @ w_out
    return out, v, h_last, conv_new


def trunk(x, h_init, conv_init, norm_mix_g, w_in, conv_w, conv_b, gate_r_w, gate_r_b, gate_i_w, gate_i_b,
          lru_lambda, sgu_norm_g, sgu_norm_b, sgu_w, sgu_b, w_out, norm_mlp_g, mlp_w1, mlp_w2, final_norm_g):
    hs, convs, vs = [], [], []
    for l in range(DEPTH):
        mix, v, h_last, conv_new = mixer(
            rms_norm(x, norm_mix_g[l]), h_init[l], conv_init[l], w_in[l], conv_w[l], conv_b[l],
            gate_r_w[l], gate_r_b[l], gate_i_w[l], gate_i_b[l], lru_lambda[l],
            sgu_norm_g[l], sgu_norm_b[l], sgu_w[l], sgu_b[l], w_out[l])
        x = x + mix
        hid = jnp.square(jax.nn.relu(rms_norm(x, norm_mlp_g[l]) @ mlp_w1[l]))
        x = x + hid @ mlp_w2[l]
        hs.append(h_last)
        convs.append(conv_new)
        vs.append(v)
    return rms_norm(x, final_norm_g), hs, convs, vs


def setup_inputs(seed: int = 0) -> dict:
    key = jax.random.key(seed)
    ks = jax.random.split(key, 24)

    def nrm(k, shape, scale):
        return jax.random.normal(k, shape, jnp.float32) * scale

    u = jax.random.uniform(ks[12], (DEPTH, D_LRU), jnp.float32, minval=0.9, maxval=0.999)
    s = u ** (1.0 / LRU_C)
    lru_lambda = jnp.log(s) - jnp.log1p(-s)
    return {
        "x_prompt": nrm(ks[0], (BATCH, SEQ, D_MODEL), 1.0),
        "x_sample": nrm(ks[1], (DEC_BATCH, DEC_SEQ, D_MODEL), 1.0),
        "state_lru_h": nrm(ks[2], (DEPTH, DEC_BATCH, D_LRU), 0.5),
        "state_conv": nrm(ks[3], (DEPTH, DEC_BATCH, CONV_W - 1, D_LRU), 1.0),
        "norm_mix_g": 1.0 + nrm(ks[4], (DEPTH, D_MODEL), 0.02),
        "w_in": nrm(ks[5], (DEPTH, D_MODEL, D_IN), D_MODEL ** -0.5),
        "conv_w": nrm(ks[6], (DEPTH, CONV_W, D_LRU), CONV_W ** -0.5),
        "conv_b": nrm(ks[7], (DEPTH, D_LRU), 0.02),
        "gate_r_w": nrm(ks[8], (DEPTH, LRU_BLOCKS, LRU_BLOCK_DIM, LRU_BLOCK_DIM), LRU_BLOCK_DIM ** -0.5),
        "gate_r_b": nrm(ks[9], (DEPTH, D_LRU), 0.02),
        "gate_i_w": nrm(ks[10], (DEPTH, LRU_BLOCKS, LRU_BLOCK_DIM, LRU_BLOCK_DIM), LRU_BLOCK_DIM ** -0.5),
        "gate_i_b": nrm(ks[11], (DEPTH, D_LRU), 0.02),
        "lru_lambda": lru_lambda,
        "sgu_norm_g": 1.0 + nrm(ks[13], (DEPTH, D_SGU), 0.02),
        "sgu_norm_b": nrm(ks[14], (DEPTH, D_SGU), 0.02),
        "sgu_w": nrm(ks[15], (DEPTH, SGU_HEADS, CHUNK, CHUNK), CHUNK ** -0.5),
        "sgu_b": 1.0 + nrm(ks[16], (DEPTH, SGU_HEADS, CHUNK), 0.02),
        "w_out": nrm(ks[17], (DEPTH, D_MIX, D_MODEL), D_MIX ** -0.5),
        "norm_mlp_g": 1.0 + nrm(ks[18], (DEPTH, D_MODEL), 0.02),
        "mlp_w1": nrm(ks[19], (DEPTH, D_MODEL, D_FF), D_MODEL ** -0.5),
        "mlp_w2": nrm(ks[20], (DEPTH, D_FF, D_MODEL), D_FF ** -0.5),
        "final_norm_g": 1.0 + nrm(ks[21], (D_MODEL,), 0.02),
    }


def reference(x_prompt, x_sample, state_lru_h, state_conv, norm_mix_g, w_in, conv_w, conv_b,
              gate_r_w, gate_r_b, gate_i_w, gate_i_b, lru_lambda, sgu_norm_g, sgu_norm_b, sgu_w, sgu_b,
              w_out, norm_mlp_g, mlp_w1, mlp_w2, final_norm_g):
    b_p = x_prompt.shape[0]
    h0_p = jnp.zeros((DEPTH, b_p, D_LRU), state_lru_h.dtype)
    conv0_p = jnp.zeros((DEPTH, b_p, CONV_W - 1, D_LRU), state_conv.dtype)
    y_prompt, hs_p, convs_p, _ = trunk(
        x_prompt, h0_p, conv0_p, norm_mix_g, w_in, conv_w, conv_b, gate_r_w, gate_r_b, gate_i_w, gate_i_b,
        lru_lambda, sgu_norm_g, sgu_norm_b, sgu_w, sgu_b, w_out, norm_mlp_g, mlp_w1, mlp_w2, final_norm_g)
    y_sample, hs_s, convs_s, vs_s = trunk(
        x_sample, state_lru_h, state_conv, norm_mix_g, w_in, conv_w, conv_b, gate_r_w, gate_r_b, gate_i_w, gate_i_b,
        lru_lambda, sgu_norm_g, sgu_norm_b, sgu_w, sgu_b, w_out, norm_mlp_g, mlp_w1, mlp_w2, final_norm_g)
    return (y_prompt, y_sample, jnp.stack(hs_p), jnp.stack(convs_p), jnp.stack(hs_s), jnp.stack(convs_s), jnp.stack(vs_s))
```

```python
import functools

import jax
import jax.numpy as jnp
from jax import lax
from jax.experimental import pallas as pl
from jax.experimental.pallas import tpu as pltpu

D_MODEL = 1024
DEPTH = 2
SGU_HEADS = 8
SGU_HEAD_DIM = 64
D_SGU = SGU_HEADS * SGU_HEAD_DIM
CHUNK = 128
D_LRU = 1024
LRU_BLOCKS = 16
LRU_BLOCK_DIM = 64
CONV_W = 4
LRU_C = 8.0
D_FF = 4 * D_MODEL
EPS = 1e-6

SUBLANES = 8
LANES = 128
GATE_TILE = 256
N_GATE_TILES = D_LRU // GATE_TILE
HEADS_PER_LANE_GROUP = LANES // SGU_HEAD_DIM
N_LANE_GROUPS = D_SGU // LANES

PROMPT_TILE = 512
MLP_TILE = 1024
VMEM_LIMIT = 56 * 1024 * 1024


def _rms_norm(x, g):
    return x * lax.rsqrt(jnp.mean(x * x, axis=-1, keepdims=True) + EPS) * g


def _layer_norm(x, g, b):
    mu = jnp.mean(x, axis=-1, keepdims=True)
    xc = x - mu
    return xc * lax.rsqrt(jnp.mean(xc * xc, axis=-1, keepdims=True) + EPS) * g + b


def _gelu(x):
    return jax.nn.gelu(x, approximate=True)


def _dot(a, b):
    return jnp.dot(a, b, preferred_element_type=jnp.float32)


def _lru_coeffs(xc, gate_w_ref, gate_rb, gate_ib, lam):
    xcb = xc.astype(jnp.bfloat16)
    r_parts, i_parts = [], []
    for j in range(N_GATE_TILES):
        ri = _dot(xcb[:, j * GATE_TILE:(j + 1) * GATE_TILE], gate_w_ref[j])
        r_parts.append(ri[:, :GATE_TILE])
        i_parts.append(ri[:, GATE_TILE:])
    r = jax.nn.sigmoid(jnp.concatenate(r_parts, axis=1) + gate_rb)
    i = jax.nn.sigmoid(jnp.concatenate(i_parts, axis=1) + gate_ib)
    log_a = (LRU_C * jax.nn.log_sigmoid(lam)) * r
    a = jnp.exp(log_a)
    b = jnp.sqrt(-jnp.tanh(log_a) * (a * a + 1.0)) * (i * xc)
    return a, b


def _mixer_prompt_kernel(x_ref, g_mix_ref, w_in_ref, conv_w_ref, conv_b_ref, gate_w_ref, gate_rb_ref,
                         gate_ib_ref, lam_ref, sgu_g_ref, sgu_bn_ref, sgu_w_ref, sgu_bias_ref, w_out_ref,
                         y_ref, hlast_ref, convnew_ref,
                         tail_sc, h_sc, a_sc, b_sc, hs_sc, *, tile):
    first = pl.program_id(1) == 0

    @pl.when(first)
    def _():
        tail_sc[...] = jnp.zeros_like(tail_sc)
        h_sc[...] = jnp.zeros_like(h_sc)

    x = x_ref[0]
    xn = _rms_norm(x, g_mix_ref[...]).astype(jnp.bfloat16)

    u = _gelu(_dot(xn, w_in_ref[:, 0:D_SGU]))
    v = _layer_norm(_gelu(_dot(xn, w_in_ref[:, D_SGU:2 * D_SGU])), sgu_g_ref[...], sgu_bn_ref[...])
    vb = v.astype(jnp.bfloat16)
    t_idx = lax.broadcasted_iota(jnp.int32, (HEADS_PER_LANE_GROUP * CHUNK, CHUNK), 0) % CHUNK
    s_idx = lax.broadcasted_iota(jnp.int32, (HEADS_PER_LANE_GROUP * CHUNK, CHUNK), 1)
    w_sgu = [jnp.where(s_idx <= t_idx, sgu_w_ref[g], 0.0).astype(jnp.bfloat16) for g in range(N_LANE_GROUPS)]
    lane = lax.broadcasted_iota(jnp.int32, (CHUNK, LANES), 1)
    gate_rows = []
    for c in range(tile // CHUNK):
        cols = []
        for g in range(N_LANE_GROUPS):
            res = _dot(w_sgu[g], vb[c * CHUNK:(c + 1) * CHUNK, g * LANES:(g + 1) * LANES])
            cols.append(jnp.where(lane < SGU_HEAD_DIM, res[:CHUNK], res[CHUNK:]))
        gate_rows.append(jnp.concatenate(cols, axis=1) + sgu_bias_ref[...])
    gate = jnp.concatenate(gate_rows, axis=0)
    out_a = (u * gate).astype(jnp.bfloat16)

    xb = _dot(xn, w_in_ref[:, 2 * D_SGU:2 * D_SGU + D_LRU])
    xp = jnp.concatenate([tail_sc[...], xb], axis=0)
    xc = conv_b_ref[...] + conv_w_ref[CONV_W - 1:CONV_W, :] * xb
    for k in range(1, CONV_W):
        shifted = pltpu.roll(xp, k, axis=0)[SUBLANES:]
        xc = xc + conv_w_ref[CONV_W - 1 - k:CONV_W - k, :] * shifted
    tail_sc[...] = xb[tile - SUBLANES:]
    convnew_ref[0] = xb[tile - SUBLANES:]

    a, b = _lru_coeffs(xc, gate_w_ref, gate_rb_ref[...], gate_ib_ref[...], lam_ref[...])
    row = lax.broadcasted_iota(jnp.int32, (tile, D_LRU), 0) % SUBLANES
    for d in (1, 2, 4):
        keep = row >= d
        a_prev = jnp.where(keep, pltpu.roll(a, d, axis=0), 1.0)
        b_prev = jnp.where(keep, pltpu.roll(b, d, axis=0), 0.0)
        b = a * b_prev + b
        a = a * a_prev
    a_sc[...] = a
    b_sc[...] = b

    def group_step(gi, h):
        r0 = pl.multiple_of(gi * SUBLANES, SUBLANES)
        h_in = jnp.broadcast_to(h[SUBLANES - 1:SUBLANES, :], (SUBLANES, D_LRU))
        h_new = a_sc[pl.ds(r0, SUBLANES), :] * h_in + b_sc[pl.ds(r0, SUBLANES), :]
        hs_sc[pl.ds(r0, SUBLANES), :] = h_new
        return h_new

    h_fin = lax.fori_loop(0, tile // SUBLANES, group_step, h_sc[...])
    h_sc[...] = h_fin
    hlast_ref[0] = h_fin

    yb = _dot(xn, w_in_ref[:, 2 * D_SGU + D_LRU:])
    out_b = (hs_sc[...] * _gelu(yb)).astype(jnp.bfloat16)
    out = _dot(out_a, w_out_ref[0:D_SGU, :]) + _dot(out_b, w_out_ref[D_SGU:, :])
    y_ref[0] = x + out


def _const_spec(shape):
    nd = len(shape)
    return pl.BlockSpec(shape, lambda *_: (0,) * nd, pipeline_mode=pl.Buffered(1))


def _mixer_prompt(x, p):
    bsz, seq, _ = x.shape
    tile = PROMPT_TILE
    consts = [p["g_mix"], p["w_in"], p["conv_w"], p["conv_b"], p["gate_w"], p["gate_rb"], p["gate_ib"], p["lam"],
              p["sgu_g"], p["sgu_bn"], p["sgu_w"], p["sgu_bias"], p["w_out"]]
    row_spec = pl.BlockSpec((1, tile, D_MODEL), lambda b, j: (b, j, 0))
    state_spec = pl.BlockSpec((1, SUBLANES, D_LRU), lambda b, j: (b, 0, 0))
    y, hlast, convnew = pl.pallas_call(
        functools.partial(_mixer_prompt_kernel, tile=tile),
        grid=(bsz, seq // tile),
        in_specs=[row_spec] + [_const_spec(c.shape) for c in consts],
        out_specs=[row_spec, state_spec, state_spec],
        out_shape=[jax.ShapeDtypeStruct(x.shape, jnp.float32),
                   jax.ShapeDtypeStruct((bsz, SUBLANES, D_LRU), jnp.float32),
                   jax.ShapeDtypeStruct((bsz, SUBLANES, D_LRU), jnp.float32)],
        scratch_shapes=[pltpu.VMEM((SUBLANES, D_LRU), jnp.float32),
                        pltpu.VMEM((SUBLANES, D_LRU), jnp.float32),
                        pltpu.VMEM((tile, D_LRU), jnp.float32),
                        pltpu.VMEM((tile, D_LRU), jnp.float32),
                        pltpu.VMEM((tile, D_LRU), jnp.float32)],
        compiler_params=pltpu.CompilerParams(dimension_semantics=("arbitrary", "arbitrary"),
                                             vmem_limit_bytes=VMEM_LIMIT),
        name="mixer_prompt",
    )(x, *consts)
    return y, hlast[:, SUBLANES - 1], convnew[:, SUBLANES - (CONV_W - 1):]


def _mixer_sample_kernel(x_ref, h0_ref, cbuf_ref, g_mix_ref, w_in_ref, conv_w_ref, conv_b_ref, gate_w_ref,
                         gate_rb_ref, gate_ib_ref, lam_ref, sgu_g_ref, sgu_bn_ref, sgu_w8_ref, sgu_b8_ref, w_out_ref,
                         y_ref, v_ref, hlast_ref, convnew_ref, *, steps, nb):
    x = x_ref[...]
    xn = _rms_norm(x, g_mix_ref[...]).astype(jnp.bfloat16)

    u = _gelu(_dot(xn, w_in_ref[:, 0:D_SGU]))
    v = _layer_norm(_gelu(_dot(xn, w_in_ref[:, D_SGU:2 * D_SGU])), sgu_g_ref[...], sgu_bn_ref[...])
    v_ref[...] = v
    gate_rows = []
    for t in range(steps):
        acc = sgu_b8_ref[t:t + 1, :] + sgu_w8_ref[t, 0:1, :] * v[0:nb]
        for s in range(1, t + 1):
            acc = acc + sgu_w8_ref[t, s:s + 1, :] * v[s * nb:(s + 1) * nb]
        gate_rows.append(acc)
    out_a = (u * jnp.concatenate(gate_rows, axis=0)).astype(jnp.bfloat16)

    xb = _dot(xn, w_in_ref[:, 2 * D_SGU:2 * D_SGU + D_LRU])
    xp = jnp.concatenate([cbuf_ref[...], xb], axis=0)
    xc = conv_b_ref[...] + conv_w_ref[0:1, :] * xp[0:steps * nb]
    for k in range(1, CONV_W):
        xc = xc + conv_w_ref[k:k + 1, :] * xp[k * nb:(k + steps) * nb]
    convnew_ref[...] = xp[steps * nb:]

    a, b = _lru_coeffs(xc, gate_w_ref, gate_rb_ref[...], gate_ib_ref[...], lam_ref[...])
    h = h0_ref[...]
    hs = []
    for t in range(steps):
        h = a[t * nb:(t + 1) * nb] * h + b[t * nb:(t + 1) * nb]
        hs.append(h)
    hlast_ref[...] = h

    yb = _dot(xn, w_in_ref[:, 2 * D_SGU + D_LRU:])
    out_b = (jnp.concatenate(hs, axis=0) * _gelu(yb)).astype(jnp.bfloat16)
    out = _dot(out_a, w_out_ref[0:D_SGU, :]) + _dot(out_b, w_out_ref[D_SGU:, :])
    y_ref[...] = x + out


def _mixer_sample(x_tm, h0, cbuf_tm, p, *, steps, nb):
    consts = [p["g_mix"], p["w_in"], p["conv_w"], p["conv_b"], p["gate_w"], p["gate_rb"], p["gate_ib"], p["lam"],
              p["sgu_g"], p["sgu_bn"], p["sgu_w8"], p["sgu_b8"], p["w_out"]]
    rows = steps * nb
    ins = [x_tm, h0, cbuf_tm] + consts
    out_shapes = [(rows, D_MODEL), (rows, D_SGU), (nb, D_LRU), ((CONV_W - 1) * nb, D_LRU)]
    return pl.pallas_call(
        functools.partial(_mixer_sample_kernel, steps=steps, nb=nb),
        grid=(1,),
        in_specs=[_const_spec(a.shape) for a in ins],
        out_specs=[_const_spec(s) for s in out_shapes],
        out_shape=[jax.ShapeDtypeStruct(s, jnp.float32) for s in out_shapes],
        compiler_params=pltpu.CompilerParams(dimension_semantics=("arbitrary",), vmem_limit_bytes=VMEM_LIMIT),
        name="mixer_sample",
    )(*ins)


def _mlp_kernel(x_ref, g_ref, w1_ref, w2_ref, gf_ref, y_ref, *, final_norm):
    x = x_ref[...]
    xn = _rms_norm(x, g_ref[...]).astype(jnp.bfloat16)
    acc = x
    for c in range(D_FF // D_MODEL):
        cols = slice(c * D_MODEL, (c + 1) * D_MODEL)
        hid = jnp.square(jnp.maximum(_dot(xn, w1_ref[:, cols]), 0.0)).astype(jnp.bfloat16)
        acc = acc + _dot(hid, w2_ref[cols, :])
    y_ref[...] = _rms_norm(acc, gf_ref[...]) if final_norm else acc


def _mlp(x2d, g, w1, w2, gf, *, final_norm):
    rows = x2d.shape[0]
    tile = min(MLP_TILE, rows)
    row_spec = pl.BlockSpec((tile, D_MODEL), lambda i: (i, 0))
    return pl.pallas_call(
        functools.partial(_mlp_kernel, final_norm=final_norm),
        grid=(rows // tile,),
        in_specs=[row_spec, _const_spec(g.shape), _const_spec(w1.shape), _const_spec(w2.shape), _const_spec(gf.shape)],
        out_specs=row_spec,
        out_shape=jax.ShapeDtypeStruct(x2d.shape, jnp.float32),
        compiler_params=pltpu.CompilerParams(dimension_semantics=("arbitrary",), vmem_limit_bytes=VMEM_LIMIT),
        name="mlp",
    )(x2d, g, w1, w2, gf)


def _block_diag_gates(w_r, w_i):
    per_tile = GATE_TILE // LRU_BLOCK_DIM
    eye = jnp.eye(per_tile, dtype=w_r.dtype)

    def tiles(w):
        w4 = w.reshape(N_GATE_TILES, per_tile, LRU_BLOCK_DIM, LRU_BLOCK_DIM)
        return jnp.einsum("jaik,ab->jaibk", w4, eye).reshape(N_GATE_TILES, GATE_TILE, GATE_TILE)

    return jnp.concatenate([tiles(w_r), tiles(w_i)], axis=-1).astype(jnp.bfloat16)


def _layer_params(l, steps, norm_mix_g, w_in, conv_w, conv_b, gate_r_w, gate_r_b, gate_i_w, gate_i_b, lru_lambda,
                  sgu_norm_g, sgu_norm_b, sgu_w, sgu_b, w_out, norm_mlp_g, mlp_w1, mlp_w2):
    row = lambda a: a.reshape(1, -1)
    return {
        "g_mix": row(norm_mix_g[l]),
        "w_in": w_in[l].astype(jnp.bfloat16),
        "conv_w": conv_w[l],
        "conv_b": row(conv_b[l]),
        "gate_w": _block_diag_gates(gate_r_w[l], gate_i_w[l]),
        "gate_rb": row(gate_r_b[l]),
        "gate_ib": row(gate_i_b[l]),
        "lam": row(lru_lambda[l]),
        "sgu_g": row(sgu_norm_g[l]),
        "sgu_bn": row(sgu_norm_b[l]),
        "sgu_w": sgu_w[l].reshape(N_LANE_GROUPS, HEADS_PER_LANE_GROUP * CHUNK, CHUNK),
        "sgu_bias": jnp.repeat(sgu_b[l].T, SGU_HEAD_DIM, axis=1),
        "sgu_w8": jnp.repeat(jnp.transpose(sgu_w[l][:, :steps, :steps], (1, 2, 0)), SGU_HEAD_DIM, axis=2),
        "sgu_b8": jnp.repeat(sgu_b[l][:, :steps].T, SGU_HEAD_DIM, axis=1),
        "w_out": w_out[l].astype(jnp.bfloat16),
        "g_mlp": row(norm_mlp_g[l]),
        "w1": mlp_w1[l].astype(jnp.bfloat16),
        "w2": mlp_w2[l].astype(jnp.bfloat16),
    }


def kernel(x_prompt, x_sample, state_lru_h, state_conv, norm_mix_g, w_in, conv_w, conv_b, gate_r_w, gate_r_b, gate_i_w, gate_i_b, lru_lambda, sgu_norm_g, sgu_norm_b, sgu_w, sgu_b, w_out, norm_mlp_g, mlp_w1, mlp_w2, final_norm_g):
    bsz, seq, _ = x_prompt.shape
    nb, steps, _ = x_sample.shape
    assert seq % PROMPT_TILE == 0 and PROMPT_TILE % CHUNK == 0 and steps <= CHUNK
    gf = final_norm_g.reshape(1, -1)

    xp = x_prompt
    xs = jnp.transpose(x_sample, (1, 0, 2)).reshape(steps * nb, D_MODEL)
    hs_p, convs_p, hs_s, convs_s, vs_s = [], [], [], [], []
    for l in range(DEPTH):
        p = _layer_params(l, steps, norm_mix_g, w_in, conv_w, conv_b, gate_r_w, gate_r_b, gate_i_w, gate_i_b,
                          lru_lambda, sgu_norm_g, sgu_norm_b, sgu_w, sgu_b, w_out, norm_mlp_g, mlp_w1, mlp_w2)
        last = l == DEPTH - 1

        xp, h_p, conv_p = _mixer_prompt(xp, p)
        xp = _mlp(xp.reshape(bsz * seq, D_MODEL), p["g_mlp"], p["w1"], p["w2"], gf,
                  final_norm=last).reshape(bsz, seq, D_MODEL)
        hs_p.append(h_p)
        convs_p.append(conv_p)

        cbuf_tm = jnp.transpose(state_conv[l], (1, 0, 2)).reshape((CONV_W - 1) * nb, D_LRU)
        xs, v_s, h_s, conv_s = _mixer_sample(xs, state_lru_h[l], cbuf_tm, p, steps=steps, nb=nb)
        xs = _mlp(xs, p["g_mlp"], p["w1"], p["w2"], gf, final_norm=last)
        hs_s.append(h_s)
        convs_s.append(jnp.transpose(conv_s.reshape(CONV_W - 1, nb, D_LRU), (1, 0, 2)))
        vs_s.append(jnp.transpose(v_s.reshape(steps, nb, D_SGU), (1, 0, 2)))

    y_sample = jnp.transpose(xs.reshape(steps, nb, D_MODEL), (1, 0, 2))
    return (xp, y_sample, jnp.stack(hs_p), jnp.stack(convs_p), jnp.stack(hs_s), jnp.stack(convs_s), jnp.stack(vs_s))
```

```python
import functools

import jax
import jax.numpy as jnp
from jax import lax
from jax.experimental import pallas as pl
from jax.experimental.pallas import tpu as pltpu

D_MODEL = 1024
DEPTH = 2
SGU_HEADS = 8
SGU_HEAD_DIM = 64
D_SGU = SGU_HEADS * SGU_HEAD_DIM
CHUNK = 128
D_LRU = 1024
LRU_BLOCKS = 16
LRU_BLOCK_DIM = 64
CONV_W = 4
LRU_C = 8.0
D_FF = 4 * D_MODEL
EPS = 1e-6

SUBLANES = 8
LANES = 128
GATE_TILE = 256
N_GATE_TILES = D_LRU // GATE_TILE
HEADS_PER_LANE_GROUP = LANES // SGU_HEAD_DIM
N_LANE_GROUPS = D_SGU // LANES
N_SLABS = D_LRU // LANES

HALF = CHUNK // 2
PITCH = HALF + SUBLANES
MLP_TILE = 1024
VMEM_LIMIT = 56 * 1024 * 1024


def _rms_norm(x, g):
    return x * lax.rsqrt(jnp.mean(x * x, axis=-1, keepdims=True) + EPS) * g


def _layer_norm(x, g, b):
    mu = jnp.mean(x, axis=-1, keepdims=True)
    xc = x - mu
    return xc * lax.rsqrt(jnp.mean(xc * xc, axis=-1, keepdims=True) + EPS) * g + b


def _gelu(x):
    return jax.nn.gelu(x, approximate=True)


def _dot(a, b):
    return jnp.dot(a, b, preferred_element_type=jnp.float32)


def _lru_coeffs(xc, gate_w_ref, gate_rb, gate_ib, lam):
    xcb = xc.astype(jnp.bfloat16)
    r_parts, i_parts = [], []
    for j in range(N_GATE_TILES):
        ri = _dot(xcb[:, j * GATE_TILE:(j + 1) * GATE_TILE], gate_w_ref[j])
        r_parts.append(ri[:, :GATE_TILE])
        i_parts.append(ri[:, GATE_TILE:])
    th_r = jnp.tanh(jnp.concatenate(r_parts, axis=1) + 0.5 * gate_rb)
    th_i = jnp.tanh(jnp.concatenate(i_parts, axis=1) + 0.5 * gate_ib)
    half_c = (0.5 * LRU_C) * jax.nn.log_sigmoid(lam)
    log_a = half_c * th_r + half_c
    i = 0.5 * th_i + 0.5
    a = jnp.exp(log_a)
    b = jnp.sqrt(-jnp.tanh(log_a) * (a * a + 1.0)) * (i * xc)
    return a, b


def _mixer_prompt_kernel(x_ref, g_mix_ref, w_in_ref, conv_wb_ref, gate_w_ref, gate_rb_ref, gate_ib_ref, lam_ref,
                         sgu_g_ref, sgu_bn_ref, sgu_w_ref, sgu_bias_ref, w_out_ref,
                         y_ref, hlast_ref, convnew_ref,
                         slab_sc, tail_sc, h_sc, vprev_sc, sgu_lhs_sc, *, nseq):
    rows = nseq * HALF
    j = pl.program_id(0)

    @pl.when(j == 0)
    def _():
        tail_sc[...] = jnp.zeros_like(tail_sc)
        h_sc[...] = jnp.zeros_like(h_sc)
        vprev_sc[...] = jnp.zeros_like(vprev_sc)
        t_idx = lax.broadcasted_iota(jnp.int32, (CHUNK, CHUNK), 0)
        s_idx = lax.broadcasted_iota(jnp.int32, (CHUNK, CHUNK), 1)
        for g in range(N_LANE_GROUPS):
            first, second = [], []
            for hh in range(HEADS_PER_LANE_GROUP):
                wm = jnp.where(s_idx <= t_idx, sgu_w_ref[HEADS_PER_LANE_GROUP * g + hh], 0.0)
                first.append(pltpu.roll(wm[:HALF], HALF, axis=1))
                second.append(wm[HALF:])
            sgu_lhs_sc[0, g] = jnp.concatenate(first, axis=0).astype(jnp.bfloat16)
            sgu_lhs_sc[1, g] = jnp.concatenate(second, axis=0).astype(jnp.bfloat16)

    parity = j % 2
    x = x_ref[...].reshape(rows, D_MODEL)
    xn = _rms_norm(x, g_mix_ref[...]).astype(jnp.bfloat16)

    u = _gelu(_dot(xn, w_in_ref[:, 0:D_SGU]))
    v = _layer_norm(_gelu(_dot(xn, w_in_ref[:, D_SGU:2 * D_SGU])), sgu_g_ref[...], sgu_bn_ref[...])
    vb = v.astype(jnp.bfloat16)
    lane = lax.broadcasted_iota(jnp.int32, (HALF, LANES), 1)
    bias = sgu_bias_ref[pl.ds(pl.multiple_of(parity * HALF, HALF), HALF), :]
    gate_rows = []
    for s in range(nseq):
        v_cur = vb[s * HALF:(s + 1) * HALF]
        v_full = jnp.concatenate([vprev_sc[1 - parity, s], v_cur], axis=0)
        vprev_sc[parity, s] = v_cur
        cols = []
        for g in range(N_LANE_GROUPS):
            res = _dot(sgu_lhs_sc[parity, g], v_full[:, g * LANES:(g + 1) * LANES])
            cols.append(jnp.where(lane < SGU_HEAD_DIM, res[:HALF], res[HALF:]))
        gate_rows.append(jnp.concatenate(cols, axis=1) + bias)
    out_a = (u * jnp.concatenate(gate_rows, axis=0)).astype(jnp.bfloat16)

    xb = _dot(xn, w_in_ref[:, 2 * D_SGU:2 * D_SGU + D_LRU])
    for c in range(N_SLABS):
        for s in range(nseq):
            slab_sc[c, s * PITCH:s * PITCH + HALF, :] = xb[s * HALF:(s + 1) * HALF, c * LANES:(c + 1) * LANES]

    xc_slabs = []
    for c in range(N_SLABS):
        taps = [conv_wb_ref[k, c] for k in range(CONV_W + 1)]
        p1, p2, p3 = (tail_sc[k, c] for k in range(CONV_W - 1))
        steps_out = []
        for t in range(HALF):
            cur = slab_sc[c, pl.ds(t, nseq, stride=PITCH), :]
            steps_out.append(taps[CONV_W] + taps[3] * cur + taps[2] * p1 + taps[1] * p2 + taps[0] * p3)
            p1, p2, p3 = cur, p1, p2
        for k, pk in enumerate((p1, p2, p3)):
            tail_sc[k, c] = pk
            convnew_ref[CONV_W - 2 - k, :, c * LANES:(c + 1) * LANES] = pk
        xc_slabs.append(jnp.concatenate(steps_out, axis=0))
    xc = jnp.concatenate(xc_slabs, axis=1)

    a, b = _lru_coeffs(xc, gate_w_ref, gate_rb_ref[...], gate_ib_ref[...], lam_ref[...])
    for c in range(N_SLABS):
        cols = slice(c * LANES, (c + 1) * LANES)
        h = h_sc[:, cols]
        for t in range(HALF):
            h = a[t * nseq:(t + 1) * nseq, cols] * h + b[t * nseq:(t + 1) * nseq, cols]
            slab_sc[c, pl.ds(t, nseq, stride=PITCH), :] = h
        h_sc[:, cols] = h
        hlast_ref[:, cols] = h
    hs = jnp.concatenate(
        [jnp.concatenate([slab_sc[c, s * PITCH:s * PITCH + HALF, :] for c in range(N_SLABS)], axis=1)
         for s in range(nseq)], axis=0)

    yb = _dot(xn, w_in_ref[:, 2 * D_SGU + D_LRU:])
    out_b = (hs * _gelu(yb)).astype(jnp.bfloat16)
    out = _dot(jnp.concatenate([out_a, out_b], axis=1), w_out_ref[...])
    y_ref[...] = (x + out).reshape(nseq, HALF, D_MODEL)


def _const_spec(shape):
    nd = len(shape)
    return pl.BlockSpec(shape, lambda *_: (0,) * nd, pipeline_mode=pl.Buffered(1))


def _mixer_prompt(x, p):
    nseq, seq, _ = x.shape
    consts = [p["g_mix"], p["w_in"], p["conv_wb"], p["gate_w"], p["gate_rb"], p["gate_ib"], p["lam"],
              p["sgu_g"], p["sgu_bn"], p["sgu_w"], p["sgu_bias"], p["w_out"]]
    row_spec = pl.BlockSpec((nseq, HALF, D_MODEL), lambda j: (0, j, 0))
    rows = nseq * HALF
    y, hlast, convnew = pl.pallas_call(
        functools.partial(_mixer_prompt_kernel, nseq=nseq),
        grid=(seq // HALF,),
        in_specs=[row_spec] + [_const_spec(c.shape) for c in consts],
        out_specs=[row_spec, _const_spec((nseq, D_LRU)), _const_spec((CONV_W - 1, nseq, D_LRU))],
        out_shape=[jax.ShapeDtypeStruct(x.shape, jnp.float32),
                   jax.ShapeDtypeStruct((nseq, D_LRU), jnp.float32),
                   jax.ShapeDtypeStruct((CONV_W - 1, nseq, D_LRU), jnp.float32)],
        scratch_shapes=[pltpu.VMEM((N_SLABS, nseq * PITCH, LANES), jnp.float32),
                        pltpu.VMEM((CONV_W - 1, N_SLABS, nseq, LANES), jnp.float32),
                        pltpu.VMEM((nseq, D_LRU), jnp.float32),
                        pltpu.VMEM((2, nseq, HALF, D_SGU), jnp.bfloat16),
                        pltpu.VMEM((2, N_LANE_GROUPS, CHUNK, CHUNK), jnp.bfloat16)],
        compiler_params=pltpu.CompilerParams(dimension_semantics=("arbitrary",), vmem_limit_bytes=VMEM_LIMIT),
        name="mixer_prompt",
    )(x, *consts)
    return y, hlast, jnp.transpose(convnew, (1, 0, 2))


def _mixer_sample_kernel(x_ref, h0_ref, cbuf_ref, g_mix_ref, w_in_ref, conv_w_ref, conv_b_ref, gate_w_ref,
                         gate_rb_ref, gate_ib_ref, lam_ref, sgu_g_ref, sgu_bn_ref, sgu_w8_ref, sgu_b8_ref, w_out_ref,
                         y_ref, v_ref, hlast_ref, convnew_ref, *, steps, nb):
    x = x_ref[...]
    xn = _rms_norm(x, g_mix_ref[...]).astype(jnp.bfloat16)

    u = _gelu(_dot(xn, w_in_ref[:, 0:D_SGU]))
    v = _layer_norm(_gelu(_dot(xn, w_in_ref[:, D_SGU:2 * D_SGU])), sgu_g_ref[...], sgu_bn_ref[...])
    v_ref[...] = v
    gate_rows = []
    for t in range(steps):
        acc = sgu_b8_ref[t:t + 1, :] + sgu_w8_ref[t, 0:1, :] * v[0:nb]
        for s in range(1, t + 1):
            acc = acc + sgu_w8_ref[t, s:s + 1, :] * v[s * nb:(s + 1) * nb]
        gate_rows.append(acc)
    out_a = (u * jnp.concatenate(gate_rows, axis=0)).astype(jnp.bfloat16)

    xb = _dot(xn, w_in_ref[:, 2 * D_SGU:2 * D_SGU + D_LRU])
    xp = jnp.concatenate([cbuf_ref[...], xb], axis=0)
    xc = conv_b_ref[...] + conv_w_ref[0:1, :] * xp[0:steps * nb]
    for k in range(1, CONV_W):
        xc = xc + conv_w_ref[k:k + 1, :] * xp[k * nb:(k + steps) * nb]
    convnew_ref[...] = xp[steps * nb:]

    a, b = _lru_coeffs(xc, gate_w_ref, gate_rb_ref[...], gate_ib_ref[...], lam_ref[...])
    h = h0_ref[...]
    hs = []
    for t in range(steps):
        h = a[t * nb:(t + 1) * nb] * h + b[t * nb:(t + 1) * nb]
        hs.append(h)
    hlast_ref[...] = h

    yb = _dot(xn, w_in_ref[:, 2 * D_SGU + D_LRU:])
    out_b = (jnp.concatenate(hs, axis=0) * _gelu(yb)).astype(jnp.bfloat16)
    out = _dot(jnp.concatenate([out_a, out_b], axis=1), w_out_ref[...])
    y_ref[...] = x + out


def _mixer_sample(x_tm, h0, cbuf_tm, p, *, steps, nb):
    consts = [p["g_mix"], p["w_in"], p["conv_w"], p["conv_b"], p["gate_w"], p["gate_rb"], p["gate_ib"], p["lam"],
              p["sgu_g"], p["sgu_bn"], p["sgu_w8"], p["sgu_b8"], p["w_out"]]
    rows = steps * nb
    ins = [x_tm, h0, cbuf_tm] + consts
    out_shapes = [(rows, D_MODEL), (rows, D_SGU), (nb, D_LRU), ((CONV_W - 1) * nb, D_LRU)]
    return pl.pallas_call(
        functools.partial(_mixer_sample_kernel, steps=steps, nb=nb),
        grid=(1,),
        in_specs=[_const_spec(a.shape) for a in ins],
        out_specs=[_const_spec(s) for s in out_shapes],
        out_shape=[jax.ShapeDtypeStruct(s, jnp.float32) for s in out_shapes],
        compiler_params=pltpu.CompilerParams(dimension_semantics=("arbitrary",), vmem_limit_bytes=VMEM_LIMIT),
        name="mixer_sample",
    )(*ins)


def _mlp_kernel(x_ref, g_ref, w1_ref, w2_ref, gf_ref, y_ref, *, final_norm):
    x = x_ref[...]
    xn = _rms_norm(x, g_ref[...]).astype(jnp.bfloat16)
    acc = x
    for c in range(D_FF // D_MODEL):
        cols = slice(c * D_MODEL, (c + 1) * D_MODEL)
        hid = jnp.square(jnp.maximum(_dot(xn, w1_ref[:, cols]), 0.0)).astype(jnp.bfloat16)
        acc = acc + _dot(hid, w2_ref[cols, :])
    y_ref[...] = _rms_norm(acc, gf_ref[...]) if final_norm else acc


def _mlp(x2d, g, w1, w2, gf, *, final_norm):
    rows = x2d.shape[0]
    tile = min(MLP_TILE, rows)
    row_spec = pl.BlockSpec((tile, D_MODEL), lambda i: (i, 0))
    return pl.pallas_call(
        functools.partial(_mlp_kernel, final_norm=final_norm),
        grid=(rows // tile,),
        in_specs=[row_spec, _const_spec(g.shape), _const_spec(w1.shape), _const_spec(w2.shape), _const_spec(gf.shape)],
        out_specs=row_spec,
        out_shape=jax.ShapeDtypeStruct(x2d.shape, jnp.float32),
        compiler_params=pltpu.CompilerParams(dimension_semantics=("arbitrary",), vmem_limit_bytes=VMEM_LIMIT),
        name="mlp",
    )(x2d, g, w1, w2, gf)


def _block_diag_gates(w_r, w_i):
    per_tile = GATE_TILE // LRU_BLOCK_DIM
    eye = jnp.eye(per_tile, dtype=w_r.dtype)

    def tiles(w):
        w4 = w.reshape(N_GATE_TILES, per_tile, LRU_BLOCK_DIM, LRU_BLOCK_DIM)
        return jnp.einsum("jaik,ab->jaibk", w4, eye).reshape(N_GATE_TILES, GATE_TILE, GATE_TILE)

    return (0.5 * jnp.concatenate([tiles(w_r), tiles(w_i)], axis=-1)).astype(jnp.bfloat16)


def _layer_params(l, steps, nseq, norm_mix_g, w_in, conv_w, conv_b, gate_r_w, gate_r_b, gate_i_w, gate_i_b,
                  lru_lambda, sgu_norm_g, sgu_norm_b, sgu_w, sgu_b, w_out, norm_mlp_g, mlp_w1, mlp_w2):
    row = lambda a: a.reshape(1, -1)
    conv_wb = jnp.concatenate([conv_w[l], row(conv_b[l])], axis=0)
    return {
        "g_mix": row(norm_mix_g[l]),
        "w_in": w_in[l].astype(jnp.bfloat16),
        "conv_w": conv_w[l],
        "conv_b": row(conv_b[l]),
        "conv_wb": jnp.broadcast_to(conv_wb.reshape(CONV_W + 1, N_SLABS, 1, LANES),
                                    (CONV_W + 1, N_SLABS, nseq, LANES)),
        "gate_w": _block_diag_gates(gate_r_w[l], gate_i_w[l]),
        "gate_rb": row(gate_r_b[l]),
        "gate_ib": row(gate_i_b[l]),
        "lam": row(lru_lambda[l]),
        "sgu_g": row(sgu_norm_g[l]),
        "sgu_bn": row(sgu_norm_b[l]),
        "sgu_w": sgu_w[l],
        "sgu_bias": jnp.repeat(sgu_b[l].T, SGU_HEAD_DIM, axis=1),
        "sgu_w8": jnp.repeat(jnp.transpose(sgu_w[l][:, :steps, :steps], (1, 2, 0)), SGU_HEAD_DIM, axis=2),
        "sgu_b8": jnp.repeat(sgu_b[l][:, :steps].T, SGU_HEAD_DIM, axis=1),
        "w_out": w_out[l].astype(jnp.bfloat16),
        "g_mlp": row(norm_mlp_g[l]),
        "w1": mlp_w1[l].astype(jnp.bfloat16),
        "w2": mlp_w2[l].astype(jnp.bfloat16),
    }


def kernel(x_prompt, x_sample, state_lru_h, state_conv, norm_mix_g, w_in, conv_w, conv_b, gate_r_w, gate_r_b, gate_i_w, gate_i_b, lru_lambda, sgu_norm_g, sgu_norm_b, sgu_w, sgu_b, w_out, norm_mlp_g, mlp_w1, mlp_w2, final_norm_g):
    nseq, seq, _ = x_prompt.shape
    nb, steps, _ = x_sample.shape
    assert seq % CHUNK == 0 and nseq == SUBLANES and steps <= CHUNK
    gf = final_norm_g.reshape(1, -1)

    xp = x_prompt
    xs = jnp.transpose(x_sample, (1, 0, 2)).reshape(steps * nb, D_MODEL)
    hs_p, convs_p, hs_s, convs_s, vs_s = [], [], [], [], []
    for l in range(DEPTH):
        p = _layer_params(l, steps, nseq, norm_mix_g, w_in, conv_w, conv_b, gate_r_w, gate_r_b, gate_i_w, gate_i_b,
                          lru_lambda, sgu_norm_g, sgu_norm_b, sgu_w, sgu_b, w_out, norm_mlp_g, mlp_w1, mlp_w2)
        last = l == DEPTH - 1

        xp, h_p, conv_p = _mixer_prompt(xp, p)
        xp = _mlp(xp.reshape(nseq * seq, D_MODEL), p["g_mlp"], p["w1"], p["w2"], gf,
                  final_norm=last).reshape(nseq, seq, D_MODEL)
        hs_p.append(h_p)
        convs_p.append(conv_p)

        cbuf_tm = jnp.transpose(state_conv[l], (1, 0, 2)).reshape((CONV_W - 1) * nb, D_LRU)
        xs, v_s, h_s, conv_s = _mixer_sample(xs, state_lru_h[l], cbuf_tm, p, steps=steps, nb=nb)
        xs = _mlp(xs, p["g_mlp"], p["w1"], p["w2"], gf, final_norm=last)
        hs_s.append(h_s)
        convs_s.append(jnp.transpose(conv_s.reshape(CONV_W - 1, nb, D_LRU), (1, 0, 2)))
        vs_s.append(jnp.transpose(v_s.reshape(steps, nb, D_SGU), (1, 0, 2)))

    y_sample = jnp.transpose(xs.reshape(steps, nb, D_MODEL), (1, 0, 2))
    return (xp, y_sample, jnp.stack(hs_p), jnp.stack(convs_p), jnp.stack(hs_s), jnp.stack(convs_s), jnp.stack(vs_s))
```

```python
import functools

import jax
import jax.numpy as jnp
from jax import lax
from jax.experimental import pallas as pl
from jax.experimental.pallas import tpu as pltpu

D_MODEL = 1024
DEPTH = 2
SGU_HEADS = 8
SGU_HEAD_DIM = 64
D_SGU = SGU_HEADS * SGU_HEAD_DIM
CHUNK = 128
D_LRU = 1024
LRU_BLOCKS = 16
LRU_BLOCK_DIM = 64
CONV_W = 4
LRU_C = 8.0
D_FF = 4 * D_MODEL
EPS = 1e-6

SUBLANES = 8
LANES = 128
GATE_TILE = 256
N_GATE_TILES = D_LRU // GATE_TILE
HEADS_PER_LANE_GROUP = LANES // SGU_HEAD_DIM
N_LANE_GROUPS = D_SGU // LANES
N_SLABS = D_LRU // LANES
FF_CHUNK = D_MODEL
N_FF_CHUNKS = D_FF // FF_CHUNK

HALF = CHUNK // 2
PITCH = HALF + SUBLANES
MLP_TILE = 1024
VMEM_LIMIT = 60 * 1024 * 1024


def _rms_norm(x, g):
    return x * lax.rsqrt(jnp.mean(x * x, axis=-1, keepdims=True) + EPS) * g


def _layer_norm(x, g, b):
    mu = jnp.mean(x, axis=-1, keepdims=True)
    xc = x - mu
    return xc * lax.rsqrt(jnp.mean(xc * xc, axis=-1, keepdims=True) + EPS) * g + b


def _gelu(x):
    return jax.nn.gelu(x, approximate=True)


def _dot(a, b):
    return jnp.dot(a, b, preferred_element_type=jnp.float32)


def _lru_elementwise(r_pre, i_pre, xc, half_rb, half_ib, half_c):
    th_r = jnp.tanh(r_pre + half_rb)
    th_i = jnp.tanh(i_pre + half_ib)
    log_a = half_c * th_r + half_c
    i = 0.5 * th_i + 0.5
    a = jnp.exp(log_a)
    b = jnp.sqrt(-jnp.tanh(log_a) * (a * a + 1.0)) * (i * xc)
    return a, b


def _lru_consts(gate_rb, gate_ib, lam):
    return 0.5 * gate_rb, 0.5 * gate_ib, (0.5 * LRU_C) * jax.nn.log_sigmoid(lam)


def _lru_coeffs(xc, gate_w_ref, gate_rb, gate_ib, lam):
    xcb = xc.astype(jnp.bfloat16)
    r_parts, i_parts = [], []
    for j in range(N_GATE_TILES):
        ri = _dot(xcb[:, j * GATE_TILE:(j + 1) * GATE_TILE], gate_w_ref[j])
        r_parts.append(ri[:, :GATE_TILE])
        i_parts.append(ri[:, GATE_TILE:])
    return _lru_elementwise(jnp.concatenate(r_parts, axis=1), jnp.concatenate(i_parts, axis=1), xc,
                            *_lru_consts(gate_rb, gate_ib, lam))


def _mlp_up(xn, w1_ref, c):
    cols = slice(c * FF_CHUNK, (c + 1) * FF_CHUNK)
    return jnp.square(jnp.maximum(_dot(xn, w1_ref[:, cols]), 0.0)).astype(jnp.bfloat16)


def _mlp_down(hid, w2_ref, c):
    return _dot(hid, w2_ref[c * FF_CHUNK:(c + 1) * FF_CHUNK, :])


def _layer_prompt_kernel(x_ref, g_mix_ref, w_in_ref, conv_wb_ref, gate_w_ref, gate_rb_ref, gate_ib_ref, lam_ref,
                         sgu_g_ref, sgu_bn_ref, sgu_w_ref, sgu_bias_ref, w_out_ref, g_mlp_ref, w1_ref, w2_ref, gf_ref,
                         y_ref, hlast_ref, convnew_ref, vprev_sc, sgu_lhs_sc, ymix_sc, *slab_scratch,
                         nseq, n_tiles, final_norm):
    rows = nseq * HALF
    j = pl.program_id(0)
    slab_sc = slab_scratch[0:N_SLABS]
    tail_sc = slab_scratch[N_SLABS:2 * N_SLABS]
    h_sc = slab_scratch[2 * N_SLABS:3 * N_SLABS]

    @pl.when(j == 0)
    def _():
        for c in range(N_SLABS):
            tail_sc[c][...] = jnp.zeros_like(tail_sc[c])
            h_sc[c][...] = jnp.zeros_like(h_sc[c])
        vprev_sc[...] = jnp.zeros_like(vprev_sc)
        ymix_sc[...] = jnp.zeros_like(ymix_sc)
        t_idx = lax.broadcasted_iota(jnp.int32, (CHUNK, CHUNK), 0)
        s_idx = lax.broadcasted_iota(jnp.int32, (CHUNK, CHUNK), 1)
        for g in range(N_LANE_GROUPS):
            first, second = [], []
            for hh in range(HEADS_PER_LANE_GROUP):
                wm = jnp.where(s_idx <= t_idx, sgu_w_ref[HEADS_PER_LANE_GROUP * g + hh], 0.0)
                first.append(pltpu.roll(wm[:HALF], HALF, axis=1))
                second.append(wm[HALF:])
            sgu_lhs_sc[0, g] = jnp.concatenate(first, axis=0).astype(jnp.bfloat16)
            sgu_lhs_sc[1, g] = jnp.concatenate(second, axis=0).astype(jnp.bfloat16)

    parity = j % 2
    half_rb, half_ib, half_c = _lru_consts(gate_rb_ref[...], gate_ib_ref[...], lam_ref[...])
    xb_col0 = 2 * D_SGU
    yb_col0 = 2 * D_SGU + D_LRU
    slabs_per_tile = GATE_TILE // LANES
    st = [{} for _ in range(N_GATE_TILES)]
    sg = {}
    ml = {}

    def mlp_norm():
        xm = ymix_sc[1 - parity]
        ml["acc"] = xm
        ml["xn"] = _rms_norm(xm, g_mlp_ref[...]).astype(jnp.bfloat16)

    def mlp_up(c):
        ml["hid", c] = _mlp_up(ml["xn"], w1_ref, c)

    def mlp_down(c):
        ml["acc"] = ml["acc"] + _mlp_down(ml.pop(("hid", c)), w2_ref, c)

    def mlp_store():
        out = _rms_norm(ml["acc"], gf_ref[...]) if final_norm else ml["acc"]
        y_ref[...] = out.reshape(nseq, HALF, D_MODEL)

    def mix_norm():
        sg["x"] = x_ref[...].reshape(rows, D_MODEL)
        sg["xn"] = _rms_norm(sg["x"], g_mix_ref[...]).astype(jnp.bfloat16)

    def lru_proj(q):
        xb = _dot(sg["xn"], w_in_ref[:, xb_col0 + q * GATE_TILE:xb_col0 + (q + 1) * GATE_TILE])
        for i in range(slabs_per_tile):
            for s in range(nseq):
                slab_sc[slabs_per_tile * q + i][s * PITCH:s * PITCH + HALF, :] = (
                    xb[s * HALF:(s + 1) * HALF, i * LANES:(i + 1) * LANES])

    def lru_conv(q):
        xc_slabs = []
        for c in range(slabs_per_tile * q, slabs_per_tile * (q + 1)):
            taps = [conv_wb_ref[k, c] for k in range(CONV_W + 1)]
            p1, p2, p3 = (tail_sc[c][k] for k in range(CONV_W - 1))
            steps_out = []
            for t in range(HALF):
                cur = slab_sc[c][pl.ds(t, nseq, stride=PITCH), :]
                steps_out.append(taps[CONV_W] + taps[3] * cur + taps[2] * p1 + taps[1] * p2 + taps[0] * p3)
                p1, p2, p3 = cur, p1, p2
            for k, pk in enumerate((p1, p2, p3)):
                tail_sc[c][k] = pk
            xc_slabs.append(jnp.concatenate(steps_out, axis=0))
        st[q]["xc"] = jnp.concatenate(xc_slabs, axis=1)

    def lru_gates(q):
        st[q]["ri"] = _dot(st[q]["xc"].astype(jnp.bfloat16), gate_w_ref[q])

    def lru_coeffs(q):
        cols = slice(q * GATE_TILE, (q + 1) * GATE_TILE)
        ri = st[q].pop("ri")
        st[q]["ab"] = _lru_elementwise(ri[:, :GATE_TILE], ri[:, GATE_TILE:], st[q].pop("xc"),
                                       half_rb[:, cols], half_ib[:, cols], half_c[:, cols])

    def lru_scan(q):
        a, b = st[q].pop("ab")
        for i in range(slabs_per_tile):
            c = slabs_per_tile * q + i
            h = h_sc[c][...]
            for t in range(HALF):
                h = (a[t * nseq:(t + 1) * nseq, i * LANES:(i + 1) * LANES] * h
                     + b[t * nseq:(t + 1) * nseq, i * LANES:(i + 1) * LANES])
                slab_sc[c][pl.ds(t, nseq, stride=PITCH), :] = h
            h_sc[c][...] = h

    def lru_out(q):
        hs = jnp.concatenate(
            [jnp.concatenate([slab_sc[c][s * PITCH:s * PITCH + HALF, :]
                              for c in range(slabs_per_tile * q, slabs_per_tile * (q + 1))], axis=1)
             for s in range(nseq)], axis=0)
        yb = _dot(sg["xn"], w_in_ref[:, yb_col0 + q * GATE_TILE:yb_col0 + (q + 1) * GATE_TILE])
        st[q]["out_b"] = (hs * _gelu(yb)).astype(jnp.bfloat16)

    def lru_mix(q):
        r0 = D_SGU + q * GATE_TILE
        sg["acc"] = sg["acc"] + _dot(st[q].pop("out_b"), w_out_ref[r0:r0 + GATE_TILE, :])

    def sgu_u():
        sg["u"] = _gelu(_dot(sg["xn"], w_in_ref[:, 0:D_SGU]))

    def sgu_v():
        v = _layer_norm(_gelu(_dot(sg["xn"], w_in_ref[:, D_SGU:2 * D_SGU])), sgu_g_ref[...], sgu_bn_ref[...])
        sg["vb"] = v.astype(jnp.bfloat16)

    def sgu_gate():
        vb = sg.pop("vb")
        lane = lax.broadcasted_iota(jnp.int32, (HALF, LANES), 1)
        bias = sgu_bias_ref[pl.ds(pl.multiple_of(parity * HALF, HALF), HALF), :]
        v_full = [jnp.concatenate([vprev_sc[1 - parity, s], vb[s * HALF:(s + 1) * HALF]], axis=0)
                  for s in range(nseq)]
        vprev_sc[parity] = vb.reshape(nseq, HALF, D_SGU)
        gate_rows = [[] for _ in range(nseq)]
        for s in range(0, nseq, 2):
            for g in range(N_LANE_GROUPS):
                lanes = slice(g * LANES, (g + 1) * LANES)
                rhs = jnp.concatenate([v_full[s][:, lanes], v_full[s + 1][:, lanes]], axis=1)
                res = _dot(sgu_lhs_sc[parity, g], rhs)
                for i in range(2):
                    part = res[:, i * LANES:(i + 1) * LANES]
                    gate_rows[s + i].append(jnp.where(lane < SGU_HEAD_DIM, part[:HALF], part[HALF:]))
        gate = jnp.concatenate([jnp.concatenate(r, axis=1) + bias for r in gate_rows], axis=0)
        sg["out_a"] = (sg.pop("u") * gate).astype(jnp.bfloat16)

    def sgu_mix():
        sg["acc"] = sg["x"] + _dot(sg.pop("out_a"), w_out_ref[0:D_SGU, :])

    def mix_store():
        ymix_sc[parity] = sg["acc"]

    tiles = range(N_GATE_TILES)
    mlp_norm()
    mix_norm()
    for q in tiles:
        lru_proj(q)
    mlp_up(0)
    for q in tiles:
        lru_conv(q)
        lru_gates(q)
    mlp_up(1)
    sgu_u()
    sgu_v()
    for q in tiles:
        lru_coeffs(q)
        lru_scan(q)
    mlp_down(0)
    mlp_down(1)
    for q in tiles:
        lru_out(q)
    mlp_up(2)
    mlp_up(3)
    sgu_gate()
    sgu_mix()
    mlp_down(2)
    mlp_down(3)
    for q in tiles:
        lru_mix(q)
    mix_store()
    mlp_store()

    @pl.when(j == n_tiles - 1)
    def _():
        for c in range(N_SLABS):
            cols = slice(c * LANES, (c + 1) * LANES)
            hlast_ref[:, cols] = h_sc[c][...]
            for k in range(CONV_W - 1):
                convnew_ref[CONV_W - 2 - k, :, cols] = tail_sc[c][k]


def _const_spec(shape):
    nd = len(shape)
    return pl.BlockSpec(shape, lambda *_: (0,) * nd, pipeline_mode=pl.Buffered(1))


def _layer_spec(arr, l):
    nd = arr.ndim - 1
    return pl.BlockSpec((None,) + arr.shape[1:], lambda *_: (l,) + (0,) * nd, pipeline_mode=pl.Buffered(1))


_PROMPT_PARAMS = ("g_mix", "w_in", "conv_wb", "gate_w", "gate_rb", "gate_ib", "lam", "sgu_g", "sgu_bn", "sgu_w",
                  "sgu_bias", "w_out", "g_mlp", "w1", "w2")
_SAMPLE_MIXER_PARAMS = ("g_mix", "w_in", "conv_w", "conv_b", "gate_w", "gate_rb", "gate_ib", "lam", "sgu_g",
                        "sgu_bn", "sgu_w8", "sgu_b8", "w_out")


def _layer_prompt(x, p, gf, l, *, final_norm):
    nseq, seq, _ = x.shape
    n_tiles = seq // HALF
    rows = nseq * HALF
    consts = [p[k] for k in _PROMPT_PARAMS]
    x_spec = pl.BlockSpec((nseq, HALF, D_MODEL), lambda j: (0, jnp.minimum(j, n_tiles - 1), 0))
    y_spec = pl.BlockSpec((nseq, HALF, D_MODEL), lambda j: (0, jnp.maximum(j - 1, 0), 0))
    y, hlast, convnew = pl.pallas_call(
        functools.partial(_layer_prompt_kernel, nseq=nseq, n_tiles=n_tiles, final_norm=final_norm),
        grid=(n_tiles + 1,),
        in_specs=[x_spec] + [_layer_spec(c, l) for c in consts] + [_const_spec(gf.shape)],
        out_specs=[y_spec, _const_spec((nseq, D_LRU)), _const_spec((CONV_W - 1, nseq, D_LRU))],
        out_shape=[jax.ShapeDtypeStruct(x.shape, jnp.float32),
                   jax.ShapeDtypeStruct((nseq, D_LRU), jnp.float32),
                   jax.ShapeDtypeStruct((CONV_W - 1, nseq, D_LRU), jnp.float32)],
        scratch_shapes=([pltpu.VMEM((2, nseq, HALF, D_SGU), jnp.bfloat16),
                         pltpu.VMEM((2, N_LANE_GROUPS, CHUNK, CHUNK), jnp.bfloat16),
                         pltpu.VMEM((2, rows, D_MODEL), jnp.float32)]
                        + [pltpu.VMEM((nseq * PITCH, LANES), jnp.float32)] * N_SLABS
                        + [pltpu.VMEM((CONV_W - 1, nseq, LANES), jnp.float32)] * N_SLABS
                        + [pltpu.VMEM((nseq, LANES), jnp.float32)] * N_SLABS),
        compiler_params=pltpu.CompilerParams(dimension_semantics=("arbitrary",), vmem_limit_bytes=VMEM_LIMIT),
        name="layer_prompt",
    )(x, *consts, gf)
    return y, hlast, jnp.transpose(convnew, (1, 0, 2))


def _mixer_sample_kernel(x_ref, h0_ref, cbuf_ref, g_mix_ref, w_in_ref, conv_w_ref, conv_b_ref, gate_w_ref,
                         gate_rb_ref, gate_ib_ref, lam_ref, sgu_g_ref, sgu_bn_ref, sgu_w8_ref, sgu_b8_ref, w_out_ref,
                         y_ref, v_ref, hlast_ref, convnew_ref, *, steps, nb):
    x = x_ref[...]
    xn = _rms_norm(x, g_mix_ref[...]).astype(jnp.bfloat16)

    u = _gelu(_dot(xn, w_in_ref[:, 0:D_SGU]))
    v = _layer_norm(_gelu(_dot(xn, w_in_ref[:, D_SGU:2 * D_SGU])), sgu_g_ref[...], sgu_bn_ref[...])
    v_ref[...] = v
    gate_rows = []
    for t in range(steps):
        acc = sgu_b8_ref[t:t + 1, :] + sgu_w8_ref[t, 0:1, :] * v[0:nb]
        for s in range(1, t + 1):
            acc = acc + sgu_w8_ref[t, s:s + 1, :] * v[s * nb:(s + 1) * nb]
        gate_rows.append(acc)
    out_a = (u * jnp.concatenate(gate_rows, axis=0)).astype(jnp.bfloat16)

    xb = _dot(xn, w_in_ref[:, 2 * D_SGU:2 * D_SGU + D_LRU])
    xp = jnp.concatenate([cbuf_ref[...], xb], axis=0)
    xc = conv_b_ref[...] + conv_w_ref[0:1, :] * xp[0:steps * nb]
    for k in range(1, CONV_W):
        xc = xc + conv_w_ref[k:k + 1, :] * xp[k * nb:(k + steps) * nb]
    convnew_ref[...] = xp[steps * nb:]

    a, b = _lru_coeffs(xc, gate_w_ref, gate_rb_ref[...], gate_ib_ref[...], lam_ref[...])
    h = h0_ref[...]
    hs = []
    for t in range(steps):
        h = a[t * nb:(t + 1) * nb] * h + b[t * nb:(t + 1) * nb]
        hs.append(h)
    hlast_ref[...] = h

    yb = _dot(xn, w_in_ref[:, 2 * D_SGU + D_LRU:])
    out_b = (jnp.concatenate(hs, axis=0) * _gelu(yb)).astype(jnp.bfloat16)
    out = _dot(jnp.concatenate([out_a, out_b], axis=1), w_out_ref[...])
    y_ref[...] = x + out


def _mixer_sample(x_tm, h0_all, cbuf_tm_all, p, l, *, steps, nb):
    consts = [p[k] for k in _SAMPLE_MIXER_PARAMS]
    rows = steps * nb
    out_shapes = [(rows, D_MODEL), (rows, D_SGU), (nb, D_LRU), ((CONV_W - 1) * nb, D_LRU)]
    return pl.pallas_call(
        functools.partial(_mixer_sample_kernel, steps=steps, nb=nb),
        grid=(1,),
        in_specs=([_const_spec(x_tm.shape), _layer_spec(h0_all, l), _layer_spec(cbuf_tm_all, l)]
                  + [_layer_spec(c, l) for c in consts]),
        out_specs=[_const_spec(s) for s in out_shapes],
        out_shape=[jax.ShapeDtypeStruct(s, jnp.float32) for s in out_shapes],
        compiler_params=pltpu.CompilerParams(dimension_semantics=("arbitrary",), vmem_limit_bytes=VMEM_LIMIT),
        name="mixer_sample",
    )(x_tm, h0_all, cbuf_tm_all, *consts)


def _mlp_kernel(x_ref, g_ref, w1_ref, w2_ref, gf_ref, y_ref, *, final_norm):
    x = x_ref[...]
    xn = _rms_norm(x, g_ref[...]).astype(jnp.bfloat16)
    acc = x
    for c in range(N_FF_CHUNKS):
        acc = acc + _mlp_down(_mlp_up(xn, w1_ref, c), w2_ref, c)
    y_ref[...] = _rms_norm(acc, gf_ref[...]) if final_norm else acc


def _mlp(x2d, p, gf, l, *, final_norm):
    rows = x2d.shape[0]
    tile = min(MLP_TILE, rows)
    row_spec = pl.BlockSpec((tile, D_MODEL), lambda i: (i, 0))
    return pl.pallas_call(
        functools.partial(_mlp_kernel, final_norm=final_norm),
        grid=(rows // tile,),
        in_specs=[row_spec, _layer_spec(p["g_mlp"], l), _layer_spec(p["w1"], l), _layer_spec(p["w2"], l),
                  _const_spec(gf.shape)],
        out_specs=row_spec,
        out_shape=jax.ShapeDtypeStruct(x2d.shape, jnp.float32),
        compiler_params=pltpu.CompilerParams(dimension_semantics=("arbitrary",), vmem_limit_bytes=VMEM_LIMIT),
        name="mlp",
    )(x2d, p["g_mlp"], p["w1"], p["w2"], gf)


def _block_diag_gates(w_r, w_i):
    per_tile = GATE_TILE // LRU_BLOCK_DIM
    eye = jnp.eye(per_tile, dtype=w_r.dtype)

    def tiles(w):
        w4 = w.reshape(DEPTH, N_GATE_TILES, per_tile, LRU_BLOCK_DIM, LRU_BLOCK_DIM)
        return jnp.einsum("ljaik,ab->ljaibk", w4, eye).reshape(DEPTH, N_GATE_TILES, GATE_TILE, GATE_TILE)

    return (0.5 * jnp.concatenate([tiles(w_r), tiles(w_i)], axis=-1)).astype(jnp.bfloat16)


def _prepare_params(steps, nseq, norm_mix_g, w_in, conv_w, conv_b, gate_r_w, gate_r_b, gate_i_w, gate_i_b,
                    lru_lambda, sgu_norm_g, sgu_norm_b, sgu_w, sgu_b, w_out, norm_mlp_g, mlp_w1, mlp_w2):
    row = lambda a: a.reshape(DEPTH, 1, -1)
    conv_wb = jnp.concatenate([conv_w, conv_b[:, None, :]], axis=1)
    return {
        "g_mix": row(norm_mix_g),
        "w_in": w_in.astype(jnp.bfloat16),
        "conv_w": conv_w,
        "conv_b": row(conv_b),
        "conv_wb": jnp.broadcast_to(conv_wb.reshape(DEPTH, CONV_W + 1, N_SLABS, 1, LANES),
                                    (DEPTH, CONV_W + 1, N_SLABS, nseq, LANES)),
        "gate_w": _block_diag_gates(gate_r_w, gate_i_w),
        "gate_rb": row(gate_r_b),
        "gate_ib": row(gate_i_b),
        "lam": row(lru_lambda),
        "sgu_g": row(sgu_norm_g),
        "sgu_bn": row(sgu_norm_b),
        "sgu_w": sgu_w,
        "sgu_bias": jnp.repeat(jnp.transpose(sgu_b, (0, 2, 1)), SGU_HEAD_DIM, axis=2),
        "sgu_w8": jnp.repeat(jnp.transpose(sgu_w[:, :, :steps, :steps], (0, 2, 3, 1)), SGU_HEAD_DIM, axis=3),
        "sgu_b8": jnp.repeat(jnp.transpose(sgu_b[:, :, :steps], (0, 2, 1)), SGU_HEAD_DIM, axis=2),
        "w_out": w_out.astype(jnp.bfloat16),
        "g_mlp": row(norm_mlp_g),
        "w1": mlp_w1.astype(jnp.bfloat16),
        "w2": mlp_w2.astype(jnp.bfloat16),
    }


def kernel(x_prompt, x_sample, state_lru_h, state_conv, norm_mix_g, w_in, conv_w, conv_b, gate_r_w, gate_r_b, gate_i_w, gate_i_b, lru_lambda, sgu_norm_g, sgu_norm_b, sgu_w, sgu_b, w_out, norm_mlp_g, mlp_w1, mlp_w2, final_norm_g):
    nseq, seq, _ = x_prompt.shape
    nb, steps, _ = x_sample.shape
    assert seq % CHUNK == 0 and nseq == SUBLANES and steps <= CHUNK
    gf = final_norm_g.reshape(1, -1)
    p = _prepare_params(steps, nseq, norm_mix_g, w_in, conv_w, conv_b, gate_r_w, gate_r_b, gate_i_w, gate_i_b,
                        lru_lambda, sgu_norm_g, sgu_norm_b, sgu_w, sgu_b, w_out, norm_mlp_g, mlp_w1, mlp_w2)

    xp = x_prompt
    xs = jnp.transpose(x_sample, (1, 0, 2)).reshape(steps * nb, D_MODEL)
    cbuf_tm = jnp.transpose(state_conv, (0, 2, 1, 3)).reshape(DEPTH, (CONV_W - 1) * nb, D_LRU)
    hs_p, convs_p, hs_s, convs_s, vs_s = [], [], [], [], []
    for l in range(DEPTH):
        last = l == DEPTH - 1
        xp, h_p, conv_p = _layer_prompt(xp, p, gf, l, final_norm=last)
        hs_p.append(h_p)
        convs_p.append(conv_p)

        xs, v_s, h_s, conv_s = _mixer_sample(xs, state_lru_h, cbuf_tm, p, l, steps=steps, nb=nb)
        xs = _mlp(xs, p, gf, l, final_norm=last)
        hs_s.append(h_s)
        convs_s.append(conv_s.reshape(CONV_W - 1, nb, D_LRU))
        vs_s.append(v_s.reshape(steps, nb, D_SGU))

    y_sample = jnp.transpose(xs.reshape(steps, nb, D_MODEL), (1, 0, 2))
    return (xp, y_sample, jnp.stack(hs_p), jnp.stack(convs_p), jnp.stack(hs_s),
            jnp.transpose(jnp.stack(convs_s), (0, 2, 1, 3)), jnp.transpose(jnp.stack(vs_s), (0, 2, 1, 3)))
```

```python
import functools

import jax
import jax.numpy as jnp
from jax import lax
from jax.experimental import pallas as pl
from jax.experimental.pallas import tpu as pltpu

D_MODEL = 1024
DEPTH = 2
SGU_HEADS = 8
SGU_HEAD_DIM = 64
D_SGU = SGU_HEADS * SGU_HEAD_DIM
CHUNK = 128
D_LRU = 1024
LRU_BLOCKS = 16
LRU_BLOCK_DIM = 64
CONV_W = 4
LRU_C = 8.0
D_FF = 4 * D_MODEL
EPS = 1e-6

SUBLANES = 8
LANES = 128
GATE_TILE = 256
N_GATE_TILES = D_LRU // GATE_TILE
HEADS_PER_LANE_GROUP = LANES // SGU_HEAD_DIM
N_LANE_GROUPS = D_SGU // LANES
N_SLABS = D_LRU // LANES
FF_CHUNK = D_MODEL
N_FF_CHUNKS = D_FF // FF_CHUNK

HALF = CHUNK // 2
PITCH = HALF + SUBLANES
MLP_TILE = 1024
VMEM_LIMIT = 60 * 1024 * 1024
F32_TINY = 1.1754944e-38


def _rms_norm(x, g):
    return x * lax.rsqrt(jnp.mean(x * x, axis=-1, keepdims=True) + EPS) * g


def _layer_norm(x, g, b):
    mu = jnp.mean(x, axis=-1, keepdims=True)
    xc = x - mu
    return xc * lax.rsqrt(jnp.mean(xc * xc, axis=-1, keepdims=True) + EPS) * g + b


def _gelu(x):
    return jax.nn.gelu(x, approximate=True)


def _dot(a, b):
    return jnp.dot(a, b, preferred_element_type=jnp.float32)


def _lru_elementwise(r_pre, i_pre, xc, half_rb, half_ib, half_c):
    th_r = jnp.tanh(r_pre + half_rb)
    th_i = jnp.tanh(i_pre + half_ib)
    log_a = half_c * th_r + half_c
    i = 0.5 * th_i + 0.5
    a = jnp.exp(log_a)
    y = jnp.tanh(log_a) * (-1.0 - a * a)
    b = (y * lax.rsqrt(jnp.maximum(y, F32_TINY))) * (i * xc)
    return a, b


def _lru_consts(gate_rb, gate_ib, lam):
    return 0.5 * gate_rb, 0.5 * gate_ib, (0.5 * LRU_C) * jax.nn.log_sigmoid(lam)


def _build_gate_tiles(gate_rw_ref, gate_iw_ref, gate_sc):
    k_idx = lax.broadcasted_iota(jnp.int32, (LRU_BLOCK_DIM, GATE_TILE), 0)
    n_idx = lax.broadcasted_iota(jnp.int32, (LRU_BLOCK_DIM, GATE_TILE), 1)
    replicate = (n_idx % LRU_BLOCK_DIM == k_idx).astype(jnp.bfloat16)
    row_blk = lax.broadcasted_iota(jnp.int32, (GATE_TILE, GATE_TILE), 0) // LRU_BLOCK_DIM
    col_blk = lax.broadcasted_iota(jnp.int32, (GATE_TILE, GATE_TILE), 1) // LRU_BLOCK_DIM
    for j in range(N_GATE_TILES):
        for k, ref in enumerate((gate_rw_ref, gate_iw_ref)):
            rep = _dot(ref[j].astype(jnp.bfloat16), replicate)
            gate_sc[j, :, k * GATE_TILE:(k + 1) * GATE_TILE] = jnp.where(
                row_blk == col_blk, 0.5 * rep, 0.0).astype(jnp.bfloat16)


def _lru_coeffs(xc, gate_w_ref, gate_rb, gate_ib, lam):
    xcb = xc.astype(jnp.bfloat16)
    r_parts, i_parts = [], []
    for j in range(N_GATE_TILES):
        ri = _dot(xcb[:, j * GATE_TILE:(j + 1) * GATE_TILE], gate_w_ref[j])
        r_parts.append(ri[:, :GATE_TILE])
        i_parts.append(ri[:, GATE_TILE:])
    return _lru_elementwise(jnp.concatenate(r_parts, axis=1), jnp.concatenate(i_parts, axis=1), xc,
                            *_lru_consts(gate_rb, gate_ib, lam))


def _mlp_up(xn, w1_ref, c):
    cols = slice(c * FF_CHUNK, (c + 1) * FF_CHUNK)
    return jnp.square(jnp.maximum(_dot(xn, w1_ref[:, cols]), 0.0)).astype(jnp.bfloat16)


def _mlp_down(hid, w2_ref, c):
    return _dot(hid, w2_ref[c * FF_CHUNK:(c + 1) * FF_CHUNK, :])


def _layer_prompt_kernel(x_ref, g_mix_ref, w_in_ref, conv_wb_ref, gate_rw_ref, gate_iw_ref, gate_rb_ref, gate_ib_ref, lam_ref,
                         sgu_g_ref, sgu_bn_ref, sgu_w_ref, sgu_bias_ref, w_out_ref, g_mlp_ref, w1_ref, w2_ref, gf_ref,
                         y_ref, hlast_ref, convnew_ref, vprev_sc, sgu_lhs_sc, ymix_sc, gate_sc, *slab_scratch,
                         nseq, n_tiles, final_norm):
    rows = nseq * HALF
    j = pl.program_id(0)
    slab_sc = slab_scratch[0:N_SLABS]
    tail_sc = slab_scratch[N_SLABS:2 * N_SLABS]
    h_sc = slab_scratch[2 * N_SLABS:3 * N_SLABS]

    @pl.when(j == 0)
    def _():
        for c in range(N_SLABS):
            tail_sc[c][...] = jnp.zeros_like(tail_sc[c])
            h_sc[c][...] = jnp.zeros_like(h_sc[c])
        vprev_sc[...] = jnp.zeros_like(vprev_sc)
        ymix_sc[...] = jnp.zeros_like(ymix_sc)
        _build_gate_tiles(gate_rw_ref, gate_iw_ref, gate_sc)
        t_idx = lax.broadcasted_iota(jnp.int32, (CHUNK, CHUNK), 0)
        s_idx = lax.broadcasted_iota(jnp.int32, (CHUNK, CHUNK), 1)
        for g in range(N_LANE_GROUPS):
            first, second = [], []
            for hh in range(HEADS_PER_LANE_GROUP):
                wm = jnp.where(s_idx <= t_idx, sgu_w_ref[HEADS_PER_LANE_GROUP * g + hh], 0.0)
                first.append(pltpu.roll(wm[:HALF], HALF, axis=1))
                second.append(wm[HALF:])
            sgu_lhs_sc[0, g] = jnp.concatenate(first, axis=0).astype(jnp.bfloat16)
            sgu_lhs_sc[1, g] = jnp.concatenate(second, axis=0).astype(jnp.bfloat16)

    parity = j % 2
    half_rb, half_ib, half_c = _lru_consts(gate_rb_ref[...], gate_ib_ref[...], lam_ref[...])
    xb_col0 = 2 * D_SGU
    yb_col0 = 2 * D_SGU + D_LRU
    slabs_per_tile = GATE_TILE // LANES
    st = [{} for _ in range(N_GATE_TILES)]
    sg = {}
    ml = {}

    def mlp_norm():
        xm = ymix_sc[1 - parity]
        ml["acc"] = xm
        ml["xn"] = _rms_norm(xm, g_mlp_ref[...]).astype(jnp.bfloat16)

    def mlp_up(c):
        ml["hid", c] = _mlp_up(ml["xn"], w1_ref, c)

    def mlp_down(c):
        ml["acc"] = ml["acc"] + _mlp_down(ml.pop(("hid", c)), w2_ref, c)

    def mlp_store():
        out = _rms_norm(ml["acc"], gf_ref[...]) if final_norm else ml["acc"]
        y_ref[...] = out.reshape(nseq, HALF, D_MODEL)

    def mix_norm():
        sg["x"] = x_ref[...].reshape(rows, D_MODEL)
        sg["xn"] = _rms_norm(sg["x"], g_mix_ref[...]).astype(jnp.bfloat16)

    def lru_proj(q):
        xb = _dot(sg["xn"], w_in_ref[:, xb_col0 + q * GATE_TILE:xb_col0 + (q + 1) * GATE_TILE])
        for i in range(slabs_per_tile):
            for s in range(nseq):
                slab_sc[slabs_per_tile * q + i][s * PITCH:s * PITCH + HALF, :] = (
                    xb[s * HALF:(s + 1) * HALF, i * LANES:(i + 1) * LANES])

    def lru_conv(q):
        xc_slabs = []
        for c in range(slabs_per_tile * q, slabs_per_tile * (q + 1)):
            taps = [conv_wb_ref[k, c] for k in range(CONV_W + 1)]
            p1, p2, p3 = (tail_sc[c][k] for k in range(CONV_W - 1))
            steps_out = []
            for t in range(HALF):
                cur = slab_sc[c][pl.ds(t, nseq, stride=PITCH), :]
                steps_out.append(taps[CONV_W] + taps[3] * cur + taps[2] * p1 + taps[1] * p2 + taps[0] * p3)
                p1, p2, p3 = cur, p1, p2
            for k, pk in enumerate((p1, p2, p3)):
                tail_sc[c][k] = pk
            xc_slabs.append(jnp.concatenate(steps_out, axis=0))
        st[q]["xc"] = jnp.concatenate(xc_slabs, axis=1)

    def lru_gates(q):
        st[q]["ri"] = _dot(st[q]["xc"].astype(jnp.bfloat16), gate_sc[q])

    def lru_coeffs(q):
        cols = slice(q * GATE_TILE, (q + 1) * GATE_TILE)
        ri = st[q].pop("ri")
        st[q]["ab"] = _lru_elementwise(ri[:, :GATE_TILE], ri[:, GATE_TILE:], st[q].pop("xc"),
                                       half_rb[:, cols], half_ib[:, cols], half_c[:, cols])

    def lru_scan(q):
        a, b = st[q].pop("ab")
        for i in range(slabs_per_tile):
            c = slabs_per_tile * q + i
            h = h_sc[c][...]
            for t in range(HALF):
                h = (a[t * nseq:(t + 1) * nseq, i * LANES:(i + 1) * LANES] * h
                     + b[t * nseq:(t + 1) * nseq, i * LANES:(i + 1) * LANES])
                slab_sc[c][pl.ds(t, nseq, stride=PITCH), :] = h
            h_sc[c][...] = h

    def lru_out(q):
        hs = jnp.concatenate(
            [jnp.concatenate([slab_sc[c][s * PITCH:s * PITCH + HALF, :]
                              for c in range(slabs_per_tile * q, slabs_per_tile * (q + 1))], axis=1)
             for s in range(nseq)], axis=0)
        yb = _dot(sg["xn"], w_in_ref[:, yb_col0 + q * GATE_TILE:yb_col0 + (q + 1) * GATE_TILE])
        st[q]["out_b"] = (hs * _gelu(yb)).astype(jnp.bfloat16)

    def lru_mix(q):
        r0 = D_SGU + q * GATE_TILE
        sg["acc"] = sg["acc"] + _dot(st[q].pop("out_b"), w_out_ref[r0:r0 + GATE_TILE, :])

    def sgu_u():
        sg["u"] = _gelu(_dot(sg["xn"], w_in_ref[:, 0:D_SGU]))

    def sgu_v():
        v = _layer_norm(_gelu(_dot(sg["xn"], w_in_ref[:, D_SGU:2 * D_SGU])), sgu_g_ref[...], sgu_bn_ref[...])
        sg["vb"] = v.astype(jnp.bfloat16)

    def sgu_gate():
        vb = sg.pop("vb")
        lane = lax.broadcasted_iota(jnp.int32, (HALF, LANES), 1)
        bias = sgu_bias_ref[pl.ds(pl.multiple_of(parity * HALF, HALF), HALF), :]
        v_full = [jnp.concatenate([vprev_sc[1 - parity, s], vb[s * HALF:(s + 1) * HALF]], axis=0)
                  for s in range(nseq)]
        vprev_sc[parity] = vb.reshape(nseq, HALF, D_SGU)
        gate_rows = [[] for _ in range(nseq)]
        for s in range(0, nseq, 2):
            for g in range(N_LANE_GROUPS):
                lanes = slice(g * LANES, (g + 1) * LANES)
                rhs = jnp.concatenate([v_full[s][:, lanes], v_full[s + 1][:, lanes]], axis=1)
                res = _dot(sgu_lhs_sc[parity, g], rhs)
                for i in range(2):
                    part = res[:, i * LANES:(i + 1) * LANES]
                    gate_rows[s + i].append(jnp.where(lane < SGU_HEAD_DIM, part[:HALF], part[HALF:]))
        gate = jnp.concatenate([jnp.concatenate(r, axis=1) + bias for r in gate_rows], axis=0)
        sg["out_a"] = (sg.pop("u") * gate).astype(jnp.bfloat16)

    def sgu_mix():
        sg["acc"] = sg["x"] + _dot(sg.pop("out_a"), w_out_ref[0:D_SGU, :])

    def mix_store():
        ymix_sc[parity] = sg["acc"]

    tiles = range(N_GATE_TILES)
    mlp_norm()
    mix_norm()
    for q in tiles:
        lru_proj(q)
    mlp_up(0)
    for q in tiles:
        lru_conv(q)
        lru_gates(q)
    mlp_up(1)
    sgu_u()
    sgu_v()
    for q in tiles:
        lru_coeffs(q)
        lru_scan(q)
    mlp_down(0)
    mlp_down(1)
    for q in tiles:
        lru_out(q)
    mlp_up(2)
    mlp_up(3)
    sgu_gate()
    sgu_mix()
    mlp_down(2)
    mlp_down(3)
    for q in tiles:
        lru_mix(q)
    mix_store()
    mlp_store()

    @pl.when(j == n_tiles - 1)
    def _():
        for c in range(N_SLABS):
            cols = slice(c * LANES, (c + 1) * LANES)
            hlast_ref[:, cols] = h_sc[c][...]
            for k in range(CONV_W - 1):
                convnew_ref[CONV_W - 2 - k, :, cols] = tail_sc[c][k]


def _const_spec(shape):
    nd = len(shape)
    return pl.BlockSpec(shape, lambda *_: (0,) * nd, pipeline_mode=pl.Buffered(1))


def _layer_spec(arr, l):
    nd = arr.ndim - 1
    return pl.BlockSpec((None,) + arr.shape[1:], lambda *_: (l,) + (0,) * nd, pipeline_mode=pl.Buffered(1))


_PROMPT_PARAMS = ("g_mix", "w_in", "conv_wb", "gate_rw", "gate_iw", "gate_rb", "gate_ib", "lam", "sgu_g", "sgu_bn", "sgu_w",
                  "sgu_bias", "w_out", "g_mlp", "w1", "w2")
_SAMPLE_MIXER_PARAMS = ("g_mix", "w_in", "conv_w", "conv_b", "gate_rw", "gate_iw", "gate_rb", "gate_ib", "lam", "sgu_g",
                        "sgu_bn", "sgu_w8", "sgu_b8", "w_out")


def _layer_prompt(x, p, gf, l, *, final_norm):
    nseq, seq, _ = x.shape
    n_tiles = seq // HALF
    rows = nseq * HALF
    consts = [p[k] for k in _PROMPT_PARAMS]
    x_spec = pl.BlockSpec((nseq, HALF, D_MODEL), lambda j: (0, jnp.minimum(j, n_tiles - 1), 0))
    y_spec = pl.BlockSpec((nseq, HALF, D_MODEL), lambda j: (0, jnp.maximum(j - 1, 0), 0))
    y, hlast, convnew = pl.pallas_call(
        functools.partial(_layer_prompt_kernel, nseq=nseq, n_tiles=n_tiles, final_norm=final_norm),
        grid=(n_tiles + 1,),
        in_specs=[x_spec] + [_layer_spec(c, l) for c in consts] + [_const_spec(gf.shape)],
        out_specs=[y_spec, _const_spec((nseq, D_LRU)), _const_spec((CONV_W - 1, nseq, D_LRU))],
        out_shape=[jax.ShapeDtypeStruct(x.shape, jnp.float32),
                   jax.ShapeDtypeStruct((nseq, D_LRU), jnp.float32),
                   jax.ShapeDtypeStruct((CONV_W - 1, nseq, D_LRU), jnp.float32)],
        scratch_shapes=([pltpu.VMEM((2, nseq, HALF, D_SGU), jnp.bfloat16),
                         pltpu.VMEM((2, N_LANE_GROUPS, CHUNK, CHUNK), jnp.bfloat16),
                         pltpu.VMEM((2, rows, D_MODEL), jnp.float32),
                         pltpu.VMEM((N_GATE_TILES, GATE_TILE, 2 * GATE_TILE), jnp.bfloat16)]
                        + [pltpu.VMEM((nseq * PITCH, LANES), jnp.float32)] * N_SLABS
                        + [pltpu.VMEM((CONV_W - 1, nseq, LANES), jnp.float32)] * N_SLABS
                        + [pltpu.VMEM((nseq, LANES), jnp.float32)] * N_SLABS),
        compiler_params=pltpu.CompilerParams(dimension_semantics=("arbitrary",), vmem_limit_bytes=VMEM_LIMIT),
        name="layer_prompt",
    )(x, *consts, gf)
    return y, hlast, jnp.transpose(convnew, (1, 0, 2))


def _mixer_sample_kernel(*refs, steps, nb):
    x_refs, refs = refs[:steps], refs[steps:]
    cbuf_refs, refs = refs[:CONV_W - 1], refs[CONV_W - 1:]
    (h0_ref, g_mix_ref, w_in_ref, conv_w_ref, conv_b_ref, gate_rw_ref, gate_iw_ref, gate_rb_ref, gate_ib_ref, lam_ref,
     sgu_g_ref, sgu_bn_ref, sgu_w8_ref, sgu_b8_ref, w_out_ref, y_ref, v_ref, hlast_ref, convnew_ref, gate_sc) = refs
    _build_gate_tiles(gate_rw_ref, gate_iw_ref, gate_sc)
    x = jnp.concatenate([r[...] for r in x_refs], axis=0)
    xn = _rms_norm(x, g_mix_ref[...]).astype(jnp.bfloat16)

    u = _gelu(_dot(xn, w_in_ref[:, 0:D_SGU]))
    v = _layer_norm(_gelu(_dot(xn, w_in_ref[:, D_SGU:2 * D_SGU])), sgu_g_ref[...], sgu_bn_ref[...])
    v_ref[...] = v
    gate_rows = []
    for t in range(steps):
        acc = sgu_b8_ref[t:t + 1, :] + sgu_w8_ref[t, 0:1, :] * v[0:nb]
        for s in range(1, t + 1):
            acc = acc + sgu_w8_ref[t, s:s + 1, :] * v[s * nb:(s + 1) * nb]
        gate_rows.append(acc)
    out_a = (u * jnp.concatenate(gate_rows, axis=0)).astype(jnp.bfloat16)

    xb = _dot(xn, w_in_ref[:, 2 * D_SGU:2 * D_SGU + D_LRU])
    xp = jnp.concatenate([r[...] for r in cbuf_refs] + [xb], axis=0)
    xc = conv_b_ref[...] + conv_w_ref[0:1, :] * xp[0:steps * nb]
    for k in range(1, CONV_W):
        xc = xc + conv_w_ref[k:k + 1, :] * xp[k * nb:(k + steps) * nb]
    convnew_ref[...] = xp[steps * nb:]

    a, b = _lru_coeffs(xc, gate_sc, gate_rb_ref[...], gate_ib_ref[...], lam_ref[...])
    h = h0_ref[...]
    hs = []
    for t in range(steps):
        h = a[t * nb:(t + 1) * nb] * h + b[t * nb:(t + 1) * nb]
        hs.append(h)
    hlast_ref[...] = h

    yb = _dot(xn, w_in_ref[:, 2 * D_SGU + D_LRU:])
    out_b = (jnp.concatenate(hs, axis=0) * _gelu(yb)).astype(jnp.bfloat16)
    out = _dot(jnp.concatenate([out_a, out_b], axis=1), w_out_ref[...])
    y_ref[...] = x + out


def _mixer_sample(x, h0_all, cbuf_all, p, l):
    consts = [p[k] for k in _SAMPLE_MIXER_PARAMS]
    nb = h0_all.shape[1]
    steps = x.shape[0] // nb
    rows = steps * nb
    x_specs = [pl.BlockSpec((nb, D_MODEL), lambda i, t=t: (t, 0), pipeline_mode=pl.Buffered(1))
               for t in range(steps)]
    cbuf_specs = [pl.BlockSpec((None, nb, D_LRU), lambda i, k=k: (l, k, 0), pipeline_mode=pl.Buffered(1))
                  for k in range(CONV_W - 1)]
    out_shapes = [(rows, D_MODEL), (rows, D_SGU), (nb, D_LRU), ((CONV_W - 1) * nb, D_LRU)]
    return pl.pallas_call(
        functools.partial(_mixer_sample_kernel, steps=steps, nb=nb),
        grid=(1,),
        in_specs=x_specs + cbuf_specs + [_layer_spec(h0_all, l)] + [_layer_spec(c, l) for c in consts],
        out_specs=[_const_spec(s) for s in out_shapes],
        out_shape=[jax.ShapeDtypeStruct(s, jnp.float32) for s in out_shapes],
        scratch_shapes=[pltpu.VMEM((N_GATE_TILES, GATE_TILE, 2 * GATE_TILE), jnp.bfloat16)],
        compiler_params=pltpu.CompilerParams(dimension_semantics=("arbitrary",), vmem_limit_bytes=VMEM_LIMIT),
        name="mixer_sample",
    )(*([x] * steps), *([cbuf_all] * (CONV_W - 1)), h0_all, *consts)


def _mlp_kernel(x_ref, g_ref, w1_ref, w2_ref, gf_ref, y_ref, *, final_norm):
    x = x_ref[...]
    xn = _rms_norm(x, g_ref[...]).astype(jnp.bfloat16)
    acc = x
    for c in range(N_FF_CHUNKS):
        acc = acc + _mlp_down(_mlp_up(xn, w1_ref, c), w2_ref, c)
    y_ref[...] = _rms_norm(acc, gf_ref[...]) if final_norm else acc


def _mlp(x2d, p, gf, l, *, final_norm):
    rows = x2d.shape[0]
    tile = min(MLP_TILE, rows)
    row_spec = pl.BlockSpec((tile, D_MODEL), lambda i: (i, 0))
    return pl.pallas_call(
        functools.partial(_mlp_kernel, final_norm=final_norm),
        grid=(rows // tile,),
        in_specs=[row_spec, _layer_spec(p["g_mlp"], l), _layer_spec(p["w1"], l), _layer_spec(p["w2"], l),
                  _const_spec(gf.shape)],
        out_specs=row_spec,
        out_shape=jax.ShapeDtypeStruct(x2d.shape, jnp.float32),
        compiler_params=pltpu.CompilerParams(dimension_semantics=("arbitrary",), vmem_limit_bytes=VMEM_LIMIT),
        name="mlp",
    )(x2d, p["g_mlp"], p["w1"], p["w2"], gf)


def _prepare_params(steps, nseq, norm_mix_g, w_in, conv_w, conv_b, gate_r_w, gate_r_b, gate_i_w, gate_i_b,
                    lru_lambda, sgu_norm_g, sgu_norm_b, sgu_w, sgu_b, w_out, norm_mlp_g, mlp_w1, mlp_w2):
    row = lambda a: a.reshape(DEPTH, 1, -1)
    conv_wb = jnp.concatenate([conv_w, conv_b[:, None, :]], axis=1)
    return {
        "g_mix": row(norm_mix_g),
        "w_in": w_in.astype(jnp.bfloat16),
        "conv_w": conv_w,
        "conv_b": row(conv_b),
        "conv_wb": jnp.broadcast_to(conv_wb.reshape(DEPTH, CONV_W + 1, N_SLABS, 1, LANES),
                                    (DEPTH, CONV_W + 1, N_SLABS, nseq, LANES)),
        "gate_rw": gate_r_w.reshape(DEPTH, N_GATE_TILES, GATE_TILE, LRU_BLOCK_DIM),
        "gate_iw": gate_i_w.reshape(DEPTH, N_GATE_TILES, GATE_TILE, LRU_BLOCK_DIM),
        "gate_rb": row(gate_r_b),
        "gate_ib": row(gate_i_b),
        "lam": row(lru_lambda),
        "sgu_g": row(sgu_norm_g),
        "sgu_bn": row(sgu_norm_b),
        "sgu_w": sgu_w,
        "sgu_bias": jnp.repeat(jnp.transpose(sgu_b, (0, 2, 1)), SGU_HEAD_DIM, axis=2),
        "sgu_w8": jnp.repeat(jnp.transpose(sgu_w[:, :, :steps, :steps], (0, 2, 3, 1)), SGU_HEAD_DIM, axis=3),
        "sgu_b8": jnp.repeat(jnp.transpose(sgu_b[:, :, :steps], (0, 2, 1)), SGU_HEAD_DIM, axis=2),
        "w_out": w_out.astype(jnp.bfloat16),
        "g_mlp": row(norm_mlp_g),
        "w1": mlp_w1.astype(jnp.bfloat16),
        "w2": mlp_w2.astype(jnp.bfloat16),
    }


def kernel(x_prompt, x_sample, state_lru_h, state_conv, norm_mix_g, w_in, conv_w, conv_b, gate_r_w, gate_r_b, gate_i_w, gate_i_b, lru_lambda, sgu_norm_g, sgu_norm_b, sgu_w, sgu_b, w_out, norm_mlp_g, mlp_w1, mlp_w2, final_norm_g):
    nseq, seq, _ = x_prompt.shape
    nb, steps, _ = x_sample.shape
    assert seq % CHUNK == 0 and nseq == SUBLANES and steps <= CHUNK
    gf = final_norm_g.reshape(1, -1)
    p = _prepare_params(steps, nseq, norm_mix_g, w_in, conv_w, conv_b, gate_r_w, gate_r_b, gate_i_w, gate_i_b,
                        lru_lambda, sgu_norm_g, sgu_norm_b, sgu_w, sgu_b, w_out, norm_mlp_g, mlp_w1, mlp_w2)

    xp = x_prompt
    xs = jnp.transpose(x_sample, (1, 0, 2)).reshape(steps * nb, D_MODEL)
    cbuf_tm = jnp.transpose(state_conv, (0, 2, 1, 3)).reshape(DEPTH, (CONV_W - 1) * nb, D_LRU)
    hs_p, convs_p, hs_s, convs_s, vs_s = [], [], [], [], []
    for l in range(DEPTH):
        last = l == DEPTH - 1
        xp, h_p, conv_p = _layer_prompt(xp, p, gf, l, final_norm=last)
        hs_p.append(h_p)
        convs_p.append(conv_p)

        xs, v_s, h_s, conv_s = _mixer_sample(xs, state_lru_h, cbuf_tm, p, l)
        xs = _mlp(xs, p, gf, l, final_norm=last)
        hs_s.append(h_s)
        convs_s.append(conv_s.reshape(CONV_W - 1, nb, D_LRU))
        vs_s.append(v_s.reshape(steps, nb, D_SGU))

    y_sample = jnp.transpose(xs.reshape(steps, nb, D_MODEL), (1, 0, 2))
    return (xp, y_sample, jnp.stack(hs_p), jnp.stack(convs_p), jnp.stack(hs_s),
            jnp.transpose(jnp.stack(convs_s), (0, 2, 1, 3)), jnp.transpose(jnp.stack(vs_s), (0, 2, 1, 3)))
```

```python
import functools

import jax
import jax.numpy as jnp
from jax import lax
from jax.experimental import pallas as pl
from jax.experimental.pallas import tpu as pltpu

D_MODEL = 1024
DEPTH = 2
SGU_HEADS = 8
SGU_HEAD_DIM = 64
D_SGU = SGU_HEADS * SGU_HEAD_DIM
CHUNK = 128
D_LRU = 1024
LRU_BLOCKS = 16
LRU_BLOCK_DIM = 64
CONV_W = 4
LRU_C = 8.0
D_FF = 4 * D_MODEL
EPS = 1e-6

SUBLANES = 8
LANES = 128
GATE_TILE = 256
N_GATE_TILES = D_LRU // GATE_TILE
HEADS_PER_LANE_GROUP = LANES // SGU_HEAD_DIM
N_LANE_GROUPS = D_SGU // LANES
N_SLABS = D_LRU // LANES
FF_CHUNK = D_MODEL
N_FF_CHUNKS = D_FF // FF_CHUNK

HALF = CHUNK // 2
PITCH = HALF + SUBLANES
VMEM_LIMIT = 60 * 1024 * 1024
F32_TINY = 1.1754944e-38


def _rms_norm(x, g):
    return x * lax.rsqrt(jnp.mean(x * x, axis=-1, keepdims=True) + EPS) * g


def _layer_norm(x, g, b):
    mu = jnp.mean(x, axis=-1, keepdims=True)
    xc = x - mu
    return xc * lax.rsqrt(jnp.mean(xc * xc, axis=-1, keepdims=True) + EPS) * g + b


def _gelu(x):
    return jax.nn.gelu(x, approximate=True)


def _dot(a, b):
    return jnp.dot(a, b, preferred_element_type=jnp.float32)


def _lru_elementwise(r_pre, i_pre, xc, half_rb, half_ib, half_c):
    th_r = jnp.tanh(r_pre + half_rb)
    th_i = jnp.tanh(i_pre + half_ib)
    log_a = half_c * th_r + half_c
    i = 0.5 * th_i + 0.5
    a = jnp.exp(log_a)
    y = jnp.tanh(log_a) * (-1.0 - a * a)
    b = (y * lax.rsqrt(jnp.maximum(y, F32_TINY))) * (i * xc)
    return a, b


def _lru_consts(gate_rb, gate_ib, lam):
    return 0.5 * gate_rb, 0.5 * gate_ib, (0.5 * LRU_C) * jax.nn.log_sigmoid(lam)


def _build_gate_tiles(gate_rw_ref, gate_iw_ref, gate_sc):
    k_idx = lax.broadcasted_iota(jnp.int32, (LRU_BLOCK_DIM, GATE_TILE), 0)
    n_idx = lax.broadcasted_iota(jnp.int32, (LRU_BLOCK_DIM, GATE_TILE), 1)
    replicate = (n_idx % LRU_BLOCK_DIM == k_idx).astype(jnp.bfloat16)
    row_blk = lax.broadcasted_iota(jnp.int32, (GATE_TILE, GATE_TILE), 0) // LRU_BLOCK_DIM
    col_blk = lax.broadcasted_iota(jnp.int32, (GATE_TILE, GATE_TILE), 1) // LRU_BLOCK_DIM
    for j in range(N_GATE_TILES):
        for k, ref in enumerate((gate_rw_ref, gate_iw_ref)):
            rep = _dot(ref[j].astype(jnp.bfloat16), replicate)
            gate_sc[j, :, k * GATE_TILE:(k + 1) * GATE_TILE] = jnp.where(
                row_blk == col_blk, 0.5 * rep, 0.0).astype(jnp.bfloat16)


def _lru_coeffs(xc, gate_w_ref, gate_rb, gate_ib, lam):
    xcb = xc.astype(jnp.bfloat16)
    r_parts, i_parts = [], []
    for j in range(N_GATE_TILES):
        ri = _dot(xcb[:, j * GATE_TILE:(j + 1) * GATE_TILE], gate_w_ref[j])
        r_parts.append(ri[:, :GATE_TILE])
        i_parts.append(ri[:, GATE_TILE:])
    return _lru_elementwise(jnp.concatenate(r_parts, axis=1), jnp.concatenate(i_parts, axis=1), xc,
                            *_lru_consts(gate_rb, gate_ib, lam))


def _mlp_up(xn, w1_ref, c):
    cols = slice(c * FF_CHUNK, (c + 1) * FF_CHUNK)
    return jnp.square(jnp.maximum(_dot(xn, w1_ref[:, cols]), 0.0)).astype(jnp.bfloat16)


def _mlp_down(hid, w2_ref, c):
    return _dot(hid, w2_ref[c * FF_CHUNK:(c + 1) * FF_CHUNK, :])


def _layer_prompt_kernel(x_ref, xs_ref, g_mix_ref, w_in_ref, conv_wb_ref, gate_rw_ref, gate_iw_ref, gate_rb_ref, gate_ib_ref, lam_ref,
                         sgu_g_ref, sgu_bn_ref, sgu_w_ref, sgu_bias_ref, w_out_ref, g_mlp_ref, w1_ref, w2_ref, gf_ref,
                         y_ref, ys_ref, hlast_ref, convnew_ref, vprev_sc, sgu_lhs_sc, ymix_sc, gate_sc, *slab_scratch,
                         nseq, n_tiles, final_norm):
    rows = nseq * HALF
    j = pl.program_id(0)
    slab_sc = slab_scratch[0:N_SLABS]
    tail_sc = slab_scratch[N_SLABS:2 * N_SLABS]
    h_sc = slab_scratch[2 * N_SLABS:3 * N_SLABS]

    @pl.when(j == 0)
    def _():
        for c in range(N_SLABS):
            tail_sc[c][...] = jnp.zeros_like(tail_sc[c])
            h_sc[c][...] = jnp.zeros_like(h_sc[c])
        vprev_sc[...] = jnp.zeros_like(vprev_sc)
        ymix_sc[1] = xs_ref[...]
        _build_gate_tiles(gate_rw_ref, gate_iw_ref, gate_sc)
        t_idx = lax.broadcasted_iota(jnp.int32, (CHUNK, CHUNK), 0)
        s_idx = lax.broadcasted_iota(jnp.int32, (CHUNK, CHUNK), 1)
        for g in range(N_LANE_GROUPS):
            first, second = [], []
            for hh in range(HEADS_PER_LANE_GROUP):
                wm = jnp.where(s_idx <= t_idx, sgu_w_ref[HEADS_PER_LANE_GROUP * g + hh], 0.0)
                first.append(pltpu.roll(wm[:HALF], HALF, axis=1))
                second.append(wm[HALF:])
            sgu_lhs_sc[0, g] = jnp.concatenate(first, axis=0).astype(jnp.bfloat16)
            sgu_lhs_sc[1, g] = jnp.concatenate(second, axis=0).astype(jnp.bfloat16)

    parity = j % 2

    @pl.when(j == n_tiles + 1)
    def _():
        ymix_sc[1 - parity] = xs_ref[...]

    def mlp_only(store):
        xm = ymix_sc[1 - parity]
        xn = _rms_norm(xm, g_mlp_ref[...]).astype(jnp.bfloat16)
        acc = xm
        for c in range(N_FF_CHUNKS):
            acc = acc + _mlp_down(_mlp_up(xn, w1_ref, c), w2_ref, c)
        store(_rms_norm(acc, gf_ref[...]) if final_norm else acc)

    @pl.when(j == n_tiles)
    def _():
        def store(out):
            y_ref[...] = out.reshape(nseq, HALF, D_MODEL)
        mlp_only(store)

    @pl.when(j == n_tiles + 1)
    def _():
        def store(out):
            ys_ref[...] = out
        mlp_only(store)

    @pl.when(j < n_tiles)
    def _():
        _fused_step(x_ref, g_mix_ref, w_in_ref, conv_wb_ref, gate_rb_ref, gate_ib_ref, lam_ref, sgu_g_ref, sgu_bn_ref,
                    sgu_bias_ref, w_out_ref, g_mlp_ref, w1_ref, w2_ref, gf_ref, y_ref, vprev_sc, sgu_lhs_sc, ymix_sc,
                    gate_sc, slab_sc, tail_sc, h_sc, parity=parity, nseq=nseq, final_norm=final_norm)

    @pl.when(j == 0)
    def _():
        ys_ref[...] = y_ref[...].reshape(rows, D_MODEL)

    @pl.when(j == n_tiles - 1)
    def _():
        for c in range(N_SLABS):
            cols = slice(c * LANES, (c + 1) * LANES)
            hlast_ref[:, cols] = h_sc[c][...]
            for k in range(CONV_W - 1):
                convnew_ref[CONV_W - 2 - k, :, cols] = tail_sc[c][k]


def _fused_step(x_ref, g_mix_ref, w_in_ref, conv_wb_ref, gate_rb_ref, gate_ib_ref, lam_ref, sgu_g_ref, sgu_bn_ref,
                sgu_bias_ref, w_out_ref, g_mlp_ref, w1_ref, w2_ref, gf_ref, y_ref, vprev_sc, sgu_lhs_sc, ymix_sc,
                gate_sc, slab_sc, tail_sc, h_sc, *, parity, nseq, final_norm):
    rows = nseq * HALF
    half_rb, half_ib, half_c = _lru_consts(gate_rb_ref[...], gate_ib_ref[...], lam_ref[...])
    xb_col0 = 2 * D_SGU
    yb_col0 = 2 * D_SGU + D_LRU
    slabs_per_tile = GATE_TILE // LANES
    st = [{} for _ in range(N_GATE_TILES)]
    sg = {}
    ml = {}

    def mlp_norm():
        xm = ymix_sc[1 - parity]
        ml["acc"] = xm
        ml["xn"] = _rms_norm(xm, g_mlp_ref[...]).astype(jnp.bfloat16)

    def mlp_up(c):
        ml["hid", c] = _mlp_up(ml["xn"], w1_ref, c)

    def mlp_down(c):
        ml["acc"] = ml["acc"] + _mlp_down(ml.pop(("hid", c)), w2_ref, c)

    def mlp_store():
        out = _rms_norm(ml["acc"], gf_ref[...]) if final_norm else ml["acc"]
        y_ref[...] = out.reshape(nseq, HALF, D_MODEL)

    def mix_norm():
        sg["x"] = x_ref[...].reshape(rows, D_MODEL)
        sg["xn"] = _rms_norm(sg["x"], g_mix_ref[...]).astype(jnp.bfloat16)

    def lru_proj(q):
        xb = _dot(sg["xn"], w_in_ref[:, xb_col0 + q * GATE_TILE:xb_col0 + (q + 1) * GATE_TILE])
        for i in range(slabs_per_tile):
            for s in range(nseq):
                slab_sc[slabs_per_tile * q + i][s * PITCH:s * PITCH + HALF, :] = (
                    xb[s * HALF:(s + 1) * HALF, i * LANES:(i + 1) * LANES])

    def lru_conv(q):
        xc_slabs = []
        for c in range(slabs_per_tile * q, slabs_per_tile * (q + 1)):
            taps = [conv_wb_ref[k, c] for k in range(CONV_W + 1)]
            p1, p2, p3 = (tail_sc[c][k] for k in range(CONV_W - 1))
            steps_out = []
            for t in range(HALF):
                cur = slab_sc[c][pl.ds(t, nseq, stride=PITCH), :]
                steps_out.append(taps[CONV_W] + taps[3] * cur + taps[2] * p1 + taps[1] * p2 + taps[0] * p3)
                p1, p2, p3 = cur, p1, p2
            for k, pk in enumerate((p1, p2, p3)):
                tail_sc[c][k] = pk
            xc_slabs.append(jnp.concatenate(steps_out, axis=0))
        st[q]["xc"] = jnp.concatenate(xc_slabs, axis=1)

    def lru_gates(q):
        st[q]["ri"] = _dot(st[q]["xc"].astype(jnp.bfloat16), gate_sc[q])

    def lru_coeffs(q):
        cols = slice(q * GATE_TILE, (q + 1) * GATE_TILE)
        ri = st[q].pop("ri")
        st[q]["ab"] = _lru_elementwise(ri[:, :GATE_TILE], ri[:, GATE_TILE:], st[q].pop("xc"),
                                       half_rb[:, cols], half_ib[:, cols], half_c[:, cols])

    def lru_scan(q):
        a, b = st[q].pop("ab")
        for i in range(slabs_per_tile):
            c = slabs_per_tile * q + i
            h = h_sc[c][...]
            for t in range(HALF):
                h = (a[t * nseq:(t + 1) * nseq, i * LANES:(i + 1) * LANES] * h
                     + b[t * nseq:(t + 1) * nseq, i * LANES:(i + 1) * LANES])
                slab_sc[c][pl.ds(t, nseq, stride=PITCH), :] = h
            h_sc[c][...] = h

    def lru_out(q):
        hs = jnp.concatenate(
            [jnp.concatenate([slab_sc[c][s * PITCH:s * PITCH + HALF, :]
                              for c in range(slabs_per_tile * q, slabs_per_tile * (q + 1))], axis=1)
             for s in range(nseq)], axis=0)
        yb = _dot(sg["xn"], w_in_ref[:, yb_col0 + q * GATE_TILE:yb_col0 + (q + 1) * GATE_TILE])
        st[q]["out_b"] = (hs * _gelu(yb)).astype(jnp.bfloat16)

    def lru_mix(q):
        r0 = D_SGU + q * GATE_TILE
        sg["acc"] = sg["acc"] + _dot(st[q].pop("out_b"), w_out_ref[r0:r0 + GATE_TILE, :])

    def sgu_u():
        sg["u"] = _gelu(_dot(sg["xn"], w_in_ref[:, 0:D_SGU]))

    def sgu_v():
        v = _layer_norm(_gelu(_dot(sg["xn"], w_in_ref[:, D_SGU:2 * D_SGU])), sgu_g_ref[...], sgu_bn_ref[...])
        sg["vb"] = v.astype(jnp.bfloat16)

    def sgu_gate():
        vb = sg.pop("vb")
        lane = lax.broadcasted_iota(jnp.int32, (HALF, LANES), 1)
        bias = sgu_bias_ref[pl.ds(pl.multiple_of(parity * HALF, HALF), HALF), :]
        v_full = [jnp.concatenate([vprev_sc[1 - parity, s], vb[s * HALF:(s + 1) * HALF]], axis=0)
                  for s in range(nseq)]
        vprev_sc[parity] = vb.reshape(nseq, HALF, D_SGU)
        gate_rows = [[] for _ in range(nseq)]
        for s in range(0, nseq, 2):
            for g in range(N_LANE_GROUPS):
                lanes = slice(g * LANES, (g + 1) * LANES)
                rhs = jnp.concatenate([v_full[s][:, lanes], v_full[s + 1][:, lanes]], axis=1)
                res = _dot(sgu_lhs_sc[parity, g], rhs)
                for i in range(2):
                    part = res[:, i * LANES:(i + 1) * LANES]
                    gate_rows[s + i].append(jnp.where(lane < SGU_HEAD_DIM, part[:HALF], part[HALF:]))
        gate = jnp.concatenate([jnp.concatenate(r, axis=1) + bias for r in gate_rows], axis=0)
        sg["out_a"] = (sg.pop("u") * gate).astype(jnp.bfloat16)

    def sgu_mix():
        sg["acc"] = sg["x"] + _dot(sg.pop("out_a"), w_out_ref[0:D_SGU, :])

    def mix_store():
        ymix_sc[parity] = sg["acc"]

    tiles = range(N_GATE_TILES)
    mlp_norm()
    mix_norm()
    for q in tiles:
        lru_proj(q)
    mlp_up(0)
    for q in tiles:
        lru_conv(q)
        lru_gates(q)
    mlp_up(1)
    sgu_u()
    sgu_v()
    for q in tiles:
        lru_coeffs(q)
        lru_scan(q)
    mlp_down(0)
    mlp_down(1)
    for q in tiles:
        lru_out(q)
    mlp_up(2)
    mlp_up(3)
    sgu_gate()
    sgu_mix()
    mlp_down(2)
    mlp_down(3)
    for q in tiles:
        lru_mix(q)
    mix_store()
    mlp_store()


def _const_spec(shape):
    nd = len(shape)
    return pl.BlockSpec(shape, lambda *_: (0,) * nd, pipeline_mode=pl.Buffered(1))


def _layer_spec(arr, l):
    nd = arr.ndim - 1
    return pl.BlockSpec((None,) + arr.shape[1:], lambda *_: (l,) + (0,) * nd, pipeline_mode=pl.Buffered(1))


_PROMPT_PARAMS = ("g_mix", "w_in", "conv_wb", "gate_rw", "gate_iw", "gate_rb", "gate_ib", "lam", "sgu_g", "sgu_bn", "sgu_w",
                  "sgu_bias", "w_out", "g_mlp", "w1", "w2")
_SAMPLE_MIXER_PARAMS = ("g_mix", "w_in", "conv_w", "conv_b", "gate_rw", "gate_iw", "gate_rb", "gate_ib", "lam", "sgu_g",
                        "sgu_bn", "sgu_w8", "sgu_b8", "w_out")


def _layer_prompt(x, xs, p, gf, l, *, final_norm):
    nseq, seq, _ = x.shape
    n_tiles = seq // HALF
    rows = nseq * HALF
    assert xs.shape == (2 * rows, D_MODEL)
    consts = [p[k] for k in _PROMPT_PARAMS]
    x_spec = pl.BlockSpec((nseq, HALF, D_MODEL), lambda j: (0, jnp.minimum(j, n_tiles - 1), 0))
    y_spec = pl.BlockSpec((nseq, HALF, D_MODEL), lambda j: (0, jnp.clip(j - 1, 0, n_tiles - 1), 0))
    xs_spec = pl.BlockSpec((rows, D_MODEL), lambda j: (jnp.where(j > n_tiles, 1, 0), 0),
                           pipeline_mode=pl.Buffered(1))
    y, ys, hlast, convnew = pl.pallas_call(
        functools.partial(_layer_prompt_kernel, nseq=nseq, n_tiles=n_tiles, final_norm=final_norm),
        grid=(n_tiles + 2,),
        in_specs=[x_spec, xs_spec] + [_layer_spec(c, l) for c in consts] + [_const_spec(gf.shape)],
        out_specs=[y_spec, xs_spec, _const_spec((nseq, D_LRU)), _const_spec((CONV_W - 1, nseq, D_LRU))],
        out_shape=[jax.ShapeDtypeStruct(x.shape, jnp.float32),
                   jax.ShapeDtypeStruct(xs.shape, jnp.float32),
                   jax.ShapeDtypeStruct((nseq, D_LRU), jnp.float32),
                   jax.ShapeDtypeStruct((CONV_W - 1, nseq, D_LRU), jnp.float32)],
        scratch_shapes=([pltpu.VMEM((2, nseq, HALF, D_SGU), jnp.bfloat16),
                         pltpu.VMEM((2, N_LANE_GROUPS, CHUNK, CHUNK), jnp.bfloat16),
                         pltpu.VMEM((2, rows, D_MODEL), jnp.float32),
                         pltpu.VMEM((N_GATE_TILES, GATE_TILE, 2 * GATE_TILE), jnp.bfloat16)]
                        + [pltpu.VMEM((nseq * PITCH, LANES), jnp.float32)] * N_SLABS
                        + [pltpu.VMEM((CONV_W - 1, nseq, LANES), jnp.float32)] * N_SLABS
                        + [pltpu.VMEM((nseq, LANES), jnp.float32)] * N_SLABS),
        compiler_params=pltpu.CompilerParams(dimension_semantics=("arbitrary",), vmem_limit_bytes=VMEM_LIMIT),
        name="layer_prompt",
    )(x, xs, *consts, gf)
    return y, ys, hlast, jnp.transpose(convnew, (1, 0, 2))


def _mixer_sample_kernel(*refs, steps, nb):
    x_refs, refs = refs[:steps], refs[steps:]
    cbuf_refs, refs = refs[:CONV_W - 1], refs[CONV_W - 1:]
    (h0_ref, g_mix_ref, w_in_ref, conv_w_ref, conv_b_ref, gate_rw_ref, gate_iw_ref, gate_rb_ref, gate_ib_ref, lam_ref,
     sgu_g_ref, sgu_bn_ref, sgu_w8_ref, sgu_b8_ref, w_out_ref, y_ref, v_ref, hlast_ref, convnew_ref, gate_sc) = refs
    _build_gate_tiles(gate_rw_ref, gate_iw_ref, gate_sc)
    x = jnp.concatenate([r[...] for r in x_refs], axis=0)
    xn = _rms_norm(x, g_mix_ref[...]).astype(jnp.bfloat16)

    u = _gelu(_dot(xn, w_in_ref[:, 0:D_SGU]))
    v = _layer_norm(_gelu(_dot(xn, w_in_ref[:, D_SGU:2 * D_SGU])), sgu_g_ref[...], sgu_bn_ref[...])
    v_ref[...] = v
    gate_rows = []
    for t in range(steps):
        acc = sgu_b8_ref[t:t + 1, :] + sgu_w8_ref[t, 0:1, :] * v[0:nb]
        for s in range(1, t + 1):
            acc = acc + sgu_w8_ref[t, s:s + 1, :] * v[s * nb:(s + 1) * nb]
        gate_rows.append(acc)
    out_a = (u * jnp.concatenate(gate_rows, axis=0)).astype(jnp.bfloat16)

    xb = _dot(xn, w_in_ref[:, 2 * D_SGU:2 * D_SGU + D_LRU])
    xp = jnp.concatenate([r[...] for r in cbuf_refs] + [xb], axis=0)
    xc = conv_b_ref[...] + conv_w_ref[0:1, :] * xp[0:steps * nb]
    for k in range(1, CONV_W):
        xc = xc + conv_w_ref[k:k + 1, :] * xp[k * nb:(k + steps) * nb]
    convnew_ref[...] = xp[steps * nb:]

    a, b = _lru_coeffs(xc, gate_sc, gate_rb_ref[...], gate_ib_ref[...], lam_ref[...])
    h = h0_ref[...]
    hs = []
    for t in range(steps):
        h = a[t * nb:(t + 1) * nb] * h + b[t * nb:(t + 1) * nb]
        hs.append(h)
    hlast_ref[...] = h

    yb = _dot(xn, w_in_ref[:, 2 * D_SGU + D_LRU:])
    out_b = (jnp.concatenate(hs, axis=0) * _gelu(yb)).astype(jnp.bfloat16)
    out = _dot(jnp.concatenate([out_a, out_b], axis=1), w_out_ref[...])
    y_ref[...] = x + out


def _mixer_sample(x, h0_all, cbuf_all, p, l):
    consts = [p[k] for k in _SAMPLE_MIXER_PARAMS]
    nb = h0_all.shape[1]
    steps = x.shape[0] // nb
    rows = steps * nb
    x_specs = [pl.BlockSpec((nb, D_MODEL), lambda i, t=t: (t, 0), pipeline_mode=pl.Buffered(1))
               for t in range(steps)]
    cbuf_specs = [pl.BlockSpec((None, nb, D_LRU), lambda i, k=k: (l, k, 0), pipeline_mode=pl.Buffered(1))
                  for k in range(CONV_W - 1)]
    out_shapes = [(rows, D_MODEL), (rows, D_SGU), (nb, D_LRU), ((CONV_W - 1) * nb, D_LRU)]
    return pl.pallas_call(
        functools.partial(_mixer_sample_kernel, steps=steps, nb=nb),
        grid=(1,),
        in_specs=x_specs + cbuf_specs + [_layer_spec(h0_all, l)] + [_layer_spec(c, l) for c in consts],
        out_specs=[_const_spec(s) for s in out_shapes],
        out_shape=[jax.ShapeDtypeStruct(s, jnp.float32) for s in out_shapes],
        scratch_shapes=[pltpu.VMEM((N_GATE_TILES, GATE_TILE, 2 * GATE_TILE), jnp.bfloat16)],
        compiler_params=pltpu.CompilerParams(dimension_semantics=("arbitrary",), vmem_limit_bytes=VMEM_LIMIT),
        name="mixer_sample",
    )(*([x] * steps), *([cbuf_all] * (CONV_W - 1)), h0_all, *consts)


def _prepare_params(steps, nseq, norm_mix_g, w_in, conv_w, conv_b, gate_r_w, gate_r_b, gate_i_w, gate_i_b,
                    lru_lambda, sgu_norm_g, sgu_norm_b, sgu_w, sgu_b, w_out, norm_mlp_g, mlp_w1, mlp_w2):
    row = lambda a: a.reshape(DEPTH, 1, -1)
    conv_wb = jnp.concatenate([conv_w, conv_b[:, None, :]], axis=1)
    return {
        "g_mix": row(norm_mix_g),
        "w_in": w_in.astype(jnp.bfloat16),
        "conv_w": conv_w,
        "conv_b": row(conv_b),
        "conv_wb": jnp.broadcast_to(conv_wb.reshape(DEPTH, CONV_W + 1, N_SLABS, 1, LANES),
                                    (DEPTH, CONV_W + 1, N_SLABS, nseq, LANES)),
        "gate_rw": gate_r_w.reshape(DEPTH, N_GATE_TILES, GATE_TILE, LRU_BLOCK_DIM),
        "gate_iw": gate_i_w.reshape(DEPTH, N_GATE_TILES, GATE_TILE, LRU_BLOCK_DIM),
        "gate_rb": row(gate_r_b),
        "gate_ib": row(gate_i_b),
        "lam": row(lru_lambda),
        "sgu_g": row(sgu_norm_g),
        "sgu_bn": row(sgu_norm_b),
        "sgu_w": sgu_w,
        "sgu_bias": jnp.repeat(jnp.transpose(sgu_b, (0, 2, 1)), SGU_HEAD_DIM, axis=2),
        "sgu_w8": jnp.repeat(jnp.transpose(sgu_w[:, :, :steps, :steps], (0, 2, 3, 1)), SGU_HEAD_DIM, axis=3),
        "sgu_b8": jnp.repeat(jnp.transpose(sgu_b[:, :, :steps], (0, 2, 1)), SGU_HEAD_DIM, axis=2),
        "w_out": w_out.astype(jnp.bfloat16),
        "g_mlp": row(norm_mlp_g),
        "w1": mlp_w1.astype(jnp.bfloat16),
        "w2": mlp_w2.astype(jnp.bfloat16),
    }


def kernel(x_prompt, x_sample, state_lru_h, state_conv, norm_mix_g, w_in, conv_w, conv_b, gate_r_w, gate_r_b, gate_i_w, gate_i_b, lru_lambda, sgu_norm_g, sgu_norm_b, sgu_w, sgu_b, w_out, norm_mlp_g, mlp_w1, mlp_w2, final_norm_g):
    nseq, seq, _ = x_prompt.shape
    nb, steps, _ = x_sample.shape
    assert seq % CHUNK == 0 and nseq == SUBLANES and steps <= CHUNK
    gf = final_norm_g.reshape(1, -1)
    p = _prepare_params(steps, nseq, norm_mix_g, w_in, conv_w, conv_b, gate_r_w, gate_r_b, gate_i_w, gate_i_b,
                        lru_lambda, sgu_norm_g, sgu_norm_b, sgu_w, sgu_b, w_out, norm_mlp_g, mlp_w1, mlp_w2)

    xp = x_prompt
    xs = jnp.transpose(x_sample, (1, 0, 2)).reshape(steps * nb, D_MODEL)
    cbuf_tm = jnp.transpose(state_conv, (0, 2, 1, 3)).reshape(DEPTH, (CONV_W - 1) * nb, D_LRU)
    hs_p, convs_p, hs_s, convs_s, vs_s = [], [], [], [], []
    for l in range(DEPTH):
        last = l == DEPTH - 1
        xs, v_s, h_s, conv_s = _mixer_sample(xs, state_lru_h, cbuf_tm, p, l)
        xp, xs, h_p, conv_p = _layer_prompt(xp, xs, p, gf, l, final_norm=last)
        hs_p.append(h_p)
        convs_p.append(conv_p)
        hs_s.append(h_s)
        convs_s.append(conv_s.reshape(CONV_W - 1, nb, D_LRU))
        vs_s.append(v_s.reshape(steps, nb, D_SGU))

    y_sample = jnp.transpose(xs.reshape(steps, nb, D_MODEL), (1, 0, 2))
    return (xp, y_sample, jnp.stack(hs_p), jnp.stack(convs_p), jnp.stack(hs_s),
            jnp.transpose(jnp.stack(convs_s), (0, 2, 1, 3)), jnp.transpose(jnp.stack(vs_s), (0, 2, 1, 3)))
```

```python
import functools

import jax
import jax.numpy as jnp
from jax import lax
from jax.experimental import pallas as pl
from jax.experimental.pallas import tpu as pltpu

D_MODEL = 1024
DEPTH = 2
SGU_HEADS = 8
SGU_HEAD_DIM = 64
D_SGU = SGU_HEADS * SGU_HEAD_DIM
CHUNK = 128
D_LRU = 1024
LRU_BLOCKS = 16
LRU_BLOCK_DIM = 64
CONV_W = 4
LRU_C = 8.0
D_FF = 4 * D_MODEL
EPS = 1e-6

SUBLANES = 8
LANES = 128
GATE_TILE = 256
N_GATE_TILES = D_LRU // GATE_TILE
HEADS_PER_LANE_GROUP = LANES // SGU_HEAD_DIM
N_LANE_GROUPS = D_SGU // LANES
N_SLABS = D_LRU // LANES
FF_CHUNK = D_MODEL
N_FF_CHUNKS = D_FF // FF_CHUNK

HALF = CHUNK // 2
PITCH = HALF + SUBLANES
VMEM_LIMIT = 60 * 1024 * 1024
F32_TINY = 1.1754944e-38


def _rms_norm(x, g):
    return x * lax.rsqrt(jnp.mean(x * x, axis=-1, keepdims=True) + EPS) * g


def _layer_norm(x, g, b):
    mu = jnp.mean(x, axis=-1, keepdims=True)
    xc = x - mu
    return xc * lax.rsqrt(jnp.mean(xc * xc, axis=-1, keepdims=True) + EPS) * g + b


def _gelu(x):
    return jax.nn.gelu(x, approximate=True)


def _dot(a, b):
    return jnp.dot(a, b, preferred_element_type=jnp.float32)


def _lru_elementwise(r_pre, i_pre, xc, half_rb, half_ib, half_c):
    th_r = jnp.tanh(r_pre + half_rb)
    th_i = jnp.tanh(i_pre + half_ib)
    log_a = half_c * th_r + half_c
    i = 0.5 * th_i + 0.5
    a = jnp.exp(log_a)
    y = jnp.tanh(log_a) * (-1.0 - a * a)
    b = (y * lax.rsqrt(jnp.maximum(y, F32_TINY))) * (i * xc)
    return a, b


def _lru_consts(gate_rb, gate_ib, lam):
    return 0.5 * gate_rb, 0.5 * gate_ib, (0.5 * LRU_C) * jax.nn.log_sigmoid(lam)


def _build_gate_tiles(gate_rw_ref, gate_iw_ref, gate_sc):
    k_idx = lax.broadcasted_iota(jnp.int32, (LRU_BLOCK_DIM, GATE_TILE), 0)
    n_idx = lax.broadcasted_iota(jnp.int32, (LRU_BLOCK_DIM, GATE_TILE), 1)
    replicate = (n_idx % LRU_BLOCK_DIM == k_idx).astype(jnp.bfloat16)
    row_blk = lax.broadcasted_iota(jnp.int32, (GATE_TILE, GATE_TILE), 0) // LRU_BLOCK_DIM
    col_blk = lax.broadcasted_iota(jnp.int32, (GATE_TILE, GATE_TILE), 1) // LRU_BLOCK_DIM
    for j in range(N_GATE_TILES):
        for k, ref in enumerate((gate_rw_ref, gate_iw_ref)):
            rep = _dot(ref[j].astype(jnp.bfloat16), replicate)
            gate_sc[j, :, k * GATE_TILE:(k + 1) * GATE_TILE] = jnp.where(
                row_blk == col_blk, 0.5 * rep, 0.0).astype(jnp.bfloat16)


def _lru_coeffs(xc, gate_w_ref, gate_rb, gate_ib, lam):
    xcb = xc.astype(jnp.bfloat16)
    r_parts, i_parts = [], []
    for j in range(N_GATE_TILES):
        ri = _dot(xcb[:, j * GATE_TILE:(j + 1) * GATE_TILE], gate_w_ref[j])
        r_parts.append(ri[:, :GATE_TILE])
        i_parts.append(ri[:, GATE_TILE:])
    return _lru_elementwise(jnp.concatenate(r_parts, axis=1), jnp.concatenate(i_parts, axis=1), xc,
                            *_lru_consts(gate_rb, gate_ib, lam))


def _mlp_up(xn, w1_ref, c):
    cols = slice(c * FF_CHUNK, (c + 1) * FF_CHUNK)
    return jnp.square(jnp.maximum(_dot(xn, w1_ref[:, cols]), 0.0)).astype(jnp.bfloat16)


def _mlp_down(hid, w2_ref, c):
    return _dot(hid, w2_ref[c * FF_CHUNK:(c + 1) * FF_CHUNK, :])


def _layer_prompt_kernel(x_ref, xs_ref, g_mix_ref, w_in_ref, conv_wb_ref, gate_rw_ref, gate_iw_ref, gate_rb_ref, gate_ib_ref, lam_ref,
                         sgu_g_ref, sgu_bn_ref, sgu_w_ref, sgu_bias_ref, w_out_ref, g_mlp_ref, w1_ref, w2_ref, gf_ref,
                         y_ref, ys_ref, hlast_ref, convnew_ref, vprev_sc, sgu_lhs_sc, ymix_sc, gate_sc, *slab_scratch,
                         nseq, n_tiles, final_norm):
    rows = nseq * HALF
    j = pl.program_id(0)
    slab_sc = slab_scratch[0:N_SLABS]
    tail_sc = slab_scratch[N_SLABS:2 * N_SLABS]
    h_sc = slab_scratch[2 * N_SLABS:3 * N_SLABS]

    @pl.when(j == 0)
    def _():
        for c in range(N_SLABS):
            tail_sc[c][...] = jnp.zeros_like(tail_sc[c])
            h_sc[c][...] = jnp.zeros_like(h_sc[c])
        vprev_sc[...] = jnp.zeros_like(vprev_sc)
        ymix_sc[1] = xs_ref[...]
        _build_gate_tiles(gate_rw_ref, gate_iw_ref, gate_sc)
        t_idx = lax.broadcasted_iota(jnp.int32, (CHUNK, CHUNK), 0)
        s_idx = lax.broadcasted_iota(jnp.int32, (CHUNK, CHUNK), 1)
        for g in range(N_LANE_GROUPS):
            first, second = [], []
            for hh in range(HEADS_PER_LANE_GROUP):
                wm = jnp.where(s_idx <= t_idx, sgu_w_ref[HEADS_PER_LANE_GROUP * g + hh], 0.0)
                first.append(pltpu.roll(wm[:HALF], HALF, axis=1))
                second.append(wm[HALF:])
            sgu_lhs_sc[0, g] = jnp.concatenate(first, axis=0).astype(jnp.bfloat16)
            sgu_lhs_sc[1, g] = jnp.concatenate(second, axis=0).astype(jnp.bfloat16)

    parity = j % 2

    @pl.when(j < n_tiles)
    def _():
        _fused_step(x_ref, g_mix_ref, w_in_ref, conv_wb_ref, gate_rb_ref, gate_ib_ref, lam_ref, sgu_g_ref, sgu_bn_ref,
                    sgu_bias_ref, w_out_ref, g_mlp_ref, w1_ref, w2_ref, gf_ref, y_ref, vprev_sc, sgu_lhs_sc, ymix_sc,
                    gate_sc, slab_sc, tail_sc, h_sc, parity=parity, nseq=nseq, final_norm=final_norm)

    @pl.when(j == 0)
    def _():
        ys_ref[...] = y_ref[...].reshape(rows, D_MODEL)

    @pl.when(j == n_tiles - 1)
    def _():
        for c in range(N_SLABS):
            cols = slice(c * LANES, (c + 1) * LANES)
            hlast_ref[:, cols] = h_sc[c][...]
            for k in range(CONV_W - 1):
                convnew_ref[CONV_W - 2 - k, :, cols] = tail_sc[c][k]

    @pl.when(j == n_tiles + 1)
    def _():
        ymix_sc[1 - parity] = xs_ref[...]

    @pl.when(j >= n_tiles)
    def _():
        xm = ymix_sc[1 - parity]
        xn = _rms_norm(xm, g_mlp_ref[...]).astype(jnp.bfloat16)
        acc = xm
        for c in range(N_FF_CHUNKS):
            acc = acc + _mlp_down(_mlp_up(xn, w1_ref, c), w2_ref, c)
        ymix_sc[parity] = _rms_norm(acc, gf_ref[...]) if final_norm else acc

    @pl.when(j == n_tiles)
    def _():
        y_ref[...] = ymix_sc[parity].reshape(nseq, HALF, D_MODEL)

    @pl.when(j == n_tiles + 1)
    def _():
        ys_ref[...] = ymix_sc[parity]


def _fused_step(x_ref, g_mix_ref, w_in_ref, conv_wb_ref, gate_rb_ref, gate_ib_ref, lam_ref, sgu_g_ref, sgu_bn_ref,
                sgu_bias_ref, w_out_ref, g_mlp_ref, w1_ref, w2_ref, gf_ref, y_ref, vprev_sc, sgu_lhs_sc, ymix_sc,
                gate_sc, slab_sc, tail_sc, h_sc, *, parity, nseq, final_norm):
    rows = nseq * HALF
    half_rb, half_ib, half_c = _lru_consts(gate_rb_ref[...], gate_ib_ref[...], lam_ref[...])
    xb_col0 = 2 * D_SGU
    yb_col0 = 2 * D_SGU + D_LRU
    slabs_per_tile = GATE_TILE // LANES
    st = [{} for _ in range(N_GATE_TILES)]
    sg = {}
    ml = {}

    def mlp_norm():
        xm = ymix_sc[1 - parity]
        ml["acc"] = xm
        ml["xn"] = _rms_norm(xm, g_mlp_ref[...]).astype(jnp.bfloat16)

    def mlp_up(c):
        ml["hid", c] = _mlp_up(ml["xn"], w1_ref, c)

    def mlp_down(c):
        ml["acc"] = ml["acc"] + _mlp_down(ml.pop(("hid", c)), w2_ref, c)

    def mlp_store():
        out = _rms_norm(ml["acc"], gf_ref[...]) if final_norm else ml["acc"]
        y_ref[...] = out.reshape(nseq, HALF, D_MODEL)

    def mix_norm():
        sg["x"] = x_ref[...].reshape(rows, D_MODEL)
        sg["xn"] = _rms_norm(sg["x"], g_mix_ref[...]).astype(jnp.bfloat16)

    def lru_proj(q):
        xb = _dot(sg["xn"], w_in_ref[:, xb_col0 + q * GATE_TILE:xb_col0 + (q + 1) * GATE_TILE])
        for i in range(slabs_per_tile):
            for s in range(nseq):
                slab_sc[slabs_per_tile * q + i][s * PITCH:s * PITCH + HALF, :] = (
                    xb[s * HALF:(s + 1) * HALF, i * LANES:(i + 1) * LANES])

    def lru_conv(q):
        xc_slabs = []
        for c in range(slabs_per_tile * q, slabs_per_tile * (q + 1)):
            taps = [conv_wb_ref[k, c] for k in range(CONV_W + 1)]
            p1, p2, p3 = (tail_sc[c][k] for k in range(CONV_W - 1))
            steps_out = []
            for t in range(HALF):
                cur = slab_sc[c][pl.ds(t, nseq, stride=PITCH), :]
                steps_out.append(taps[CONV_W] + taps[3] * cur + taps[2] * p1 + taps[1] * p2 + taps[0] * p3)
                p1, p2, p3 = cur, p1, p2
            for k, pk in enumerate((p1, p2, p3)):
                tail_sc[c][k] = pk
            xc_slabs.append(jnp.concatenate(steps_out, axis=0))
        st[q]["xc"] = jnp.concatenate(xc_slabs, axis=1)

    def lru_gates(q):
        st[q]["ri"] = _dot(st[q]["xc"].astype(jnp.bfloat16), gate_sc[q])

    def lru_coeffs(q):
        cols = slice(q * GATE_TILE, (q + 1) * GATE_TILE)
        ri = st[q].pop("ri")
        st[q]["ab"] = _lru_elementwise(ri[:, :GATE_TILE], ri[:, GATE_TILE:], st[q].pop("xc"),
                                       half_rb[:, cols], half_ib[:, cols], half_c[:, cols])

    def lru_scan(q):
        a, b = st[q].pop("ab")
        for i in range(slabs_per_tile):
            c = slabs_per_tile * q + i
            h = h_sc[c][...]
            for t in range(HALF):
                h = (a[t * nseq:(t + 1) * nseq, i * LANES:(i + 1) * LANES] * h
                     + b[t * nseq:(t + 1) * nseq, i * LANES:(i + 1) * LANES])
                slab_sc[c][pl.ds(t, nseq, stride=PITCH), :] = h
            h_sc[c][...] = h

    def lru_out(q):
        hs = jnp.concatenate(
            [jnp.concatenate([slab_sc[c][s * PITCH:s * PITCH + HALF, :]
                              for c in range(slabs_per_tile * q, slabs_per_tile * (q + 1))], axis=1)
             for s in range(nseq)], axis=0)
        yb = _dot(sg["xn"], w_in_ref[:, yb_col0 + q * GATE_TILE:yb_col0 + (q + 1) * GATE_TILE])
        st[q]["out_b"] = (hs * _gelu(yb)).astype(jnp.bfloat16)

    def lru_mix(q):
        r0 = D_SGU + q * GATE_TILE
        sg["acc"] = sg["acc"] + _dot(st[q].pop("out_b"), w_out_ref[r0:r0 + GATE_TILE, :])

    def sgu_u():
        sg["u"] = _gelu(_dot(sg["xn"], w_in_ref[:, 0:D_SGU]))

    def sgu_v():
        v = _layer_norm(_gelu(_dot(sg["xn"], w_in_ref[:, D_SGU:2 * D_SGU])), sgu_g_ref[...], sgu_bn_ref[...])
        sg["vb"] = v.astype(jnp.bfloat16)

    def sgu_gate():
        vb = sg.pop("vb")
        lane = lax.broadcasted_iota(jnp.int32, (HALF, LANES), 1)
        bias = sgu_bias_ref[pl.ds(pl.multiple_of(parity * HALF, HALF), HALF), :]
        v_full = [jnp.concatenate([vprev_sc[1 - parity, s], vb[s * HALF:(s + 1) * HALF]], axis=0)
                  for s in range(nseq)]
        vprev_sc[parity] = vb.reshape(nseq, HALF, D_SGU)
        gate_rows = [[] for _ in range(nseq)]
        for s in range(0, nseq, 2):
            for g in range(N_LANE_GROUPS):
                lanes = slice(g * LANES, (g + 1) * LANES)
                rhs = jnp.concatenate([v_full[s][:, lanes], v_full[s + 1][:, lanes]], axis=1)
                res = _dot(sgu_lhs_sc[parity, g], rhs)
                for i in range(2):
                    part = res[:, i * LANES:(i + 1) * LANES]
                    gate_rows[s + i].append(jnp.where(lane < SGU_HEAD_DIM, part[:HALF], part[HALF:]))
        gate = jnp.concatenate([jnp.concatenate(r, axis=1) + bias for r in gate_rows], axis=0)
        sg["out_a"] = (sg.pop("u") * gate).astype(jnp.bfloat16)

    def sgu_mix():
        sg["acc"] = sg["x"] + _dot(sg.pop("out_a"), w_out_ref[0:D_SGU, :])

    def mix_store():
        ymix_sc[parity] = sg["acc"]

    tiles = range(N_GATE_TILES)
    mlp_norm()
    mix_norm()
    for q in tiles:
        lru_proj(q)
    mlp_up(0)
    for q in tiles:
        lru_conv(q)
        lru_gates(q)
    mlp_up(1)
    sgu_u()
    sgu_v()
    for q in tiles:
        lru_coeffs(q)
        lru_scan(q)
    mlp_down(0)
    mlp_down(1)
    for q in tiles:
        lru_out(q)
    mlp_up(2)
    mlp_up(3)
    sgu_gate()
    sgu_mix()
    mlp_down(2)
    mlp_down(3)
    for q in tiles:
        lru_mix(q)
    mix_store()
    mlp_store()


def _const_spec(shape):
    nd = len(shape)
    return pl.BlockSpec(shape, lambda *_: (0,) * nd, pipeline_mode=pl.Buffered(1))


def _layer_spec(arr, l):
    nd = arr.ndim - 1
    return pl.BlockSpec((None,) + arr.shape[1:], lambda *_: (l,) + (0,) * nd, pipeline_mode=pl.Buffered(1))


_PROMPT_PARAMS = ("g_mix", "w_in", "conv_wb", "gate_rw", "gate_iw", "gate_rb", "gate_ib", "lam", "sgu_g", "sgu_bn", "sgu_w",
                  "sgu_bias", "w_out", "g_mlp", "w1", "w2")
_SAMPLE_MIXER_PARAMS = ("g_mix", "w_in", "conv_w", "conv_b", "gate_rw", "gate_iw", "gate_rb", "gate_ib", "lam", "sgu_g",
                        "sgu_bn", "sgu_w8", "sgu_b8", "w_out")


def _layer_prompt(x, xs, p, gf, l, *, final_norm):
    nseq, seq, _ = x.shape
    n_tiles = seq // HALF
    rows = nseq * HALF
    assert xs.shape == (2 * rows, D_MODEL)
    consts = [p[k] for k in _PROMPT_PARAMS]
    x_spec = pl.BlockSpec((nseq, HALF, D_MODEL), lambda j: (0, jnp.minimum(j, n_tiles - 1), 0))
    y_spec = pl.BlockSpec((nseq, HALF, D_MODEL), lambda j: (0, jnp.clip(j - 1, 0, n_tiles - 1), 0))
    xs_spec = pl.BlockSpec((rows, D_MODEL), lambda j: (jnp.where(j > n_tiles, 1, 0), 0),
                           pipeline_mode=pl.Buffered(1))
    y, ys, hlast, convnew = pl.pallas_call(
        functools.partial(_layer_prompt_kernel, nseq=nseq, n_tiles=n_tiles, final_norm=final_norm),
        grid=(n_tiles + 2,),
        in_specs=[x_spec, xs_spec] + [_layer_spec(c, l) for c in consts] + [_const_spec(gf.shape)],
        out_specs=[y_spec, xs_spec, _const_spec((nseq, D_LRU)), _const_spec((CONV_W - 1, nseq, D_LRU))],
        out_shape=[jax.ShapeDtypeStruct(x.shape, jnp.float32),
                   jax.ShapeDtypeStruct(xs.shape, jnp.float32),
                   jax.ShapeDtypeStruct((nseq, D_LRU), jnp.float32),
                   jax.ShapeDtypeStruct((CONV_W - 1, nseq, D_LRU), jnp.float32)],
        scratch_shapes=([pltpu.VMEM((2, nseq, HALF, D_SGU), jnp.bfloat16),
                         pltpu.VMEM((2, N_LANE_GROUPS, CHUNK, CHUNK), jnp.bfloat16),
                         pltpu.VMEM((2, rows, D_MODEL), jnp.float32),
                         pltpu.VMEM((N_GATE_TILES, GATE_TILE, 2 * GATE_TILE), jnp.bfloat16)]
                        + [pltpu.VMEM((nseq * PITCH, LANES), jnp.float32)] * N_SLABS
                        + [pltpu.VMEM((CONV_W - 1, nseq, LANES), jnp.float32)] * N_SLABS
                        + [pltpu.VMEM((nseq, LANES), jnp.float32)] * N_SLABS),
        compiler_params=pltpu.CompilerParams(dimension_semantics=("arbitrary",), vmem_limit_bytes=VMEM_LIMIT),
        name="layer_prompt",
    )(x, xs, *consts, gf)
    return y, ys, hlast, jnp.transpose(convnew, (1, 0, 2))


def _mixer_sample_kernel(*refs, steps, nb):
    x_refs, refs = refs[:steps], refs[steps:]
    cbuf_refs, refs = refs[:CONV_W - 1], refs[CONV_W - 1:]
    (h0_ref, g_mix_ref, w_in_ref, conv_w_ref, conv_b_ref, gate_rw_ref, gate_iw_ref, gate_rb_ref, gate_ib_ref, lam_ref,
     sgu_g_ref, sgu_bn_ref, sgu_w8_ref, sgu_b8_ref, w_out_ref, y_ref, v_ref, hlast_ref, convnew_ref, gate_sc) = refs
    _build_gate_tiles(gate_rw_ref, gate_iw_ref, gate_sc)
    x = jnp.concatenate([r[...] for r in x_refs], axis=0)
    xn = _rms_norm(x, g_mix_ref[...]).astype(jnp.bfloat16)

    u = _gelu(_dot(xn, w_in_ref[:, 0:D_SGU]))
    v = _layer_norm(_gelu(_dot(xn, w_in_ref[:, D_SGU:2 * D_SGU])), sgu_g_ref[...], sgu_bn_ref[...])
    v_ref[...] = v
    gate_rows = []
    for t in range(steps):
        acc = sgu_b8_ref[t:t + 1, :] + sgu_w8_ref[t, 0:1, :] * v[0:nb]
        for s in range(1, t + 1):
            acc = acc + sgu_w8_ref[t, s:s + 1, :] * v[s * nb:(s + 1) * nb]
        gate_rows.append(acc)
    out_a = (u * jnp.concatenate(gate_rows, axis=0)).astype(jnp.bfloat16)

    xb = _dot(xn, w_in_ref[:, 2 * D_SGU:2 * D_SGU + D_LRU])
    xp = jnp.concatenate([r[...] for r in cbuf_refs] + [xb], axis=0)
    xc = conv_b_ref[...] + conv_w_ref[0:1, :] * xp[0:steps * nb]
    for k in range(1, CONV_W):
        xc = xc + conv_w_ref[k:k + 1, :] * xp[k * nb:(k + steps) * nb]
    convnew_ref[...] = xp[steps * nb:]

    a, b = _lru_coeffs(xc, gate_sc, gate_rb_ref[...], gate_ib_ref[...], lam_ref[...])
    h = h0_ref[...]
    hs = []
    for t in range(steps):
        h = a[t * nb:(t + 1) * nb] * h + b[t * nb:(t + 1) * nb]
        hs.append(h)
    hlast_ref[...] = h

    yb = _dot(xn, w_in_ref[:, 2 * D_SGU + D_LRU:])
    out_b = (jnp.concatenate(hs, axis=0) * _gelu(yb)).astype(jnp.bfloat16)
    out = _dot(jnp.concatenate([out_a, out_b], axis=1), w_out_ref[...])
    y_ref[...] = x + out


def _mixer_sample(x, h0_all, cbuf_all, p, l):
    consts = [p[k] for k in _SAMPLE_MIXER_PARAMS]
    nb = h0_all.shape[1]
    steps = x.shape[0] // nb
    rows = steps * nb
    x_specs = [pl.BlockSpec((nb, D_MODEL), lambda i, t=t: (t, 0), pipeline_mode=pl.Buffered(1))
               for t in range(steps)]
    cbuf_specs = [pl.BlockSpec((None, nb, D_LRU), lambda i, k=k: (l, k, 0), pipeline_mode=pl.Buffered(1))
                  for k in range(CONV_W - 1)]
    out_shapes = [(rows, D_MODEL), (rows, D_SGU), (nb, D_LRU), ((CONV_W - 1) * nb, D_LRU)]
    return pl.pallas_call(
        functools.partial(_mixer_sample_kernel, steps=steps, nb=nb),
        grid=(1,),
        in_specs=x_specs + cbuf_specs + [_layer_spec(h0_all, l)] + [_layer_spec(c, l) for c in consts],
        out_specs=[_const_spec(s) for s in out_shapes],
        out_shape=[jax.ShapeDtypeStruct(s, jnp.float32) for s in out_shapes],
        scratch_shapes=[pltpu.VMEM((N_GATE_TILES, GATE_TILE, 2 * GATE_TILE), jnp.bfloat16)],
        compiler_params=pltpu.CompilerParams(dimension_semantics=("arbitrary",), vmem_limit_bytes=VMEM_LIMIT),
        name="mixer_sample",
    )(*([x] * steps), *([cbuf_all] * (CONV_W - 1)), h0_all, *consts)


def _prepare_params(steps, nseq, norm_mix_g, w_in, conv_w, conv_b, gate_r_w, gate_r_b, gate_i_w, gate_i_b,
                    lru_lambda, sgu_norm_g, sgu_norm_b, sgu_w, sgu_b, w_out, norm_mlp_g, mlp_w1, mlp_w2):
    row = lambda a: a.reshape(DEPTH, 1, -1)
    conv_wb = jnp.concatenate([conv_w, conv_b[:, None, :]], axis=1)
    return {
        "g_mix": row(norm_mix_g),
        "w_in": w_in.astype(jnp.bfloat16),
        "conv_w": conv_w,
        "conv_b": row(conv_b),
        "conv_wb": jnp.broadcast_to(conv_wb.reshape(DEPTH, CONV_W + 1, N_SLABS, 1, LANES),
                                    (DEPTH, CONV_W + 1, N_SLABS, nseq, LANES)),
        "gate_rw": gate_r_w.reshape(DEPTH, N_GATE_TILES, GATE_TILE, LRU_BLOCK_DIM),
        "gate_iw": gate_i_w.reshape(DEPTH, N_GATE_TILES, GATE_TILE, LRU_BLOCK_DIM),
        "gate_rb": row(gate_r_b),
        "gate_ib": row(gate_i_b),
        "lam": row(lru_lambda),
        "sgu_g": row(sgu_norm_g),
        "sgu_bn": row(sgu_norm_b),
        "sgu_w": sgu_w,
        "sgu_bias": jnp.repeat(jnp.transpose(sgu_b, (0, 2, 1)), SGU_HEAD_DIM, axis=2),
        "sgu_w8": jnp.repeat(jnp.transpose(sgu_w[:, :, :steps, :steps], (0, 2, 3, 1)), SGU_HEAD_DIM, axis=3),
        "sgu_b8": jnp.repeat(jnp.transpose(sgu_b[:, :, :steps], (0, 2, 1)), SGU_HEAD_DIM, axis=2),
        "w_out": w_out.astype(jnp.bfloat16),
        "g_mlp": row(norm_mlp_g),
        "w1": mlp_w1.astype(jnp.bfloat16),
        "w2": mlp_w2.astype(jnp.bfloat16),
    }


def kernel(x_prompt, x_sample, state_lru_h, state_conv, norm_mix_g, w_in, conv_w, conv_b, gate_r_w, gate_r_b, gate_i_w, gate_i_b, lru_lambda, sgu_norm_g, sgu_norm_b, sgu_w, sgu_b, w_out, norm_mlp_g, mlp_w1, mlp_w2, final_norm_g):
    nseq, seq, _ = x_prompt.shape
    nb, steps, _ = x_sample.shape
    assert seq % CHUNK == 0 and nseq == SUBLANES and steps <= CHUNK
    gf = final_norm_g.reshape(1, -1)
    p = _prepare_params(steps, nseq, norm_mix_g, w_in, conv_w, conv_b, gate_r_w, gate_r_b, gate_i_w, gate_i_b,
                        lru_lambda, sgu_norm_g, sgu_norm_b, sgu_w, sgu_b, w_out, norm_mlp_g, mlp_w1, mlp_w2)

    xp = x_prompt
    xs = jnp.transpose(x_sample, (1, 0, 2)).reshape(steps * nb, D_MODEL)
    cbuf_tm = jnp.transpose(state_conv, (0, 2, 1, 3)).reshape(DEPTH, (CONV_W - 1) * nb, D_LRU)
    hs_p, convs_p, hs_s, convs_s, vs_s = [], [], [], [], []
    for l in range(DEPTH):
        last = l == DEPTH - 1
        xs, v_s, h_s, conv_s = _mixer_sample(xs, state_lru_h, cbuf_tm, p, l)
        xp, xs, h_p, conv_p = _layer_prompt(xp, xs, p, gf, l, final_norm=last)
        hs_p.append(h_p)
        convs_p.append(conv_p)
        hs_s.append(h_s)
        convs_s.append(conv_s.reshape(CONV_W - 1, nb, D_LRU))
        vs_s.append(v_s.reshape(steps, nb, D_SGU))

    y_sample = jnp.transpose(xs.reshape(steps, nb, D_MODEL), (1, 0, 2))
    return (xp, y_sample, jnp.stack(hs_p), jnp.stack(convs_p), jnp.stack(hs_s),
            jnp.transpose(jnp.stack(convs_s), (0, 2, 1, 3)), jnp.transpose(jnp.stack(vs_s), (0, 2, 1, 3)))
```

```python
import functools

import jax
import jax.numpy as jnp
from jax import lax
from jax.experimental import pallas as pl
from jax.experimental.pallas import tpu as pltpu

D_MODEL = 1024
DEPTH = 2
SGU_HEADS = 8
SGU_HEAD_DIM = 64
D_SGU = SGU_HEADS * SGU_HEAD_DIM
CHUNK = 128
D_LRU = 1024
LRU_BLOCKS = 16
LRU_BLOCK_DIM = 64
CONV_W = 4
LRU_C = 8.0
D_FF = 4 * D_MODEL
EPS = 1e-6

SUBLANES = 8
LANES = 128
GATE_TILE = 256
N_GATE_TILES = D_LRU // GATE_TILE
HEADS_PER_LANE_GROUP = LANES // SGU_HEAD_DIM
N_LANE_GROUPS = D_SGU // LANES
N_SLABS = D_LRU // LANES
FF_CHUNK = D_MODEL
N_FF_CHUNKS = D_FF // FF_CHUNK

HALF = CHUNK // 2
PITCH = HALF + SUBLANES
VMEM_LIMIT = 60 * 1024 * 1024
F32_TINY = 1.1754944e-38


def _rms_norm(x, g):
    return x * lax.rsqrt(jnp.mean(x * x, axis=-1, keepdims=True) + EPS) * g


def _layer_norm(x, g, b):
    mu = jnp.mean(x, axis=-1, keepdims=True)
    xc = x - mu
    return xc * lax.rsqrt(jnp.mean(xc * xc, axis=-1, keepdims=True) + EPS) * g + b


def _gelu(x):
    return jax.nn.gelu(x, approximate=True)


def _dot(a, b):
    return jnp.dot(a, b, preferred_element_type=jnp.float32)


def _lru_elementwise(r_pre, i_pre, xc, half_rb, half_ib, half_c):
    th_r = jnp.tanh(r_pre + half_rb)
    th_i = jnp.tanh(i_pre + half_ib)
    log_a = half_c * th_r + half_c
    i = 0.5 * th_i + 0.5
    a = jnp.exp(log_a)
    y = jnp.tanh(log_a) * (-1.0 - a * a)
    b = (y * lax.rsqrt(jnp.maximum(y, F32_TINY))) * (i * xc)
    return a, b


def _lru_consts(gate_rb, gate_ib, lam):
    return 0.5 * gate_rb, 0.5 * gate_ib, (0.5 * LRU_C) * jax.nn.log_sigmoid(lam)


def _build_gate_tiles(gate_rw_ref, gate_iw_ref, gate_sc):
    k_idx = lax.broadcasted_iota(jnp.int32, (LRU_BLOCK_DIM, GATE_TILE), 0)
    n_idx = lax.broadcasted_iota(jnp.int32, (LRU_BLOCK_DIM, GATE_TILE), 1)
    replicate = (n_idx % LRU_BLOCK_DIM == k_idx).astype(jnp.bfloat16)
    row_blk = lax.broadcasted_iota(jnp.int32, (GATE_TILE, GATE_TILE), 0) // LRU_BLOCK_DIM
    col_blk = lax.broadcasted_iota(jnp.int32, (GATE_TILE, GATE_TILE), 1) // LRU_BLOCK_DIM
    for j in range(N_GATE_TILES):
        for k, ref in enumerate((gate_rw_ref, gate_iw_ref)):
            rep = _dot(ref[j].astype(jnp.bfloat16), replicate)
            gate_sc[j, :, k * GATE_TILE:(k + 1) * GATE_TILE] = jnp.where(
                row_blk == col_blk, 0.5 * rep, 0.0).astype(jnp.bfloat16)


def _lru_coeffs(xc, gate_w_ref, gate_rb, gate_ib, lam):
    xcb = xc.astype(jnp.bfloat16)
    r_parts, i_parts = [], []
    for j in range(N_GATE_TILES):
        ri = _dot(xcb[:, j * GATE_TILE:(j + 1) * GATE_TILE], gate_w_ref[j])
        r_parts.append(ri[:, :GATE_TILE])
        i_parts.append(ri[:, GATE_TILE:])
    return _lru_elementwise(jnp.concatenate(r_parts, axis=1), jnp.concatenate(i_parts, axis=1), xc,
                            *_lru_consts(gate_rb, gate_ib, lam))


def _mlp_up(xn, w1_ref, c):
    cols = slice(c * FF_CHUNK, (c + 1) * FF_CHUNK)
    return jnp.square(jnp.maximum(_dot(xn, w1_ref[:, cols]), 0.0)).astype(jnp.bfloat16)


def _mlp_down(hid, w2_ref, c):
    return _dot(hid, w2_ref[c * FF_CHUNK:(c + 1) * FF_CHUNK, :])


def _layer_prompt_kernel(*refs, nseq, n_tiles, final_norm, n_cast):
    n_in = 2 + len(_PROMPT_PARAMS) + 1
    (x_ref, xs_ref, g_mix_ref, w_in_ref, conv_wb_ref, gate_rw_ref, gate_iw_ref, gate_rb_ref, gate_ib_ref, lam_ref,
     sgu_g_ref, sgu_bn_ref, sgu_w_ref, sgu_bias_ref, w_out_ref, g_mlp_ref, w1_ref, w2_ref, gf_ref) = refs[:n_in]
    cast_in, refs = refs[n_in:n_in + n_cast], refs[n_in + n_cast:]
    y_ref, ys_ref, hlast_ref, convnew_ref = refs[:4]
    cast_out, refs = refs[4:4 + n_cast], refs[4 + n_cast:]
    vprev_sc, sgu_lhs_sc, ymix_sc, gate_sc = refs[:4]
    slab_scratch = refs[4:]
    rows = nseq * HALF
    j = pl.program_id(0)
    slab_sc = slab_scratch[0:N_SLABS]
    tail_sc = slab_scratch[N_SLABS:2 * N_SLABS]
    h_sc = slab_scratch[2 * N_SLABS:3 * N_SLABS]

    @pl.when(j == 0)
    def _():
        for c in range(N_SLABS):
            tail_sc[c][...] = jnp.zeros_like(tail_sc[c])
            h_sc[c][...] = jnp.zeros_like(h_sc[c])
        vprev_sc[...] = jnp.zeros_like(vprev_sc)
        ymix_sc[1] = xs_ref[...]
        _build_gate_tiles(gate_rw_ref, gate_iw_ref, gate_sc)
        t_idx = lax.broadcasted_iota(jnp.int32, (CHUNK, CHUNK), 0)
        s_idx = lax.broadcasted_iota(jnp.int32, (CHUNK, CHUNK), 1)
        for g in range(N_LANE_GROUPS):
            first, second = [], []
            for hh in range(HEADS_PER_LANE_GROUP):
                wm = jnp.where(s_idx <= t_idx, sgu_w_ref[HEADS_PER_LANE_GROUP * g + hh], 0.0)
                first.append(pltpu.roll(wm[:HALF], HALF, axis=1))
                second.append(wm[HALF:])
            sgu_lhs_sc[0, g] = jnp.concatenate(first, axis=0).astype(jnp.bfloat16)
            sgu_lhs_sc[1, g] = jnp.concatenate(second, axis=0).astype(jnp.bfloat16)

    parity = j % 2

    @pl.when(j < n_tiles)
    def _():
        _fused_step(x_ref, g_mix_ref, w_in_ref, conv_wb_ref, gate_rb_ref, gate_ib_ref, lam_ref, sgu_g_ref, sgu_bn_ref,
                    sgu_bias_ref, w_out_ref, g_mlp_ref, w1_ref, w2_ref, gf_ref, y_ref, vprev_sc, sgu_lhs_sc, ymix_sc,
                    gate_sc, slab_sc, tail_sc, h_sc, parity=parity, nseq=nseq, final_norm=final_norm)
        for w_f32, w_bf16 in zip(cast_in, cast_out):
            w_bf16[...] = w_f32[...].astype(jnp.bfloat16)

    @pl.when(j == 0)
    def _():
        ys_ref[...] = y_ref[...].reshape(rows, D_MODEL)

    @pl.when(j == n_tiles - 1)
    def _():
        for c in range(N_SLABS):
            cols = slice(c * LANES, (c + 1) * LANES)
            hlast_ref[:, cols] = h_sc[c][...]
            for k in range(CONV_W - 1):
                convnew_ref[CONV_W - 2 - k, :, cols] = tail_sc[c][k]

    @pl.when(j == n_tiles + 1)
    def _():
        ymix_sc[1 - parity] = xs_ref[...]

    @pl.when(j >= n_tiles)
    def _():
        xm = ymix_sc[1 - parity]
        xn = _rms_norm(xm, g_mlp_ref[...]).astype(jnp.bfloat16)
        acc = xm
        for c in range(N_FF_CHUNKS):
            acc = acc + _mlp_down(_mlp_up(xn, w1_ref, c), w2_ref, c)
        ymix_sc[parity] = _rms_norm(acc, gf_ref[...]) if final_norm else acc

    @pl.when(j == n_tiles)
    def _():
        y_ref[...] = ymix_sc[parity].reshape(nseq, HALF, D_MODEL)

    @pl.when(j == n_tiles + 1)
    def _():
        ys_ref[...] = ymix_sc[parity]


def _fused_step(x_ref, g_mix_ref, w_in_ref, conv_wb_ref, gate_rb_ref, gate_ib_ref, lam_ref, sgu_g_ref, sgu_bn_ref,
                sgu_bias_ref, w_out_ref, g_mlp_ref, w1_ref, w2_ref, gf_ref, y_ref, vprev_sc, sgu_lhs_sc, ymix_sc,
                gate_sc, slab_sc, tail_sc, h_sc, *, parity, nseq, final_norm):
    rows = nseq * HALF
    half_rb, half_ib, half_c = _lru_consts(gate_rb_ref[...], gate_ib_ref[...], lam_ref[...])
    xb_col0 = 2 * D_SGU
    yb_col0 = 2 * D_SGU + D_LRU
    slabs_per_tile = GATE_TILE // LANES
    st = [{} for _ in range(N_GATE_TILES)]
    sg = {}
    ml = {}

    def mlp_norm():
        xm = ymix_sc[1 - parity]
        ml["acc"] = xm
        ml["xn"] = _rms_norm(xm, g_mlp_ref[...]).astype(jnp.bfloat16)

    def mlp_up(c):
        ml["hid", c] = _mlp_up(ml["xn"], w1_ref, c)

    def mlp_down(c):
        ml["acc"] = ml["acc"] + _mlp_down(ml.pop(("hid", c)), w2_ref, c)

    def mlp_store():
        out = _rms_norm(ml["acc"], gf_ref[...]) if final_norm else ml["acc"]
        y_ref[...] = out.reshape(nseq, HALF, D_MODEL)

    def mix_norm():
        sg["x"] = x_ref[...].reshape(rows, D_MODEL)
        sg["xn"] = _rms_norm(sg["x"], g_mix_ref[...]).astype(jnp.bfloat16)

    def lru_proj(q):
        xb = _dot(sg["xn"], w_in_ref[:, xb_col0 + q * GATE_TILE:xb_col0 + (q + 1) * GATE_TILE])
        for i in range(slabs_per_tile):
            for s in range(nseq):
                slab_sc[slabs_per_tile * q + i][s * PITCH:s * PITCH + HALF, :] = (
                    xb[s * HALF:(s + 1) * HALF, i * LANES:(i + 1) * LANES])

    def lru_conv(q):
        xc_slabs = []
        for c in range(slabs_per_tile * q, slabs_per_tile * (q + 1)):
            taps = [conv_wb_ref[k, c] for k in range(CONV_W + 1)]
            p1, p2, p3 = (tail_sc[c][k] for k in range(CONV_W - 1))
            steps_out = []
            for t in range(HALF):
                cur = slab_sc[c][pl.ds(t, nseq, stride=PITCH), :]
                steps_out.append(taps[CONV_W] + taps[3] * cur + taps[2] * p1 + taps[1] * p2 + taps[0] * p3)
                p1, p2, p3 = cur, p1, p2
            for k, pk in enumerate((p1, p2, p3)):
                tail_sc[c][k] = pk
            xc_slabs.append(jnp.concatenate(steps_out, axis=0))
        st[q]["xc"] = jnp.concatenate(xc_slabs, axis=1)

    def lru_gates(q):
        st[q]["ri"] = _dot(st[q]["xc"].astype(jnp.bfloat16), gate_sc[q])

    def lru_coeffs(q):
        cols = slice(q * GATE_TILE, (q + 1) * GATE_TILE)
        ri = st[q].pop("ri")
        st[q]["ab"] = _lru_elementwise(ri[:, :GATE_TILE], ri[:, GATE_TILE:], st[q].pop("xc"),
                                       half_rb[:, cols], half_ib[:, cols], half_c[:, cols])

    def lru_scan(q):
        a, b = st[q].pop("ab")
        for i in range(slabs_per_tile):
            c = slabs_per_tile * q + i
            h = h_sc[c][...]
            for t in range(HALF):
                h = (a[t * nseq:(t + 1) * nseq, i * LANES:(i + 1) * LANES] * h
                     + b[t * nseq:(t + 1) * nseq, i * LANES:(i + 1) * LANES])
                slab_sc[c][pl.ds(t, nseq, stride=PITCH), :] = h
            h_sc[c][...] = h

    def lru_out(q):
        hs = jnp.concatenate(
            [jnp.concatenate([slab_sc[c][s * PITCH:s * PITCH + HALF, :]
                              for c in range(slabs_per_tile * q, slabs_per_tile * (q + 1))], axis=1)
             for s in range(nseq)], axis=0)
        yb = _dot(sg["xn"], w_in_ref[:, yb_col0 + q * GATE_TILE:yb_col0 + (q + 1) * GATE_TILE])
        st[q]["out_b"] = (hs * _gelu(yb)).astype(jnp.bfloat16)

    def lru_mix(q):
        r0 = D_SGU + q * GATE_TILE
        sg["acc"] = sg["acc"] + _dot(st[q].pop("out_b"), w_out_ref[r0:r0 + GATE_TILE, :])

    def sgu_u():
        sg["u"] = _gelu(_dot(sg["xn"], w_in_ref[:, 0:D_SGU]))

    def sgu_v():
        v = _layer_norm(_gelu(_dot(sg["xn"], w_in_ref[:, D_SGU:2 * D_SGU])), sgu_g_ref[...], sgu_bn_ref[...])
        sg["vb"] = v.astype(jnp.bfloat16)

    def sgu_gate():
        vb = sg.pop("vb")
        lane = lax.broadcasted_iota(jnp.int32, (HALF, LANES), 1)
        bias = sgu_bias_ref[pl.ds(pl.multiple_of(parity * HALF, HALF), HALF), :]
        v_full = [jnp.concatenate([vprev_sc[1 - parity, s], vb[s * HALF:(s + 1) * HALF]], axis=0)
                  for s in range(nseq)]
        vprev_sc[parity] = vb.reshape(nseq, HALF, D_SGU)
        gate_rows = [[] for _ in range(nseq)]
        for s in range(0, nseq, 2):
            for g in range(N_LANE_GROUPS):
                lanes = slice(g * LANES, (g + 1) * LANES)
                rhs = jnp.concatenate([v_full[s][:, lanes], v_full[s + 1][:, lanes]], axis=1)
                res = _dot(sgu_lhs_sc[parity, g], rhs)
                for i in range(2):
                    part = res[:, i * LANES:(i + 1) * LANES]
                    gate_rows[s + i].append(jnp.where(lane < SGU_HEAD_DIM, part[:HALF], part[HALF:]))
        gate = jnp.concatenate([jnp.concatenate(r, axis=1) + bias for r in gate_rows], axis=0)
        sg["out_a"] = (sg.pop("u") * gate).astype(jnp.bfloat16)

    def sgu_mix():
        sg["acc"] = sg["x"] + _dot(sg.pop("out_a"), w_out_ref[0:D_SGU, :])

    def mix_store():
        ymix_sc[parity] = sg["acc"]

    tiles = range(N_GATE_TILES)
    mlp_norm()
    mix_norm()
    for q in tiles:
        lru_proj(q)
    mlp_up(0)
    for q in tiles:
        lru_conv(q)
        lru_gates(q)
    mlp_up(1)
    sgu_u()
    sgu_v()
    for q in tiles:
        lru_coeffs(q)
        lru_scan(q)
    mlp_down(0)
    mlp_down(1)
    for q in tiles:
        lru_out(q)
    mlp_up(2)
    mlp_up(3)
    sgu_gate()
    sgu_mix()
    mlp_down(2)
    mlp_down(3)
    for q in tiles:
        lru_mix(q)
    mix_store()
    mlp_store()


def _const_spec(shape):
    nd = len(shape)
    return pl.BlockSpec(shape, lambda *_: (0,) * nd, pipeline_mode=pl.Buffered(1))


def _layer_spec(arr, l):
    nd = arr.ndim - 1
    return pl.BlockSpec((None,) + arr.shape[1:], lambda *_: (l,) + (0,) * nd, pipeline_mode=pl.Buffered(1))


_BIG_WEIGHTS = ("w_in", "w_out", "w1", "w2")
_PROMPT_PARAMS = ("g_mix", "w_in", "conv_wb", "gate_rw", "gate_iw", "gate_rb", "gate_ib", "lam", "sgu_g", "sgu_bn", "sgu_w",
                  "sgu_bias", "w_out", "g_mlp", "w1", "w2")
_SAMPLE_MIXER_PARAMS = ("g_mix", "w_in", "conv_w", "conv_b", "gate_rw", "gate_iw", "gate_rb", "gate_ib", "lam", "sgu_g",
                        "sgu_bn", "sgu_w8", "sgu_b8", "w_out")


def _param_spec(p, k, l):
    return _layer_spec(p[k], 0 if k in _BIG_WEIGHTS else l)


def _layer_prompt(x, xs, p, gf, l, *, final_norm, cast_next=()):
    nseq, seq, _ = x.shape
    n_tiles = seq // HALF
    rows = nseq * HALF
    assert xs.shape == (2 * rows, D_MODEL)
    consts = [p[k] for k in _PROMPT_PARAMS]
    cast_in_specs, cast_out_specs, cast_shapes = [], [], []
    for w in cast_next:
        k_rows, n_cols = w.shape[1:]
        blk = k_rows // n_tiles
        assert blk * n_tiles == k_rows and blk % (2 * SUBLANES) == 0
        cast_in_specs.append(pl.BlockSpec((None, blk, n_cols), lambda j: (l + 1, jnp.minimum(j, n_tiles - 1), 0)))
        cast_out_specs.append(pl.BlockSpec((None, blk, n_cols), lambda j: (0, jnp.minimum(j, n_tiles - 1), 0)))
        cast_shapes.append(jax.ShapeDtypeStruct((1, k_rows, n_cols), jnp.bfloat16))
    x_spec = pl.BlockSpec((nseq, HALF, D_MODEL), lambda j: (0, jnp.minimum(j, n_tiles - 1), 0))
    y_spec = pl.BlockSpec((nseq, HALF, D_MODEL), lambda j: (0, jnp.clip(j - 1, 0, n_tiles - 1), 0))
    xs_spec = pl.BlockSpec((rows, D_MODEL), lambda j: (jnp.where(j > n_tiles, 1, 0), 0),
                           pipeline_mode=pl.Buffered(1))
    y, ys, hlast, convnew, *cast = pl.pallas_call(
        functools.partial(_layer_prompt_kernel, nseq=nseq, n_tiles=n_tiles, final_norm=final_norm,
                          n_cast=len(cast_next)),
        grid=(n_tiles + 2,),
        in_specs=([x_spec, xs_spec] + [_param_spec(p, k, l) for k in _PROMPT_PARAMS] + [_const_spec(gf.shape)]
                  + cast_in_specs),
        out_specs=([y_spec, xs_spec, _const_spec((nseq, D_LRU)), _const_spec((CONV_W - 1, nseq, D_LRU))]
                   + cast_out_specs),
        out_shape=[jax.ShapeDtypeStruct(x.shape, jnp.float32),
                   jax.ShapeDtypeStruct(xs.shape, jnp.float32),
                   jax.ShapeDtypeStruct((nseq, D_LRU), jnp.float32),
                   jax.ShapeDtypeStruct((CONV_W - 1, nseq, D_LRU), jnp.float32)] + cast_shapes,
        scratch_shapes=([pltpu.VMEM((2, nseq, HALF, D_SGU), jnp.bfloat16),
                         pltpu.VMEM((2, N_LANE_GROUPS, CHUNK, CHUNK), jnp.bfloat16),
                         pltpu.VMEM((2, rows, D_MODEL), jnp.float32),
                         pltpu.VMEM((N_GATE_TILES, GATE_TILE, 2 * GATE_TILE), jnp.bfloat16)]
                        + [pltpu.VMEM((nseq * PITCH, LANES), jnp.float32)] * N_SLABS
                        + [pltpu.VMEM((CONV_W - 1, nseq, LANES), jnp.float32)] * N_SLABS
                        + [pltpu.VMEM((nseq, LANES), jnp.float32)] * N_SLABS),
        compiler_params=pltpu.CompilerParams(dimension_semantics=("arbitrary",), vmem_limit_bytes=VMEM_LIMIT),
        name="layer_prompt",
    )(x, xs, *consts, gf, *cast_next)
    return y, ys, hlast, jnp.transpose(convnew, (1, 0, 2)), cast


def _mixer_sample_kernel(*refs, steps, nb):
    x_refs, refs = refs[:steps], refs[steps:]
    cbuf_refs, refs = refs[:CONV_W - 1], refs[CONV_W - 1:]
    (h0_ref, g_mix_ref, w_in_ref, conv_w_ref, conv_b_ref, gate_rw_ref, gate_iw_ref, gate_rb_ref, gate_ib_ref, lam_ref,
     sgu_g_ref, sgu_bn_ref, sgu_w8_ref, sgu_b8_ref, w_out_ref, y_ref, v_ref, hlast_ref, convnew_ref, gate_sc) = refs
    _build_gate_tiles(gate_rw_ref, gate_iw_ref, gate_sc)
    x = jnp.concatenate([r[...] for r in x_refs], axis=0)
    xn = _rms_norm(x, g_mix_ref[...]).astype(jnp.bfloat16)

    u = _gelu(_dot(xn, w_in_ref[:, 0:D_SGU]))
    v = _layer_norm(_gelu(_dot(xn, w_in_ref[:, D_SGU:2 * D_SGU])), sgu_g_ref[...], sgu_bn_ref[...])
    v_ref[...] = v
    gate_rows = []
    for t in range(steps):
        acc = sgu_b8_ref[t:t + 1, :] + sgu_w8_ref[t, 0:1, :] * v[0:nb]
        for s in range(1, t + 1):
            acc = acc + sgu_w8_ref[t, s:s + 1, :] * v[s * nb:(s + 1) * nb]
        gate_rows.append(acc)
    out_a = (u * jnp.concatenate(gate_rows, axis=0)).astype(jnp.bfloat16)

    xb = _dot(xn, w_in_ref[:, 2 * D_SGU:2 * D_SGU + D_LRU])
    xp = jnp.concatenate([r[...] for r in cbuf_refs] + [xb], axis=0)
    xc = conv_b_ref[...] + conv_w_ref[0:1, :] * xp[0:steps * nb]
    for k in range(1, CONV_W):
        xc = xc + conv_w_ref[k:k + 1, :] * xp[k * nb:(k + steps) * nb]
    convnew_ref[...] = xp[steps * nb:]

    a, b = _lru_coeffs(xc, gate_sc, gate_rb_ref[...], gate_ib_ref[...], lam_ref[...])
    h = h0_ref[...]
    hs = []
    for t in range(steps):
        h = a[t * nb:(t + 1) * nb] * h + b[t * nb:(t + 1) * nb]
        hs.append(h)
    hlast_ref[...] = h

    yb = _dot(xn, w_in_ref[:, 2 * D_SGU + D_LRU:])
    out_b = (jnp.concatenate(hs, axis=0) * _gelu(yb)).astype(jnp.bfloat16)
    out = _dot(jnp.concatenate([out_a, out_b], axis=1), w_out_ref[...])
    y_ref[...] = x + out


def _mixer_sample(x, h0_all, cbuf_all, p, l):
    consts = [p[k] for k in _SAMPLE_MIXER_PARAMS]
    nb = h0_all.shape[1]
    steps = x.shape[0] // nb
    rows = steps * nb
    x_specs = [pl.BlockSpec((nb, D_MODEL), lambda i, t=t: (t, 0), pipeline_mode=pl.Buffered(1))
               for t in range(steps)]
    cbuf_specs = [pl.BlockSpec((None, nb, D_LRU), lambda i, k=k: (l, k, 0), pipeline_mode=pl.Buffered(1))
                  for k in range(CONV_W - 1)]
    out_shapes = [(rows, D_MODEL), (rows, D_SGU), (nb, D_LRU), ((CONV_W - 1) * nb, D_LRU)]
    return pl.pallas_call(
        functools.partial(_mixer_sample_kernel, steps=steps, nb=nb),
        grid=(1,),
        in_specs=(x_specs + cbuf_specs + [_layer_spec(h0_all, l)]
                  + [_param_spec(p, k, l) for k in _SAMPLE_MIXER_PARAMS]),
        out_specs=[_const_spec(s) for s in out_shapes],
        out_shape=[jax.ShapeDtypeStruct(s, jnp.float32) for s in out_shapes],
        scratch_shapes=[pltpu.VMEM((N_GATE_TILES, GATE_TILE, 2 * GATE_TILE), jnp.bfloat16)],
        compiler_params=pltpu.CompilerParams(dimension_semantics=("arbitrary",), vmem_limit_bytes=VMEM_LIMIT),
        name="mixer_sample",
    )(*([x] * steps), *([cbuf_all] * (CONV_W - 1)), h0_all, *consts)


def _prepare_params(steps, nseq, norm_mix_g, w_in, conv_w, conv_b, gate_r_w, gate_r_b, gate_i_w, gate_i_b,
                    lru_lambda, sgu_norm_g, sgu_norm_b, sgu_w, sgu_b, w_out, norm_mlp_g, mlp_w1, mlp_w2):
    row = lambda a: a.reshape(DEPTH, 1, -1)
    conv_wb = jnp.concatenate([conv_w, conv_b[:, None, :]], axis=1)
    return {
        "g_mix": row(norm_mix_g),
        "conv_w": conv_w,
        "conv_b": row(conv_b),
        "conv_wb": jnp.broadcast_to(conv_wb.reshape(DEPTH, CONV_W + 1, N_SLABS, 1, LANES),
                                    (DEPTH, CONV_W + 1, N_SLABS, nseq, LANES)),
        "gate_rw": gate_r_w.reshape(DEPTH, N_GATE_TILES, GATE_TILE, LRU_BLOCK_DIM),
        "gate_iw": gate_i_w.reshape(DEPTH, N_GATE_TILES, GATE_TILE, LRU_BLOCK_DIM),
        "gate_rb": row(gate_r_b),
        "gate_ib": row(gate_i_b),
        "lam": row(lru_lambda),
        "sgu_g": row(sgu_norm_g),
        "sgu_bn": row(sgu_norm_b),
        "sgu_w": sgu_w,
        "sgu_bias": jnp.repeat(jnp.transpose(sgu_b, (0, 2, 1)), SGU_HEAD_DIM, axis=2),
        "sgu_w8": jnp.repeat(jnp.transpose(sgu_w[:, :, :steps, :steps], (0, 2, 3, 1)), SGU_HEAD_DIM, axis=3),
        "sgu_b8": jnp.repeat(jnp.transpose(sgu_b[:, :, :steps], (0, 2, 1)), SGU_HEAD_DIM, axis=2),
        "g_mlp": row(norm_mlp_g),
    }


def kernel(x_prompt, x_sample, state_lru_h, state_conv, norm_mix_g, w_in, conv_w, conv_b, gate_r_w, gate_r_b, gate_i_w, gate_i_b, lru_lambda, sgu_norm_g, sgu_norm_b, sgu_w, sgu_b, w_out, norm_mlp_g, mlp_w1, mlp_w2, final_norm_g):
    nseq, seq, _ = x_prompt.shape
    nb, steps, _ = x_sample.shape
    assert seq % CHUNK == 0 and nseq == SUBLANES and steps <= CHUNK
    gf = final_norm_g.reshape(1, -1)
    p = _prepare_params(steps, nseq, norm_mix_g, w_in, conv_w, conv_b, gate_r_w, gate_r_b, gate_i_w, gate_i_b,
                        lru_lambda, sgu_norm_g, sgu_norm_b, sgu_w, sgu_b, w_out, norm_mlp_g, mlp_w1, mlp_w2)

    xp = x_prompt
    xs = jnp.transpose(x_sample, (1, 0, 2)).reshape(steps * nb, D_MODEL)
    cbuf_tm = jnp.transpose(state_conv, (0, 2, 1, 3)).reshape(DEPTH, (CONV_W - 1) * nb, D_LRU)
    hs_p, convs_p, hs_s, convs_s, vs_s = [], [], [], [], []
    big_f32 = (w_in, w_out, mlp_w1, mlp_w2)
    big = [w[0:1].astype(jnp.bfloat16) for w in big_f32]
    for l in range(DEPTH):
        last = l == DEPTH - 1
        p.update(zip(_BIG_WEIGHTS, big))
        xs, v_s, h_s, conv_s = _mixer_sample(xs, state_lru_h, cbuf_tm, p, l)
        xp, xs, h_p, conv_p, big = _layer_prompt(xp, xs, p, gf, l, final_norm=last,
                                                 cast_next=() if last else big_f32)
        hs_p.append(h_p)
        convs_p.append(conv_p)
        hs_s.append(h_s)
        convs_s.append(conv_s.reshape(CONV_W - 1, nb, D_LRU))
        vs_s.append(v_s.reshape(steps, nb, D_SGU))

    y_sample = jnp.transpose(xs.reshape(steps, nb, D_MODEL), (1, 0, 2))
    return (xp, y_sample, jnp.stack(hs_p), jnp.stack(convs_p), jnp.stack(hs_s),
            jnp.transpose(jnp.stack(convs_s), (0, 2, 1, 3)), jnp.transpose(jnp.stack(vs_s), (0, 2, 1, 3)))
```

```python
import functools

import jax
import jax.numpy as jnp
from jax import lax
from jax.experimental import pallas as pl
from jax.experimental.pallas import tpu as pltpu

D_MODEL = 1024
DEPTH = 2
SGU_HEADS = 8
SGU_HEAD_DIM = 64
D_SGU = SGU_HEADS * SGU_HEAD_DIM
CHUNK = 128
D_LRU = 1024
LRU_BLOCKS = 16
LRU_BLOCK_DIM = 64
CONV_W = 4
LRU_C = 8.0
D_FF = 4 * D_MODEL
EPS = 1e-6

SUBLANES = 8
LANES = 128
GATE_TILE = 256
N_GATE_TILES = D_LRU // GATE_TILE
HEADS_PER_LANE_GROUP = LANES // SGU_HEAD_DIM
N_LANE_GROUPS = D_SGU // LANES
N_SLABS = D_LRU // LANES
FF_CHUNK = D_MODEL
N_FF_CHUNKS = D_FF // FF_CHUNK

HALF = CHUNK // 2
PITCH = HALF + SUBLANES
VMEM_LIMIT = 60 * 1024 * 1024
F32_TINY = 1.1754944e-38


def _rms_norm(x, g):
    return x * lax.rsqrt(jnp.mean(x * x, axis=-1, keepdims=True) + EPS) * g


def _layer_norm(x, g, b):
    mu = jnp.mean(x, axis=-1, keepdims=True)
    xc = x - mu
    return xc * lax.rsqrt(jnp.mean(xc * xc, axis=-1, keepdims=True) + EPS) * g + b


def _gelu(x):
    return jax.nn.gelu(x, approximate=True)


def _dot(a, b):
    return jnp.dot(a, b, preferred_element_type=jnp.float32)


def _lru_elementwise(r_pre, i_pre, xc, half_rb, half_ib, half_c):
    th_r = jnp.tanh(r_pre + half_rb)
    th_i = jnp.tanh(i_pre + half_ib)
    log_a = half_c * th_r + half_c
    i = 0.5 * th_i + 0.5
    a = jnp.exp(log_a)
    y = jnp.tanh(log_a) * (-1.0 - a * a)
    b = (y * lax.rsqrt(jnp.maximum(y, F32_TINY))) * (i * xc)
    return a, b


def _lru_consts(gate_rb, gate_ib, lam):
    return 0.5 * gate_rb, 0.5 * gate_ib, (0.5 * LRU_C) * jax.nn.log_sigmoid(lam)


def _build_gate_tiles(gate_rw_ref, gate_iw_ref, gate_sc):
    k_idx = lax.broadcasted_iota(jnp.int32, (LRU_BLOCK_DIM, GATE_TILE), 0)
    n_idx = lax.broadcasted_iota(jnp.int32, (LRU_BLOCK_DIM, GATE_TILE), 1)
    replicate = (n_idx % LRU_BLOCK_DIM == k_idx).astype(jnp.bfloat16)
    row_blk = lax.broadcasted_iota(jnp.int32, (GATE_TILE, GATE_TILE), 0) // LRU_BLOCK_DIM
    col_blk = lax.broadcasted_iota(jnp.int32, (GATE_TILE, GATE_TILE), 1) // LRU_BLOCK_DIM
    for j in range(N_GATE_TILES):
        for k, ref in enumerate((gate_rw_ref, gate_iw_ref)):
            rep = _dot(ref[j].astype(jnp.bfloat16), replicate)
            gate_sc[j, :, k * GATE_TILE:(k + 1) * GATE_TILE] = jnp.where(
                row_blk == col_blk, 0.5 * rep, 0.0).astype(jnp.bfloat16)


def _lru_coeffs(xc, gate_w_ref, gate_rb, gate_ib, lam):
    xcb = xc.astype(jnp.bfloat16)
    r_parts, i_parts = [], []
    for j in range(N_GATE_TILES):
        ri = _dot(xcb[:, j * GATE_TILE:(j + 1) * GATE_TILE], gate_w_ref[j])
        r_parts.append(ri[:, :GATE_TILE])
        i_parts.append(ri[:, GATE_TILE:])
    return _lru_elementwise(jnp.concatenate(r_parts, axis=1), jnp.concatenate(i_parts, axis=1), xc,
                            *_lru_consts(gate_rb, gate_ib, lam))


def _mlp_up(xn, w1_ref, c):
    cols = slice(c * FF_CHUNK, (c + 1) * FF_CHUNK)
    return jnp.square(jnp.maximum(_dot(xn, w1_ref[:, cols]), 0.0)).astype(jnp.bfloat16)


def _mlp_down(hid, w2_ref, c):
    return _dot(hid, w2_ref[c * FF_CHUNK:(c + 1) * FF_CHUNK, :])


def _layer_prompt_kernel(*refs, nseq, n_tiles, final_norm, n_cast):
    n_in = 2 + len(_PROMPT_PARAMS) + 1
    (x_ref, xs_ref, g_mix_ref, w_in_ref, conv_wb_ref, gate_rw_ref, gate_iw_ref, gate_rb_ref, gate_ib_ref, lam_ref,
     sgu_g_ref, sgu_bn_ref, sgu_w_ref, sgu_bias_ref, w_out_ref, g_mlp_ref, w1_ref, w2_ref, gf_ref) = refs[:n_in]
    cast_in, refs = refs[n_in:n_in + n_cast], refs[n_in + n_cast:]
    y_ref, ys_ref, hlast_ref, convnew_ref = refs[:4]
    cast_out, refs = refs[4:4 + n_cast], refs[4 + n_cast:]
    vprev_sc, sgu_lhs_sc, ymix_sc, gate_sc = refs[:4]
    slab_scratch = refs[4:]
    rows = nseq * HALF
    j = pl.program_id(0)
    slab_sc = slab_scratch[0:N_SLABS]
    tail_sc = slab_scratch[N_SLABS:2 * N_SLABS]
    h_sc = slab_scratch[2 * N_SLABS:3 * N_SLABS]

    @pl.when(j == 0)
    def _():
        for c in range(N_SLABS):
            tail_sc[c][...] = jnp.zeros_like(tail_sc[c])
            h_sc[c][...] = jnp.zeros_like(h_sc[c])
        vprev_sc[...] = jnp.zeros_like(vprev_sc)
        ymix_sc[1] = xs_ref[...]
        _build_gate_tiles(gate_rw_ref, gate_iw_ref, gate_sc)
        t_idx = lax.broadcasted_iota(jnp.int32, (CHUNK, CHUNK), 0)
        s_idx = lax.broadcasted_iota(jnp.int32, (CHUNK, CHUNK), 1)
        for g in range(N_LANE_GROUPS):
            first, second = [], []
            for hh in range(HEADS_PER_LANE_GROUP):
                wm = jnp.where(s_idx <= t_idx, sgu_w_ref[HEADS_PER_LANE_GROUP * g + hh], 0.0)
                first.append(pltpu.roll(wm[:HALF], HALF, axis=1))
                second.append(wm[HALF:])
            sgu_lhs_sc[0, g] = jnp.concatenate(first, axis=0).astype(jnp.bfloat16)
            sgu_lhs_sc[1, g] = jnp.concatenate(second, axis=0).astype(jnp.bfloat16)

    parity = j % 2

    @pl.when(j < n_tiles)
    def _():
        _fused_step(x_ref, g_mix_ref, w_in_ref, conv_wb_ref, gate_rb_ref, gate_ib_ref, lam_ref, sgu_g_ref, sgu_bn_ref,
                    sgu_bias_ref, w_out_ref, g_mlp_ref, w1_ref, w2_ref, gf_ref, y_ref, vprev_sc, sgu_lhs_sc, ymix_sc,
                    gate_sc, slab_sc, tail_sc, h_sc, parity=parity, nseq=nseq, final_norm=final_norm)
        for w_f32, w_bf16 in zip(cast_in, cast_out):
            w_bf16[...] = w_f32[...].astype(jnp.bfloat16)

    @pl.when(j == 0)
    def _():
        ys_ref[...] = y_ref[...].reshape(rows, D_MODEL)

    @pl.when(j == n_tiles - 1)
    def _():
        for c in range(N_SLABS):
            cols = slice(c * LANES, (c + 1) * LANES)
            hlast_ref[:, cols] = h_sc[c][...]
            for k in range(CONV_W - 1):
                convnew_ref[CONV_W - 2 - k, :, cols] = tail_sc[c][k]

    @pl.when(j == n_tiles + 1)
    def _():
        ymix_sc[1 - parity] = xs_ref[...]

    @pl.when(j >= n_tiles)
    def _():
        xm = ymix_sc[1 - parity]
        xn = _rms_norm(xm, g_mlp_ref[...]).astype(jnp.bfloat16)
        acc = xm
        for c in range(N_FF_CHUNKS):
            acc = acc + _mlp_down(_mlp_up(xn, w1_ref, c), w2_ref, c)
        ymix_sc[parity] = _rms_norm(acc, gf_ref[...]) if final_norm else acc

    @pl.when(j == n_tiles)
    def _():
        y_ref[...] = ymix_sc[parity].reshape(nseq, HALF, D_MODEL)

    @pl.when(j == n_tiles + 1)
    def _():
        ys_ref[...] = ymix_sc[parity]


def _fused_step(x_ref, g_mix_ref, w_in_ref, conv_wb_ref, gate_rb_ref, gate_ib_ref, lam_ref, sgu_g_ref, sgu_bn_ref,
                sgu_bias_ref, w_out_ref, g_mlp_ref, w1_ref, w2_ref, gf_ref, y_ref, vprev_sc, sgu_lhs_sc, ymix_sc,
                gate_sc, slab_sc, tail_sc, h_sc, *, parity, nseq, final_norm):
    rows = nseq * HALF
    half_rb, half_ib, half_c = _lru_consts(gate_rb_ref[...], gate_ib_ref[...], lam_ref[...])
    xb_col0 = 2 * D_SGU
    yb_col0 = 2 * D_SGU + D_LRU
    slabs_per_tile = GATE_TILE // LANES
    st = [{} for _ in range(N_GATE_TILES)]
    sg = {}
    ml = {}

    def mlp_norm():
        xm = ymix_sc[1 - parity]
        ml["acc"] = xm
        ml["xn"] = _rms_norm(xm, g_mlp_ref[...]).astype(jnp.bfloat16)

    def mlp_up(c):
        ml["hid", c] = _mlp_up(ml["xn"], w1_ref, c)

    def mlp_down(c):
        ml["acc"] = ml["acc"] + _mlp_down(ml.pop(("hid", c)), w2_ref, c)

    def mlp_store():
        out = _rms_norm(ml["acc"], gf_ref[...]) if final_norm else ml["acc"]
        y_ref[...] = out.reshape(nseq, HALF, D_MODEL)

    def mix_norm():
        sg["x"] = x_ref[...].reshape(rows, D_MODEL)
        sg["xn"] = _rms_norm(sg["x"], g_mix_ref[...]).astype(jnp.bfloat16)

    def lru_proj(q):
        xb = _dot(sg["xn"], w_in_ref[:, xb_col0 + q * GATE_TILE:xb_col0 + (q + 1) * GATE_TILE])
        for i in range(slabs_per_tile):
            for s in range(nseq):
                slab_sc[slabs_per_tile * q + i][s * PITCH:s * PITCH + HALF, :] = (
                    xb[s * HALF:(s + 1) * HALF, i * LANES:(i + 1) * LANES])

    def lru_conv(q):
        xc_slabs = []
        for c in range(slabs_per_tile * q, slabs_per_tile * (q + 1)):
            taps = [conv_wb_ref[k, c] for k in range(CONV_W + 1)]
            p1, p2, p3 = (tail_sc[c][k] for k in range(CONV_W - 1))
            steps_out = []
            for t in range(HALF):
                cur = slab_sc[c][pl.ds(t, nseq, stride=PITCH), :]
                steps_out.append(taps[CONV_W] + taps[3] * cur + taps[2] * p1 + taps[1] * p2 + taps[0] * p3)
                p1, p2, p3 = cur, p1, p2
            for k, pk in enumerate((p1, p2, p3)):
                tail_sc[c][k] = pk
            xc_slabs.append(jnp.concatenate(steps_out, axis=0))
        st[q]["xc"] = jnp.concatenate(xc_slabs, axis=1)

    def lru_gates(q):
        st[q]["ri"] = _dot(st[q]["xc"].astype(jnp.bfloat16), gate_sc[q])

    def lru_coeffs(q):
        cols = slice(q * GATE_TILE, (q + 1) * GATE_TILE)
        ri = st[q].pop("ri")
        st[q]["ab"] = _lru_elementwise(ri[:, :GATE_TILE], ri[:, GATE_TILE:], st[q].pop("xc"),
                                       half_rb[:, cols], half_ib[:, cols], half_c[:, cols])

    def lru_scan(q):
        a, b = st[q].pop("ab")
        for i in range(slabs_per_tile):
            c = slabs_per_tile * q + i
            h = h_sc[c][...]
            for t in range(HALF):
                h = (a[t * nseq:(t + 1) * nseq, i * LANES:(i + 1) * LANES] * h
                     + b[t * nseq:(t + 1) * nseq, i * LANES:(i + 1) * LANES])
                slab_sc[c][pl.ds(t, nseq, stride=PITCH), :] = h
            h_sc[c][...] = h

    def lru_out(q):
        hs = jnp.concatenate(
            [jnp.concatenate([slab_sc[c][s * PITCH:s * PITCH + HALF, :]
                              for c in range(slabs_per_tile * q, slabs_per_tile * (q + 1))], axis=1)
             for s in range(nseq)], axis=0)
        yb = _dot(sg["xn"], w_in_ref[:, yb_col0 + q * GATE_TILE:yb_col0 + (q + 1) * GATE_TILE])
        st[q]["out_b"] = (hs * _gelu(yb)).astype(jnp.bfloat16)

    def lru_mix(q):
        r0 = D_SGU + q * GATE_TILE
        sg["acc"] = sg["acc"] + _dot(st[q].pop("out_b"), w_out_ref[r0:r0 + GATE_TILE, :])

    def sgu_u():
        sg["u"] = _gelu(_dot(sg["xn"], w_in_ref[:, 0:D_SGU]))

    def sgu_v():
        v = _layer_norm(_gelu(_dot(sg["xn"], w_in_ref[:, D_SGU:2 * D_SGU])), sgu_g_ref[...], sgu_bn_ref[...])
        sg["vb"] = v.astype(jnp.bfloat16)

    def sgu_gate():
        vb = sg.pop("vb")
        lane = lax.broadcasted_iota(jnp.int32, (HALF, LANES), 1)
        bias = sgu_bias_ref[pl.ds(pl.multiple_of(parity * HALF, HALF), HALF), :]
        v_full = [jnp.concatenate([vprev_sc[1 - parity, s], vb[s * HALF:(s + 1) * HALF]], axis=0)
                  for s in range(nseq)]
        vprev_sc[parity] = vb.reshape(nseq, HALF, D_SGU)
        gate_rows = [[] for _ in range(nseq)]
        for s in range(0, nseq, 2):
            for g in range(N_LANE_GROUPS):
                lanes = slice(g * LANES, (g + 1) * LANES)
                rhs = jnp.concatenate([v_full[s][:, lanes], v_full[s + 1][:, lanes]], axis=1)
                res = _dot(sgu_lhs_sc[parity, g], rhs)
                for i in range(2):
                    part = res[:, i * LANES:(i + 1) * LANES]
                    gate_rows[s + i].append(jnp.where(lane < SGU_HEAD_DIM, part[:HALF], part[HALF:]))
        gate = jnp.concatenate([jnp.concatenate(r, axis=1) + bias for r in gate_rows], axis=0)
        sg["out_a"] = (sg.pop("u") * gate).astype(jnp.bfloat16)

    def sgu_mix():
        sg["acc"] = sg["x"] + _dot(sg.pop("out_a"), w_out_ref[0:D_SGU, :])

    def mix_store():
        ymix_sc[parity] = sg["acc"]

    tiles = range(N_GATE_TILES)
    mlp_norm()
    mix_norm()
    for q in tiles:
        lru_proj(q)
    mlp_up(0)
    for q in tiles:
        lru_conv(q)
        lru_gates(q)
    mlp_up(1)
    sgu_u()
    sgu_v()
    for q in tiles:
        lru_coeffs(q)
        lru_scan(q)
    mlp_down(0)
    mlp_down(1)
    for q in tiles:
        lru_out(q)
    mlp_up(2)
    mlp_up(3)
    sgu_gate()
    sgu_mix()
    mlp_down(2)
    mlp_down(3)
    for q in tiles:
        lru_mix(q)
    mix_store()
    mlp_store()


def _const_spec(shape):
    nd = len(shape)
    return pl.BlockSpec(shape, lambda *_: (0,) * nd, pipeline_mode=pl.Buffered(1))


def _layer_spec(arr, l):
    nd = arr.ndim - 1
    return pl.BlockSpec((None,) + arr.shape[1:], lambda *_: (l,) + (0,) * nd, pipeline_mode=pl.Buffered(1))


_BIG_WEIGHTS = ("w_in", "w_out", "w1", "w2")
_PROMPT_PARAMS = ("g_mix", "w_in", "conv_wb", "gate_rw", "gate_iw", "gate_rb", "gate_ib", "lam", "sgu_g", "sgu_bn", "sgu_w",
                  "sgu_bias", "w_out", "g_mlp", "w1", "w2")
_SAMPLE_MIXER_PARAMS = ("g_mix", "w_in", "conv_w", "conv_b", "gate_rw", "gate_iw", "gate_rb", "gate_ib", "lam", "sgu_g",
                        "sgu_bn", "sgu_w8", "sgu_b8", "w_out")


def _param_spec(p, k, l):
    return _layer_spec(p[k], 0 if k in _BIG_WEIGHTS else l)


def _layer_prompt(x, xs, p, gf, l, *, final_norm, cast_next=()):
    nseq, seq, _ = x.shape
    n_tiles = seq // HALF
    rows = nseq * HALF
    assert xs.shape == (2 * rows, D_MODEL)
    consts = [p[k] for k in _PROMPT_PARAMS]
    cast_in_specs, cast_out_specs, cast_shapes = [], [], []
    for w in cast_next:
        k_rows, n_cols = w.shape[1:]
        blk = k_rows // n_tiles
        assert blk * n_tiles == k_rows and blk % (2 * SUBLANES) == 0
        cast_in_specs.append(pl.BlockSpec((None, blk, n_cols), lambda j: (l + 1, jnp.minimum(j, n_tiles - 1), 0)))
        cast_out_specs.append(pl.BlockSpec((None, blk, n_cols), lambda j: (0, jnp.minimum(j, n_tiles - 1), 0)))
        cast_shapes.append(jax.ShapeDtypeStruct((1, k_rows, n_cols), jnp.bfloat16))
    x_spec = pl.BlockSpec((nseq, HALF, D_MODEL), lambda j: (0, jnp.minimum(j, n_tiles - 1), 0))
    y_spec = pl.BlockSpec((nseq, HALF, D_MODEL), lambda j: (0, jnp.clip(j - 1, 0, n_tiles - 1), 0))
    xs_spec = pl.BlockSpec((rows, D_MODEL), lambda j: (jnp.where(j > n_tiles, 1, 0), 0),
                           pipeline_mode=pl.Buffered(1))
    y, ys, hlast, convnew, *cast = pl.pallas_call(
        functools.partial(_layer_prompt_kernel, nseq=nseq, n_tiles=n_tiles, final_norm=final_norm,
                          n_cast=len(cast_next)),
        grid=(n_tiles + 2,),
        in_specs=([x_spec, xs_spec] + [_param_spec(p, k, l) for k in _PROMPT_PARAMS] + [_const_spec(gf.shape)]
                  + cast_in_specs),
        out_specs=([y_spec, xs_spec, _const_spec((nseq, D_LRU)), _const_spec((CONV_W - 1, nseq, D_LRU))]
                   + cast_out_specs),
        out_shape=[jax.ShapeDtypeStruct(x.shape, jnp.float32),
                   jax.ShapeDtypeStruct(xs.shape, jnp.float32),
                   jax.ShapeDtypeStruct((nseq, D_LRU), jnp.float32),
                   jax.ShapeDtypeStruct((CONV_W - 1, nseq, D_LRU), jnp.float32)] + cast_shapes,
        scratch_shapes=([pltpu.VMEM((2, nseq, HALF, D_SGU), jnp.bfloat16),
                         pltpu.VMEM((2, N_LANE_GROUPS, CHUNK, CHUNK), jnp.bfloat16),
                         pltpu.VMEM((2, rows, D_MODEL), jnp.float32),
                         pltpu.VMEM((N_GATE_TILES, GATE_TILE, 2 * GATE_TILE), jnp.bfloat16)]
                        + [pltpu.VMEM((nseq * PITCH, LANES), jnp.float32)] * N_SLABS
                        + [pltpu.VMEM((CONV_W - 1, nseq, LANES), jnp.float32)] * N_SLABS
                        + [pltpu.VMEM((nseq, LANES), jnp.float32)] * N_SLABS),
        compiler_params=pltpu.CompilerParams(dimension_semantics=("arbitrary",), vmem_limit_bytes=VMEM_LIMIT),
        name="layer_prompt",
    )(x, xs, *consts, gf, *cast_next)
    return y, ys, hlast, jnp.transpose(convnew, (1, 0, 2)), cast


def _mixer_sample_kernel(*refs, steps, nb, seq_major):
    n_x = 1 if seq_major else steps
    x_refs, refs = refs[:n_x], refs[n_x:]
    cbuf_refs, refs = refs[:CONV_W - 1], refs[CONV_W - 1:]
    (h0_ref, g_mix_ref, w_in_ref, conv_w_ref, conv_b_ref, gate_rw_ref, gate_iw_ref, gate_rb_ref, gate_ib_ref, lam_ref,
     sgu_g_ref, sgu_bn_ref, sgu_w8_ref, sgu_b8_ref, w_out_ref, y_ref, v_ref, hlast_ref, convnew_ref,
     gate_sc, *relay_sc) = refs
    _build_gate_tiles(gate_rw_ref, gate_iw_ref, gate_sc)
    if seq_major:
        for c in range(D_MODEL // LANES):
            relay_sc[c][...] = x_refs[0][:, c * LANES:(c + 1) * LANES]
        x = jnp.concatenate(
            [jnp.concatenate([relay_sc[c][pl.ds(t, nb, stride=steps), :] for c in range(D_MODEL // LANES)], axis=1)
             for t in range(steps)], axis=0)
    else:
        x = jnp.concatenate([r[...] for r in x_refs], axis=0)
    xn = _rms_norm(x, g_mix_ref[...]).astype(jnp.bfloat16)

    u = _gelu(_dot(xn, w_in_ref[:, 0:D_SGU]))
    v = _layer_norm(_gelu(_dot(xn, w_in_ref[:, D_SGU:2 * D_SGU])), sgu_g_ref[...], sgu_bn_ref[...])
    v_ref[...] = v
    gate_rows = []
    for t in range(steps):
        acc = sgu_b8_ref[t:t + 1, :] + sgu_w8_ref[t, 0:1, :] * v[0:nb]
        for s in range(1, t + 1):
            acc = acc + sgu_w8_ref[t, s:s + 1, :] * v[s * nb:(s + 1) * nb]
        gate_rows.append(acc)
    out_a = (u * jnp.concatenate(gate_rows, axis=0)).astype(jnp.bfloat16)

    xb = _dot(xn, w_in_ref[:, 2 * D_SGU:2 * D_SGU + D_LRU])
    xp = jnp.concatenate([r[...] for r in cbuf_refs] + [xb], axis=0)
    xc = conv_b_ref[...] + conv_w_ref[0:1, :] * xp[0:steps * nb]
    for k in range(1, CONV_W):
        xc = xc + conv_w_ref[k:k + 1, :] * xp[k * nb:(k + steps) * nb]
    convnew_ref[...] = xp[steps * nb:]

    a, b = _lru_coeffs(xc, gate_sc, gate_rb_ref[...], gate_ib_ref[...], lam_ref[...])
    h = h0_ref[...]
    hs = []
    for t in range(steps):
        h = a[t * nb:(t + 1) * nb] * h + b[t * nb:(t + 1) * nb]
        hs.append(h)
    hlast_ref[...] = h

    yb = _dot(xn, w_in_ref[:, 2 * D_SGU + D_LRU:])
    out_b = (jnp.concatenate(hs, axis=0) * _gelu(yb)).astype(jnp.bfloat16)
    out = _dot(jnp.concatenate([out_a, out_b], axis=1), w_out_ref[...])
    y_ref[...] = x + out


def _mixer_sample(x, h0_all, cbuf_all, p, l, *, seq_major):
    consts = [p[k] for k in _SAMPLE_MIXER_PARAMS]
    nb = h0_all.shape[1]
    steps = x.shape[0] // nb
    rows = steps * nb
    if seq_major:
        x_specs = [_const_spec(x.shape)]
        relay = [pltpu.VMEM((rows, LANES), jnp.float32)] * (D_MODEL // LANES)
    else:
        x_specs = [pl.BlockSpec((nb, D_MODEL), lambda i, t=t: (t, 0), pipeline_mode=pl.Buffered(1))
                   for t in range(steps)]
        relay = []
    cbuf_specs = [pl.BlockSpec((None, nb, D_LRU), lambda i, k=k: (l, k, 0), pipeline_mode=pl.Buffered(1))
                  for k in range(CONV_W - 1)]
    out_shapes = [(rows, D_MODEL), (rows, D_SGU), (nb, D_LRU), ((CONV_W - 1) * nb, D_LRU)]
    return pl.pallas_call(
        functools.partial(_mixer_sample_kernel, steps=steps, nb=nb, seq_major=seq_major),
        grid=(1,),
        in_specs=(x_specs + cbuf_specs + [_layer_spec(h0_all, l)]
                  + [_param_spec(p, k, l) for k in _SAMPLE_MIXER_PARAMS]),
        out_specs=[_const_spec(s) for s in out_shapes],
        out_shape=[jax.ShapeDtypeStruct(s, jnp.float32) for s in out_shapes],
        scratch_shapes=[pltpu.VMEM((N_GATE_TILES, GATE_TILE, 2 * GATE_TILE), jnp.bfloat16)] + relay,
        compiler_params=pltpu.CompilerParams(dimension_semantics=("arbitrary",), vmem_limit_bytes=VMEM_LIMIT),
        name="mixer_sample",
    )(*([x] * len(x_specs)), *([cbuf_all] * (CONV_W - 1)), h0_all, *consts)


CAST_STEPS = 8


def _cast_kernel(*refs):
    n = len(refs) // 2
    for w_f32, w_bf16 in zip(refs[:n], refs[n:]):
        w_bf16[...] = w_f32[...].astype(jnp.bfloat16)


def _cast_layer(weights, l):
    in_specs, out_specs, out_shapes = [], [], []
    for w in weights:
        k_rows, n_cols = w.shape[1:]
        blk = k_rows // CAST_STEPS
        assert blk * CAST_STEPS == k_rows and blk % (2 * SUBLANES) == 0
        in_specs.append(pl.BlockSpec((None, blk, n_cols), lambda j: (l, j, 0)))
        out_specs.append(pl.BlockSpec((None, blk, n_cols), lambda j: (0, j, 0)))
        out_shapes.append(jax.ShapeDtypeStruct((1, k_rows, n_cols), jnp.bfloat16))
    return pl.pallas_call(
        _cast_kernel,
        grid=(CAST_STEPS,),
        in_specs=in_specs,
        out_specs=out_specs,
        out_shape=out_shapes,
        compiler_params=pltpu.CompilerParams(dimension_semantics=("arbitrary",), vmem_limit_bytes=VMEM_LIMIT),
        name="cast_weights",
    )(*weights)


def _prepare_params(steps, nseq, norm_mix_g, w_in, conv_w, conv_b, gate_r_w, gate_r_b, gate_i_w, gate_i_b,
                    lru_lambda, sgu_norm_g, sgu_norm_b, sgu_w, sgu_b, w_out, norm_mlp_g, mlp_w1, mlp_w2):
    row = lambda a: a.reshape(DEPTH, 1, -1)
    conv_wb = jnp.concatenate([conv_w, conv_b[:, None, :]], axis=1)
    return {
        "g_mix": row(norm_mix_g),
        "conv_w": conv_w,
        "conv_b": row(conv_b),
        "conv_wb": jnp.broadcast_to(conv_wb.reshape(DEPTH, CONV_W + 1, N_SLABS, 1, LANES),
                                    (DEPTH, CONV_W + 1, N_SLABS, nseq, LANES)),
        "gate_rw": gate_r_w.reshape(DEPTH, N_GATE_TILES, GATE_TILE, LRU_BLOCK_DIM),
        "gate_iw": gate_i_w.reshape(DEPTH, N_GATE_TILES, GATE_TILE, LRU_BLOCK_DIM),
        "gate_rb": row(gate_r_b),
        "gate_ib": row(gate_i_b),
        "lam": row(lru_lambda),
        "sgu_g": row(sgu_norm_g),
        "sgu_bn": row(sgu_norm_b),
        "sgu_w": sgu_w,
        "sgu_bias": jnp.repeat(jnp.transpose(sgu_b, (0, 2, 1)), SGU_HEAD_DIM, axis=2),
        "sgu_w8": jnp.repeat(jnp.transpose(sgu_w[:, :, :steps, :steps], (0, 2, 3, 1)), SGU_HEAD_DIM, axis=3),
        "sgu_b8": jnp.repeat(jnp.transpose(sgu_b[:, :, :steps], (0, 2, 1)), SGU_HEAD_DIM, axis=2),
        "g_mlp": row(norm_mlp_g),
    }


def kernel(x_prompt, x_sample, state_lru_h, state_conv, norm_mix_g, w_in, conv_w, conv_b, gate_r_w, gate_r_b, gate_i_w, gate_i_b, lru_lambda, sgu_norm_g, sgu_norm_b, sgu_w, sgu_b, w_out, norm_mlp_g, mlp_w1, mlp_w2, final_norm_g):
    nseq, seq, _ = x_prompt.shape
    nb, steps, _ = x_sample.shape
    assert seq % CHUNK == 0 and nseq == SUBLANES and steps <= CHUNK
    gf = final_norm_g.reshape(1, -1)
    p = _prepare_params(steps, nseq, norm_mix_g, w_in, conv_w, conv_b, gate_r_w, gate_r_b, gate_i_w, gate_i_b,
                        lru_lambda, sgu_norm_g, sgu_norm_b, sgu_w, sgu_b, w_out, norm_mlp_g, mlp_w1, mlp_w2)

    xp = x_prompt
    xs = x_sample.reshape(nb * steps, D_MODEL)
    cbuf_tm = jnp.transpose(state_conv, (0, 2, 1, 3)).reshape(DEPTH, (CONV_W - 1) * nb, D_LRU)
    hs_p, convs_p, hs_s, convs_s, vs_s = [], [], [], [], []
    big_f32 = (w_in, w_out, mlp_w1, mlp_w2)
    big = _cast_layer(big_f32, 0)
    for l in range(DEPTH):
        last = l == DEPTH - 1
        p.update(zip(_BIG_WEIGHTS, big))
        xs, v_s, h_s, conv_s = _mixer_sample(xs, state_lru_h, cbuf_tm, p, l, seq_major=l == 0)
        xp, xs, h_p, conv_p, big = _layer_prompt(xp, xs, p, gf, l, final_norm=last,
                                                 cast_next=() if last else big_f32)
        hs_p.append(h_p)
        convs_p.append(conv_p)
        hs_s.append(h_s)
        convs_s.append(conv_s.reshape(CONV_W - 1, nb, D_LRU))
        vs_s.append(v_s.reshape(steps, nb, D_SGU))

    y_sample = jnp.transpose(xs.reshape(steps, nb, D_MODEL), (1, 0, 2))
    return (xp, y_sample, jnp.stack(hs_p), jnp.stack(convs_p), jnp.stack(hs_s),
            jnp.transpose(jnp.stack(convs_s), (0, 2, 1, 3)), jnp.transpose(jnp.stack(vs_s), (0, 2, 1, 3)))
```

```python
import functools

import jax
import jax.numpy as jnp
from jax import lax
from jax.experimental import pallas as pl
from jax.experimental.pallas import tpu as pltpu

D_MODEL = 1024
DEPTH = 2
SGU_HEADS = 8
SGU_HEAD_DIM = 64
D_SGU = SGU_HEADS * SGU_HEAD_DIM
CHUNK = 128
D_LRU = 1024
LRU_BLOCKS = 16
LRU_BLOCK_DIM = 64
CONV_W = 4
LRU_C = 8.0
D_FF = 4 * D_MODEL
EPS = 1e-6

SUBLANES = 8
LANES = 128
GATE_TILE = 256
N_GATE_TILES = D_LRU // GATE_TILE
HEADS_PER_LANE_GROUP = LANES // SGU_HEAD_DIM
N_LANE_GROUPS = D_SGU // LANES
N_SLABS = D_LRU // LANES
FF_CHUNK = D_MODEL
N_FF_CHUNKS = D_FF // FF_CHUNK

SAMPLE_GROUPS = 2
HALF = CHUNK // 2
PITCH = HALF + SUBLANES
VMEM_LIMIT = 62 * 1024 * 1024
F32_TINY = 1.1754944e-38


def _rms_norm(x, g):
    return x * lax.rsqrt(jnp.mean(x * x, axis=-1, keepdims=True) + EPS) * g


def _layer_norm(x, g, b):
    mu = jnp.mean(x, axis=-1, keepdims=True)
    xc = x - mu
    return xc * lax.rsqrt(jnp.mean(xc * xc, axis=-1, keepdims=True) + EPS) * g + b


def _gelu(x):
    return jax.nn.gelu(x, approximate=True)


def _dot(a, b):
    return jnp.dot(a, b, preferred_element_type=jnp.float32)


def _lru_elementwise(r_pre, i_pre, xc, half_rb, half_ib, half_c):
    th_r = jnp.tanh(r_pre + half_rb)
    th_i = jnp.tanh(i_pre + half_ib)
    log_a = half_c * th_r + half_c
    i = 0.5 * th_i + 0.5
    a = jnp.exp(log_a)
    y = jnp.tanh(log_a) * (-1.0 - a * a)
    b = (y * lax.rsqrt(jnp.maximum(y, F32_TINY))) * (i * xc)
    return a, b


def _lru_consts(gate_rb, gate_ib, lam):
    return 0.5 * gate_rb, 0.5 * gate_ib, (0.5 * LRU_C) * jax.nn.log_sigmoid(lam)


def _build_gate_tiles(gate_rw_ref, gate_iw_ref, gate_sc):
    per_tile = GATE_TILE // LRU_BLOCK_DIM
    k_idx = lax.broadcasted_iota(jnp.int32, (LRU_BLOCK_DIM, GATE_TILE), 0)
    n_idx = lax.broadcasted_iota(jnp.int32, (LRU_BLOCK_DIM, GATE_TILE), 1)
    replicate = (n_idx % LRU_BLOCK_DIM == k_idx).astype(jnp.bfloat16)
    row_blk = lax.broadcasted_iota(jnp.int32, (GATE_TILE, GATE_TILE), 0) // LRU_BLOCK_DIM
    col_blk = lax.broadcasted_iota(jnp.int32, (GATE_TILE, GATE_TILE), 1) // LRU_BLOCK_DIM
    for j in range(N_GATE_TILES):
        for k, ref in enumerate((gate_rw_ref, gate_iw_ref)):
            stacked = ref[per_tile * j:per_tile * (j + 1)].reshape(GATE_TILE, LRU_BLOCK_DIM)
            rep = _dot(stacked.astype(jnp.bfloat16), replicate)
            gate_sc[j, :, k * GATE_TILE:(k + 1) * GATE_TILE] = jnp.where(
                row_blk == col_blk, 0.5 * rep, 0.0).astype(jnp.bfloat16)


def _lru_coeffs(xc, gate_w_ref, gate_rb, gate_ib, lam):
    xcb = xc.astype(jnp.bfloat16)
    r_parts, i_parts = [], []
    for j in range(N_GATE_TILES):
        ri = _dot(xcb[:, j * GATE_TILE:(j + 1) * GATE_TILE], gate_w_ref[j])
        r_parts.append(ri[:, :GATE_TILE])
        i_parts.append(ri[:, GATE_TILE:])
    return _lru_elementwise(jnp.concatenate(r_parts, axis=1), jnp.concatenate(i_parts, axis=1), xc,
                            *_lru_consts(gate_rb, gate_ib, lam))


ROW_G_MIX, ROW_G_MLP, ROW_GATE_RB, ROW_GATE_IB, ROW_LAM, ROW_CONV_B, ROW_SGU, ROW_G_FINAL = range(8)
N_ROWS = 8


def _row_views(rows_ref):
    one = lambda k, lo=0, hi=D_MODEL: rows_ref.at[k:k + 1, lo:hi]
    return (one(ROW_G_MIX), one(ROW_G_MLP), one(ROW_GATE_RB), one(ROW_GATE_IB), one(ROW_LAM), one(ROW_CONV_B),
            one(ROW_SGU, 0, D_SGU), one(ROW_SGU, D_SGU, 2 * D_SGU), one(ROW_G_FINAL))


def _mlp_up(xn, w1_ref, c):
    cols = slice(c * FF_CHUNK, (c + 1) * FF_CHUNK)
    return jnp.square(jnp.maximum(_dot(xn, w1_ref[:, cols]), 0.0)).astype(jnp.bfloat16)


def _mlp_down(hid, w2_ref, c):
    return _dot(hid, w2_ref[c * FF_CHUNK:(c + 1) * FF_CHUNK, :])


def _layer_prompt_kernel(*refs, nseq, n_tiles, final_norm, n_cast):
    n_in = 2 + len(_PROMPT_PARAMS)
    (x_ref, xs_ref, rows_ref, w_in_ref, conv_wb_ref, gate_rw_ref, gate_iw_ref, sgu_w_ref, sgu_bias_ref, w_out_ref,
     w1_ref, w2_ref) = refs[:n_in]
    (g_mix_ref, g_mlp_ref, gate_rb_ref, gate_ib_ref, lam_ref, _, sgu_g_ref, sgu_bn_ref, gf_ref) = _row_views(rows_ref)
    cast_in, refs = refs[n_in:n_in + n_cast], refs[n_in + n_cast:]
    y_ref, ys_ref, hlast_ref, convnew_ref = refs[:4]
    cast_out, refs = refs[4:4 + n_cast], refs[4 + n_cast:]
    vprev_sc, sgu_lhs_sc, ymix_sc, gate_sc = refs[:4]
    slab_scratch = refs[4:]
    rows = nseq * HALF
    j = pl.program_id(0)
    slab_sc = slab_scratch[0:N_SLABS]
    tail_sc = slab_scratch[N_SLABS:2 * N_SLABS]
    h_sc = slab_scratch[2 * N_SLABS:3 * N_SLABS]

    @pl.when(j == 0)
    def _():
        for c in range(N_SLABS):
            tail_sc[c][...] = jnp.zeros_like(tail_sc[c])
            h_sc[c][...] = jnp.zeros_like(h_sc[c])
        vprev_sc[...] = jnp.zeros_like(vprev_sc)
        ymix_sc[1] = xs_ref[...]
        _build_gate_tiles(gate_rw_ref, gate_iw_ref, gate_sc)
        t_idx = lax.broadcasted_iota(jnp.int32, (CHUNK, CHUNK), 0)
        s_idx = lax.broadcasted_iota(jnp.int32, (CHUNK, CHUNK), 1)
        for g in range(N_LANE_GROUPS):
            first, second = [], []
            for hh in range(HEADS_PER_LANE_GROUP):
                wm = jnp.where(s_idx <= t_idx, sgu_w_ref[HEADS_PER_LANE_GROUP * g + hh], 0.0)
                first.append(pltpu.roll(wm[:HALF], HALF, axis=1))
                second.append(wm[HALF:])
            sgu_lhs_sc[0, g] = jnp.concatenate(first, axis=0).astype(jnp.bfloat16)
            sgu_lhs_sc[1, g] = jnp.concatenate(second, axis=0).astype(jnp.bfloat16)

    parity = j % 2

    @pl.when(j < n_tiles)
    def _():
        _fused_step(x_ref, g_mix_ref, w_in_ref, conv_wb_ref, gate_rb_ref, gate_ib_ref, lam_ref, sgu_g_ref, sgu_bn_ref,
                    sgu_bias_ref, w_out_ref, g_mlp_ref, w1_ref, w2_ref, gf_ref, y_ref, vprev_sc, sgu_lhs_sc, ymix_sc,
                    gate_sc, slab_sc, tail_sc, h_sc, parity=parity, nseq=nseq, final_norm=final_norm)
        for w_f32, w_bf16 in zip(cast_in, cast_out):
            w_bf16[...] = w_f32[...].astype(jnp.bfloat16)

    @pl.when(j == 0)
    def _():
        ys_ref[...] = y_ref[...].reshape(rows, D_MODEL)

    @pl.when(j == n_tiles - 1)
    def _():
        for c in range(N_SLABS):
            cols = slice(c * LANES, (c + 1) * LANES)
            hlast_ref[:, cols] = h_sc[c][...]
            for k in range(CONV_W - 1):
                convnew_ref[CONV_W - 2 - k, :, cols] = tail_sc[c][parity, k]

    @pl.when(j == n_tiles + 1)
    def _():
        ymix_sc[1 - parity] = xs_ref[...]

    @pl.when(j >= n_tiles)
    def _():
        xm = ymix_sc[1 - parity]
        xn = _rms_norm(xm, g_mlp_ref[...]).astype(jnp.bfloat16)
        acc = xm
        for c in range(N_FF_CHUNKS):
            acc = acc + _mlp_down(_mlp_up(xn, w1_ref, c), w2_ref, c)
        ymix_sc[parity] = _rms_norm(acc, gf_ref[...]) if final_norm else acc

    @pl.when(j == n_tiles)
    def _():
        y_ref[...] = ymix_sc[parity].reshape(nseq, HALF, D_MODEL)

    @pl.when(j == n_tiles + 1)
    def _():
        ys_ref[...] = ymix_sc[parity]


def _fused_step(x_ref, g_mix_ref, w_in_ref, conv_wb_ref, gate_rb_ref, gate_ib_ref, lam_ref, sgu_g_ref, sgu_bn_ref,
                sgu_bias_ref, w_out_ref, g_mlp_ref, w1_ref, w2_ref, gf_ref, y_ref, vprev_sc, sgu_lhs_sc, ymix_sc,
                gate_sc, slab_sc, tail_sc, h_sc, *, parity, nseq, final_norm):
    rows = nseq * HALF
    half_rb, half_ib, half_c = _lru_consts(gate_rb_ref[...], gate_ib_ref[...], lam_ref[...])
    xb_col0 = 2 * D_SGU
    yb_col0 = 2 * D_SGU + D_LRU
    slabs_per_tile = GATE_TILE // LANES
    st = [{} for _ in range(N_GATE_TILES)]
    sg = {}
    ml = {}

    def mlp_norm():
        xm = ymix_sc[1 - parity]
        ml["acc"] = xm
        ml["xn"] = _rms_norm(xm, g_mlp_ref[...]).astype(jnp.bfloat16)

    def mlp_up(c):
        ml["hid", c] = _mlp_up(ml["xn"], w1_ref, c)

    def mlp_down(c):
        ml["acc"] = ml["acc"] + _mlp_down(ml.pop(("hid", c)), w2_ref, c)

    def mlp_store():
        out = _rms_norm(ml["acc"], gf_ref[...]) if final_norm else ml["acc"]
        y_ref[...] = out.reshape(nseq, HALF, D_MODEL)

    def mix_norm():
        sg["x"] = x_ref[...].reshape(rows, D_MODEL)
        sg["xn"] = _rms_norm(sg["x"], g_mix_ref[...]).astype(jnp.bfloat16)

    def lru_proj(q):
        xb = _dot(sg["xn"], w_in_ref[:, xb_col0 + q * GATE_TILE:xb_col0 + (q + 1) * GATE_TILE])
        for i in range(slabs_per_tile):
            for s in range(nseq):
                slab_sc[slabs_per_tile * q + i][s * PITCH:s * PITCH + HALF, :] = (
                    xb[s * HALF:(s + 1) * HALF, i * LANES:(i + 1) * LANES])

    def lru_conv(q):
        xc_slabs = []
        for c in range(slabs_per_tile * q, slabs_per_tile * (q + 1)):
            taps = [conv_wb_ref[k, c] for k in range(CONV_W + 1)]
            p1, p2, p3 = (tail_sc[c][1 - parity, k] for k in range(CONV_W - 1))
            steps_out = []
            for t in range(HALF):
                cur = slab_sc[c][pl.ds(t, nseq, stride=PITCH), :]
                steps_out.append(taps[CONV_W] + taps[3] * cur + taps[2] * p1 + taps[1] * p2 + taps[0] * p3)
                p1, p2, p3 = cur, p1, p2
            for k, pk in enumerate((p1, p2, p3)):
                tail_sc[c][parity, k] = pk
            xc_slabs.append(jnp.concatenate(steps_out, axis=0))
        st[q]["xc"] = jnp.concatenate(xc_slabs, axis=1)

    def lru_gates(q):
        st[q]["ri"] = _dot(st[q]["xc"].astype(jnp.bfloat16), gate_sc[q])

    def lru_coeffs(q):
        cols = slice(q * GATE_TILE, (q + 1) * GATE_TILE)
        ri = st[q].pop("ri")
        st[q]["ab"] = _lru_elementwise(ri[:, :GATE_TILE], ri[:, GATE_TILE:], st[q].pop("xc"),
                                       half_rb[:, cols], half_ib[:, cols], half_c[:, cols])

    def lru_scan(q):
        a, b = st[q].pop("ab")
        for i in range(slabs_per_tile):
            c = slabs_per_tile * q + i
            h = h_sc[c][...]
            for t in range(HALF):
                h = (a[t * nseq:(t + 1) * nseq, i * LANES:(i + 1) * LANES] * h
                     + b[t * nseq:(t + 1) * nseq, i * LANES:(i + 1) * LANES])
                slab_sc[c][pl.ds(t, nseq, stride=PITCH), :] = h
            h_sc[c][...] = h

    def lru_out(q):
        hs = jnp.concatenate(
            [jnp.concatenate([slab_sc[c][s * PITCH:s * PITCH + HALF, :]
                              for c in range(slabs_per_tile * q, slabs_per_tile * (q + 1))], axis=1)
             for s in range(nseq)], axis=0)
        yb = _dot(sg["xn"], w_in_ref[:, yb_col0 + q * GATE_TILE:yb_col0 + (q + 1) * GATE_TILE])
        st[q]["out_b"] = (hs * _gelu(yb)).astype(jnp.bfloat16)

    def lru_mix(q):
        r0 = D_SGU + q * GATE_TILE
        sg["acc"] = sg["acc"] + _dot(st[q].pop("out_b"), w_out_ref[r0:r0 + GATE_TILE, :])

    def sgu_u():
        sg["u"] = _gelu(_dot(sg["xn"], w_in_ref[:, 0:D_SGU]))

    def sgu_v():
        v = _layer_norm(_gelu(_dot(sg["xn"], w_in_ref[:, D_SGU:2 * D_SGU])), sgu_g_ref[...], sgu_bn_ref[...])
        sg["vb"] = v.astype(jnp.bfloat16)

    def sgu_gate():
        vb = sg.pop("vb")
        lane = lax.broadcasted_iota(jnp.int32, (HALF, LANES), 1)
        bias = sgu_bias_ref[pl.ds(pl.multiple_of(parity * HALF, HALF), HALF), :]
        v_full = [jnp.concatenate([vprev_sc[1 - parity, s], vb[s * HALF:(s + 1) * HALF]], axis=0)
                  for s in range(nseq)]
        vprev_sc[parity] = vb.reshape(nseq, HALF, D_SGU)
        gate_rows = [[] for _ in range(nseq)]
        for s in range(0, nseq, 2):
            for g in range(N_LANE_GROUPS):
                lanes = slice(g * LANES, (g + 1) * LANES)
                rhs = jnp.concatenate([v_full[s][:, lanes], v_full[s + 1][:, lanes]], axis=1)
                res = _dot(sgu_lhs_sc[parity, g], rhs)
                for i in range(2):
                    part = res[:, i * LANES:(i + 1) * LANES]
                    gate_rows[s + i].append(jnp.where(lane < SGU_HEAD_DIM, part[:HALF], part[HALF:]))
        gate = jnp.concatenate([jnp.concatenate(r, axis=1) + bias for r in gate_rows], axis=0)
        sg["out_a"] = (sg.pop("u") * gate).astype(jnp.bfloat16)

    def sgu_mix():
        sg["acc"] = sg["x"] + _dot(sg.pop("out_a"), w_out_ref[0:D_SGU, :])

    def mix_store():
        ymix_sc[parity] = sg["acc"]

    tiles = range(N_GATE_TILES)
    mlp_norm()
    mix_norm()
    for q in tiles:
        lru_proj(q)
    mlp_up(0)
    for q in tiles:
        lru_conv(q)
        lru_gates(q)
    mlp_up(1)
    sgu_u()
    sgu_v()
    for q in tiles:
        lru_coeffs(q)
        lru_scan(q)
    mlp_down(0)
    mlp_down(1)
    for q in tiles:
        lru_out(q)
    mlp_up(2)
    mlp_up(3)
    sgu_gate()
    sgu_mix()
    mlp_down(2)
    mlp_down(3)
    for q in tiles:
        lru_mix(q)
    mix_store()
    mlp_store()


def _const_spec(shape):
    nd = len(shape)
    return pl.BlockSpec(shape, lambda *_: (0,) * nd, pipeline_mode=pl.Buffered(1))


def _layer_spec(arr, l):
    nd = arr.ndim - 1
    return pl.BlockSpec((None,) + arr.shape[1:], lambda *_: (l,) + (0,) * nd, pipeline_mode=pl.Buffered(1))


_BIG_WEIGHTS = ("w_in", "w_out", "w1", "w2")
_PROMPT_PARAMS = ("rows", "w_in", "conv_wb", "gate_rw", "gate_iw", "sgu_w", "sgu_bias", "w_out", "w1", "w2")
_SAMPLE_MIXER_PARAMS = ("rows", "w_in", "conv_w", "gate_rw", "gate_iw", "sgu_w8", "sgu_b8", "w_out")


def _param_spec(p, k, l):
    return _layer_spec(p[k], 0 if k in _BIG_WEIGHTS else l)


def _layer_prompt(x, xs, p, l, *, final_norm, cast_next=()):
    nseq, seq, _ = x.shape
    n_tiles = seq // HALF
    rows = nseq * HALF
    assert xs.shape == (2 * rows, D_MODEL)
    consts = [p[k] for k in _PROMPT_PARAMS]
    cast_in_specs, cast_out_specs, cast_shapes = [], [], []
    for w in cast_next:
        k_rows, n_cols = w.shape[1:]
        blk = k_rows // n_tiles
        assert blk * n_tiles == k_rows and blk % (2 * SUBLANES) == 0
        cast_in_specs.append(pl.BlockSpec((None, blk, n_cols), lambda j: (l + 1, jnp.minimum(j, n_tiles - 1), 0)))
        cast_out_specs.append(pl.BlockSpec((None, blk, n_cols), lambda j: (0, jnp.minimum(j, n_tiles - 1), 0)))
        cast_shapes.append(jax.ShapeDtypeStruct((1, k_rows, n_cols), jnp.bfloat16))
    x_spec = pl.BlockSpec((nseq, HALF, D_MODEL), lambda j: (0, jnp.minimum(j, n_tiles - 1), 0))
    y_spec = pl.BlockSpec((nseq, HALF, D_MODEL), lambda j: (0, jnp.clip(j - 1, 0, n_tiles - 1), 0))
    xs_spec = pl.BlockSpec((rows, D_MODEL), lambda j: (jnp.where(j > n_tiles, 1, 0), 0),
                           pipeline_mode=pl.Buffered(1))
    y, ys, hlast, convnew, *cast = pl.pallas_call(
        functools.partial(_layer_prompt_kernel, nseq=nseq, n_tiles=n_tiles, final_norm=final_norm,
                          n_cast=len(cast_next)),
        grid=(n_tiles + 2,),
        in_specs=[x_spec, xs_spec] + [_param_spec(p, k, l) for k in _PROMPT_PARAMS] + cast_in_specs,
        out_specs=([y_spec, xs_spec, _const_spec((nseq, D_LRU)), _const_spec((CONV_W - 1, nseq, D_LRU))]
                   + cast_out_specs),
        out_shape=[jax.ShapeDtypeStruct(x.shape, jnp.float32),
                   jax.ShapeDtypeStruct(xs.shape, jnp.float32),
                   jax.ShapeDtypeStruct((nseq, D_LRU), jnp.float32),
                   jax.ShapeDtypeStruct((CONV_W - 1, nseq, D_LRU), jnp.float32)] + cast_shapes,
        scratch_shapes=([pltpu.VMEM((2, nseq, HALF, D_SGU), jnp.bfloat16),
                         pltpu.VMEM((2, N_LANE_GROUPS, CHUNK, CHUNK), jnp.bfloat16),
                         pltpu.VMEM((2, rows, D_MODEL), jnp.float32),
                         pltpu.VMEM((N_GATE_TILES, GATE_TILE, 2 * GATE_TILE), jnp.bfloat16)]
                        + [pltpu.VMEM((nseq * PITCH, LANES), jnp.float32)] * N_SLABS
                        + [pltpu.VMEM((2, CONV_W - 1, nseq, LANES), jnp.float32)] * N_SLABS
                        + [pltpu.VMEM((nseq, LANES), jnp.float32)] * N_SLABS),
        compiler_params=pltpu.CompilerParams(dimension_semantics=("arbitrary",), vmem_limit_bytes=VMEM_LIMIT),
        name="layer_prompt",
    )(x, xs, *consts, *cast_next)
    return y, ys, hlast, jnp.transpose(convnew, (1, 0, 2)), cast


def _mixer_sample_kernel(x_ref, cbuf_ref, h0_ref, rows_ref, w_in_ref, conv_w_ref, gate_rw_ref, gate_iw_ref,
                         sgu_w8_ref, sgu_b8_ref, w_out_ref, y_ref, v_ref, hlast_ref, convnew_ref, gate_sc, *relay_sc,
                         steps, nb, seq_major):
    (g_mix_ref, _, gate_rb_ref, gate_ib_ref, lam_ref, conv_b_ref, sgu_g_ref, sgu_bn_ref, _) = _row_views(rows_ref)

    @pl.when(pl.program_id(0) == 0)
    def _():
        _build_gate_tiles(gate_rw_ref, gate_iw_ref, gate_sc)

    if seq_major:
        for c in range(D_MODEL // LANES):
            relay_sc[c][...] = x_ref[:, c * LANES:(c + 1) * LANES]
        x = jnp.concatenate(
            [jnp.concatenate([relay_sc[c][pl.ds(t, nb, stride=steps), :] for c in range(D_MODEL // LANES)], axis=1)
             for t in range(steps)], axis=0)
    else:
        x = x_ref[...].reshape(steps * nb, D_MODEL)
    xn = _rms_norm(x, g_mix_ref[...]).astype(jnp.bfloat16)

    u = _gelu(_dot(xn, w_in_ref[:, 0:D_SGU]))
    v = _layer_norm(_gelu(_dot(xn, w_in_ref[:, D_SGU:2 * D_SGU])), sgu_g_ref[...], sgu_bn_ref[...])
    v_ref[...] = v.reshape(steps, nb, D_SGU)
    gate_rows = []
    for t in range(steps):
        acc = sgu_b8_ref[t:t + 1, :] + sgu_w8_ref[t, 0:1, :] * v[0:nb]
        for s in range(1, t + 1):
            acc = acc + sgu_w8_ref[t, s:s + 1, :] * v[s * nb:(s + 1) * nb]
        gate_rows.append(acc)
    out_a = (u * jnp.concatenate(gate_rows, axis=0)).astype(jnp.bfloat16)

    xb = _dot(xn, w_in_ref[:, 2 * D_SGU:2 * D_SGU + D_LRU])
    xp = jnp.concatenate([cbuf_ref[k] for k in range(CONV_W - 1)] + [xb], axis=0)
    xc = conv_b_ref[...] + conv_w_ref[0:1, :] * xp[0:steps * nb]
    for k in range(1, CONV_W):
        xc = xc + conv_w_ref[k:k + 1, :] * xp[k * nb:(k + steps) * nb]
    convnew_ref[...] = xp[steps * nb:].reshape(CONV_W - 1, nb, D_LRU)

    a, b = _lru_coeffs(xc, gate_sc, gate_rb_ref[...], gate_ib_ref[...], lam_ref[...])
    h = h0_ref[...]
    hs = []
    for t in range(steps):
        h = a[t * nb:(t + 1) * nb] * h + b[t * nb:(t + 1) * nb]
        hs.append(h)
    hlast_ref[...] = h

    yb = _dot(xn, w_in_ref[:, 2 * D_SGU + D_LRU:])
    out_b = (jnp.concatenate(hs, axis=0) * _gelu(yb)).astype(jnp.bfloat16)
    out = _dot(jnp.concatenate([out_a, out_b], axis=1), w_out_ref[...])
    y_ref[...] = (x + out).reshape(steps, nb, D_MODEL)


def _mixer_sample(x, h0_all, cbuf_all, p, l, *, seq_major):
    nb_all = h0_all.shape[1]
    steps = x.shape[0] // nb_all if seq_major else x.shape[0]
    nb = nb_all // SAMPLE_GROUPS
    if seq_major:
        x_spec = pl.BlockSpec((nb * steps, D_MODEL), lambda i: (i, 0))
        relay = [pltpu.VMEM((nb * steps, LANES), jnp.float32)] * (D_MODEL // LANES)
    else:
        x_spec = pl.BlockSpec((steps, nb, D_MODEL), lambda i: (0, i, 0))
        relay = []
    group = lambda lead, width: pl.BlockSpec((lead, nb, width), lambda i: (0, i, 0))
    out_shapes = [(steps, nb_all, D_MODEL), (steps, nb_all, D_SGU), (nb_all, D_LRU), (CONV_W - 1, nb_all, D_LRU)]
    return pl.pallas_call(
        functools.partial(_mixer_sample_kernel, steps=steps, nb=nb, seq_major=seq_major),
        grid=(SAMPLE_GROUPS,),
        in_specs=([x_spec,
                   pl.BlockSpec((None, CONV_W - 1, nb, D_LRU), lambda i: (l, 0, i, 0)),
                   pl.BlockSpec((None, nb, D_LRU), lambda i: (l, i, 0))]
                  + [_param_spec(p, k, l) for k in _SAMPLE_MIXER_PARAMS]),
        out_specs=[group(steps, D_MODEL), group(steps, D_SGU), pl.BlockSpec((nb, D_LRU), lambda i: (i, 0)),
                   group(CONV_W - 1, D_LRU)],
        out_shape=[jax.ShapeDtypeStruct(s, jnp.float32) for s in out_shapes],
        scratch_shapes=[pltpu.VMEM((N_GATE_TILES, GATE_TILE, 2 * GATE_TILE), jnp.bfloat16)] + relay,
        compiler_params=pltpu.CompilerParams(dimension_semantics=("arbitrary",), vmem_limit_bytes=VMEM_LIMIT),
        name="mixer_sample",
    )(x, cbuf_all, h0_all, *[p[k] for k in _SAMPLE_MIXER_PARAMS])


CAST_STEPS = 8


def _cast_kernel(*refs):
    n = len(refs) // 2
    for w_f32, w_bf16 in zip(refs[:n], refs[n:]):
        w_bf16[...] = w_f32[...].astype(jnp.bfloat16)


def _cast_layer(weights, l):
    in_specs, out_specs, out_shapes = [], [], []
    for w in weights:
        k_rows, n_cols = w.shape[1:]
        blk = k_rows // CAST_STEPS
        assert blk * CAST_STEPS == k_rows and blk % (2 * SUBLANES) == 0
        in_specs.append(pl.BlockSpec((None, blk, n_cols), lambda j: (l, j, 0)))
        out_specs.append(pl.BlockSpec((None, blk, n_cols), lambda j: (0, j, 0)))
        out_shapes.append(jax.ShapeDtypeStruct((1, k_rows, n_cols), jnp.bfloat16))
    return pl.pallas_call(
        _cast_kernel,
        grid=(CAST_STEPS,),
        in_specs=in_specs,
        out_specs=out_specs,
        out_shape=out_shapes,
        compiler_params=pltpu.CompilerParams(dimension_semantics=("arbitrary",), vmem_limit_bytes=VMEM_LIMIT),
        name="cast_weights",
    )(*weights)


def _prepare_params(steps, nseq, norm_mix_g, w_in, conv_w, conv_b, gate_r_w, gate_r_b, gate_i_w, gate_i_b,
                    lru_lambda, sgu_norm_g, sgu_norm_b, sgu_w, sgu_b, w_out, norm_mlp_g, mlp_w1, mlp_w2, final_norm_g):
    conv_wb = jnp.concatenate([conv_w, conv_b[:, None, :]], axis=1)
    rows = [None] * N_ROWS
    rows[ROW_G_MIX], rows[ROW_G_MLP] = norm_mix_g, norm_mlp_g
    rows[ROW_GATE_RB], rows[ROW_GATE_IB], rows[ROW_LAM], rows[ROW_CONV_B] = gate_r_b, gate_i_b, lru_lambda, conv_b
    rows[ROW_SGU] = jnp.concatenate([sgu_norm_g, sgu_norm_b], axis=-1)
    rows[ROW_G_FINAL] = jnp.broadcast_to(final_norm_g, (DEPTH, D_MODEL))
    return {
        "rows": jnp.stack(rows, axis=1),
        "conv_w": conv_w,
        "conv_wb": jnp.broadcast_to(conv_wb.reshape(DEPTH, CONV_W + 1, N_SLABS, 1, LANES),
                                    (DEPTH, CONV_W + 1, N_SLABS, nseq, LANES)),
        "gate_rw": gate_r_w,
        "gate_iw": gate_i_w,
        "sgu_w": sgu_w,
        "sgu_bias": jnp.repeat(jnp.transpose(sgu_b, (0, 2, 1)), SGU_HEAD_DIM, axis=2),
        "sgu_w8": jnp.repeat(jnp.transpose(sgu_w[:, :, :steps, :steps], (0, 2, 3, 1)), SGU_HEAD_DIM, axis=3),
        "sgu_b8": jnp.repeat(jnp.transpose(sgu_b[:, :, :steps], (0, 2, 1)), SGU_HEAD_DIM, axis=2),
    }


def kernel(x_prompt, x_sample, state_lru_h, state_conv, norm_mix_g, w_in, conv_w, conv_b, gate_r_w, gate_r_b, gate_i_w, gate_i_b, lru_lambda, sgu_norm_g, sgu_norm_b, sgu_w, sgu_b, w_out, norm_mlp_g, mlp_w1, mlp_w2, final_norm_g):
    nseq, seq, _ = x_prompt.shape
    nb, steps, _ = x_sample.shape
    assert seq % CHUNK == 0 and nseq == SUBLANES and steps <= CHUNK
    p = _prepare_params(steps, nseq, norm_mix_g, w_in, conv_w, conv_b, gate_r_w, gate_r_b, gate_i_w, gate_i_b,
                        lru_lambda, sgu_norm_g, sgu_norm_b, sgu_w, sgu_b, w_out, norm_mlp_g, mlp_w1, mlp_w2,
                        final_norm_g)

    xp = x_prompt
    xs = x_sample.reshape(nb * steps, D_MODEL)
    cbuf_tm = jnp.transpose(state_conv, (0, 2, 1, 3))
    hs_p, convs_p, hs_s, convs_s, vs_s = [], [], [], [], []
    big_f32 = (w_in, w_out, mlp_w1, mlp_w2)
    big = _cast_layer(big_f32, 0)
    for l in range(DEPTH):
        last = l == DEPTH - 1
        p.update(zip(_BIG_WEIGHTS, big))
        xs, v_s, h_s, conv_s = _mixer_sample(xs, state_lru_h, cbuf_tm, p, l, seq_major=l == 0)
        xp, xs, h_p, conv_p, big = _layer_prompt(xp, xs.reshape(steps * nb, D_MODEL), p, l, final_norm=last,
                                                 cast_next=() if last else big_f32)
        xs = xs.reshape(steps, nb, D_MODEL)
        hs_p.append(h_p)
        convs_p.append(conv_p)
        hs_s.append(h_s)
        convs_s.append(conv_s)
        vs_s.append(v_s)

    y_sample = jnp.transpose(xs, (1, 0, 2))
    return (xp, y_sample, jnp.stack(hs_p), jnp.stack(convs_p), jnp.stack(hs_s),
            jnp.transpose(jnp.stack(convs_s), (0, 2, 1, 3)), jnp.transpose(jnp.stack(vs_s), (0, 2, 1, 3)))
```

```python
import functools

import jax
import jax.numpy as jnp
from jax import lax
from jax.experimental import pallas as pl
from jax.experimental.pallas import tpu as pltpu

D_MODEL = 1024
DEPTH = 2
SGU_HEADS = 8
SGU_HEAD_DIM = 64
D_SGU = SGU_HEADS * SGU_HEAD_DIM
CHUNK = 128
D_LRU = 1024
LRU_BLOCKS = 16
LRU_BLOCK_DIM = 64
CONV_W = 4
LRU_C = 8.0
D_FF = 4 * D_MODEL
EPS = 1e-6

SUBLANES = 8
LANES = 128
GATE_TILE = 256
N_GATE_TILES = D_LRU // GATE_TILE
HEADS_PER_LANE_GROUP = LANES // SGU_HEAD_DIM
N_LANE_GROUPS = D_SGU // LANES
N_SLABS = D_LRU // LANES
FF_CHUNK = D_MODEL
N_FF_CHUNKS = D_FF // FF_CHUNK

SAMPLE_GROUPS = 2
SAMPLE_GROUPS_CAST = 4
HALF = CHUNK // 2
PITCH = HALF + SUBLANES
VMEM_LIMIT = 62 * 1024 * 1024
F32_TINY = 1.1754944e-38


def _rms_norm(x, g):
    return x * lax.rsqrt(jnp.mean(x * x, axis=-1, keepdims=True) + EPS) * g


def _layer_norm(x, g, b):
    mu = jnp.mean(x, axis=-1, keepdims=True)
    xc = x - mu
    return xc * lax.rsqrt(jnp.mean(xc * xc, axis=-1, keepdims=True) + EPS) * g + b


def _gelu(x):
    return jax.nn.gelu(x, approximate=True)


def _dot(a, b):
    return jnp.dot(a, b, preferred_element_type=jnp.float32)


def _lru_elementwise(r_pre, i_pre, xc, half_rb, half_ib, half_c):
    th_r = jnp.tanh(r_pre + half_rb)
    th_i = jnp.tanh(i_pre + half_ib)
    log_a = half_c * th_r + half_c
    i = 0.5 * th_i + 0.5
    a = jnp.exp(log_a)
    y = jnp.tanh(log_a) * (-1.0 - a * a)
    b = (y * lax.rsqrt(jnp.maximum(y, F32_TINY))) * (i * xc)
    return a, b


def _lru_consts(gate_rb, gate_ib, lam):
    return 0.5 * gate_rb, 0.5 * gate_ib, (0.5 * LRU_C) * jax.nn.log_sigmoid(lam)


def _build_gate_tiles(gate_rw_ref, gate_iw_ref, gate_sc):
    per_tile = GATE_TILE // LRU_BLOCK_DIM
    k_idx = lax.broadcasted_iota(jnp.int32, (LRU_BLOCK_DIM, GATE_TILE), 0)
    n_idx = lax.broadcasted_iota(jnp.int32, (LRU_BLOCK_DIM, GATE_TILE), 1)
    replicate = (n_idx % LRU_BLOCK_DIM == k_idx).astype(jnp.bfloat16)
    row_blk = lax.broadcasted_iota(jnp.int32, (GATE_TILE, GATE_TILE), 0) // LRU_BLOCK_DIM
    col_blk = lax.broadcasted_iota(jnp.int32, (GATE_TILE, GATE_TILE), 1) // LRU_BLOCK_DIM
    for j in range(N_GATE_TILES):
        for k, ref in enumerate((gate_rw_ref, gate_iw_ref)):
            stacked = ref[per_tile * j:per_tile * (j + 1)].reshape(GATE_TILE, LRU_BLOCK_DIM)
            rep = _dot(stacked.astype(jnp.bfloat16), replicate)
            gate_sc[j, :, k * GATE_TILE:(k + 1) * GATE_TILE] = jnp.where(
                row_blk == col_blk, 0.5 * rep, 0.0).astype(jnp.bfloat16)


def _lru_coeffs(xc, gate_w_ref, gate_rb, gate_ib, lam):
    xcb = xc.astype(jnp.bfloat16)
    r_parts, i_parts = [], []
    for j in range(N_GATE_TILES):
        ri = _dot(xcb[:, j * GATE_TILE:(j + 1) * GATE_TILE], gate_w_ref[j])
        r_parts.append(ri[:, :GATE_TILE])
        i_parts.append(ri[:, GATE_TILE:])
    return _lru_elementwise(jnp.concatenate(r_parts, axis=1), jnp.concatenate(i_parts, axis=1), xc,
                            *_lru_consts(gate_rb, gate_ib, lam))


ROW_G_MIX, ROW_G_MLP, ROW_GATE_RB, ROW_GATE_IB, ROW_LAM, ROW_CONV_B, ROW_SGU, ROW_G_FINAL = range(8)
N_ROWS = 8


def _row_views(rows_ref):
    one = lambda k, lo=0, hi=D_MODEL: rows_ref.at[k:k + 1, lo:hi]
    return (one(ROW_G_MIX), one(ROW_G_MLP), one(ROW_GATE_RB), one(ROW_GATE_IB), one(ROW_LAM), one(ROW_CONV_B),
            one(ROW_SGU, 0, D_SGU), one(ROW_SGU, D_SGU, 2 * D_SGU), one(ROW_G_FINAL))


def _mlp_up(xn, w1_ref, c):
    cols = slice(c * FF_CHUNK, (c + 1) * FF_CHUNK)
    return jnp.square(jnp.maximum(_dot(xn, w1_ref[:, cols]), 0.0)).astype(jnp.bfloat16)


def _mlp_down(hid, w2_ref, c):
    return _dot(hid, w2_ref[c * FF_CHUNK:(c + 1) * FF_CHUNK, :])


def _layer_prompt_kernel(*refs, nseq, n_tiles, final_norm, n_cast):
    n_in = 2 + len(_PROMPT_PARAMS)
    (x_ref, xs_ref, rows_ref, w_in_ref, conv_wb_ref, gate_rw_ref, gate_iw_ref, sgu_w_ref, sgu_bias_ref, w_out_ref,
     w1_ref, w2_ref) = refs[:n_in]
    (g_mix_ref, g_mlp_ref, gate_rb_ref, gate_ib_ref, lam_ref, _, sgu_g_ref, sgu_bn_ref, gf_ref) = _row_views(rows_ref)
    cast_in, refs = refs[n_in:n_in + n_cast], refs[n_in + n_cast:]
    y_ref, ys_ref, hlast_ref, convnew_ref = refs[:4]
    cast_out, refs = refs[4:4 + n_cast], refs[4 + n_cast:]
    vprev_sc, sgu_lhs_sc, ymix_sc, gate_sc = refs[:4]
    slab_scratch = refs[4:]
    rows = nseq * HALF
    j = pl.program_id(0)
    slab_sc = slab_scratch[0:N_SLABS]
    tail_sc = slab_scratch[N_SLABS:2 * N_SLABS]
    h_sc = slab_scratch[2 * N_SLABS:3 * N_SLABS]

    @pl.when(j == 0)
    def _():
        for c in range(N_SLABS):
            tail_sc[c][...] = jnp.zeros_like(tail_sc[c])
            h_sc[c][...] = jnp.zeros_like(h_sc[c])
        vprev_sc[...] = jnp.zeros_like(vprev_sc)
        ymix_sc[1] = xs_ref[...]
        _build_gate_tiles(gate_rw_ref, gate_iw_ref, gate_sc)
        t_idx = lax.broadcasted_iota(jnp.int32, (CHUNK, CHUNK), 0)
        s_idx = lax.broadcasted_iota(jnp.int32, (CHUNK, CHUNK), 1)
        for g in range(N_LANE_GROUPS):
            first, second = [], []
            for hh in range(HEADS_PER_LANE_GROUP):
                wm = jnp.where(s_idx <= t_idx, sgu_w_ref[HEADS_PER_LANE_GROUP * g + hh], 0.0)
                first.append(pltpu.roll(wm[:HALF], HALF, axis=1))
                second.append(wm[HALF:])
            sgu_lhs_sc[0, g] = jnp.concatenate(first, axis=0).astype(jnp.bfloat16)
            sgu_lhs_sc[1, g] = jnp.concatenate(second, axis=0).astype(jnp.bfloat16)

    parity = j % 2

    @pl.when(j < n_tiles)
    def _():
        _fused_step(x_ref, g_mix_ref, w_in_ref, conv_wb_ref, gate_rb_ref, gate_ib_ref, lam_ref, sgu_g_ref, sgu_bn_ref,
                    sgu_bias_ref, w_out_ref, g_mlp_ref, w1_ref, w2_ref, gf_ref, y_ref, vprev_sc, sgu_lhs_sc, ymix_sc,
                    gate_sc, slab_sc, tail_sc, h_sc, parity=parity, nseq=nseq, final_norm=final_norm)
        for w_f32, w_bf16 in zip(cast_in, cast_out):
            w_bf16[...] = w_f32[...].astype(jnp.bfloat16)

    @pl.when(j == 0)
    def _():
        ys_ref[...] = y_ref[...].reshape(rows, D_MODEL)

    @pl.when(j == n_tiles - 1)
    def _():
        for c in range(N_SLABS):
            cols = slice(c * LANES, (c + 1) * LANES)
            hlast_ref[:, cols] = h_sc[c][...]
            for k in range(CONV_W - 1):
                convnew_ref[CONV_W - 2 - k, :, cols] = tail_sc[c][parity, k]

    @pl.when(j == n_tiles + 1)
    def _():
        ymix_sc[1 - parity] = xs_ref[...]

    @pl.when(j >= n_tiles)
    def _():
        xm = ymix_sc[1 - parity]
        xn = _rms_norm(xm, g_mlp_ref[...]).astype(jnp.bfloat16)
        acc = xm
        for c in range(N_FF_CHUNKS):
            acc = acc + _mlp_down(_mlp_up(xn, w1_ref, c), w2_ref, c)
        ymix_sc[parity] = _rms_norm(acc, gf_ref[...]) if final_norm else acc

    @pl.when(j == n_tiles)
    def _():
        y_ref[...] = ymix_sc[parity].reshape(nseq, HALF, D_MODEL)

    @pl.when(j == n_tiles + 1)
    def _():
        ys_ref[...] = ymix_sc[parity]


def _fused_step(x_ref, g_mix_ref, w_in_ref, conv_wb_ref, gate_rb_ref, gate_ib_ref, lam_ref, sgu_g_ref, sgu_bn_ref,
                sgu_bias_ref, w_out_ref, g_mlp_ref, w1_ref, w2_ref, gf_ref, y_ref, vprev_sc, sgu_lhs_sc, ymix_sc,
                gate_sc, slab_sc, tail_sc, h_sc, *, parity, nseq, final_norm):
    rows = nseq * HALF
    half_rb, half_ib, half_c = _lru_consts(gate_rb_ref[...], gate_ib_ref[...], lam_ref[...])
    xb_col0 = 2 * D_SGU
    yb_col0 = 2 * D_SGU + D_LRU
    slabs_per_tile = GATE_TILE // LANES
    st = [{} for _ in range(N_GATE_TILES)]
    sg = {}
    ml = {}

    def mlp_norm():
        xm = ymix_sc[1 - parity]
        ml["acc"] = xm
        ml["xn"] = _rms_norm(xm, g_mlp_ref[...]).astype(jnp.bfloat16)

    def mlp_up(c):
        ml["hid", c] = _mlp_up(ml["xn"], w1_ref, c)

    def mlp_down(c):
        ml["acc"] = ml["acc"] + _mlp_down(ml.pop(("hid", c)), w2_ref, c)

    def mlp_store():
        out = _rms_norm(ml["acc"], gf_ref[...]) if final_norm else ml["acc"]
        y_ref[...] = out.reshape(nseq, HALF, D_MODEL)

    def mix_norm():
        sg["x"] = x_ref[...].reshape(rows, D_MODEL)
        sg["xn"] = _rms_norm(sg["x"], g_mix_ref[...]).astype(jnp.bfloat16)

    def lru_proj(q):
        xb = _dot(sg["xn"], w_in_ref[:, xb_col0 + q * GATE_TILE:xb_col0 + (q + 1) * GATE_TILE])
        for i in range(slabs_per_tile):
            for s in range(nseq):
                slab_sc[slabs_per_tile * q + i][s * PITCH:s * PITCH + HALF, :] = (
                    xb[s * HALF:(s + 1) * HALF, i * LANES:(i + 1) * LANES])

    def lru_conv(q):
        xc_slabs = []
        for c in range(slabs_per_tile * q, slabs_per_tile * (q + 1)):
            taps = [conv_wb_ref[k, c] for k in range(CONV_W + 1)]
            p1, p2, p3 = (tail_sc[c][1 - parity, k] for k in range(CONV_W - 1))
            steps_out = []
            for t in range(HALF):
                cur = slab_sc[c][pl.ds(t, nseq, stride=PITCH), :]
                steps_out.append(taps[CONV_W] + taps[3] * cur + taps[2] * p1 + taps[1] * p2 + taps[0] * p3)
                p1, p2, p3 = cur, p1, p2
            for k, pk in enumerate((p1, p2, p3)):
                tail_sc[c][parity, k] = pk
            xc_slabs.append(jnp.concatenate(steps_out, axis=0))
        st[q]["xc"] = jnp.concatenate(xc_slabs, axis=1)

    def lru_gates(q):
        st[q]["ri"] = _dot(st[q]["xc"].astype(jnp.bfloat16), gate_sc[q])

    def lru_coeffs(q):
        cols = slice(q * GATE_TILE, (q + 1) * GATE_TILE)
        ri = st[q].pop("ri")
        st[q]["ab"] = _lru_elementwise(ri[:, :GATE_TILE], ri[:, GATE_TILE:], st[q].pop("xc"),
                                       half_rb[:, cols], half_ib[:, cols], half_c[:, cols])

    def lru_scan(q):
        a, b = st[q].pop("ab")
        for i in range(slabs_per_tile):
            c = slabs_per_tile * q + i
            h = h_sc[c][...]
            for t in range(HALF):
                h = (a[t * nseq:(t + 1) * nseq, i * LANES:(i + 1) * LANES] * h
                     + b[t * nseq:(t + 1) * nseq, i * LANES:(i + 1) * LANES])
                slab_sc[c][pl.ds(t, nseq, stride=PITCH), :] = h
            h_sc[c][...] = h

    def lru_out(q):
        hs = jnp.concatenate(
            [jnp.concatenate([slab_sc[c][s * PITCH:s * PITCH + HALF, :]
                              for c in range(slabs_per_tile * q, slabs_per_tile * (q + 1))], axis=1)
             for s in range(nseq)], axis=0)
        yb = _dot(sg["xn"], w_in_ref[:, yb_col0 + q * GATE_TILE:yb_col0 + (q + 1) * GATE_TILE])
        st[q]["out_b"] = (hs * _gelu(yb)).astype(jnp.bfloat16)

    def lru_mix(q):
        r0 = D_SGU + q * GATE_TILE
        sg["acc"] = sg["acc"] + _dot(st[q].pop("out_b"), w_out_ref[r0:r0 + GATE_TILE, :])

    def sgu_u():
        sg["u"] = _gelu(_dot(sg["xn"], w_in_ref[:, 0:D_SGU]))

    def sgu_v():
        v = _layer_norm(_gelu(_dot(sg["xn"], w_in_ref[:, D_SGU:2 * D_SGU])), sgu_g_ref[...], sgu_bn_ref[...])
        sg["vb"] = v.astype(jnp.bfloat16)

    def sgu_gate():
        vb = sg.pop("vb")
        lane = lax.broadcasted_iota(jnp.int32, (HALF, LANES), 1)
        bias = sgu_bias_ref[pl.ds(pl.multiple_of(parity * HALF, HALF), HALF), :]
        v_full = [jnp.concatenate([vprev_sc[1 - parity, s], vb[s * HALF:(s + 1) * HALF]], axis=0)
                  for s in range(nseq)]
        vprev_sc[parity] = vb.reshape(nseq, HALF, D_SGU)
        gate_rows = [[] for _ in range(nseq)]
        for s in range(0, nseq, 2):
            for g in range(N_LANE_GROUPS):
                lanes = slice(g * LANES, (g + 1) * LANES)
                rhs = jnp.concatenate([v_full[s][:, lanes], v_full[s + 1][:, lanes]], axis=1)
                res = _dot(sgu_lhs_sc[parity, g], rhs)
                for i in range(2):
                    part = res[:, i * LANES:(i + 1) * LANES]
                    gate_rows[s + i].append(jnp.where(lane < SGU_HEAD_DIM, part[:HALF], part[HALF:]))
        gate = jnp.concatenate([jnp.concatenate(r, axis=1) + bias for r in gate_rows], axis=0)
        sg["out_a"] = (sg.pop("u") * gate).astype(jnp.bfloat16)

    def sgu_mix():
        sg["acc"] = sg["x"] + _dot(sg.pop("out_a"), w_out_ref[0:D_SGU, :])

    def mix_store():
        ymix_sc[parity] = sg["acc"]

    tiles = range(N_GATE_TILES)
    mlp_norm()
    mix_norm()
    for q in tiles:
        lru_proj(q)
    mlp_up(0)
    for q in tiles:
        lru_conv(q)
        lru_gates(q)
    mlp_up(1)
    sgu_u()
    sgu_v()
    for q in tiles:
        lru_coeffs(q)
        lru_scan(q)
    mlp_down(0)
    mlp_down(1)
    for q in tiles:
        lru_out(q)
    mlp_up(2)
    mlp_up(3)
    sgu_gate()
    sgu_mix()
    mlp_down(2)
    mlp_down(3)
    for q in tiles:
        lru_mix(q)
    mix_store()
    mlp_store()


def _const_spec(shape):
    nd = len(shape)
    return pl.BlockSpec(shape, lambda *_: (0,) * nd, pipeline_mode=pl.Buffered(1))


def _layer_spec(arr, l):
    nd = arr.ndim - 1
    return pl.BlockSpec((None,) + arr.shape[1:], lambda *_: (l,) + (0,) * nd, pipeline_mode=pl.Buffered(1))


_BIG_WEIGHTS = ("w_in", "w_out", "w1", "w2")
_PROMPT_PARAMS = ("rows", "w_in", "conv_wb", "gate_rw", "gate_iw", "sgu_w", "sgu_bias", "w_out", "w1", "w2")
_SAMPLE_MIXER_PARAMS = ("rows", "w_in", "conv_w", "gate_rw", "gate_iw", "sgu_w8", "sgu_b8", "w_out")


def _param_spec(p, k, l):
    return _layer_spec(p[k], 0 if k in _BIG_WEIGHTS else l)


def _layer_prompt(x, xs, p, l, *, final_norm, cast_next=()):
    nseq, seq, _ = x.shape
    n_tiles = seq // HALF
    rows = nseq * HALF
    assert xs.shape == (2 * rows, D_MODEL)
    consts = [p[k] for k in _PROMPT_PARAMS]
    cast_in_specs, cast_out_specs, cast_shapes = [], [], []
    for w in cast_next:
        k_rows, n_cols = w.shape[1:]
        blk = k_rows // n_tiles
        assert blk * n_tiles == k_rows and blk % (2 * SUBLANES) == 0
        cast_in_specs.append(pl.BlockSpec((None, blk, n_cols), lambda j: (l + 1, jnp.minimum(j, n_tiles - 1), 0)))
        cast_out_specs.append(pl.BlockSpec((None, blk, n_cols), lambda j: (0, jnp.minimum(j, n_tiles - 1), 0)))
        cast_shapes.append(jax.ShapeDtypeStruct((1, k_rows, n_cols), jnp.bfloat16))
    x_spec = pl.BlockSpec((nseq, HALF, D_MODEL), lambda j: (0, jnp.minimum(j, n_tiles - 1), 0))
    y_spec = pl.BlockSpec((nseq, HALF, D_MODEL), lambda j: (0, jnp.clip(j - 1, 0, n_tiles - 1), 0))
    xs_spec = pl.BlockSpec((rows, D_MODEL), lambda j: (jnp.where(j > n_tiles, 1, 0), 0),
                           pipeline_mode=pl.Buffered(1))
    y, ys, hlast, convnew, *cast = pl.pallas_call(
        functools.partial(_layer_prompt_kernel, nseq=nseq, n_tiles=n_tiles, final_norm=final_norm,
                          n_cast=len(cast_next)),
        grid=(n_tiles + 2,),
        in_specs=[x_spec, xs_spec] + [_param_spec(p, k, l) for k in _PROMPT_PARAMS] + cast_in_specs,
        out_specs=([y_spec, xs_spec, _const_spec((nseq, D_LRU)), _const_spec((CONV_W - 1, nseq, D_LRU))]
                   + cast_out_specs),
        out_shape=[jax.ShapeDtypeStruct(x.shape, jnp.float32),
                   jax.ShapeDtypeStruct(xs.shape, jnp.float32),
                   jax.ShapeDtypeStruct((nseq, D_LRU), jnp.float32),
                   jax.ShapeDtypeStruct((CONV_W - 1, nseq, D_LRU), jnp.float32)] + cast_shapes,
        scratch_shapes=([pltpu.VMEM((2, nseq, HALF, D_SGU), jnp.bfloat16),
                         pltpu.VMEM((2, N_LANE_GROUPS, CHUNK, CHUNK), jnp.bfloat16),
                         pltpu.VMEM((2, rows, D_MODEL), jnp.float32),
                         pltpu.VMEM((N_GATE_TILES, GATE_TILE, 2 * GATE_TILE), jnp.bfloat16)]
                        + [pltpu.VMEM((nseq * PITCH, LANES), jnp.float32)] * N_SLABS
                        + [pltpu.VMEM((2, CONV_W - 1, nseq, LANES), jnp.float32)] * N_SLABS
                        + [pltpu.VMEM((nseq, LANES), jnp.float32)] * N_SLABS),
        compiler_params=pltpu.CompilerParams(dimension_semantics=("arbitrary",), vmem_limit_bytes=VMEM_LIMIT),
        name="layer_prompt",
    )(x, xs, *consts, *cast_next)
    return y, ys, hlast, jnp.transpose(convnew, (1, 0, 2)), cast


def _mixer_sample_kernel(*refs, steps, nb, seq_major, n_cast):
    n_in = 3 + len(_SAMPLE_MIXER_PARAMS)
    (x_ref, cbuf_ref, h0_ref, rows_ref, w_in_ref, conv_w_ref, gate_rw_ref, gate_iw_ref, sgu_w8_ref, sgu_b8_ref,
     w_out_ref) = refs[:n_in]
    cast_in, refs = refs[n_in:n_in + n_cast], refs[n_in + n_cast:]
    y_ref, v_ref, hlast_ref, convnew_ref = refs[:4]
    cast_out, refs = refs[4:4 + n_cast], refs[4 + n_cast:]
    gate_sc, relay_sc = refs[0], refs[1:]
    for w_f32, w_bf16 in zip(cast_in, cast_out):
        w_bf16[...] = w_f32[...].astype(jnp.bfloat16)
    (g_mix_ref, _, gate_rb_ref, gate_ib_ref, lam_ref, conv_b_ref, sgu_g_ref, sgu_bn_ref, _) = _row_views(rows_ref)

    @pl.when(pl.program_id(0) == 0)
    def _():
        _build_gate_tiles(gate_rw_ref, gate_iw_ref, gate_sc)

    if seq_major:
        for c in range(D_MODEL // LANES):
            relay_sc[c][...] = x_ref[:, c * LANES:(c + 1) * LANES]
        x = jnp.concatenate(
            [jnp.concatenate([relay_sc[c][pl.ds(t, nb, stride=steps), :] for c in range(D_MODEL // LANES)], axis=1)
             for t in range(steps)], axis=0)
    else:
        x = x_ref[...].reshape(steps * nb, D_MODEL)
    xn = _rms_norm(x, g_mix_ref[...]).astype(jnp.bfloat16)

    u = _gelu(_dot(xn, w_in_ref[:, 0:D_SGU]))
    v = _layer_norm(_gelu(_dot(xn, w_in_ref[:, D_SGU:2 * D_SGU])), sgu_g_ref[...], sgu_bn_ref[...])
    v_ref[...] = v.reshape(steps, nb, D_SGU)
    gate_rows = []
    for t in range(steps):
        acc = sgu_b8_ref[t:t + 1, :] + sgu_w8_ref[t, 0:1, :] * v[0:nb]
        for s in range(1, t + 1):
            acc = acc + sgu_w8_ref[t, s:s + 1, :] * v[s * nb:(s + 1) * nb]
        gate_rows.append(acc)
    out_a = (u * jnp.concatenate(gate_rows, axis=0)).astype(jnp.bfloat16)

    xb = _dot(xn, w_in_ref[:, 2 * D_SGU:2 * D_SGU + D_LRU])
    xp = jnp.concatenate([cbuf_ref[k] for k in range(CONV_W - 1)] + [xb], axis=0)
    xc = conv_b_ref[...] + conv_w_ref[0:1, :] * xp[0:steps * nb]
    for k in range(1, CONV_W):
        xc = xc + conv_w_ref[k:k + 1, :] * xp[k * nb:(k + steps) * nb]
    convnew_ref[...] = xp[steps * nb:].reshape(CONV_W - 1, nb, D_LRU)

    a, b = _lru_coeffs(xc, gate_sc, gate_rb_ref[...], gate_ib_ref[...], lam_ref[...])
    h = h0_ref[...]
    hs = []
    for t in range(steps):
        h = a[t * nb:(t + 1) * nb] * h + b[t * nb:(t + 1) * nb]
        hs.append(h)
    hlast_ref[...] = h

    yb = _dot(xn, w_in_ref[:, 2 * D_SGU + D_LRU:])
    out_b = (jnp.concatenate(hs, axis=0) * _gelu(yb)).astype(jnp.bfloat16)
    out = _dot(jnp.concatenate([out_a, out_b], axis=1), w_out_ref[...])
    y_ref[...] = (x + out).reshape(steps, nb, D_MODEL)


def _mixer_sample(x, h0_all, cbuf_all, p, l, *, seq_major, groups, cast=()):
    nb_all = h0_all.shape[1]
    steps = x.shape[0] // nb_all if seq_major else x.shape[0]
    nb = nb_all // groups
    cast_in_specs, cast_out_specs, cast_shapes = [], [], []
    for w in cast:
        k_rows, n_cols = w.shape[1:]
        blk = k_rows // groups
        assert blk * groups == k_rows and blk % (2 * SUBLANES) == 0
        cast_in_specs.append(pl.BlockSpec((None, blk, n_cols), lambda i: (l, i, 0)))
        cast_out_specs.append(pl.BlockSpec((None, blk, n_cols), lambda i: (0, i, 0)))
        cast_shapes.append(jax.ShapeDtypeStruct((1, k_rows, n_cols), jnp.bfloat16))
    if seq_major:
        x_spec = pl.BlockSpec((nb * steps, D_MODEL), lambda i: (i, 0))
        relay = [pltpu.VMEM((nb * steps, LANES), jnp.float32)] * (D_MODEL // LANES)
    else:
        x_spec = pl.BlockSpec((steps, nb, D_MODEL), lambda i: (0, i, 0))
        relay = []
    group = lambda lead, width: pl.BlockSpec((lead, nb, width), lambda i: (0, i, 0))
    out_shapes = [(steps, nb_all, D_MODEL), (steps, nb_all, D_SGU), (nb_all, D_LRU), (CONV_W - 1, nb_all, D_LRU)]
    y, v, hlast, convnew, *cast_bf16 = pl.pallas_call(
        functools.partial(_mixer_sample_kernel, steps=steps, nb=nb, seq_major=seq_major, n_cast=len(cast)),
        grid=(groups,),
        in_specs=([x_spec,
                   pl.BlockSpec((None, CONV_W - 1, nb, D_LRU), lambda i: (l, 0, i, 0)),
                   pl.BlockSpec((None, nb, D_LRU), lambda i: (l, i, 0))]
                  + [_param_spec(p, k, l) for k in _SAMPLE_MIXER_PARAMS] + cast_in_specs),
        out_specs=[group(steps, D_MODEL), group(steps, D_SGU), pl.BlockSpec((nb, D_LRU), lambda i: (i, 0)),
                   group(CONV_W - 1, D_LRU)] + cast_out_specs,
        out_shape=[jax.ShapeDtypeStruct(s, jnp.float32) for s in out_shapes] + cast_shapes,
        scratch_shapes=[pltpu.VMEM((N_GATE_TILES, GATE_TILE, 2 * GATE_TILE), jnp.bfloat16)] + relay,
        compiler_params=pltpu.CompilerParams(dimension_semantics=("arbitrary",), vmem_limit_bytes=VMEM_LIMIT),
        name="mixer_sample",
    )(x, cbuf_all, h0_all, *[p[k] for k in _SAMPLE_MIXER_PARAMS], *cast)
    return y, v, hlast, convnew, cast_bf16


CAST_STEPS = 8


def _cast_kernel(*refs):
    n = len(refs) // 2
    for w_f32, w_bf16 in zip(refs[:n], refs[n:]):
        w_bf16[...] = w_f32[...].astype(jnp.bfloat16)


def _cast_layer(weights, l):
    in_specs, out_specs, out_shapes = [], [], []
    for w in weights:
        k_rows, n_cols = w.shape[1:]
        blk = k_rows // CAST_STEPS
        assert blk * CAST_STEPS == k_rows and blk % (2 * SUBLANES) == 0
        in_specs.append(pl.BlockSpec((None, blk, n_cols), lambda j: (l, j, 0)))
        out_specs.append(pl.BlockSpec((None, blk, n_cols), lambda j: (0, j, 0)))
        out_shapes.append(jax.ShapeDtypeStruct((1, k_rows, n_cols), jnp.bfloat16))
    return pl.pallas_call(
        _cast_kernel,
        grid=(CAST_STEPS,),
        in_specs=in_specs,
        out_specs=out_specs,
        out_shape=out_shapes,
        compiler_params=pltpu.CompilerParams(dimension_semantics=("arbitrary",), vmem_limit_bytes=VMEM_LIMIT),
        name="cast_weights",
    )(*weights)


def _prepare_params(steps, nseq, norm_mix_g, w_in, conv_w, conv_b, gate_r_w, gate_r_b, gate_i_w, gate_i_b,
                    lru_lambda, sgu_norm_g, sgu_norm_b, sgu_w, sgu_b, w_out, norm_mlp_g, mlp_w1, mlp_w2, final_norm_g):
    conv_wb = jnp.concatenate([conv_w, conv_b[:, None, :]], axis=1)
    rows = [None] * N_ROWS
    rows[ROW_G_MIX], rows[ROW_G_MLP] = norm_mix_g, norm_mlp_g
    rows[ROW_GATE_RB], rows[ROW_GATE_IB], rows[ROW_LAM], rows[ROW_CONV_B] = gate_r_b, gate_i_b, lru_lambda, conv_b
    rows[ROW_SGU] = jnp.concatenate([sgu_norm_g, sgu_norm_b], axis=-1)
    rows[ROW_G_FINAL] = jnp.broadcast_to(final_norm_g, (DEPTH, D_MODEL))
    return {
        "rows": jnp.stack(rows, axis=1),
        "conv_w": conv_w,
        "conv_wb": jnp.broadcast_to(conv_wb.reshape(DEPTH, CONV_W + 1, N_SLABS, 1, LANES),
                                    (DEPTH, CONV_W + 1, N_SLABS, nseq, LANES)),
        "gate_rw": gate_r_w,
        "gate_iw": gate_i_w,
        "sgu_w": sgu_w,
        "sgu_bias": jnp.repeat(jnp.transpose(sgu_b, (0, 2, 1)), SGU_HEAD_DIM, axis=2),
        "sgu_w8": jnp.repeat(jnp.transpose(sgu_w[:, :, :steps, :steps], (0, 2, 3, 1)), SGU_HEAD_DIM, axis=3),
        "sgu_b8": jnp.repeat(jnp.transpose(sgu_b[:, :, :steps], (0, 2, 1)), SGU_HEAD_DIM, axis=2),
    }


def kernel(x_prompt, x_sample, state_lru_h, state_conv, norm_mix_g, w_in, conv_w, conv_b, gate_r_w, gate_r_b, gate_i_w, gate_i_b, lru_lambda, sgu_norm_g, sgu_norm_b, sgu_w, sgu_b, w_out, norm_mlp_g, mlp_w1, mlp_w2, final_norm_g):
    nseq, seq, _ = x_prompt.shape
    nb, steps, _ = x_sample.shape
    assert seq % CHUNK == 0 and nseq == SUBLANES and steps <= CHUNK
    p = _prepare_params(steps, nseq, norm_mix_g, w_in, conv_w, conv_b, gate_r_w, gate_r_b, gate_i_w, gate_i_b,
                        lru_lambda, sgu_norm_g, sgu_norm_b, sgu_w, sgu_b, w_out, norm_mlp_g, mlp_w1, mlp_w2,
                        final_norm_g)

    xp = x_prompt
    xs = x_sample.reshape(nb * steps, D_MODEL)
    cbuf_tm = jnp.transpose(state_conv, (0, 2, 1, 3))
    hs_p, convs_p, hs_s, convs_s, vs_s = [], [], [], [], []
    big_f32 = (w_in, w_out, mlp_w1, mlp_w2)
    big = _cast_layer(big_f32[:2], 0)
    for l in range(DEPTH):
        last = l == DEPTH - 1
        first = l == 0
        p.update(zip(_BIG_WEIGHTS, big))
        xs, v_s, h_s, conv_s, mlp_bf16 = _mixer_sample(
            xs, state_lru_h, cbuf_tm, p, l, seq_major=first,
            groups=SAMPLE_GROUPS_CAST if first else SAMPLE_GROUPS, cast=big_f32[2:] if first else ())
        p.update(zip(_BIG_WEIGHTS[2:], mlp_bf16))
        xp, xs, h_p, conv_p, big = _layer_prompt(xp, xs.reshape(steps * nb, D_MODEL), p, l, final_norm=last,
                                                 cast_next=() if last else big_f32)
        xs = xs.reshape(steps, nb, D_MODEL)
        hs_p.append(h_p)
        convs_p.append(conv_p)
        hs_s.append(h_s)
        convs_s.append(conv_s)
        vs_s.append(v_s)

    y_sample = jnp.transpose(xs, (1, 0, 2))
    return (xp, y_sample, jnp.stack(hs_p), jnp.stack(convs_p), jnp.stack(hs_s),
            jnp.transpose(jnp.stack(convs_s), (0, 2, 1, 3)), jnp.transpose(jnp.stack(vs_s), (0, 2, 1, 3)))
```

```python
import functools

import jax
import jax.numpy as jnp
from jax import lax
from jax.experimental import pallas as pl
from jax.experimental.pallas import tpu as pltpu

D_MODEL = 1024
DEPTH = 2
SGU_HEADS = 8
SGU_HEAD_DIM = 64
D_SGU = SGU_HEADS * SGU_HEAD_DIM
CHUNK = 128
D_LRU = 1024
LRU_BLOCKS = 16
LRU_BLOCK_DIM = 64
CONV_W = 4
LRU_C = 8.0
D_FF = 4 * D_MODEL
EPS = 1e-6

SUBLANES = 8
LANES = 128
GATE_TILE = 256
N_GATE_TILES = D_LRU // GATE_TILE
HEADS_PER_LANE_GROUP = LANES // SGU_HEAD_DIM
N_LANE_GROUPS = D_SGU // LANES
N_SLABS = D_LRU // LANES
FF_CHUNK = D_MODEL
N_FF_CHUNKS = D_FF // FF_CHUNK

SAMPLE_GROUPS = 2
SAMPLE_GROUPS_CAST = 4
HALF = CHUNK // 2
PITCH = HALF + SUBLANES
V7X_VMEM_BYTES = 64 * 1024 * 1024
VMEM_HEADROOM = 2 * 1024 * 1024
VMEM_LIMIT = V7X_VMEM_BYTES - VMEM_HEADROOM
F32_TINY = 1.1754944e-38


def _rms_norm(x, g):
    return x * lax.rsqrt(jnp.mean(x * x, axis=-1, keepdims=True) + EPS) * g


def _layer_norm(x, g, b):
    mu = jnp.mean(x, axis=-1, keepdims=True)
    xc = x - mu
    return xc * lax.rsqrt(jnp.mean(xc * xc, axis=-1, keepdims=True) + EPS) * g + b


def _gelu(x):
    return jax.nn.gelu(x, approximate=True)


def _dot(a, b):
    return jnp.dot(a, b, preferred_element_type=jnp.float32)


def _lru_elementwise(r_pre, i_pre, xc, half_rb, half_ib, half_c):
    th_r = jnp.tanh(r_pre + half_rb)
    th_i = jnp.tanh(i_pre + half_ib)
    log_a = half_c * th_r + half_c
    i = 0.5 * th_i + 0.5
    a = jnp.exp(log_a)
    y = jnp.tanh(log_a) * (-1.0 - a * a)
    b = (y * lax.rsqrt(jnp.maximum(y, F32_TINY))) * (i * xc)
    return a, b


def _lru_consts(gate_rb, gate_ib, lam):
    return 0.5 * gate_rb, 0.5 * gate_ib, (0.5 * LRU_C) * jax.nn.log_sigmoid(lam)


def _build_gate_tiles(gate_rw_ref, gate_iw_ref, gate_sc):
    per_tile = GATE_TILE // LRU_BLOCK_DIM
    k_idx = lax.broadcasted_iota(jnp.int32, (LRU_BLOCK_DIM, GATE_TILE), 0)
    n_idx = lax.broadcasted_iota(jnp.int32, (LRU_BLOCK_DIM, GATE_TILE), 1)
    replicate = (n_idx % LRU_BLOCK_DIM == k_idx).astype(jnp.bfloat16)
    row_blk = lax.broadcasted_iota(jnp.int32, (GATE_TILE, GATE_TILE), 0) // LRU_BLOCK_DIM
    col_blk = lax.broadcasted_iota(jnp.int32, (GATE_TILE, GATE_TILE), 1) // LRU_BLOCK_DIM
    for j in range(N_GATE_TILES):
        for k, ref in enumerate((gate_rw_ref, gate_iw_ref)):
            stacked = ref[per_tile * j:per_tile * (j + 1)].reshape(GATE_TILE, LRU_BLOCK_DIM)
            rep = _dot(stacked.astype(jnp.bfloat16), replicate)
            gate_sc[j, :, k * GATE_TILE:(k + 1) * GATE_TILE] = jnp.where(
                row_blk == col_blk, 0.5 * rep, 0.0).astype(jnp.bfloat16)


def _lru_coeffs(xc, gate_w_ref, gate_rb, gate_ib, lam):
    xcb = xc.astype(jnp.bfloat16)
    r_parts, i_parts = [], []
    for j in range(N_GATE_TILES):
        ri = _dot(xcb[:, j * GATE_TILE:(j + 1) * GATE_TILE], gate_w_ref[j])
        r_parts.append(ri[:, :GATE_TILE])
        i_parts.append(ri[:, GATE_TILE:])
    return _lru_elementwise(jnp.concatenate(r_parts, axis=1), jnp.concatenate(i_parts, axis=1), xc,
                            *_lru_consts(gate_rb, gate_ib, lam))


ROW_G_MIX, ROW_G_MLP, ROW_GATE_RB, ROW_GATE_IB, ROW_LAM, ROW_CONV_B, ROW_SGU, ROW_G_FINAL = range(8)
N_ROWS = 8


def _row_views(rows_ref):
    one = lambda k, lo=0, hi=D_MODEL: rows_ref.at[k:k + 1, lo:hi]
    return (one(ROW_G_MIX), one(ROW_G_MLP), one(ROW_GATE_RB), one(ROW_GATE_IB), one(ROW_LAM), one(ROW_CONV_B),
            one(ROW_SGU, 0, D_SGU), one(ROW_SGU, D_SGU, 2 * D_SGU), one(ROW_G_FINAL))


def _mlp_up(xn, w1_ref, c):
    cols = slice(c * FF_CHUNK, (c + 1) * FF_CHUNK)
    return jnp.square(jnp.maximum(_dot(xn, w1_ref[:, cols]), 0.0)).astype(jnp.bfloat16)


def _mlp_down(hid, w2_ref, c):
    return _dot(hid, w2_ref[c * FF_CHUNK:(c + 1) * FF_CHUNK, :])


def _layer_prompt_kernel(*refs, nseq, n_tiles, final_norm, n_cast):
    n_in = 2 + len(_PROMPT_PARAMS)
    (x_ref, xs_ref, rows_ref, w_in_ref, conv_wb_ref, gate_rw_ref, gate_iw_ref, sgu_w_ref, sgu_bias_ref, w_out_ref,
     w1_ref, w2_ref) = refs[:n_in]
    (g_mix_ref, g_mlp_ref, gate_rb_ref, gate_ib_ref, lam_ref, _, sgu_g_ref, sgu_bn_ref, gf_ref) = _row_views(rows_ref)
    cast_in, refs = refs[n_in:n_in + n_cast], refs[n_in + n_cast:]
    y_ref, ys_ref, hlast_ref, convnew_ref = refs[:4]
    cast_out, refs = refs[4:4 + n_cast], refs[4 + n_cast:]
    vprev_sc, sgu_lhs_sc, ymix_sc, gate_sc = refs[:4]
    slab_scratch = refs[4:]
    rows = nseq * HALF
    j = pl.program_id(0)
    slab_sc = slab_scratch[0:N_SLABS]
    tail_sc = slab_scratch[N_SLABS:2 * N_SLABS]
    h_sc = slab_scratch[2 * N_SLABS:3 * N_SLABS]

    @pl.when(j == 0)
    def _():
        for c in range(N_SLABS):
            tail_sc[c][...] = jnp.zeros_like(tail_sc[c])
            h_sc[c][...] = jnp.zeros_like(h_sc[c])
        vprev_sc[...] = jnp.zeros_like(vprev_sc)
        ymix_sc[1] = xs_ref[...]
        _build_gate_tiles(gate_rw_ref, gate_iw_ref, gate_sc)
        t_idx = lax.broadcasted_iota(jnp.int32, (CHUNK, CHUNK), 0)
        s_idx = lax.broadcasted_iota(jnp.int32, (CHUNK, CHUNK), 1)
        for g in range(N_LANE_GROUPS):
            first, second = [], []
            for hh in range(HEADS_PER_LANE_GROUP):
                wm = jnp.where(s_idx <= t_idx, sgu_w_ref[HEADS_PER_LANE_GROUP * g + hh], 0.0)
                first.append(pltpu.roll(wm[:HALF], HALF, axis=1))
                second.append(wm[HALF:])
            sgu_lhs_sc[0, g] = jnp.concatenate(first, axis=0).astype(jnp.bfloat16)
            sgu_lhs_sc[1, g] = jnp.concatenate(second, axis=0).astype(jnp.bfloat16)

    parity = j % 2

    @pl.when(j < n_tiles)
    def _():
        _fused_step(x_ref, g_mix_ref, w_in_ref, conv_wb_ref, gate_rb_ref, gate_ib_ref, lam_ref, sgu_g_ref, sgu_bn_ref,
                    sgu_bias_ref, w_out_ref, g_mlp_ref, w1_ref, w2_ref, gf_ref, y_ref, vprev_sc, sgu_lhs_sc, ymix_sc,
                    gate_sc, slab_sc, tail_sc, h_sc, parity=parity, nseq=nseq, final_norm=final_norm)
        for w_f32, w_bf16 in zip(cast_in, cast_out):
            w_bf16[...] = w_f32[...].astype(jnp.bfloat16)

    @pl.when(j == 0)
    def _():
        ys_ref[...] = y_ref[...].reshape(rows, D_MODEL)

    @pl.when(j == n_tiles - 1)
    def _():
        for c in range(N_SLABS):
            cols = slice(c * LANES, (c + 1) * LANES)
            hlast_ref[:, cols] = h_sc[c][...]
            for k in range(CONV_W - 1):
                convnew_ref[CONV_W - 2 - k, :, cols] = tail_sc[c][parity, k]

    @pl.when(j == n_tiles + 1)
    def _():
        ymix_sc[1 - parity] = xs_ref[...]

    @pl.when(j >= n_tiles)
    def _():
        xm = ymix_sc[1 - parity]
        xn = _rms_norm(xm, g_mlp_ref[...]).astype(jnp.bfloat16)
        acc = xm
        for c in range(N_FF_CHUNKS):
            acc = acc + _mlp_down(_mlp_up(xn, w1_ref, c), w2_ref, c)
        ymix_sc[parity] = _rms_norm(acc, gf_ref[...]) if final_norm else acc

    @pl.when(j == n_tiles)
    def _():
        y_ref[...] = ymix_sc[parity].reshape(nseq, HALF, D_MODEL)

    @pl.when(j == n_tiles + 1)
    def _():
        ys_ref[...] = ymix_sc[parity]


def _fused_step(x_ref, g_mix_ref, w_in_ref, conv_wb_ref, gate_rb_ref, gate_ib_ref, lam_ref, sgu_g_ref, sgu_bn_ref,
                sgu_bias_ref, w_out_ref, g_mlp_ref, w1_ref, w2_ref, gf_ref, y_ref, vprev_sc, sgu_lhs_sc, ymix_sc,
                gate_sc, slab_sc, tail_sc, h_sc, *, parity, nseq, final_norm):
    rows = nseq * HALF
    half_rb, half_ib, half_c = _lru_consts(gate_rb_ref[...], gate_ib_ref[...], lam_ref[...])
    xb_col0 = 2 * D_SGU
    yb_col0 = 2 * D_SGU + D_LRU
    slabs_per_tile = GATE_TILE // LANES
    st = [{} for _ in range(N_GATE_TILES)]
    sg = {}
    ml = {}

    def mlp_norm():
        xm = ymix_sc[1 - parity]
        ml["acc"] = xm
        ml["xn"] = _rms_norm(xm, g_mlp_ref[...]).astype(jnp.bfloat16)

    def mlp_up(c):
        ml["hid", c] = _mlp_up(ml["xn"], w1_ref, c)

    def mlp_down(c):
        ml["acc"] = ml["acc"] + _mlp_down(ml.pop(("hid", c)), w2_ref, c)

    def mlp_store():
        out = _rms_norm(ml["acc"], gf_ref[...]) if final_norm else ml["acc"]
        y_ref[...] = out.reshape(nseq, HALF, D_MODEL)

    def mix_norm():
        sg["x"] = x_ref[...].reshape(rows, D_MODEL)
        sg["xn"] = _rms_norm(sg["x"], g_mix_ref[...]).astype(jnp.bfloat16)

    def lru_proj(q):
        xb = _dot(sg["xn"], w_in_ref[:, xb_col0 + q * GATE_TILE:xb_col0 + (q + 1) * GATE_TILE])
        for i in range(slabs_per_tile):
            for s in range(nseq):
                slab_sc[slabs_per_tile * q + i][s * PITCH:s * PITCH + HALF, :] = (
                    xb[s * HALF:(s + 1) * HALF, i * LANES:(i + 1) * LANES])

    def lru_conv(q):
        xc_slabs = []
        for c in range(slabs_per_tile * q, slabs_per_tile * (q + 1)):
            taps = [conv_wb_ref[k, c] for k in range(CONV_W + 1)]
            p1, p2, p3 = (tail_sc[c][1 - parity, k] for k in range(CONV_W - 1))
            steps_out = []
            for t in range(HALF):
                cur = slab_sc[c][pl.ds(t, nseq, stride=PITCH), :]
                steps_out.append(taps[CONV_W] + taps[3] * cur + taps[2] * p1 + taps[1] * p2 + taps[0] * p3)
                p1, p2, p3 = cur, p1, p2
            for k, pk in enumerate((p1, p2, p3)):
                tail_sc[c][parity, k] = pk
            xc_slabs.append(jnp.concatenate(steps_out, axis=0))
        st[q]["xc"] = jnp.concatenate(xc_slabs, axis=1)

    def lru_gates(q):
        st[q]["ri"] = _dot(st[q]["xc"].astype(jnp.bfloat16), gate_sc[q])

    def lru_coeffs(q):
        cols = slice(q * GATE_TILE, (q + 1) * GATE_TILE)
        ri = st[q].pop("ri")
        st[q]["ab"] = _lru_elementwise(ri[:, :GATE_TILE], ri[:, GATE_TILE:], st[q].pop("xc"),
                                       half_rb[:, cols], half_ib[:, cols], half_c[:, cols])

    def lru_scan(q):
        a, b = st[q].pop("ab")
        for i in range(slabs_per_tile):
            c = slabs_per_tile * q + i
            h = h_sc[c][...]
            for t in range(HALF):
                h = (a[t * nseq:(t + 1) * nseq, i * LANES:(i + 1) * LANES] * h
                     + b[t * nseq:(t + 1) * nseq, i * LANES:(i + 1) * LANES])
                slab_sc[c][pl.ds(t, nseq, stride=PITCH), :] = h
            h_sc[c][...] = h

    def lru_out(q):
        hs = jnp.concatenate(
            [jnp.concatenate([slab_sc[c][s * PITCH:s * PITCH + HALF, :]
                              for c in range(slabs_per_tile * q, slabs_per_tile * (q + 1))], axis=1)
             for s in range(nseq)], axis=0)
        yb = _dot(sg["xn"], w_in_ref[:, yb_col0 + q * GATE_TILE:yb_col0 + (q + 1) * GATE_TILE])
        st[q]["out_b"] = (hs * _gelu(yb)).astype(jnp.bfloat16)

    def lru_mix(q):
        r0 = D_SGU + q * GATE_TILE
        sg["acc"] = sg["acc"] + _dot(st[q].pop("out_b"), w_out_ref[r0:r0 + GATE_TILE, :])

    def sgu_u():
        sg["u"] = _gelu(_dot(sg["xn"], w_in_ref[:, 0:D_SGU]))

    def sgu_v():
        v = _layer_norm(_gelu(_dot(sg["xn"], w_in_ref[:, D_SGU:2 * D_SGU])), sgu_g_ref[...], sgu_bn_ref[...])
        sg["vb"] = v.astype(jnp.bfloat16)

    def sgu_gate():
        vb = sg.pop("vb")
        lane = lax.broadcasted_iota(jnp.int32, (HALF, LANES), 1)
        bias = sgu_bias_ref[pl.ds(pl.multiple_of(parity * HALF, HALF), HALF), :]
        v_full = [jnp.concatenate([vprev_sc[1 - parity, s], vb[s * HALF:(s + 1) * HALF]], axis=0)
                  for s in range(nseq)]
        vprev_sc[parity] = vb.reshape(nseq, HALF, D_SGU)
        gate_rows = [[] for _ in range(nseq)]
        for s in range(0, nseq, 2):
            for g in range(N_LANE_GROUPS):
                lanes = slice(g * LANES, (g + 1) * LANES)
                rhs = jnp.concatenate([v_full[s][:, lanes], v_full[s + 1][:, lanes]], axis=1)
                res = _dot(sgu_lhs_sc[parity, g], rhs)
                for i in range(2):
                    part = res[:, i * LANES:(i + 1) * LANES]
                    gate_rows[s + i].append(jnp.where(lane < SGU_HEAD_DIM, part[:HALF], part[HALF:]))
        gate = jnp.concatenate([jnp.concatenate(r, axis=1) + bias for r in gate_rows], axis=0)
        sg["out_a"] = (sg.pop("u") * gate).astype(jnp.bfloat16)

    def sgu_mix():
        sg["acc"] = sg["x"] + _dot(sg.pop("out_a"), w_out_ref[0:D_SGU, :])

    def mix_store():
        ymix_sc[parity] = sg["acc"]

    tiles = range(N_GATE_TILES)
    mlp_norm()
    mix_norm()
    for q in tiles:
        lru_proj(q)
    mlp_up(0)
    for q in tiles:
        lru_conv(q)
        lru_gates(q)
    mlp_up(1)
    sgu_u()
    sgu_v()
    for q in tiles:
        lru_coeffs(q)
        lru_scan(q)
    mlp_down(0)
    mlp_down(1)
    for q in tiles:
        lru_out(q)
    mlp_up(2)
    mlp_up(3)
    sgu_gate()
    sgu_mix()
    mlp_down(2)
    mlp_down(3)
    for q in tiles:
        lru_mix(q)
    mix_store()
    mlp_store()


def _const_spec(shape):
    nd = len(shape)
    return pl.BlockSpec(shape, lambda *_: (0,) * nd, pipeline_mode=pl.Buffered(1))


def _layer_spec(arr, l):
    nd = arr.ndim - 1
    return pl.BlockSpec((None,) + arr.shape[1:], lambda *_: (l,) + (0,) * nd, pipeline_mode=pl.Buffered(1))


_BIG_WEIGHTS = ("w_in", "w_out", "w1", "w2")
_PROMPT_PARAMS = ("rows", "w_in", "conv_wb", "gate_rw", "gate_iw", "sgu_w", "sgu_bias", "w_out", "w1", "w2")
_SAMPLE_MIXER_PARAMS = ("rows", "w_in", "conv_w", "gate_rw", "gate_iw", "sgu_w8", "sgu_b8", "w_out")


def _param_spec(p, k, l):
    return _layer_spec(p[k], 0 if k in _BIG_WEIGHTS else l)


def _layer_prompt(x, xs, p, l, *, final_norm, cast_next=()):
    nseq, seq, _ = x.shape
    n_tiles = seq // HALF
    rows = nseq * HALF
    assert xs.shape == (2 * rows, D_MODEL)
    consts = [p[k] for k in _PROMPT_PARAMS]
    cast_in_specs, cast_out_specs, cast_shapes = [], [], []
    for w in cast_next:
        k_rows, n_cols = w.shape[1:]
        blk = k_rows // n_tiles
        assert blk * n_tiles == k_rows and blk % (2 * SUBLANES) == 0
        cast_in_specs.append(pl.BlockSpec((None, blk, n_cols), lambda j: (l + 1, jnp.minimum(j, n_tiles - 1), 0)))
        cast_out_specs.append(pl.BlockSpec((None, blk, n_cols), lambda j: (0, jnp.minimum(j, n_tiles - 1), 0)))
        cast_shapes.append(jax.ShapeDtypeStruct((1, k_rows, n_cols), jnp.bfloat16))
    x_spec = pl.BlockSpec((nseq, HALF, D_MODEL), lambda j: (0, jnp.minimum(j, n_tiles - 1), 0))
    y_spec = pl.BlockSpec((nseq, HALF, D_MODEL), lambda j: (0, jnp.clip(j - 1, 0, n_tiles - 1), 0))
    xs_spec = pl.BlockSpec((rows, D_MODEL), lambda j: (jnp.where(j > n_tiles, 1, 0), 0),
                           pipeline_mode=pl.Buffered(1))
    y, ys, hlast, convnew, *cast = pl.pallas_call(
        functools.partial(_layer_prompt_kernel, nseq=nseq, n_tiles=n_tiles, final_norm=final_norm,
                          n_cast=len(cast_next)),
        grid=(n_tiles + 2,),
        in_specs=[x_spec, xs_spec] + [_param_spec(p, k, l) for k in _PROMPT_PARAMS] + cast_in_specs,
        out_specs=([y_spec, xs_spec, _const_spec((nseq, D_LRU)), _const_spec((CONV_W - 1, nseq, D_LRU))]
                   + cast_out_specs),
        out_shape=[jax.ShapeDtypeStruct(x.shape, jnp.float32),
                   jax.ShapeDtypeStruct(xs.shape, jnp.float32),
                   jax.ShapeDtypeStruct((nseq, D_LRU), jnp.float32),
                   jax.ShapeDtypeStruct((CONV_W - 1, nseq, D_LRU), jnp.float32)] + cast_shapes,
        scratch_shapes=([pltpu.VMEM((2, nseq, HALF, D_SGU), jnp.bfloat16),
                         pltpu.VMEM((2, N_LANE_GROUPS, CHUNK, CHUNK), jnp.bfloat16),
                         pltpu.VMEM((2, rows, D_MODEL), jnp.float32),
                         pltpu.VMEM((N_GATE_TILES, GATE_TILE, 2 * GATE_TILE), jnp.bfloat16)]
                        + [pltpu.VMEM((nseq * PITCH, LANES), jnp.float32)] * N_SLABS
                        + [pltpu.VMEM((2, CONV_W - 1, nseq, LANES), jnp.float32)] * N_SLABS
                        + [pltpu.VMEM((nseq, LANES), jnp.float32)] * N_SLABS),
        compiler_params=pltpu.CompilerParams(dimension_semantics=("arbitrary",), vmem_limit_bytes=VMEM_LIMIT),
        name="layer_prompt",
    )(x, xs, *consts, *cast_next)
    return y, ys, hlast, jnp.transpose(convnew, (1, 0, 2)), cast


def _mixer_sample_kernel(*refs, steps, nb, seq_major, n_cast):
    n_in = 3 + len(_SAMPLE_MIXER_PARAMS)
    (x_ref, cbuf_ref, h0_ref, rows_ref, w_in_ref, conv_w_ref, gate_rw_ref, gate_iw_ref, sgu_w8_ref, sgu_b8_ref,
     w_out_ref) = refs[:n_in]
    cast_in, refs = refs[n_in:n_in + n_cast], refs[n_in + n_cast:]
    y_ref, v_ref, hlast_ref, convnew_ref = refs[:4]
    cast_out, refs = refs[4:4 + n_cast], refs[4 + n_cast:]
    gate_sc, relay_sc = refs[0], refs[1:]
    for w_f32, w_bf16 in zip(cast_in, cast_out):
        w_bf16[...] = w_f32[...].astype(jnp.bfloat16)
    (g_mix_ref, _, gate_rb_ref, gate_ib_ref, lam_ref, conv_b_ref, sgu_g_ref, sgu_bn_ref, _) = _row_views(rows_ref)

    @pl.when(pl.program_id(0) == 0)
    def _():
        _build_gate_tiles(gate_rw_ref, gate_iw_ref, gate_sc)

    if seq_major:
        for c in range(D_MODEL // LANES):
            relay_sc[c][...] = x_ref[:, c * LANES:(c + 1) * LANES]
        x = jnp.concatenate(
            [jnp.concatenate([relay_sc[c][pl.ds(t, nb, stride=steps), :] for c in range(D_MODEL // LANES)], axis=1)
             for t in range(steps)], axis=0)
    else:
        x = x_ref[...].reshape(steps * nb, D_MODEL)
    xn = _rms_norm(x, g_mix_ref[...]).astype(jnp.bfloat16)

    u = _gelu(_dot(xn, w_in_ref[:, 0:D_SGU]))
    v = _layer_norm(_gelu(_dot(xn, w_in_ref[:, D_SGU:2 * D_SGU])), sgu_g_ref[...], sgu_bn_ref[...])
    v_ref[...] = v.reshape(steps, nb, D_SGU)
    gate_rows = []
    for t in range(steps):
        acc = sgu_b8_ref[t:t + 1, :] + sgu_w8_ref[t, 0:1, :] * v[0:nb]
        for s in range(1, t + 1):
            acc = acc + sgu_w8_ref[t, s:s + 1, :] * v[s * nb:(s + 1) * nb]
        gate_rows.append(acc)
    out_a = (u * jnp.concatenate(gate_rows, axis=0)).astype(jnp.bfloat16)

    xb = _dot(xn, w_in_ref[:, 2 * D_SGU:2 * D_SGU + D_LRU])
    xp = jnp.concatenate([cbuf_ref[k] for k in range(CONV_W - 1)] + [xb], axis=0)
    xc = conv_b_ref[...] + conv_w_ref[0:1, :] * xp[0:steps * nb]
    for k in range(1, CONV_W):
        xc = xc + conv_w_ref[k:k + 1, :] * xp[k * nb:(k + steps) * nb]
    convnew_ref[...] = xp[steps * nb:].reshape(CONV_W - 1, nb, D_LRU)

    a, b = _lru_coeffs(xc, gate_sc, gate_rb_ref[...], gate_ib_ref[...], lam_ref[...])
    h = h0_ref[...]
    hs = []
    for t in range(steps):
        h = a[t * nb:(t + 1) * nb] * h + b[t * nb:(t + 1) * nb]
        hs.append(h)
    hlast_ref[...] = h

    yb = _dot(xn, w_in_ref[:, 2 * D_SGU + D_LRU:])
    out_b = (jnp.concatenate(hs, axis=0) * _gelu(yb)).astype(jnp.bfloat16)
    out = _dot(jnp.concatenate([out_a, out_b], axis=1), w_out_ref[...])
    y_ref[...] = (x + out).reshape(steps, nb, D_MODEL)


def _mixer_sample(x, h0_all, cbuf_all, p, l, *, seq_major, groups, cast=()):
    nb_all = h0_all.shape[1]
    steps = x.shape[0] // nb_all if seq_major else x.shape[0]
    nb = nb_all // groups
    cast_in_specs, cast_out_specs, cast_shapes = [], [], []
    for w in cast:
        k_rows, n_cols = w.shape[1:]
        blk = k_rows // groups
        assert blk * groups == k_rows and blk % (2 * SUBLANES) == 0
        cast_in_specs.append(pl.BlockSpec((None, blk, n_cols), lambda i: (l, i, 0)))
        cast_out_specs.append(pl.BlockSpec((None, blk, n_cols), lambda i: (0, i, 0)))
        cast_shapes.append(jax.ShapeDtypeStruct((1, k_rows, n_cols), jnp.bfloat16))
    if seq_major:
        x_spec = pl.BlockSpec((nb * steps, D_MODEL), lambda i: (i, 0))
        relay = [pltpu.VMEM((nb * steps, LANES), jnp.float32)] * (D_MODEL // LANES)
    else:
        x_spec = pl.BlockSpec((steps, nb, D_MODEL), lambda i: (0, i, 0))
        relay = []
    group = lambda lead, width: pl.BlockSpec((lead, nb, width), lambda i: (0, i, 0))
    out_shapes = [(steps, nb_all, D_MODEL), (steps, nb_all, D_SGU), (nb_all, D_LRU), (CONV_W - 1, nb_all, D_LRU)]
    y, v, hlast, convnew, *cast_bf16 = pl.pallas_call(
        functools.partial(_mixer_sample_kernel, steps=steps, nb=nb, seq_major=seq_major, n_cast=len(cast)),
        grid=(groups,),
        in_specs=([x_spec,
                   pl.BlockSpec((None, CONV_W - 1, nb, D_LRU), lambda i: (l, 0, i, 0)),
                   pl.BlockSpec((None, nb, D_LRU), lambda i: (l, i, 0))]
                  + [_param_spec(p, k, l) for k in _SAMPLE_MIXER_PARAMS] + cast_in_specs),
        out_specs=[group(steps, D_MODEL), group(steps, D_SGU), pl.BlockSpec((nb, D_LRU), lambda i: (i, 0)),
                   group(CONV_W - 1, D_LRU)] + cast_out_specs,
        out_shape=[jax.ShapeDtypeStruct(s, jnp.float32) for s in out_shapes] + cast_shapes,
        scratch_shapes=[pltpu.VMEM((N_GATE_TILES, GATE_TILE, 2 * GATE_TILE), jnp.bfloat16)] + relay,
        compiler_params=pltpu.CompilerParams(dimension_semantics=("arbitrary",), vmem_limit_bytes=VMEM_LIMIT),
        name="mixer_sample",
    )(x, cbuf_all, h0_all, *[p[k] for k in _SAMPLE_MIXER_PARAMS], *cast)
    return y, v, hlast, convnew, cast_bf16


CAST_STEPS = 4


def _cast_kernel(*refs):
    n = len(refs) // 2
    for w_f32, w_bf16 in zip(refs[:n], refs[n:]):
        w_bf16[...] = w_f32[...].astype(jnp.bfloat16)


def _cast_layer(weights, l):
    in_specs, out_specs, out_shapes = [], [], []
    for w in weights:
        k_rows, n_cols = w.shape[1:]
        blk = k_rows // CAST_STEPS
        assert blk * CAST_STEPS == k_rows and blk % (2 * SUBLANES) == 0
        in_specs.append(pl.BlockSpec((None, blk, n_cols), lambda j: (l, j, 0)))
        out_specs.append(pl.BlockSpec((None, blk, n_cols), lambda j: (0, j, 0)))
        out_shapes.append(jax.ShapeDtypeStruct((1, k_rows, n_cols), jnp.bfloat16))
    return pl.pallas_call(
        _cast_kernel,
        grid=(CAST_STEPS,),
        in_specs=in_specs,
        out_specs=out_specs,
        out_shape=out_shapes,
        compiler_params=pltpu.CompilerParams(dimension_semantics=("arbitrary",), vmem_limit_bytes=VMEM_LIMIT),
        name="cast_weights",
    )(*weights)


def _prepare_params(steps, nseq, norm_mix_g, w_in, conv_w, conv_b, gate_r_w, gate_r_b, gate_i_w, gate_i_b,
                    lru_lambda, sgu_norm_g, sgu_norm_b, sgu_w, sgu_b, w_out, norm_mlp_g, mlp_w1, mlp_w2, final_norm_g):
    conv_wb = jnp.concatenate([conv_w, conv_b[:, None, :]], axis=1)
    rows = [None] * N_ROWS
    rows[ROW_G_MIX], rows[ROW_G_MLP] = norm_mix_g, norm_mlp_g
    rows[ROW_GATE_RB], rows[ROW_GATE_IB], rows[ROW_LAM], rows[ROW_CONV_B] = gate_r_b, gate_i_b, lru_lambda, conv_b
    rows[ROW_SGU] = jnp.concatenate([sgu_norm_g, sgu_norm_b], axis=-1)
    rows[ROW_G_FINAL] = jnp.broadcast_to(final_norm_g, (DEPTH, D_MODEL))
    return {
        "rows": jnp.stack(rows, axis=1),
        "conv_w": conv_w,
        "conv_wb": jnp.broadcast_to(conv_wb.reshape(DEPTH, CONV_W + 1, N_SLABS, 1, LANES),
                                    (DEPTH, CONV_W + 1, N_SLABS, nseq, LANES)),
        "gate_rw": gate_r_w,
        "gate_iw": gate_i_w,
        "sgu_w": sgu_w,
        "sgu_bias": jnp.repeat(jnp.transpose(sgu_b, (0, 2, 1)), SGU_HEAD_DIM, axis=2),
        "sgu_w8": jnp.repeat(jnp.transpose(sgu_w[:, :, :steps, :steps], (0, 2, 3, 1)), SGU_HEAD_DIM, axis=3),
        "sgu_b8": jnp.repeat(jnp.transpose(sgu_b[:, :, :steps], (0, 2, 1)), SGU_HEAD_DIM, axis=2),
    }


def kernel(x_prompt, x_sample, state_lru_h, state_conv, norm_mix_g, w_in, conv_w, conv_b, gate_r_w, gate_r_b, gate_i_w, gate_i_b, lru_lambda, sgu_norm_g, sgu_norm_b, sgu_w, sgu_b, w_out, norm_mlp_g, mlp_w1, mlp_w2, final_norm_g):
    nseq, seq, _ = x_prompt.shape
    nb, steps, _ = x_sample.shape
    assert seq % CHUNK == 0 and nseq == SUBLANES and steps <= CHUNK
    p = _prepare_params(steps, nseq, norm_mix_g, w_in, conv_w, conv_b, gate_r_w, gate_r_b, gate_i_w, gate_i_b,
                        lru_lambda, sgu_norm_g, sgu_norm_b, sgu_w, sgu_b, w_out, norm_mlp_g, mlp_w1, mlp_w2,
                        final_norm_g)

    xp = x_prompt
    xs = x_sample.reshape(nb * steps, D_MODEL)
    cbuf_tm = jnp.transpose(state_conv, (0, 2, 1, 3))
    hs_p, convs_p, hs_s, convs_s, vs_s = [], [], [], [], []
    big_f32 = (w_in, w_out, mlp_w1, mlp_w2)
    big = _cast_layer(big_f32[:2], 0)
    for l in range(DEPTH):
        last = l == DEPTH - 1
        first = l == 0
        p.update(zip(_BIG_WEIGHTS, big))
        xs, v_s, h_s, conv_s, mlp_bf16 = _mixer_sample(
            xs, state_lru_h, cbuf_tm, p, l, seq_major=first,
            groups=SAMPLE_GROUPS_CAST if first else SAMPLE_GROUPS, cast=big_f32[2:] if first else ())
        p.update(zip(_BIG_WEIGHTS[2:], mlp_bf16))
        xp, xs, h_p, conv_p, big = _layer_prompt(xp, xs.reshape(steps * nb, D_MODEL), p, l, final_norm=last,
                                                 cast_next=() if last else big_f32)
        xs = xs.reshape(steps, nb, D_MODEL)
        hs_p.append(h_p)
        convs_p.append(conv_p)
        hs_s.append(h_s)
        convs_s.append(conv_s)
        vs_s.append(v_s)

    y_sample = jnp.transpose(xs, (1, 0, 2))
    return (xp, y_sample, jnp.stack(hs_p), jnp.stack(convs_p), jnp.stack(hs_s),
            jnp.transpose(jnp.stack(convs_s), (0, 2, 1, 3)), jnp.transpose(jnp.stack(vs_s), (0, 2, 1, 3)))
```

```python
import functools

import jax
import jax.numpy as jnp
from jax import lax
from jax.experimental import pallas as pl
from jax.experimental.pallas import tpu as pltpu

D_MODEL = 1024
DEPTH = 2
SGU_HEADS = 8
SGU_HEAD_DIM = 64
D_SGU = SGU_HEADS * SGU_HEAD_DIM
CHUNK = 128
D_LRU = 1024
LRU_BLOCKS = 16
LRU_BLOCK_DIM = 64
CONV_W = 4
LRU_C = 8.0
D_FF = 4 * D_MODEL
EPS = 1e-6

SUBLANES = 8
LANES = 128
GATE_TILE = 256
N_GATE_TILES = D_LRU // GATE_TILE
HEADS_PER_LANE_GROUP = LANES // SGU_HEAD_DIM
N_LANE_GROUPS = D_SGU // LANES
N_SLABS = D_LRU // LANES
FF_CHUNK = D_MODEL
N_FF_CHUNKS = D_FF // FF_CHUNK

SAMPLE_GROUPS = 2
SAMPLE_GROUPS_CAST = 4
HALF = CHUNK // 2
PITCH = HALF + SUBLANES
V7X_VMEM_BYTES = 64 * 1024 * 1024
VMEM_HEADROOM = 2 * 1024 * 1024
VMEM_LIMIT = V7X_VMEM_BYTES - VMEM_HEADROOM
F32_TINY = 1.1754944e-38


def _rms_norm(x, g):
    return x * lax.rsqrt(jnp.mean(x * x, axis=-1, keepdims=True) + EPS) * g


def _layer_norm(x, g, b):
    mu = jnp.mean(x, axis=-1, keepdims=True)
    xc = x - mu
    return xc * lax.rsqrt(jnp.mean(xc * xc, axis=-1, keepdims=True) + EPS) * g + b


def _gelu(x):
    return jax.nn.gelu(x, approximate=True)


def _dot(a, b):
    return jnp.dot(a, b, preferred_element_type=jnp.float32)


def _lru_elementwise(r_pre, i_pre, xc, half_rb, half_ib, half_c):
    th_r = jnp.tanh(r_pre + half_rb)
    th_i = jnp.tanh(i_pre + half_ib)
    log_a = half_c * th_r + half_c
    i = 0.5 * th_i + 0.5
    a = jnp.exp(log_a)
    y = jnp.tanh(log_a) * (-1.0 - a * a)
    b = (y * lax.rsqrt(jnp.maximum(y, F32_TINY))) * (i * xc)
    return a, b


def _lru_consts(gate_rb, gate_ib, lam):
    return 0.5 * gate_rb, 0.5 * gate_ib, (0.5 * LRU_C) * jax.nn.log_sigmoid(lam)


def _build_gate_tiles(gate_rw_ref, gate_iw_ref, gate_sc):
    per_tile = GATE_TILE // LRU_BLOCK_DIM
    k_idx = lax.broadcasted_iota(jnp.int32, (LRU_BLOCK_DIM, GATE_TILE), 0)
    n_idx = lax.broadcasted_iota(jnp.int32, (LRU_BLOCK_DIM, GATE_TILE), 1)
    replicate = (n_idx % LRU_BLOCK_DIM == k_idx).astype(jnp.bfloat16)
    row_blk = lax.broadcasted_iota(jnp.int32, (GATE_TILE, GATE_TILE), 0) // LRU_BLOCK_DIM
    col_blk = lax.broadcasted_iota(jnp.int32, (GATE_TILE, GATE_TILE), 1) // LRU_BLOCK_DIM
    for j in range(N_GATE_TILES):
        for k, ref in enumerate((gate_rw_ref, gate_iw_ref)):
            stacked = ref[per_tile * j:per_tile * (j + 1)].reshape(GATE_TILE, LRU_BLOCK_DIM)
            rep = _dot(stacked.astype(jnp.bfloat16), replicate)
            gate_sc[j, :, k * GATE_TILE:(k + 1) * GATE_TILE] = jnp.where(
                row_blk == col_blk, 0.5 * rep, 0.0).astype(jnp.bfloat16)


def _lru_coeffs(xc, gate_w_ref, gate_rb, gate_ib, lam):
    xcb = xc.astype(jnp.bfloat16)
    r_parts, i_parts = [], []
    for j in range(N_GATE_TILES):
        ri = _dot(xcb[:, j * GATE_TILE:(j + 1) * GATE_TILE], gate_w_ref[j])
        r_parts.append(ri[:, :GATE_TILE])
        i_parts.append(ri[:, GATE_TILE:])
    return _lru_elementwise(jnp.concatenate(r_parts, axis=1), jnp.concatenate(i_parts, axis=1), xc,
                            *_lru_consts(gate_rb, gate_ib, lam))


ROW_G_MIX, ROW_G_MLP, ROW_GATE_RB, ROW_GATE_IB, ROW_LAM, ROW_CONV_B, ROW_SGU, ROW_G_FINAL = range(8)
N_ROWS = 8


def _row_views(rows_ref):
    one = lambda k, lo=0, hi=D_MODEL: rows_ref.at[k:k + 1, lo:hi]
    return (one(ROW_G_MIX), one(ROW_G_MLP), one(ROW_GATE_RB), one(ROW_GATE_IB), one(ROW_LAM), one(ROW_CONV_B),
            one(ROW_SGU, 0, D_SGU), one(ROW_SGU, D_SGU, 2 * D_SGU), one(ROW_G_FINAL))


def _mlp_up(xn, w1_ref, c):
    cols = slice(c * FF_CHUNK, (c + 1) * FF_CHUNK)
    return jnp.square(jnp.maximum(_dot(xn, w1_ref[:, cols]), 0.0)).astype(jnp.bfloat16)


def _mlp_down(hid, w2_ref, c):
    return _dot(hid, w2_ref[c * FF_CHUNK:(c + 1) * FF_CHUNK, :])


def _layer_prompt_kernel(*refs, nseq, n_tiles, final_norm, n_cast):
    n_in = 2 + len(_PROMPT_PARAMS)
    (x_ref, xs_ref, rows_ref, w_in_ref, conv_wb_ref, gate_rw_ref, gate_iw_ref, sgu_w_ref, sgu_bias_ref, w_out_ref,
     w1_ref, w2_ref) = refs[:n_in]
    (g_mix_ref, g_mlp_ref, gate_rb_ref, gate_ib_ref, lam_ref, _, sgu_g_ref, sgu_bn_ref, gf_ref) = _row_views(rows_ref)
    cast_in, refs = refs[n_in:n_in + n_cast], refs[n_in + n_cast:]
    y_ref, ys_ref, hlast_ref, convnew_ref = refs[:4]
    cast_out, refs = refs[4:4 + n_cast], refs[4 + n_cast:]
    vprev_sc, sgu_lhs_sc, ymix_sc, gate_sc = refs[:4]
    slab_scratch = refs[4:]
    rows = nseq * HALF
    j = pl.program_id(0)
    slab_sc = slab_scratch[0:N_SLABS]
    tail_sc = slab_scratch[N_SLABS:2 * N_SLABS]
    h_sc = slab_scratch[2 * N_SLABS:3 * N_SLABS]

    @pl.when(j == 0)
    def _():
        for c in range(N_SLABS):
            tail_sc[c][...] = jnp.zeros_like(tail_sc[c])
            h_sc[c][...] = jnp.zeros_like(h_sc[c])
        vprev_sc[...] = jnp.zeros_like(vprev_sc)
        ymix_sc[1] = xs_ref[...]
        _build_gate_tiles(gate_rw_ref, gate_iw_ref, gate_sc)
        t_idx = lax.broadcasted_iota(jnp.int32, (CHUNK, CHUNK), 0)
        s_idx = lax.broadcasted_iota(jnp.int32, (CHUNK, CHUNK), 1)
        for g in range(N_LANE_GROUPS):
            first, second = [], []
            for hh in range(HEADS_PER_LANE_GROUP):
                wm = jnp.where(s_idx <= t_idx, sgu_w_ref[HEADS_PER_LANE_GROUP * g + hh], 0.0)
                first.append(pltpu.roll(wm[:HALF], HALF, axis=1))
                second.append(wm[HALF:])
            sgu_lhs_sc[0, g] = jnp.concatenate(first, axis=0).astype(jnp.bfloat16)
            sgu_lhs_sc[1, g] = jnp.concatenate(second, axis=0).astype(jnp.bfloat16)

    parity = j % 2

    @pl.when(j < n_tiles)
    def _():
        _fused_step(x_ref, g_mix_ref, w_in_ref, conv_wb_ref, gate_rb_ref, gate_ib_ref, lam_ref, sgu_g_ref, sgu_bn_ref,
                    sgu_bias_ref, w_out_ref, g_mlp_ref, w1_ref, w2_ref, gf_ref, y_ref, vprev_sc, sgu_lhs_sc, ymix_sc,
                    gate_sc, slab_sc, tail_sc, h_sc, parity=parity, nseq=nseq, final_norm=final_norm)
        for w_f32, w_bf16 in zip(cast_in, cast_out):
            w_bf16[...] = w_f32[...].astype(jnp.bfloat16)

    @pl.when(j == 0)
    def _():
        ys_ref[...] = y_ref[...].reshape(rows, D_MODEL)

    @pl.when(j == n_tiles - 1)
    def _():
        for c in range(N_SLABS):
            cols = slice(c * LANES, (c + 1) * LANES)
            hlast_ref[:, cols] = h_sc[c][...]
            for k in range(CONV_W - 1):
                convnew_ref[CONV_W - 2 - k, :, cols] = tail_sc[c][parity, k]

    @pl.when(j == n_tiles + 1)
    def _():
        ymix_sc[1 - parity] = xs_ref[...]

    @pl.when(j >= n_tiles)
    def _():
        xm = ymix_sc[1 - parity]
        xn = _rms_norm(xm, g_mlp_ref[...]).astype(jnp.bfloat16)
        acc = xm
        for c in range(N_FF_CHUNKS):
            acc = acc + _mlp_down(_mlp_up(xn, w1_ref, c), w2_ref, c)
        ymix_sc[parity] = _rms_norm(acc, gf_ref[...]) if final_norm else acc

    @pl.when(j == n_tiles)
    def _():
        y_ref[...] = ymix_sc[parity].reshape(nseq, HALF, D_MODEL)

    @pl.when(j == n_tiles + 1)
    def _():
        ys_ref[...] = ymix_sc[parity]


def _fused_step(x_ref, g_mix_ref, w_in_ref, conv_wb_ref, gate_rb_ref, gate_ib_ref, lam_ref, sgu_g_ref, sgu_bn_ref,
                sgu_bias_ref, w_out_ref, g_mlp_ref, w1_ref, w2_ref, gf_ref, y_ref, vprev_sc, sgu_lhs_sc, ymix_sc,
                gate_sc, slab_sc, tail_sc, h_sc, *, parity, nseq, final_norm):
    rows = nseq * HALF
    half_rb, half_ib, half_c = _lru_consts(gate_rb_ref[...], gate_ib_ref[...], lam_ref[...])
    xb_col0 = 2 * D_SGU
    yb_col0 = 2 * D_SGU + D_LRU
    slabs_per_tile = GATE_TILE // LANES
    st = [{} for _ in range(N_GATE_TILES)]
    sg = {}
    ml = {}

    def mlp_norm():
        xm = ymix_sc[1 - parity]
        ml["acc"] = xm
        ml["xn"] = _rms_norm(xm, g_mlp_ref[...]).astype(jnp.bfloat16)

    def mlp_up(c):
        ml["hid", c] = _mlp_up(ml["xn"], w1_ref, c)

    def mlp_down(c):
        ml["acc"] = ml["acc"] + _mlp_down(ml.pop(("hid", c)), w2_ref, c)

    def mlp_store():
        out = _rms_norm(ml["acc"], gf_ref[...]) if final_norm else ml["acc"]
        y_ref[...] = out.reshape(nseq, HALF, D_MODEL)

    def mix_norm():
        sg["x"] = x_ref[...].reshape(rows, D_MODEL)
        sg["xn"] = _rms_norm(sg["x"], g_mix_ref[...]).astype(jnp.bfloat16)

    def lru_proj(q):
        xb = _dot(sg["xn"], w_in_ref[:, xb_col0 + q * GATE_TILE:xb_col0 + (q + 1) * GATE_TILE])
        for i in range(slabs_per_tile):
            for s in range(nseq):
                slab_sc[slabs_per_tile * q + i][s * PITCH:s * PITCH + HALF, :] = (
                    xb[s * HALF:(s + 1) * HALF, i * LANES:(i + 1) * LANES])

    def lru_conv(q):
        xc_slabs = []
        for c in range(slabs_per_tile * q, slabs_per_tile * (q + 1)):
            taps = [conv_wb_ref[k, c] for k in range(CONV_W + 1)]
            p1, p2, p3 = (tail_sc[c][1 - parity, k] for k in range(CONV_W - 1))
            steps_out = []
            for t in range(HALF):
                cur = slab_sc[c][pl.ds(t, nseq, stride=PITCH), :]
                steps_out.append(taps[CONV_W] + taps[3] * cur + taps[2] * p1 + taps[1] * p2 + taps[0] * p3)
                p1, p2, p3 = cur, p1, p2
            for k, pk in enumerate((p1, p2, p3)):
                tail_sc[c][parity, k] = pk
            xc_slabs.append(jnp.concatenate(steps_out, axis=0))
        st[q]["xc"] = jnp.concatenate(xc_slabs, axis=1)

    def lru_gates(q):
        st[q]["ri"] = _dot(st[q]["xc"].astype(jnp.bfloat16), gate_sc[q])

    def lru_coeffs(q):
        cols = slice(q * GATE_TILE, (q + 1) * GATE_TILE)
        ri = st[q].pop("ri")
        st[q]["ab"] = _lru_elementwise(ri[:, :GATE_TILE], ri[:, GATE_TILE:], st[q].pop("xc"),
                                       half_rb[:, cols], half_ib[:, cols], half_c[:, cols])

    def lru_scan(q):
        a, b = st[q].pop("ab")
        for i in range(slabs_per_tile):
            c = slabs_per_tile * q + i
            h = h_sc[c][...]
            for t in range(HALF):
                h = (a[t * nseq:(t + 1) * nseq, i * LANES:(i + 1) * LANES] * h
                     + b[t * nseq:(t + 1) * nseq, i * LANES:(i + 1) * LANES])
                slab_sc[c][pl.ds(t, nseq, stride=PITCH), :] = h
            h_sc[c][...] = h

    def lru_out(q):
        hs = jnp.concatenate(
            [jnp.concatenate([slab_sc[c][s * PITCH:s * PITCH + HALF, :]
                              for c in range(slabs_per_tile * q, slabs_per_tile * (q + 1))], axis=1)
             for s in range(nseq)], axis=0)
        yb = _dot(sg["xn"], w_in_ref[:, yb_col0 + q * GATE_TILE:yb_col0 + (q + 1) * GATE_TILE])
        st[q]["out_b"] = (hs * _gelu(yb)).astype(jnp.bfloat16)

    def lru_mix(q):
        r0 = D_SGU + q * GATE_TILE
        sg["acc"] = sg["acc"] + _dot(st[q].pop("out_b"), w_out_ref[r0:r0 + GATE_TILE, :])

    def sgu_u():
        sg["u"] = _gelu(_dot(sg["xn"], w_in_ref[:, 0:D_SGU]))

    def sgu_v():
        v = _layer_norm(_gelu(_dot(sg["xn"], w_in_ref[:, D_SGU:2 * D_SGU])), sgu_g_ref[...], sgu_bn_ref[...])
        sg["vb"] = v.astype(jnp.bfloat16)

    def sgu_gate():
        vb = sg.pop("vb")
        lane = lax.broadcasted_iota(jnp.int32, (HALF, LANES), 1)
        bias = sgu_bias_ref[pl.ds(pl.multiple_of(parity * HALF, HALF), HALF), :]
        v_full = [jnp.concatenate([vprev_sc[1 - parity, s], vb[s * HALF:(s + 1) * HALF]], axis=0)
                  for s in range(nseq)]
        vprev_sc[parity] = vb.reshape(nseq, HALF, D_SGU)
        gate_rows = [[] for _ in range(nseq)]
        for s in range(0, nseq, 2):
            for g in range(N_LANE_GROUPS):
                lanes = slice(g * LANES, (g + 1) * LANES)
                rhs = jnp.concatenate([v_full[s][:, lanes], v_full[s + 1][:, lanes]], axis=1)
                res = _dot(sgu_lhs_sc[parity, g], rhs)
                for i in range(2):
                    part = res[:, i * LANES:(i + 1) * LANES]
                    gate_rows[s + i].append(jnp.where(lane < SGU_HEAD_DIM, part[:HALF], part[HALF:]))
        gate = jnp.concatenate([jnp.concatenate(r, axis=1) + bias for r in gate_rows], axis=0)
        sg["out_a"] = (sg.pop("u") * gate).astype(jnp.bfloat16)

    def sgu_mix():
        sg["acc"] = sg["x"] + _dot(sg.pop("out_a"), w_out_ref[0:D_SGU, :])

    def mix_store():
        ymix_sc[parity] = sg["acc"]

    tiles = range(N_GATE_TILES)
    mlp_norm()
    mix_norm()
    for q in tiles:
        lru_proj(q)
    mlp_up(0)
    for q in tiles:
        lru_conv(q)
        lru_gates(q)
    mlp_up(1)
    sgu_u()
    sgu_v()
    for q in tiles:
        lru_coeffs(q)
        lru_scan(q)
    mlp_down(0)
    mlp_down(1)
    for q in tiles:
        lru_out(q)
    mlp_up(2)
    mlp_up(3)
    sgu_gate()
    sgu_mix()
    mlp_down(2)
    mlp_down(3)
    for q in tiles:
        lru_mix(q)
    mix_store()
    mlp_store()


def _const_spec(shape):
    nd = len(shape)
    return pl.BlockSpec(shape, lambda *_: (0,) * nd, pipeline_mode=pl.Buffered(1))


def _layer_spec(arr, l):
    nd = arr.ndim - 1
    return pl.BlockSpec((None,) + arr.shape[1:], lambda *_: (l,) + (0,) * nd, pipeline_mode=pl.Buffered(1))


_BIG_WEIGHTS = ("w_in", "w_out", "w1", "w2")
_PROMPT_PARAMS = ("rows", "w_in", "conv_wb", "gate_rw", "gate_iw", "sgu_w", "sgu_bias", "w_out", "w1", "w2")
_SAMPLE_MIXER_PARAMS = ("rows", "w_in", "conv_w", "gate_rw", "gate_iw", "sgu_w8", "sgu_b8", "w_out")


def _param_spec(p, k, l):
    return _layer_spec(p[k], 0 if k in _BIG_WEIGHTS else l)


def _layer_prompt(x, xs, p, l, *, final_norm, cast_next=()):
    nseq, seq, _ = x.shape
    n_tiles = seq // HALF
    rows = nseq * HALF
    assert xs.shape == (2 * rows, D_MODEL)
    consts = [p[k] for k in _PROMPT_PARAMS]
    cast_in_specs, cast_out_specs, cast_shapes = [], [], []
    for w in cast_next:
        k_rows, n_cols = w.shape[1:]
        blk = k_rows // n_tiles
        assert blk * n_tiles == k_rows and blk % (2 * SUBLANES) == 0
        cast_in_specs.append(pl.BlockSpec((None, blk, n_cols), lambda j: (l + 1, jnp.minimum(j, n_tiles - 1), 0)))
        cast_out_specs.append(pl.BlockSpec((None, blk, n_cols), lambda j: (0, jnp.minimum(j, n_tiles - 1), 0)))
        cast_shapes.append(jax.ShapeDtypeStruct((1, k_rows, n_cols), jnp.bfloat16))
    x_spec = pl.BlockSpec((nseq, HALF, D_MODEL), lambda j: (0, jnp.minimum(j, n_tiles - 1), 0))
    y_spec = pl.BlockSpec((nseq, HALF, D_MODEL), lambda j: (0, jnp.clip(j - 1, 0, n_tiles - 1), 0))
    xs_spec = pl.BlockSpec((rows, D_MODEL), lambda j: (jnp.where(j > n_tiles, 1, 0), 0),
                           pipeline_mode=pl.Buffered(1))
    y, ys, hlast, convnew, *cast = pl.pallas_call(
        functools.partial(_layer_prompt_kernel, nseq=nseq, n_tiles=n_tiles, final_norm=final_norm,
                          n_cast=len(cast_next)),
        grid=(n_tiles + 2,),
        in_specs=[x_spec, xs_spec] + [_param_spec(p, k, l) for k in _PROMPT_PARAMS] + cast_in_specs,
        out_specs=([y_spec, xs_spec, _const_spec((nseq, D_LRU)), _const_spec((CONV_W - 1, nseq, D_LRU))]
                   + cast_out_specs),
        out_shape=[jax.ShapeDtypeStruct(x.shape, jnp.float32),
                   jax.ShapeDtypeStruct(xs.shape, jnp.float32),
                   jax.ShapeDtypeStruct((nseq, D_LRU), jnp.float32),
                   jax.ShapeDtypeStruct((CONV_W - 1, nseq, D_LRU), jnp.float32)] + cast_shapes,
        scratch_shapes=([pltpu.VMEM((2, nseq, HALF, D_SGU), jnp.bfloat16),
                         pltpu.VMEM((2, N_LANE_GROUPS, CHUNK, CHUNK), jnp.bfloat16),
                         pltpu.VMEM((2, rows, D_MODEL), jnp.float32),
                         pltpu.VMEM((N_GATE_TILES, GATE_TILE, 2 * GATE_TILE), jnp.bfloat16)]
                        + [pltpu.VMEM((nseq * PITCH, LANES), jnp.float32)] * N_SLABS
                        + [pltpu.VMEM((2, CONV_W - 1, nseq, LANES), jnp.float32)] * N_SLABS
                        + [pltpu.VMEM((nseq, LANES), jnp.float32)] * N_SLABS),
        compiler_params=pltpu.CompilerParams(dimension_semantics=("arbitrary",), vmem_limit_bytes=VMEM_LIMIT),
        name="layer_prompt",
    )(x, xs, *consts, *cast_next)
    return y, ys, hlast, jnp.transpose(convnew, (1, 0, 2)), cast


def _mixer_sample_kernel(*refs, steps, nb, seq_major, n_cast):
    n_in = 3 + len(_SAMPLE_MIXER_PARAMS)
    (x_ref, cbuf_ref, h0_ref, rows_ref, w_in_ref, conv_w_ref, gate_rw_ref, gate_iw_ref, sgu_w8_ref, sgu_b8_ref,
     w_out_ref) = refs[:n_in]
    cast_in, refs = refs[n_in:n_in + n_cast], refs[n_in + n_cast:]
    y_ref, v_ref, hlast_ref, convnew_ref = refs[:4]
    cast_out, refs = refs[4:4 + n_cast], refs[4 + n_cast:]
    gate_sc, v_relay_sc, relay_sc = refs[0], refs[1:1 + N_LANE_GROUPS], refs[1 + N_LANE_GROUPS:]
    for w_f32, w_bf16 in zip(cast_in, cast_out):
        w_bf16[...] = w_f32[...].astype(jnp.bfloat16)
    (g_mix_ref, _, gate_rb_ref, gate_ib_ref, lam_ref, conv_b_ref, sgu_g_ref, sgu_bn_ref, _) = _row_views(rows_ref)

    @pl.when(pl.program_id(0) == 0)
    def _():
        _build_gate_tiles(gate_rw_ref, gate_iw_ref, gate_sc)

    if seq_major:
        for c in range(D_MODEL // LANES):
            relay_sc[c][...] = x_ref[:, c * LANES:(c + 1) * LANES]
        x = jnp.concatenate(
            [jnp.concatenate([relay_sc[c][pl.ds(t, nb, stride=steps), :] for c in range(D_MODEL // LANES)], axis=1)
             for t in range(steps)], axis=0)
    else:
        x = x_ref[...].reshape(steps * nb, D_MODEL)
    xn = _rms_norm(x, g_mix_ref[...]).astype(jnp.bfloat16)

    u = _gelu(_dot(xn, w_in_ref[:, 0:D_SGU]))
    v = _layer_norm(_gelu(_dot(xn, w_in_ref[:, D_SGU:2 * D_SGU])), sgu_g_ref[...], sgu_bn_ref[...])
    pitch = nb + SUBLANES
    for g in range(N_LANE_GROUPS):
        for t in range(steps):
            v_relay_sc[g][t * pitch:t * pitch + nb, :] = v[t * nb:(t + 1) * nb, g * LANES:(g + 1) * LANES]
    for s in range(nb):
        for g in range(N_LANE_GROUPS):
            v_ref[s, :, g * LANES:(g + 1) * LANES] = v_relay_sc[g][pl.ds(s, steps, stride=pitch), :]
    gate_rows = []
    for t in range(steps):
        acc = sgu_b8_ref[t:t + 1, :] + sgu_w8_ref[t, 0:1, :] * v[0:nb]
        for s in range(1, t + 1):
            acc = acc + sgu_w8_ref[t, s:s + 1, :] * v[s * nb:(s + 1) * nb]
        gate_rows.append(acc)
    out_a = (u * jnp.concatenate(gate_rows, axis=0)).astype(jnp.bfloat16)

    xb = _dot(xn, w_in_ref[:, 2 * D_SGU:2 * D_SGU + D_LRU])
    xp = jnp.concatenate([cbuf_ref[k] for k in range(CONV_W - 1)] + [xb], axis=0)
    xc = conv_b_ref[...] + conv_w_ref[0:1, :] * xp[0:steps * nb]
    for k in range(1, CONV_W):
        xc = xc + conv_w_ref[k:k + 1, :] * xp[k * nb:(k + steps) * nb]
    convnew_ref[...] = xp[steps * nb:].reshape(CONV_W - 1, nb, D_LRU)

    a, b = _lru_coeffs(xc, gate_sc, gate_rb_ref[...], gate_ib_ref[...], lam_ref[...])
    h = h0_ref[...]
    hs = []
    for t in range(steps):
        h = a[t * nb:(t + 1) * nb] * h + b[t * nb:(t + 1) * nb]
        hs.append(h)
    hlast_ref[...] = h

    yb = _dot(xn, w_in_ref[:, 2 * D_SGU + D_LRU:])
    out_b = (jnp.concatenate(hs, axis=0) * _gelu(yb)).astype(jnp.bfloat16)
    out = _dot(jnp.concatenate([out_a, out_b], axis=1), w_out_ref[...])
    y_ref[...] = (x + out).reshape(steps, nb, D_MODEL)


def _mixer_sample(x, h0_all, cbuf_all, p, l, *, seq_major, groups, cast=()):
    nb_all = h0_all.shape[1]
    steps = x.shape[0] // nb_all if seq_major else x.shape[0]
    nb = nb_all // groups
    cast_in_specs, cast_out_specs, cast_shapes = [], [], []
    for w in cast:
        k_rows, n_cols = w.shape[1:]
        blk = k_rows // groups
        assert blk * groups == k_rows and blk % (2 * SUBLANES) == 0
        cast_in_specs.append(pl.BlockSpec((None, blk, n_cols), lambda i: (l, i, 0)))
        cast_out_specs.append(pl.BlockSpec((None, blk, n_cols), lambda i: (0, i, 0)))
        cast_shapes.append(jax.ShapeDtypeStruct((1, k_rows, n_cols), jnp.bfloat16))
    if seq_major:
        x_spec = pl.BlockSpec((nb * steps, D_MODEL), lambda i: (i, 0))
        relay = [pltpu.VMEM((nb * steps, LANES), jnp.float32)] * (D_MODEL // LANES)
    else:
        x_spec = pl.BlockSpec((steps, nb, D_MODEL), lambda i: (0, i, 0))
        relay = []
    group = lambda lead, width: pl.BlockSpec((lead, nb, width), lambda i: (0, i, 0))
    out_shapes = [(steps, nb_all, D_MODEL), (nb_all, steps, D_SGU), (nb_all, D_LRU), (CONV_W - 1, nb_all, D_LRU)]
    v_relay = [pltpu.VMEM((steps * (nb + SUBLANES), LANES), jnp.float32)] * N_LANE_GROUPS
    y, v, hlast, convnew, *cast_bf16 = pl.pallas_call(
        functools.partial(_mixer_sample_kernel, steps=steps, nb=nb, seq_major=seq_major, n_cast=len(cast)),
        grid=(groups,),
        in_specs=([x_spec,
                   pl.BlockSpec((None, CONV_W - 1, nb, D_LRU), lambda i: (l, 0, i, 0)),
                   pl.BlockSpec((None, nb, D_LRU), lambda i: (l, i, 0))]
                  + [_param_spec(p, k, l) for k in _SAMPLE_MIXER_PARAMS] + cast_in_specs),
        out_specs=[group(steps, D_MODEL), pl.BlockSpec((nb, steps, D_SGU), lambda i: (i, 0, 0)),
                   pl.BlockSpec((nb, D_LRU), lambda i: (i, 0)), group(CONV_W - 1, D_LRU)] + cast_out_specs,
        out_shape=[jax.ShapeDtypeStruct(s, jnp.float32) for s in out_shapes] + cast_shapes,
        scratch_shapes=[pltpu.VMEM((N_GATE_TILES, GATE_TILE, 2 * GATE_TILE), jnp.bfloat16)] + v_relay + relay,
        compiler_params=pltpu.CompilerParams(dimension_semantics=("arbitrary",), vmem_limit_bytes=VMEM_LIMIT),
        name="mixer_sample",
    )(x, cbuf_all, h0_all, *[p[k] for k in _SAMPLE_MIXER_PARAMS], *cast)
    return y, v, hlast, convnew, cast_bf16


CAST_STEPS = 4


def _cast_kernel(*refs):
    n = len(refs) // 2
    for w_f32, w_bf16 in zip(refs[:n], refs[n:]):
        w_bf16[...] = w_f32[...].astype(jnp.bfloat16)


def _cast_layer(weights, l):
    in_specs, out_specs, out_shapes = [], [], []
    for w in weights:
        k_rows, n_cols = w.shape[1:]
        blk = k_rows // CAST_STEPS
        assert blk * CAST_STEPS == k_rows and blk % (2 * SUBLANES) == 0
        in_specs.append(pl.BlockSpec((None, blk, n_cols), lambda j: (l, j, 0)))
        out_specs.append(pl.BlockSpec((None, blk, n_cols), lambda j: (0, j, 0)))
        out_shapes.append(jax.ShapeDtypeStruct((1, k_rows, n_cols), jnp.bfloat16))
    return pl.pallas_call(
        _cast_kernel,
        grid=(CAST_STEPS,),
        in_specs=in_specs,
        out_specs=out_specs,
        out_shape=out_shapes,
        compiler_params=pltpu.CompilerParams(dimension_semantics=("arbitrary",), vmem_limit_bytes=VMEM_LIMIT),
        name="cast_weights",
    )(*weights)


def _prepare_params(steps, nseq, norm_mix_g, w_in, conv_w, conv_b, gate_r_w, gate_r_b, gate_i_w, gate_i_b,
                    lru_lambda, sgu_norm_g, sgu_norm_b, sgu_w, sgu_b, w_out, norm_mlp_g, mlp_w1, mlp_w2, final_norm_g):
    conv_wb = jnp.concatenate([conv_w, conv_b[:, None, :]], axis=1)
    rows = [None] * N_ROWS
    rows[ROW_G_MIX], rows[ROW_G_MLP] = norm_mix_g, norm_mlp_g
    rows[ROW_GATE_RB], rows[ROW_GATE_IB], rows[ROW_LAM], rows[ROW_CONV_B] = gate_r_b, gate_i_b, lru_lambda, conv_b
    rows[ROW_SGU] = jnp.concatenate([sgu_norm_g, sgu_norm_b], axis=-1)
    rows[ROW_G_FINAL] = jnp.broadcast_to(final_norm_g, (DEPTH, D_MODEL))
    return {
        "rows": jnp.stack(rows, axis=1),
        "conv_w": conv_w,
        "conv_wb": jnp.broadcast_to(conv_wb.reshape(DEPTH, CONV_W + 1, N_SLABS, 1, LANES),
                                    (DEPTH, CONV_W + 1, N_SLABS, nseq, LANES)),
        "gate_rw": gate_r_w,
        "gate_iw": gate_i_w,
        "sgu_w": sgu_w,
        "sgu_bias": jnp.repeat(jnp.transpose(sgu_b, (0, 2, 1)), SGU_HEAD_DIM, axis=2),
        "sgu_w8": jnp.repeat(jnp.transpose(sgu_w[:, :, :steps, :steps], (0, 2, 3, 1)), SGU_HEAD_DIM, axis=3),
        "sgu_b8": jnp.repeat(jnp.transpose(sgu_b[:, :, :steps], (0, 2, 1)), SGU_HEAD_DIM, axis=2),
    }


def kernel(x_prompt, x_sample, state_lru_h, state_conv, norm_mix_g, w_in, conv_w, conv_b, gate_r_w, gate_r_b, gate_i_w, gate_i_b, lru_lambda, sgu_norm_g, sgu_norm_b, sgu_w, sgu_b, w_out, norm_mlp_g, mlp_w1, mlp_w2, final_norm_g):
    nseq, seq, _ = x_prompt.shape
    nb, steps, _ = x_sample.shape
    assert seq % CHUNK == 0 and nseq == SUBLANES and steps <= CHUNK
    p = _prepare_params(steps, nseq, norm_mix_g, w_in, conv_w, conv_b, gate_r_w, gate_r_b, gate_i_w, gate_i_b,
                        lru_lambda, sgu_norm_g, sgu_norm_b, sgu_w, sgu_b, w_out, norm_mlp_g, mlp_w1, mlp_w2,
                        final_norm_g)

    xp = x_prompt
    xs = x_sample.reshape(nb * steps, D_MODEL)
    cbuf_tm = jnp.transpose(state_conv, (0, 2, 1, 3))
    hs_p, convs_p, hs_s, convs_s, vs_s = [], [], [], [], []
    big_f32 = (w_in, w_out, mlp_w1, mlp_w2)
    big = _cast_layer(big_f32[:2], 0)
    for l in range(DEPTH):
        last = l == DEPTH - 1
        first = l == 0
        p.update(zip(_BIG_WEIGHTS, big))
        xs, v_s, h_s, conv_s, mlp_bf16 = _mixer_sample(
            xs, state_lru_h, cbuf_tm, p, l, seq_major=first,
            groups=SAMPLE_GROUPS_CAST if first else SAMPLE_GROUPS, cast=big_f32[2:] if first else ())
        p.update(zip(_BIG_WEIGHTS[2:], mlp_bf16))
        xp, xs, h_p, conv_p, big = _layer_prompt(xp, xs.reshape(steps * nb, D_MODEL), p, l, final_norm=last,
                                                 cast_next=() if last else big_f32)
        xs = xs.reshape(steps, nb, D_MODEL)
        hs_p.append(h_p)
        convs_p.append(conv_p)
        hs_s.append(h_s)
        convs_s.append(conv_s)
        vs_s.append(v_s)

    y_sample = jnp.transpose(xs, (1, 0, 2))
    return (xp, y_sample, jnp.stack(hs_p), jnp.stack(convs_p), jnp.stack(hs_s),
            jnp.transpose(jnp.stack(convs_s), (0, 2, 1, 3)), jnp.stack(vs_s))
```

```python
import functools

import jax
import jax.numpy as jnp
from jax import lax
from jax.experimental import pallas as pl
from jax.experimental.pallas import tpu as pltpu

D_MODEL = 1024
DEPTH = 2
SGU_HEADS = 8
SGU_HEAD_DIM = 64
D_SGU = SGU_HEADS * SGU_HEAD_DIM
CHUNK = 128
D_LRU = 1024
LRU_BLOCKS = 16
LRU_BLOCK_DIM = 64
CONV_W = 4
LRU_C = 8.0
D_FF = 4 * D_MODEL
EPS = 1e-6

SUBLANES = 8
LANES = 128
GATE_TILE = 256
N_GATE_TILES = D_LRU // GATE_TILE
HEADS_PER_LANE_GROUP = LANES // SGU_HEAD_DIM
N_LANE_GROUPS = D_SGU // LANES
N_SLABS = D_LRU // LANES
FF_CHUNK = D_MODEL
N_FF_CHUNKS = D_FF // FF_CHUNK

SAMPLE_GROUPS = 2
SAMPLE_GROUPS_CAST = 4
HALF = CHUNK // 2
PITCH = HALF + SUBLANES
V7X_VMEM_BYTES = 64 * 1024 * 1024
VMEM_HEADROOM = 2 * 1024 * 1024
VMEM_LIMIT = V7X_VMEM_BYTES - VMEM_HEADROOM
F32_TINY = 1.1754944e-38


def _rms_norm(x, g):
    return x * lax.rsqrt(jnp.mean(x * x, axis=-1, keepdims=True) + EPS) * g


def _layer_norm(x, g, b):
    mu = jnp.mean(x, axis=-1, keepdims=True)
    xc = x - mu
    return xc * lax.rsqrt(jnp.mean(xc * xc, axis=-1, keepdims=True) + EPS) * g + b


def _gelu(x):
    return jax.nn.gelu(x, approximate=True)


def _dot(a, b):
    return jnp.dot(a, b, preferred_element_type=jnp.float32)


def _lru_elementwise(r_pre, i_pre, xc, half_rb, half_ib, half_c):
    th_r = jnp.tanh(r_pre + half_rb)
    th_i = jnp.tanh(i_pre + half_ib)
    log_a = half_c * th_r + half_c
    i = 0.5 * th_i + 0.5
    a = jnp.exp(log_a)
    y = jnp.tanh(log_a) * (-1.0 - a * a)
    b = (y * lax.rsqrt(jnp.maximum(y, F32_TINY))) * (i * xc)
    return a, b


def _lru_consts(gate_rb, gate_ib, lam):
    return 0.5 * gate_rb, 0.5 * gate_ib, (0.5 * LRU_C) * jax.nn.log_sigmoid(lam)


def _build_gate_tiles(gate_rw_ref, gate_iw_ref, gate_sc):
    per_tile = GATE_TILE // LRU_BLOCK_DIM
    k_idx = lax.broadcasted_iota(jnp.int32, (LRU_BLOCK_DIM, GATE_TILE), 0)
    n_idx = lax.broadcasted_iota(jnp.int32, (LRU_BLOCK_DIM, GATE_TILE), 1)
    replicate = (n_idx % LRU_BLOCK_DIM == k_idx).astype(jnp.bfloat16)
    row_blk = lax.broadcasted_iota(jnp.int32, (GATE_TILE, GATE_TILE), 0) // LRU_BLOCK_DIM
    col_blk = lax.broadcasted_iota(jnp.int32, (GATE_TILE, GATE_TILE), 1) // LRU_BLOCK_DIM
    for j in range(N_GATE_TILES):
        for k, ref in enumerate((gate_rw_ref, gate_iw_ref)):
            stacked = ref[per_tile * j:per_tile * (j + 1)].reshape(GATE_TILE, LRU_BLOCK_DIM)
            rep = _dot(stacked.astype(jnp.bfloat16), replicate)
            gate_sc[j, :, k * GATE_TILE:(k + 1) * GATE_TILE] = jnp.where(
                row_blk == col_blk, 0.5 * rep, 0.0).astype(jnp.bfloat16)


def _lru_coeffs(xc, gate_w_ref, gate_rb, gate_ib, lam):
    xcb = xc.astype(jnp.bfloat16)
    r_parts, i_parts = [], []
    for j in range(N_GATE_TILES):
        ri = _dot(xcb[:, j * GATE_TILE:(j + 1) * GATE_TILE], gate_w_ref[j])
        r_parts.append(ri[:, :GATE_TILE])
        i_parts.append(ri[:, GATE_TILE:])
    return _lru_elementwise(jnp.concatenate(r_parts, axis=1), jnp.concatenate(i_parts, axis=1), xc,
                            *_lru_consts(gate_rb, gate_ib, lam))


ROW_G_MIX, ROW_G_MLP, ROW_GATE_RB, ROW_GATE_IB, ROW_LAM, ROW_CONV_B, ROW_SGU, ROW_G_FINAL = range(8)
N_ROWS = 8


def _row_views(rows_ref):
    one = lambda k, lo=0, hi=D_MODEL: rows_ref.at[k:k + 1, lo:hi]
    return (one(ROW_G_MIX), one(ROW_G_MLP), one(ROW_GATE_RB), one(ROW_GATE_IB), one(ROW_LAM), one(ROW_CONV_B),
            one(ROW_SGU, 0, D_SGU), one(ROW_SGU, D_SGU, 2 * D_SGU), one(ROW_G_FINAL))


def _mlp_up(xn, w1_ref, c):
    cols = slice(c * FF_CHUNK, (c + 1) * FF_CHUNK)
    return jnp.square(jnp.maximum(_dot(xn, w1_ref[:, cols]), 0.0)).astype(jnp.bfloat16)


def _mlp_down(hid, w2_ref, c):
    return _dot(hid, w2_ref[c * FF_CHUNK:(c + 1) * FF_CHUNK, :])


def _layer_prompt_kernel(*refs, nseq, n_tiles, final_norm, n_cast, sample_seq_major):
    n_in = 2 + len(_PROMPT_PARAMS)
    (x_ref, xs_ref, rows_ref, w_in_ref, conv_wb_ref, gate_rw_ref, gate_iw_ref, sgu_w_ref, sgu_bias_ref, w_out_ref,
     w1_ref, w2_ref) = refs[:n_in]
    (g_mix_ref, g_mlp_ref, gate_rb_ref, gate_ib_ref, lam_ref, _, sgu_g_ref, sgu_bn_ref, gf_ref) = _row_views(rows_ref)
    cast_in, refs = refs[n_in:n_in + n_cast], refs[n_in + n_cast:]
    y_ref, ys_ref, hlast_ref, convnew_ref = refs[:4]
    cast_out, refs = refs[4:4 + n_cast], refs[4 + n_cast:]
    vprev_sc, sgu_lhs_sc, ymix_sc, gate_sc = refs[:4]
    slab_scratch = refs[4:]
    rows = nseq * HALF
    j = pl.program_id(0)
    slab_sc = slab_scratch[0:N_SLABS]
    tail_sc = slab_scratch[N_SLABS:2 * N_SLABS]
    h_sc = slab_scratch[2 * N_SLABS:3 * N_SLABS]

    @pl.when(j == 0)
    def _():
        for c in range(N_SLABS):
            tail_sc[c][...] = jnp.zeros_like(tail_sc[c])
            h_sc[c][...] = jnp.zeros_like(h_sc[c])
        vprev_sc[...] = jnp.zeros_like(vprev_sc)
        ymix_sc[1] = xs_ref[...]
        _build_gate_tiles(gate_rw_ref, gate_iw_ref, gate_sc)
        t_idx = lax.broadcasted_iota(jnp.int32, (CHUNK, CHUNK), 0)
        s_idx = lax.broadcasted_iota(jnp.int32, (CHUNK, CHUNK), 1)
        for g in range(N_LANE_GROUPS):
            first, second = [], []
            for hh in range(HEADS_PER_LANE_GROUP):
                wm = jnp.where(s_idx <= t_idx, sgu_w_ref[HEADS_PER_LANE_GROUP * g + hh], 0.0)
                first.append(pltpu.roll(wm[:HALF], HALF, axis=1))
                second.append(wm[HALF:])
            sgu_lhs_sc[0, g] = jnp.concatenate(first, axis=0).astype(jnp.bfloat16)
            sgu_lhs_sc[1, g] = jnp.concatenate(second, axis=0).astype(jnp.bfloat16)

    parity = j % 2

    @pl.when(j < n_tiles)
    def _():
        _fused_step(x_ref, g_mix_ref, w_in_ref, conv_wb_ref, gate_rb_ref, gate_ib_ref, lam_ref, sgu_g_ref, sgu_bn_ref,
                    sgu_bias_ref, w_out_ref, g_mlp_ref, w1_ref, w2_ref, gf_ref, y_ref, vprev_sc, sgu_lhs_sc, ymix_sc,
                    gate_sc, slab_sc, tail_sc, h_sc, parity=parity, nseq=nseq, final_norm=final_norm)
        for w_f32, w_bf16 in zip(cast_in, cast_out):
            w_bf16[...] = w_f32[...].astype(jnp.bfloat16)

    def store_sample(tile):
        if not sample_seq_major:
            ys_ref[...] = tile
            return
        for c in range(N_SLABS):
            for t in range(nseq):
                slab_sc[c][t * PITCH:t * PITCH + HALF, :] = tile[t * HALF:(t + 1) * HALF, c * LANES:(c + 1) * LANES]
        for s in range(HALF):
            for c in range(N_SLABS):
                ys_ref[s * nseq:(s + 1) * nseq, c * LANES:(c + 1) * LANES] = (
                    slab_sc[c][pl.ds(s, nseq, stride=PITCH), :])

    @pl.when(j == 0)
    def _():
        store_sample(y_ref[...].reshape(rows, D_MODEL))

    @pl.when(j == n_tiles - 1)
    def _():
        for c in range(N_SLABS):
            cols = slice(c * LANES, (c + 1) * LANES)
            hlast_ref[:, cols] = h_sc[c][...]
            for k in range(CONV_W - 1):
                convnew_ref[CONV_W - 2 - k, :, cols] = tail_sc[c][parity, k]

    @pl.when(j == n_tiles + 1)
    def _():
        ymix_sc[1 - parity] = xs_ref[...]

    @pl.when(j >= n_tiles)
    def _():
        xm = ymix_sc[1 - parity]
        xn = _rms_norm(xm, g_mlp_ref[...]).astype(jnp.bfloat16)
        acc = xm
        for c in range(N_FF_CHUNKS):
            acc = acc + _mlp_down(_mlp_up(xn, w1_ref, c), w2_ref, c)
        ymix_sc[parity] = _rms_norm(acc, gf_ref[...]) if final_norm else acc

    @pl.when(j == n_tiles)
    def _():
        y_ref[...] = ymix_sc[parity].reshape(nseq, HALF, D_MODEL)

    @pl.when(j == n_tiles + 1)
    def _():
        store_sample(ymix_sc[parity])


def _fused_step(x_ref, g_mix_ref, w_in_ref, conv_wb_ref, gate_rb_ref, gate_ib_ref, lam_ref, sgu_g_ref, sgu_bn_ref,
                sgu_bias_ref, w_out_ref, g_mlp_ref, w1_ref, w2_ref, gf_ref, y_ref, vprev_sc, sgu_lhs_sc, ymix_sc,
                gate_sc, slab_sc, tail_sc, h_sc, *, parity, nseq, final_norm):
    rows = nseq * HALF
    half_rb, half_ib, half_c = _lru_consts(gate_rb_ref[...], gate_ib_ref[...], lam_ref[...])
    xb_col0 = 2 * D_SGU
    yb_col0 = 2 * D_SGU + D_LRU
    slabs_per_tile = GATE_TILE // LANES
    st = [{} for _ in range(N_GATE_TILES)]
    sg = {}
    ml = {}

    def mlp_norm():
        xm = ymix_sc[1 - parity]
        ml["acc"] = xm
        ml["xn"] = _rms_norm(xm, g_mlp_ref[...]).astype(jnp.bfloat16)

    def mlp_up(c):
        ml["hid", c] = _mlp_up(ml["xn"], w1_ref, c)

    def mlp_down(c):
        ml["acc"] = ml["acc"] + _mlp_down(ml.pop(("hid", c)), w2_ref, c)

    def mlp_store():
        out = _rms_norm(ml["acc"], gf_ref[...]) if final_norm else ml["acc"]
        y_ref[...] = out.reshape(nseq, HALF, D_MODEL)

    def mix_norm():
        sg["x"] = x_ref[...].reshape(rows, D_MODEL)
        sg["xn"] = _rms_norm(sg["x"], g_mix_ref[...]).astype(jnp.bfloat16)

    def lru_proj(q):
        xb = _dot(sg["xn"], w_in_ref[:, xb_col0 + q * GATE_TILE:xb_col0 + (q + 1) * GATE_TILE])
        for i in range(slabs_per_tile):
            for s in range(nseq):
                slab_sc[slabs_per_tile * q + i][s * PITCH:s * PITCH + HALF, :] = (
                    xb[s * HALF:(s + 1) * HALF, i * LANES:(i + 1) * LANES])

    def lru_conv(q):
        xc_slabs = []
        for c in range(slabs_per_tile * q, slabs_per_tile * (q + 1)):
            taps = [conv_wb_ref[k, c] for k in range(CONV_W + 1)]
            p1, p2, p3 = (tail_sc[c][1 - parity, k] for k in range(CONV_W - 1))
            steps_out = []
            for t in range(HALF):
                cur = slab_sc[c][pl.ds(t, nseq, stride=PITCH), :]
                steps_out.append(taps[CONV_W] + taps[3] * cur + taps[2] * p1 + taps[1] * p2 + taps[0] * p3)
                p1, p2, p3 = cur, p1, p2
            for k, pk in enumerate((p1, p2, p3)):
                tail_sc[c][parity, k] = pk
            xc_slabs.append(jnp.concatenate(steps_out, axis=0))
        st[q]["xc"] = jnp.concatenate(xc_slabs, axis=1)

    def lru_gates(q):
        st[q]["ri"] = _dot(st[q]["xc"].astype(jnp.bfloat16), gate_sc[q])

    def lru_coeffs(q):
        cols = slice(q * GATE_TILE, (q + 1) * GATE_TILE)
        ri = st[q].pop("ri")
        st[q]["ab"] = _lru_elementwise(ri[:, :GATE_TILE], ri[:, GATE_TILE:], st[q].pop("xc"),
                                       half_rb[:, cols], half_ib[:, cols], half_c[:, cols])

    def lru_scan(q):
        a, b = st[q].pop("ab")
        for i in range(slabs_per_tile):
            c = slabs_per_tile * q + i
            h = h_sc[c][...]
            for t in range(HALF):
                h = (a[t * nseq:(t + 1) * nseq, i * LANES:(i + 1) * LANES] * h
                     + b[t * nseq:(t + 1) * nseq, i * LANES:(i + 1) * LANES])
                slab_sc[c][pl.ds(t, nseq, stride=PITCH), :] = h
            h_sc[c][...] = h

    def lru_out(q):
        hs = jnp.concatenate(
            [jnp.concatenate([slab_sc[c][s * PITCH:s * PITCH + HALF, :]
                              for c in range(slabs_per_tile * q, slabs_per_tile * (q + 1))], axis=1)
             for s in range(nseq)], axis=0)
        yb = _dot(sg["xn"], w_in_ref[:, yb_col0 + q * GATE_TILE:yb_col0 + (q + 1) * GATE_TILE])
        st[q]["out_b"] = (hs * _gelu(yb)).astype(jnp.bfloat16)

    def lru_mix(q):
        r0 = D_SGU + q * GATE_TILE
        sg["acc"] = sg["acc"] + _dot(st[q].pop("out_b"), w_out_ref[r0:r0 + GATE_TILE, :])

    def sgu_u():
        sg["u"] = _gelu(_dot(sg["xn"], w_in_ref[:, 0:D_SGU]))

    def sgu_v():
        v = _layer_norm(_gelu(_dot(sg["xn"], w_in_ref[:, D_SGU:2 * D_SGU])), sgu_g_ref[...], sgu_bn_ref[...])
        sg["vb"] = v.astype(jnp.bfloat16)

    def sgu_gate():
        vb = sg.pop("vb")
        lane = lax.broadcasted_iota(jnp.int32, (HALF, LANES), 1)
        bias = sgu_bias_ref[pl.ds(pl.multiple_of(parity * HALF, HALF), HALF), :]
        v_full = [jnp.concatenate([vprev_sc[1 - parity, s], vb[s * HALF:(s + 1) * HALF]], axis=0)
                  for s in range(nseq)]
        vprev_sc[parity] = vb.reshape(nseq, HALF, D_SGU)
        gate_rows = [[] for _ in range(nseq)]
        for s in range(0, nseq, 2):
            for g in range(N_LANE_GROUPS):
                lanes = slice(g * LANES, (g + 1) * LANES)
                rhs = jnp.concatenate([v_full[s][:, lanes], v_full[s + 1][:, lanes]], axis=1)
                res = _dot(sgu_lhs_sc[parity, g], rhs)
                for i in range(2):
                    part = res[:, i * LANES:(i + 1) * LANES]
                    gate_rows[s + i].append(jnp.where(lane < SGU_HEAD_DIM, part[:HALF], part[HALF:]))
        gate = jnp.concatenate([jnp.concatenate(r, axis=1) + bias for r in gate_rows], axis=0)
        sg["out_a"] = (sg.pop("u") * gate).astype(jnp.bfloat16)

    def sgu_mix():
        sg["acc"] = sg["x"] + _dot(sg.pop("out_a"), w_out_ref[0:D_SGU, :])

    def mix_store():
        ymix_sc[parity] = sg["acc"]

    tiles = range(N_GATE_TILES)
    mlp_norm()
    mix_norm()
    for q in tiles:
        lru_proj(q)
    mlp_up(0)
    for q in tiles:
        lru_conv(q)
        lru_gates(q)
    mlp_up(1)
    sgu_u()
    sgu_v()
    for q in tiles:
        lru_coeffs(q)
        lru_scan(q)
    mlp_down(0)
    mlp_down(1)
    for q in tiles:
        lru_out(q)
    mlp_up(2)
    mlp_up(3)
    sgu_gate()
    sgu_mix()
    mlp_down(2)
    mlp_down(3)
    for q in tiles:
        lru_mix(q)
    mix_store()
    mlp_store()


def _const_spec(shape):
    nd = len(shape)
    return pl.BlockSpec(shape, lambda *_: (0,) * nd, pipeline_mode=pl.Buffered(1))


def _layer_spec(arr, l):
    nd = arr.ndim - 1
    return pl.BlockSpec((None,) + arr.shape[1:], lambda *_: (l,) + (0,) * nd, pipeline_mode=pl.Buffered(1))


_BIG_WEIGHTS = ("w_in", "w_out", "w1", "w2")
_PROMPT_PARAMS = ("rows", "w_in", "conv_wb", "gate_rw", "gate_iw", "sgu_w", "sgu_bias", "w_out", "w1", "w2")
_SAMPLE_MIXER_PARAMS = ("rows", "w_in", "conv_w", "gate_rw", "gate_iw", "sgu_w8", "sgu_b8", "w_out")


def _param_spec(p, k, l):
    return _layer_spec(p[k], 0 if k in _BIG_WEIGHTS else l)


def _layer_prompt(x, xs, p, l, *, final_norm, sample_seq_major, cast_next=()):
    nseq, seq, _ = x.shape
    n_tiles = seq // HALF
    rows = nseq * HALF
    assert xs.shape == (2 * rows, D_MODEL)
    consts = [p[k] for k in _PROMPT_PARAMS]
    cast_in_specs, cast_out_specs, cast_shapes = [], [], []
    for w in cast_next:
        k_rows, n_cols = w.shape[1:]
        blk = k_rows // n_tiles
        assert blk * n_tiles == k_rows and blk % (2 * SUBLANES) == 0
        cast_in_specs.append(pl.BlockSpec((None, blk, n_cols), lambda j: (l + 1, jnp.minimum(j, n_tiles - 1), 0)))
        cast_out_specs.append(pl.BlockSpec((None, blk, n_cols), lambda j: (0, jnp.minimum(j, n_tiles - 1), 0)))
        cast_shapes.append(jax.ShapeDtypeStruct((1, k_rows, n_cols), jnp.bfloat16))
    x_spec = pl.BlockSpec((nseq, HALF, D_MODEL), lambda j: (0, jnp.minimum(j, n_tiles - 1), 0))
    y_spec = pl.BlockSpec((nseq, HALF, D_MODEL), lambda j: (0, jnp.clip(j - 1, 0, n_tiles - 1), 0))
    xs_spec = pl.BlockSpec((rows, D_MODEL), lambda j: (jnp.where(j > n_tiles, 1, 0), 0),
                           pipeline_mode=pl.Buffered(1))
    y, ys, hlast, convnew, *cast = pl.pallas_call(
        functools.partial(_layer_prompt_kernel, nseq=nseq, n_tiles=n_tiles, final_norm=final_norm,
                          n_cast=len(cast_next), sample_seq_major=sample_seq_major),
        grid=(n_tiles + 2,),
        in_specs=[x_spec, xs_spec] + [_param_spec(p, k, l) for k in _PROMPT_PARAMS] + cast_in_specs,
        out_specs=([y_spec, xs_spec, _const_spec((nseq, D_LRU)), _const_spec((CONV_W - 1, nseq, D_LRU))]
                   + cast_out_specs),
        out_shape=[jax.ShapeDtypeStruct(x.shape, jnp.float32),
                   jax.ShapeDtypeStruct(xs.shape, jnp.float32),
                   jax.ShapeDtypeStruct((nseq, D_LRU), jnp.float32),
                   jax.ShapeDtypeStruct((CONV_W - 1, nseq, D_LRU), jnp.float32)] + cast_shapes,
        scratch_shapes=([pltpu.VMEM((2, nseq, HALF, D_SGU), jnp.bfloat16),
                         pltpu.VMEM((2, N_LANE_GROUPS, CHUNK, CHUNK), jnp.bfloat16),
                         pltpu.VMEM((2, rows, D_MODEL), jnp.float32),
                         pltpu.VMEM((N_GATE_TILES, GATE_TILE, 2 * GATE_TILE), jnp.bfloat16)]
                        + [pltpu.VMEM((nseq * PITCH, LANES), jnp.float32)] * N_SLABS
                        + [pltpu.VMEM((2, CONV_W - 1, nseq, LANES), jnp.float32)] * N_SLABS
                        + [pltpu.VMEM((nseq, LANES), jnp.float32)] * N_SLABS),
        compiler_params=pltpu.CompilerParams(dimension_semantics=("arbitrary",), vmem_limit_bytes=VMEM_LIMIT),
        name="layer_prompt",
    )(x, xs, *consts, *cast_next)
    return y, ys, hlast, jnp.transpose(convnew, (1, 0, 2)), cast


def _mixer_sample_kernel(*refs, steps, nb, seq_major, n_cast):
    n_in = 3 + len(_SAMPLE_MIXER_PARAMS)
    (x_ref, cbuf_ref, h0_ref, rows_ref, w_in_ref, conv_w_ref, gate_rw_ref, gate_iw_ref, sgu_w8_ref, sgu_b8_ref,
     w_out_ref) = refs[:n_in]
    cast_in, refs = refs[n_in:n_in + n_cast], refs[n_in + n_cast:]
    y_ref, v_ref, hlast_ref, convnew_ref = refs[:4]
    cast_out, refs = refs[4:4 + n_cast], refs[4 + n_cast:]
    gate_sc, v_relay_sc, relay_sc = refs[0], refs[1:1 + N_LANE_GROUPS], refs[1 + N_LANE_GROUPS:]
    for w_f32, w_bf16 in zip(cast_in, cast_out):
        w_bf16[...] = w_f32[...].astype(jnp.bfloat16)
    (g_mix_ref, _, gate_rb_ref, gate_ib_ref, lam_ref, conv_b_ref, sgu_g_ref, sgu_bn_ref, _) = _row_views(rows_ref)

    @pl.when(pl.program_id(0) == 0)
    def _():
        _build_gate_tiles(gate_rw_ref, gate_iw_ref, gate_sc)

    if seq_major:
        for c in range(D_MODEL // LANES):
            relay_sc[c][...] = x_ref[:, c * LANES:(c + 1) * LANES]
        x = jnp.concatenate(
            [jnp.concatenate([relay_sc[c][pl.ds(t, nb, stride=steps), :] for c in range(D_MODEL // LANES)], axis=1)
             for t in range(steps)], axis=0)
    else:
        x = x_ref[...].reshape(steps * nb, D_MODEL)
    xn = _rms_norm(x, g_mix_ref[...]).astype(jnp.bfloat16)

    u = _gelu(_dot(xn, w_in_ref[:, 0:D_SGU]))
    v = _layer_norm(_gelu(_dot(xn, w_in_ref[:, D_SGU:2 * D_SGU])), sgu_g_ref[...], sgu_bn_ref[...])
    pitch = nb + SUBLANES
    for g in range(N_LANE_GROUPS):
        for t in range(steps):
            v_relay_sc[g][t * pitch:t * pitch + nb, :] = v[t * nb:(t + 1) * nb, g * LANES:(g + 1) * LANES]
    for s in range(nb):
        for g in range(N_LANE_GROUPS):
            v_ref[s, :, g * LANES:(g + 1) * LANES] = v_relay_sc[g][pl.ds(s, steps, stride=pitch), :]
    gate_rows = []
    for t in range(steps):
        acc = sgu_b8_ref[t:t + 1, :] + sgu_w8_ref[t, 0:1, :] * v[0:nb]
        for s in range(1, t + 1):
            acc = acc + sgu_w8_ref[t, s:s + 1, :] * v[s * nb:(s + 1) * nb]
        gate_rows.append(acc)
    out_a = (u * jnp.concatenate(gate_rows, axis=0)).astype(jnp.bfloat16)

    xb = _dot(xn, w_in_ref[:, 2 * D_SGU:2 * D_SGU + D_LRU])
    xp = jnp.concatenate([cbuf_ref[k] for k in range(CONV_W - 1)] + [xb], axis=0)
    xc = conv_b_ref[...] + conv_w_ref[0:1, :] * xp[0:steps * nb]
    for k in range(1, CONV_W):
        xc = xc + conv_w_ref[k:k + 1, :] * xp[k * nb:(k + steps) * nb]
    convnew_ref[...] = xp[steps * nb:].reshape(CONV_W - 1, nb, D_LRU)

    a, b = _lru_coeffs(xc, gate_sc, gate_rb_ref[...], gate_ib_ref[...], lam_ref[...])
    h = h0_ref[...]
    hs = []
    for t in range(steps):
        h = a[t * nb:(t + 1) * nb] * h + b[t * nb:(t + 1) * nb]
        hs.append(h)
    hlast_ref[...] = h

    yb = _dot(xn, w_in_ref[:, 2 * D_SGU + D_LRU:])
    out_b = (jnp.concatenate(hs, axis=0) * _gelu(yb)).astype(jnp.bfloat16)
    out = _dot(jnp.concatenate([out_a, out_b], axis=1), w_out_ref[...])
    y_ref[...] = (x + out).reshape(steps, nb, D_MODEL)


def _mixer_sample(x, h0_all, cbuf_all, p, l, *, seq_major, groups, cast=()):
    nb_all = h0_all.shape[1]
    steps = x.shape[0] // nb_all if seq_major else x.shape[1]
    nb = nb_all // groups
    cast_in_specs, cast_out_specs, cast_shapes = [], [], []
    for w in cast:
        k_rows, n_cols = w.shape[1:]
        blk = k_rows // groups
        assert blk * groups == k_rows and blk % (2 * SUBLANES) == 0
        cast_in_specs.append(pl.BlockSpec((None, blk, n_cols), lambda i: (l, i, 0)))
        cast_out_specs.append(pl.BlockSpec((None, blk, n_cols), lambda i: (0, i, 0)))
        cast_shapes.append(jax.ShapeDtypeStruct((1, k_rows, n_cols), jnp.bfloat16))
    per_tile = HALF // nb
    assert per_tile * nb == HALF and nb_all % HALF == 0
    act_spec = pl.BlockSpec((None, steps, nb, D_MODEL), lambda i: (i // per_tile, 0, i % per_tile, 0))
    if seq_major:
        x_spec = pl.BlockSpec((nb * steps, D_MODEL), lambda i: (i, 0))
        relay = [pltpu.VMEM((nb * steps, LANES), jnp.float32)] * (D_MODEL // LANES)
    else:
        x_spec = act_spec
        relay = []
    group = lambda lead, width: pl.BlockSpec((lead, nb, width), lambda i: (0, i, 0))
    out_shapes = [(nb_all // HALF, steps, HALF, D_MODEL), (nb_all, steps, D_SGU), (nb_all, D_LRU),
                  (CONV_W - 1, nb_all, D_LRU)]
    v_relay = [pltpu.VMEM((steps * (nb + SUBLANES), LANES), jnp.float32)] * N_LANE_GROUPS
    y, v, hlast, convnew, *cast_bf16 = pl.pallas_call(
        functools.partial(_mixer_sample_kernel, steps=steps, nb=nb, seq_major=seq_major, n_cast=len(cast)),
        grid=(groups,),
        in_specs=([x_spec,
                   pl.BlockSpec((None, CONV_W - 1, nb, D_LRU), lambda i: (l, 0, i, 0)),
                   pl.BlockSpec((None, nb, D_LRU), lambda i: (l, i, 0))]
                  + [_param_spec(p, k, l) for k in _SAMPLE_MIXER_PARAMS] + cast_in_specs),
        out_specs=[act_spec, pl.BlockSpec((nb, steps, D_SGU), lambda i: (i, 0, 0)),
                   pl.BlockSpec((nb, D_LRU), lambda i: (i, 0)), group(CONV_W - 1, D_LRU)] + cast_out_specs,
        out_shape=[jax.ShapeDtypeStruct(s, jnp.float32) for s in out_shapes] + cast_shapes,
        scratch_shapes=[pltpu.VMEM((N_GATE_TILES, GATE_TILE, 2 * GATE_TILE), jnp.bfloat16)] + v_relay + relay,
        compiler_params=pltpu.CompilerParams(dimension_semantics=("arbitrary",), vmem_limit_bytes=VMEM_LIMIT),
        name="mixer_sample",
    )(x, cbuf_all, h0_all, *[p[k] for k in _SAMPLE_MIXER_PARAMS], *cast)
    return y, v, hlast, convnew, cast_bf16


CAST_STEPS = 4


def _cast_kernel(*refs):
    n = len(refs) // 2
    for w_f32, w_bf16 in zip(refs[:n], refs[n:]):
        w_bf16[...] = w_f32[...].astype(jnp.bfloat16)


def _cast_layer(weights, l):
    in_specs, out_specs, out_shapes = [], [], []
    for w in weights:
        k_rows, n_cols = w.shape[1:]
        blk = k_rows // CAST_STEPS
        assert blk * CAST_STEPS == k_rows and blk % (2 * SUBLANES) == 0
        in_specs.append(pl.BlockSpec((None, blk, n_cols), lambda j: (l, j, 0)))
        out_specs.append(pl.BlockSpec((None, blk, n_cols), lambda j: (0, j, 0)))
        out_shapes.append(jax.ShapeDtypeStruct((1, k_rows, n_cols), jnp.bfloat16))
    return pl.pallas_call(
        _cast_kernel,
        grid=(CAST_STEPS,),
        in_specs=in_specs,
        out_specs=out_specs,
        out_shape=out_shapes,
        compiler_params=pltpu.CompilerParams(dimension_semantics=("arbitrary",), vmem_limit_bytes=VMEM_LIMIT),
        name="cast_weights",
    )(*weights)


def _prepare_params(steps, nseq, norm_mix_g, w_in, conv_w, conv_b, gate_r_w, gate_r_b, gate_i_w, gate_i_b,
                    lru_lambda, sgu_norm_g, sgu_norm_b, sgu_w, sgu_b, w_out, norm_mlp_g, mlp_w1, mlp_w2, final_norm_g):
    conv_wb = jnp.concatenate([conv_w, conv_b[:, None, :]], axis=1)
    rows = [None] * N_ROWS
    rows[ROW_G_MIX], rows[ROW_G_MLP] = norm_mix_g, norm_mlp_g
    rows[ROW_GATE_RB], rows[ROW_GATE_IB], rows[ROW_LAM], rows[ROW_CONV_B] = gate_r_b, gate_i_b, lru_lambda, conv_b
    rows[ROW_SGU] = jnp.concatenate([sgu_norm_g, sgu_norm_b], axis=-1)
    rows[ROW_G_FINAL] = jnp.broadcast_to(final_norm_g, (DEPTH, D_MODEL))
    return {
        "rows": jnp.stack(rows, axis=1),
        "conv_w": conv_w,
        "conv_wb": jnp.broadcast_to(conv_wb.reshape(DEPTH, CONV_W + 1, N_SLABS, 1, LANES),
                                    (DEPTH, CONV_W + 1, N_SLABS, nseq, LANES)),
        "gate_rw": gate_r_w,
        "gate_iw": gate_i_w,
        "sgu_w": sgu_w,
        "sgu_bias": jnp.repeat(jnp.transpose(sgu_b, (0, 2, 1)), SGU_HEAD_DIM, axis=2),
        "sgu_w8": jnp.repeat(jnp.transpose(sgu_w[:, :, :steps, :steps], (0, 2, 3, 1)), SGU_HEAD_DIM, axis=3),
        "sgu_b8": jnp.repeat(jnp.transpose(sgu_b[:, :, :steps], (0, 2, 1)), SGU_HEAD_DIM, axis=2),
    }


def kernel(x_prompt, x_sample, state_lru_h, state_conv, norm_mix_g, w_in, conv_w, conv_b, gate_r_w, gate_r_b, gate_i_w, gate_i_b, lru_lambda, sgu_norm_g, sgu_norm_b, sgu_w, sgu_b, w_out, norm_mlp_g, mlp_w1, mlp_w2, final_norm_g):
    nseq, seq, _ = x_prompt.shape
    nb, steps, _ = x_sample.shape
    assert seq % CHUNK == 0 and nseq == SUBLANES and steps == nseq and nb == 2 * HALF
    p = _prepare_params(steps, nseq, norm_mix_g, w_in, conv_w, conv_b, gate_r_w, gate_r_b, gate_i_w, gate_i_b,
                        lru_lambda, sgu_norm_g, sgu_norm_b, sgu_w, sgu_b, w_out, norm_mlp_g, mlp_w1, mlp_w2,
                        final_norm_g)

    xp = x_prompt
    xs = x_sample.reshape(nb * steps, D_MODEL)
    cbuf_tm = jnp.transpose(state_conv, (0, 2, 1, 3))
    hs_p, convs_p, hs_s, convs_s, vs_s = [], [], [], [], []
    big_f32 = (w_in, w_out, mlp_w1, mlp_w2)
    big = _cast_layer(big_f32[:2], 0)
    for l in range(DEPTH):
        last = l == DEPTH - 1
        first = l == 0
        p.update(zip(_BIG_WEIGHTS, big))
        xs, v_s, h_s, conv_s, mlp_bf16 = _mixer_sample(
            xs, state_lru_h, cbuf_tm, p, l, seq_major=first,
            groups=SAMPLE_GROUPS_CAST if first else SAMPLE_GROUPS, cast=big_f32[2:] if first else ())
        p.update(zip(_BIG_WEIGHTS[2:], mlp_bf16))
        xp, xs, h_p, conv_p, big = _layer_prompt(xp, xs.reshape(steps * nb, D_MODEL), p, l, final_norm=last,
                                                 sample_seq_major=last, cast_next=() if last else big_f32)
        xs = xs.reshape(nb // HALF, steps, HALF, D_MODEL)
        hs_p.append(h_p)
        convs_p.append(conv_p)
        hs_s.append(h_s)
        convs_s.append(conv_s)
        vs_s.append(v_s)

    y_sample = xs.reshape(nb, steps, D_MODEL)
    return (xp, y_sample, jnp.stack(hs_p), jnp.stack(convs_p), jnp.stack(hs_s),
            jnp.transpose(jnp.stack(convs_s), (0, 2, 1, 3)), jnp.stack(vs_s))
```

```python
import functools

import jax
import jax.numpy as jnp
from jax import lax
from jax.experimental import pallas as pl
from jax.experimental.pallas import tpu as pltpu

D_MODEL = 1024
DEPTH = 2
SGU_HEADS = 8
SGU_HEAD_DIM = 64
D_SGU = SGU_HEADS * SGU_HEAD_DIM
CHUNK = 128
D_LRU = 1024
LRU_BLOCKS = 16
LRU_BLOCK_DIM = 64
CONV_W = 4
LRU_C = 8.0
D_FF = 4 * D_MODEL
EPS = 1e-6

SUBLANES = 8
LANES = 128
GATE_TILE = 256
N_GATE_TILES = D_LRU // GATE_TILE
HEADS_PER_LANE_GROUP = LANES // SGU_HEAD_DIM
N_LANE_GROUPS = D_SGU // LANES
N_SLABS = D_LRU // LANES
FF_CHUNK = D_MODEL
N_FF_CHUNKS = D_FF // FF_CHUNK

SAMPLE_GROUPS = 2
SAMPLE_GROUPS_CAST = 4
HALF = CHUNK // 2
PITCH = HALF + SUBLANES
V7X_VMEM_BYTES = 64 * 1024 * 1024
VMEM_HEADROOM = 2 * 1024 * 1024
VMEM_LIMIT = V7X_VMEM_BYTES - VMEM_HEADROOM
F32_TINY = 1.1754944e-38


def _rms_norm(x, g):
    return x * lax.rsqrt(jnp.mean(x * x, axis=-1, keepdims=True) + EPS) * g


def _layer_norm(x, g, b):
    mu = jnp.mean(x, axis=-1, keepdims=True)
    xc = x - mu
    return xc * lax.rsqrt(jnp.mean(xc * xc, axis=-1, keepdims=True) + EPS) * g + b


def _gelu(x):
    return jax.nn.gelu(x, approximate=True)


def _dot(a, b):
    return jnp.dot(a, b, preferred_element_type=jnp.float32)


def _lru_elementwise(r_pre, i_pre, xc, half_rb, half_ib, half_c):
    th_r = jnp.tanh(r_pre + half_rb)
    th_i = jnp.tanh(i_pre + half_ib)
    log_a = half_c * th_r + half_c
    i = 0.5 * th_i + 0.5
    a = jnp.exp(log_a)
    y = jnp.tanh(log_a) * (-1.0 - a * a)
    b = (y * lax.rsqrt(jnp.maximum(y, F32_TINY))) * (i * xc)
    return a, b


def _lru_consts(gate_rb, gate_ib, lam):
    return 0.5 * gate_rb, 0.5 * gate_ib, (0.5 * LRU_C) * jax.nn.log_sigmoid(lam)


def _build_gate_tiles(gate_rw_ref, gate_iw_ref, gate_sc):
    per_tile = GATE_TILE // LRU_BLOCK_DIM
    k_idx = lax.broadcasted_iota(jnp.int32, (LRU_BLOCK_DIM, GATE_TILE), 0)
    n_idx = lax.broadcasted_iota(jnp.int32, (LRU_BLOCK_DIM, GATE_TILE), 1)
    replicate = (n_idx % LRU_BLOCK_DIM == k_idx).astype(jnp.bfloat16)
    row_blk = lax.broadcasted_iota(jnp.int32, (GATE_TILE, GATE_TILE), 0) // LRU_BLOCK_DIM
    col_blk = lax.broadcasted_iota(jnp.int32, (GATE_TILE, GATE_TILE), 1) // LRU_BLOCK_DIM
    for j in range(N_GATE_TILES):
        for k, ref in enumerate((gate_rw_ref, gate_iw_ref)):
            stacked = ref[per_tile * j:per_tile * (j + 1)].reshape(GATE_TILE, LRU_BLOCK_DIM)
            rep = _dot(stacked.astype(jnp.bfloat16), replicate)
            gate_sc[j, :, k * GATE_TILE:(k + 1) * GATE_TILE] = jnp.where(
                row_blk == col_blk, 0.5 * rep, 0.0).astype(jnp.bfloat16)


def _lru_coeffs(xc, gate_w_ref, gate_rb, gate_ib, lam):
    xcb = xc.astype(jnp.bfloat16)
    r_parts, i_parts = [], []
    for j in range(N_GATE_TILES):
        ri = _dot(xcb[:, j * GATE_TILE:(j + 1) * GATE_TILE], gate_w_ref[j])
        r_parts.append(ri[:, :GATE_TILE])
        i_parts.append(ri[:, GATE_TILE:])
    return _lru_elementwise(jnp.concatenate(r_parts, axis=1), jnp.concatenate(i_parts, axis=1), xc,
                            *_lru_consts(gate_rb, gate_ib, lam))


ROW_G_MIX, ROW_G_MLP, ROW_GATE_RB, ROW_GATE_IB, ROW_LAM, ROW_CONV_B, ROW_SGU, ROW_G_FINAL = range(8)
N_ROWS = 8


def _row_views(rows_ref):
    one = lambda k, lo=0, hi=D_MODEL: rows_ref.at[k:k + 1, lo:hi]
    return (one(ROW_G_MIX), one(ROW_G_MLP), one(ROW_GATE_RB), one(ROW_GATE_IB), one(ROW_LAM), one(ROW_CONV_B),
            one(ROW_SGU, 0, D_SGU), one(ROW_SGU, D_SGU, 2 * D_SGU), one(ROW_G_FINAL))


def _mlp_up(xn, w1_ref, c):
    cols = slice(c * FF_CHUNK, (c + 1) * FF_CHUNK)
    return jnp.square(jnp.maximum(_dot(xn, w1_ref[:, cols]), 0.0)).astype(jnp.bfloat16)


def _mlp_down(hid, w2_ref, c):
    return _dot(hid, w2_ref[c * FF_CHUNK:(c + 1) * FF_CHUNK, :])


def _layer_prompt_kernel(*refs, nseq, n_tiles, final_norm, n_cast, sample_seq_major):
    n_in = 2 + len(_PROMPT_PARAMS)
    (x_ref, xs_ref, rows_ref, w_in_ref, conv_w_ref, gate_rw_ref, gate_iw_ref, sgu_w_ref, sgu_bias_ref, w_out_ref,
     w1_ref, w2_ref) = refs[:n_in]
    (g_mix_ref, g_mlp_ref, gate_rb_ref, gate_ib_ref, lam_ref, conv_b_ref, sgu_g_ref, sgu_bn_ref,
     gf_ref) = _row_views(rows_ref)
    cast_in, refs = refs[n_in:n_in + n_cast], refs[n_in + n_cast:]
    y_ref, ys_ref, hlast_ref, convnew_ref = refs[:4]
    cast_out, refs = refs[4:4 + n_cast], refs[4 + n_cast:]
    vprev_sc, sgu_lhs_sc, ymix_sc, gate_sc = refs[:4]
    slab_scratch = refs[4:]
    rows = nseq * HALF
    j = pl.program_id(0)
    slab_sc = slab_scratch[0:N_SLABS]
    tail_sc = slab_scratch[N_SLABS:2 * N_SLABS]
    h_sc = slab_scratch[2 * N_SLABS:3 * N_SLABS]

    @pl.when(j == 0)
    def _():
        for c in range(N_SLABS):
            tail_sc[c][...] = jnp.zeros_like(tail_sc[c])
            h_sc[c][...] = jnp.zeros_like(h_sc[c])
        vprev_sc[...] = jnp.zeros_like(vprev_sc)
        ymix_sc[1] = xs_ref[...]
        _build_gate_tiles(gate_rw_ref, gate_iw_ref, gate_sc)
        t_idx = lax.broadcasted_iota(jnp.int32, (CHUNK, CHUNK), 0)
        s_idx = lax.broadcasted_iota(jnp.int32, (CHUNK, CHUNK), 1)
        for g in range(N_LANE_GROUPS):
            first, second = [], []
            for hh in range(HEADS_PER_LANE_GROUP):
                wm = jnp.where(s_idx <= t_idx, sgu_w_ref[HEADS_PER_LANE_GROUP * g + hh], 0.0)
                first.append(pltpu.roll(wm[:HALF], HALF, axis=1))
                second.append(wm[HALF:])
            sgu_lhs_sc[0, g] = jnp.concatenate(first, axis=0).astype(jnp.bfloat16)
            sgu_lhs_sc[1, g] = jnp.concatenate(second, axis=0).astype(jnp.bfloat16)

    parity = j % 2

    @pl.when(j < n_tiles)
    def _():
        _fused_step(x_ref, g_mix_ref, w_in_ref, conv_w_ref, conv_b_ref, gate_rb_ref, gate_ib_ref, lam_ref, sgu_g_ref, sgu_bn_ref,
                    sgu_bias_ref, w_out_ref, g_mlp_ref, w1_ref, w2_ref, gf_ref, y_ref, vprev_sc, sgu_lhs_sc, ymix_sc,
                    gate_sc, slab_sc, tail_sc, h_sc, parity=parity, nseq=nseq, final_norm=final_norm)
        for w_f32, w_bf16 in zip(cast_in, cast_out):
            w_bf16[...] = w_f32[...].astype(jnp.bfloat16)

    def store_sample(tile):
        if not sample_seq_major:
            ys_ref[...] = tile
            return
        for c in range(N_SLABS):
            for t in range(nseq):
                slab_sc[c][t * PITCH:t * PITCH + HALF, :] = tile[t * HALF:(t + 1) * HALF, c * LANES:(c + 1) * LANES]
        for s in range(HALF):
            for c in range(N_SLABS):
                ys_ref[s * nseq:(s + 1) * nseq, c * LANES:(c + 1) * LANES] = (
                    slab_sc[c][pl.ds(s, nseq, stride=PITCH), :])

    @pl.when(j == 0)
    def _():
        store_sample(y_ref[...].reshape(rows, D_MODEL))

    @pl.when(j == n_tiles - 1)
    def _():
        for c in range(N_SLABS):
            cols = slice(c * LANES, (c + 1) * LANES)
            hlast_ref[:, cols] = h_sc[c][...]
            for k in range(CONV_W - 1):
                convnew_ref[CONV_W - 2 - k, :, cols] = tail_sc[c][parity, k]

    @pl.when(j == n_tiles + 1)
    def _():
        ymix_sc[1 - parity] = xs_ref[...]

    @pl.when(j >= n_tiles)
    def _():
        xm = ymix_sc[1 - parity]
        xn = _rms_norm(xm, g_mlp_ref[...]).astype(jnp.bfloat16)
        acc = xm
        for c in range(N_FF_CHUNKS):
            acc = acc + _mlp_down(_mlp_up(xn, w1_ref, c), w2_ref, c)
        ymix_sc[parity] = _rms_norm(acc, gf_ref[...]) if final_norm else acc

    @pl.when(j == n_tiles)
    def _():
        y_ref[...] = ymix_sc[parity].reshape(nseq, HALF, D_MODEL)

    @pl.when(j == n_tiles + 1)
    def _():
        store_sample(ymix_sc[parity])


def _fused_step(x_ref, g_mix_ref, w_in_ref, conv_w_ref, conv_b_ref, gate_rb_ref, gate_ib_ref, lam_ref, sgu_g_ref, sgu_bn_ref,
                sgu_bias_ref, w_out_ref, g_mlp_ref, w1_ref, w2_ref, gf_ref, y_ref, vprev_sc, sgu_lhs_sc, ymix_sc,
                gate_sc, slab_sc, tail_sc, h_sc, *, parity, nseq, final_norm):
    rows = nseq * HALF
    half_rb, half_ib, half_c = _lru_consts(gate_rb_ref[...], gate_ib_ref[...], lam_ref[...])
    xb_col0 = 2 * D_SGU
    yb_col0 = 2 * D_SGU + D_LRU
    slabs_per_tile = GATE_TILE // LANES
    st = [{} for _ in range(N_GATE_TILES)]
    sg = {}
    ml = {}

    def mlp_norm():
        xm = ymix_sc[1 - parity]
        ml["acc"] = xm
        ml["xn"] = _rms_norm(xm, g_mlp_ref[...]).astype(jnp.bfloat16)

    def mlp_up(c):
        ml["hid", c] = _mlp_up(ml["xn"], w1_ref, c)

    def mlp_down(c):
        ml["acc"] = ml["acc"] + _mlp_down(ml.pop(("hid", c)), w2_ref, c)

    def mlp_store():
        out = _rms_norm(ml["acc"], gf_ref[...]) if final_norm else ml["acc"]
        y_ref[...] = out.reshape(nseq, HALF, D_MODEL)

    def mix_norm():
        sg["x"] = x_ref[...].reshape(rows, D_MODEL)
        sg["xn"] = _rms_norm(sg["x"], g_mix_ref[...]).astype(jnp.bfloat16)

    def lru_proj(q):
        xb = _dot(sg["xn"], w_in_ref[:, xb_col0 + q * GATE_TILE:xb_col0 + (q + 1) * GATE_TILE])
        for i in range(slabs_per_tile):
            for s in range(nseq):
                slab_sc[slabs_per_tile * q + i][s * PITCH:s * PITCH + HALF, :] = (
                    xb[s * HALF:(s + 1) * HALF, i * LANES:(i + 1) * LANES])

    def lru_conv(q):
        xc_slabs = []
        for c in range(slabs_per_tile * q, slabs_per_tile * (q + 1)):
            lanes = slice(c * LANES, (c + 1) * LANES)
            taps = [jnp.broadcast_to(conv_w_ref[k:k + 1, lanes], (nseq, LANES)) for k in range(CONV_W)]
            taps.append(jnp.broadcast_to(conv_b_ref[:, lanes], (nseq, LANES)))
            p1, p2, p3 = (tail_sc[c][1 - parity, k] for k in range(CONV_W - 1))
            steps_out = []
            for t in range(HALF):
                cur = slab_sc[c][pl.ds(t, nseq, stride=PITCH), :]
                steps_out.append(taps[CONV_W] + taps[3] * cur + taps[2] * p1 + taps[1] * p2 + taps[0] * p3)
                p1, p2, p3 = cur, p1, p2
            for k, pk in enumerate((p1, p2, p3)):
                tail_sc[c][parity, k] = pk
            xc_slabs.append(jnp.concatenate(steps_out, axis=0))
        st[q]["xc"] = jnp.concatenate(xc_slabs, axis=1)

    def lru_gates(q):
        st[q]["ri"] = _dot(st[q]["xc"].astype(jnp.bfloat16), gate_sc[q])

    def lru_coeffs(q):
        cols = slice(q * GATE_TILE, (q + 1) * GATE_TILE)
        ri = st[q].pop("ri")
        st[q]["ab"] = _lru_elementwise(ri[:, :GATE_TILE], ri[:, GATE_TILE:], st[q].pop("xc"),
                                       half_rb[:, cols], half_ib[:, cols], half_c[:, cols])

    def lru_scan(q):
        a, b = st[q].pop("ab")
        for i in range(slabs_per_tile):
            c = slabs_per_tile * q + i
            h = h_sc[c][...]
            for t in range(HALF):
                h = (a[t * nseq:(t + 1) * nseq, i * LANES:(i + 1) * LANES] * h
                     + b[t * nseq:(t + 1) * nseq, i * LANES:(i + 1) * LANES])
                slab_sc[c][pl.ds(t, nseq, stride=PITCH), :] = h
            h_sc[c][...] = h

    def lru_out(q):
        hs = jnp.concatenate(
            [jnp.concatenate([slab_sc[c][s * PITCH:s * PITCH + HALF, :]
                              for c in range(slabs_per_tile * q, slabs_per_tile * (q + 1))], axis=1)
             for s in range(nseq)], axis=0)
        yb = _dot(sg["xn"], w_in_ref[:, yb_col0 + q * GATE_TILE:yb_col0 + (q + 1) * GATE_TILE])
        st[q]["out_b"] = (hs * _gelu(yb)).astype(jnp.bfloat16)

    def lru_mix(q):
        r0 = D_SGU + q * GATE_TILE
        sg["acc"] = sg["acc"] + _dot(st[q].pop("out_b"), w_out_ref[r0:r0 + GATE_TILE, :])

    def sgu_u():
        sg["u"] = _gelu(_dot(sg["xn"], w_in_ref[:, 0:D_SGU]))

    def sgu_v():
        v = _layer_norm(_gelu(_dot(sg["xn"], w_in_ref[:, D_SGU:2 * D_SGU])), sgu_g_ref[...], sgu_bn_ref[...])
        sg["vb"] = v.astype(jnp.bfloat16)

    def sgu_gate():
        vb = sg.pop("vb")
        lane = lax.broadcasted_iota(jnp.int32, (HALF, LANES), 1)
        bias = sgu_bias_ref[pl.ds(pl.multiple_of(parity * HALF, HALF), HALF), :]
        v_full = [jnp.concatenate([vprev_sc[1 - parity, s], vb[s * HALF:(s + 1) * HALF]], axis=0)
                  for s in range(nseq)]
        vprev_sc[parity] = vb.reshape(nseq, HALF, D_SGU)
        gate_rows = [[] for _ in range(nseq)]
        for s in range(0, nseq, 2):
            for g in range(N_LANE_GROUPS):
                lanes = slice(g * LANES, (g + 1) * LANES)
                rhs = jnp.concatenate([v_full[s][:, lanes], v_full[s + 1][:, lanes]], axis=1)
                res = _dot(sgu_lhs_sc[parity, g], rhs)
                for i in range(2):
                    part = res[:, i * LANES:(i + 1) * LANES]
                    gate_rows[s + i].append(jnp.where(lane < SGU_HEAD_DIM, part[:HALF], part[HALF:]))
        gate = jnp.concatenate([jnp.concatenate(r, axis=1) + bias for r in gate_rows], axis=0)
        sg["out_a"] = (sg.pop("u") * gate).astype(jnp.bfloat16)

    def sgu_mix():
        sg["acc"] = sg["x"] + _dot(sg.pop("out_a"), w_out_ref[0:D_SGU, :])

    def mix_store():
        ymix_sc[parity] = sg["acc"]

    tiles = range(N_GATE_TILES)
    mlp_norm()
    mix_norm()
    for q in tiles:
        lru_proj(q)
    mlp_up(0)
    for q in tiles:
        lru_conv(q)
        lru_gates(q)
    mlp_up(1)
    sgu_u()
    sgu_v()
    for q in tiles:
        lru_coeffs(q)
        lru_scan(q)
    mlp_down(0)
    mlp_down(1)
    for q in tiles:
        lru_out(q)
    mlp_up(2)
    mlp_up(3)
    sgu_gate()
    sgu_mix()
    mlp_down(2)
    mlp_down(3)
    for q in tiles:
        lru_mix(q)
    mix_store()
    mlp_store()


def _const_spec(shape):
    nd = len(shape)
    return pl.BlockSpec(shape, lambda *_: (0,) * nd, pipeline_mode=pl.Buffered(1))


def _layer_spec(arr, l):
    nd = arr.ndim - 1
    return pl.BlockSpec((None,) + arr.shape[1:], lambda *_: (l,) + (0,) * nd, pipeline_mode=pl.Buffered(1))


_BIG_WEIGHTS = ("w_in", "w_out", "w1", "w2")
_PROMPT_PARAMS = ("rows", "w_in", "conv_w", "gate_rw", "gate_iw", "sgu_w", "sgu_bias", "w_out", "w1", "w2")
_SAMPLE_MIXER_PARAMS = ("rows", "w_in", "conv_w", "gate_rw", "gate_iw", "sgu_w8", "sgu_b8", "w_out")


def _param_spec(p, k, l):
    return _layer_spec(p[k], 0 if k in _BIG_WEIGHTS else l)


def _layer_prompt(x, xs, p, l, *, final_norm, sample_seq_major, cast_next=()):
    nseq, seq, _ = x.shape
    n_tiles = seq // HALF
    rows = nseq * HALF
    assert xs.shape == (2 * rows, D_MODEL)
    consts = [p[k] for k in _PROMPT_PARAMS]
    cast_in_specs, cast_out_specs, cast_shapes = [], [], []
    for w in cast_next:
        k_rows, n_cols = w.shape[1:]
        blk = k_rows // n_tiles
        assert blk * n_tiles == k_rows and blk % (2 * SUBLANES) == 0
        cast_in_specs.append(pl.BlockSpec((None, blk, n_cols), lambda j: (l + 1, jnp.minimum(j, n_tiles - 1), 0)))
        cast_out_specs.append(pl.BlockSpec((None, blk, n_cols), lambda j: (0, jnp.minimum(j, n_tiles - 1), 0)))
        cast_shapes.append(jax.ShapeDtypeStruct((1, k_rows, n_cols), jnp.bfloat16))
    x_spec = pl.BlockSpec((nseq, HALF, D_MODEL), lambda j: (0, jnp.minimum(j, n_tiles - 1), 0))
    y_spec = pl.BlockSpec((nseq, HALF, D_MODEL), lambda j: (0, jnp.clip(j - 1, 0, n_tiles - 1), 0))
    xs_spec = pl.BlockSpec((rows, D_MODEL), lambda j: (jnp.where(j > n_tiles, 1, 0), 0),
                           pipeline_mode=pl.Buffered(1))
    y, ys, hlast, convnew, *cast = pl.pallas_call(
        functools.partial(_layer_prompt_kernel, nseq=nseq, n_tiles=n_tiles, final_norm=final_norm,
                          n_cast=len(cast_next), sample_seq_major=sample_seq_major),
        grid=(n_tiles + 2,),
        in_specs=[x_spec, xs_spec] + [_param_spec(p, k, l) for k in _PROMPT_PARAMS] + cast_in_specs,
        out_specs=([y_spec, xs_spec, _const_spec((nseq, D_LRU)), _const_spec((CONV_W - 1, nseq, D_LRU))]
                   + cast_out_specs),
        out_shape=[jax.ShapeDtypeStruct(x.shape, jnp.float32),
                   jax.ShapeDtypeStruct(xs.shape, jnp.float32),
                   jax.ShapeDtypeStruct((nseq, D_LRU), jnp.float32),
                   jax.ShapeDtypeStruct((CONV_W - 1, nseq, D_LRU), jnp.float32)] + cast_shapes,
        scratch_shapes=([pltpu.VMEM((2, nseq, HALF, D_SGU), jnp.bfloat16),
                         pltpu.VMEM((2, N_LANE_GROUPS, CHUNK, CHUNK), jnp.bfloat16),
                         pltpu.VMEM((2, rows, D_MODEL), jnp.float32),
                         pltpu.VMEM((N_GATE_TILES, GATE_TILE, 2 * GATE_TILE), jnp.bfloat16)]
                        + [pltpu.VMEM((nseq * PITCH, LANES), jnp.float32)] * N_SLABS
                        + [pltpu.VMEM((2, CONV_W - 1, nseq, LANES), jnp.float32)] * N_SLABS
                        + [pltpu.VMEM((nseq, LANES), jnp.float32)] * N_SLABS),
        compiler_params=pltpu.CompilerParams(dimension_semantics=("arbitrary",), vmem_limit_bytes=VMEM_LIMIT),
        name="layer_prompt",
    )(x, xs, *consts, *cast_next)
    return y, ys, hlast, jnp.transpose(convnew, (1, 0, 2)), cast


def _mixer_sample_kernel(*refs, steps, nb, seq_major, n_cast):
    n_in = 3 + len(_SAMPLE_MIXER_PARAMS)
    (x_ref, cbuf_ref, h0_ref, rows_ref, w_in_ref, conv_w_ref, gate_rw_ref, gate_iw_ref, sgu_w8_ref, sgu_b8_ref,
     w_out_ref) = refs[:n_in]
    cast_in, refs = refs[n_in:n_in + n_cast], refs[n_in + n_cast:]
    y_ref, v_ref, hlast_ref, convnew_ref = refs[:4]
    cast_out, refs = refs[4:4 + n_cast], refs[4 + n_cast:]
    gate_sc, v_relay_sc, relay_sc = refs[0], refs[1:1 + N_LANE_GROUPS], refs[1 + N_LANE_GROUPS:]
    for w_f32, w_bf16 in zip(cast_in, cast_out):
        w_bf16[...] = w_f32[...].astype(jnp.bfloat16)
    (g_mix_ref, _, gate_rb_ref, gate_ib_ref, lam_ref, conv_b_ref, sgu_g_ref, sgu_bn_ref, _) = _row_views(rows_ref)

    @pl.when(pl.program_id(0) == 0)
    def _():
        _build_gate_tiles(gate_rw_ref, gate_iw_ref, gate_sc)

    if seq_major:
        for c in range(D_MODEL // LANES):
            relay_sc[c][...] = x_ref[:, c * LANES:(c + 1) * LANES]
        x = jnp.concatenate(
            [jnp.concatenate([relay_sc[c][pl.ds(t, nb, stride=steps), :] for c in range(D_MODEL // LANES)], axis=1)
             for t in range(steps)], axis=0)
    else:
        x = x_ref[...].reshape(steps * nb, D_MODEL)
    xn = _rms_norm(x, g_mix_ref[...]).astype(jnp.bfloat16)

    u = _gelu(_dot(xn, w_in_ref[:, 0:D_SGU]))
    v = _layer_norm(_gelu(_dot(xn, w_in_ref[:, D_SGU:2 * D_SGU])), sgu_g_ref[...], sgu_bn_ref[...])
    pitch = nb + SUBLANES
    for g in range(N_LANE_GROUPS):
        for t in range(steps):
            v_relay_sc[g][t * pitch:t * pitch + nb, :] = v[t * nb:(t + 1) * nb, g * LANES:(g + 1) * LANES]
    for s in range(nb):
        for g in range(N_LANE_GROUPS):
            v_ref[s, :, g * LANES:(g + 1) * LANES] = v_relay_sc[g][pl.ds(s, steps, stride=pitch), :]
    gate_rows = []
    for t in range(steps):
        acc = sgu_b8_ref[t:t + 1, :] + sgu_w8_ref[t, 0:1, :] * v[0:nb]
        for s in range(1, t + 1):
            acc = acc + sgu_w8_ref[t, s:s + 1, :] * v[s * nb:(s + 1) * nb]
        gate_rows.append(acc)
    out_a = (u * jnp.concatenate(gate_rows, axis=0)).astype(jnp.bfloat16)

    xb = _dot(xn, w_in_ref[:, 2 * D_SGU:2 * D_SGU + D_LRU])
    xp = jnp.concatenate([cbuf_ref[k] for k in range(CONV_W - 1)] + [xb], axis=0)
    xc = conv_b_ref[...] + conv_w_ref[0:1, :] * xp[0:steps * nb]
    for k in range(1, CONV_W):
        xc = xc + conv_w_ref[k:k + 1, :] * xp[k * nb:(k + steps) * nb]
    convnew_ref[...] = xp[steps * nb:].reshape(CONV_W - 1, nb, D_LRU)

    a, b = _lru_coeffs(xc, gate_sc, gate_rb_ref[...], gate_ib_ref[...], lam_ref[...])
    h = h0_ref[...]
    hs = []
    for t in range(steps):
        h = a[t * nb:(t + 1) * nb] * h + b[t * nb:(t + 1) * nb]
        hs.append(h)
    hlast_ref[...] = h

    yb = _dot(xn, w_in_ref[:, 2 * D_SGU + D_LRU:])
    out_b = (jnp.concatenate(hs, axis=0) * _gelu(yb)).astype(jnp.bfloat16)
    out = _dot(jnp.concatenate([out_a, out_b], axis=1), w_out_ref[...])
    y_ref[...] = (x + out).reshape(steps, nb, D_MODEL)


def _mixer_sample(x, h0_all, cbuf_all, p, l, *, seq_major, groups, cast=()):
    nb_all = h0_all.shape[1]
    steps = x.shape[0] // nb_all if seq_major else x.shape[1]
    nb = nb_all // groups
    cast_in_specs, cast_out_specs, cast_shapes = [], [], []
    for w in cast:
        k_rows, n_cols = w.shape[1:]
        blk = k_rows // groups
        assert blk * groups == k_rows and blk % (2 * SUBLANES) == 0
        cast_in_specs.append(pl.BlockSpec((None, blk, n_cols), lambda i: (l, i, 0)))
        cast_out_specs.append(pl.BlockSpec((None, blk, n_cols), lambda i: (0, i, 0)))
        cast_shapes.append(jax.ShapeDtypeStruct((1, k_rows, n_cols), jnp.bfloat16))
    per_tile = HALF // nb
    assert per_tile * nb == HALF and nb_all % HALF == 0
    act_spec = pl.BlockSpec((None, steps, nb, D_MODEL), lambda i: (i // per_tile, 0, i % per_tile, 0))
    if seq_major:
        x_spec = pl.BlockSpec((nb * steps, D_MODEL), lambda i: (i, 0))
        relay = [pltpu.VMEM((nb * steps, LANES), jnp.float32)] * (D_MODEL // LANES)
    else:
        x_spec = act_spec
        relay = []
    group = lambda lead, width: pl.BlockSpec((lead, nb, width), lambda i: (0, i, 0))
    out_shapes = [(nb_all // HALF, steps, HALF, D_MODEL), (nb_all, steps, D_SGU), (nb_all, D_LRU),
                  (CONV_W - 1, nb_all, D_LRU)]
    v_relay = [pltpu.VMEM((steps * (nb + SUBLANES), LANES), jnp.float32)] * N_LANE_GROUPS
    y, v, hlast, convnew, *cast_bf16 = pl.pallas_call(
        functools.partial(_mixer_sample_kernel, steps=steps, nb=nb, seq_major=seq_major, n_cast=len(cast)),
        grid=(groups,),
        in_specs=([x_spec,
                   pl.BlockSpec((None, CONV_W - 1, nb, D_LRU), lambda i: (l, 0, i, 0)),
                   pl.BlockSpec((None, nb, D_LRU), lambda i: (l, i, 0))]
                  + [_param_spec(p, k, l) for k in _SAMPLE_MIXER_PARAMS] + cast_in_specs),
        out_specs=[act_spec, pl.BlockSpec((nb, steps, D_SGU), lambda i: (i, 0, 0)),
                   pl.BlockSpec((nb, D_LRU), lambda i: (i, 0)), group(CONV_W - 1, D_LRU)] + cast_out_specs,
        out_shape=[jax.ShapeDtypeStruct(s, jnp.float32) for s in out_shapes] + cast_shapes,
        scratch_shapes=[pltpu.VMEM((N_GATE_TILES, GATE_TILE, 2 * GATE_TILE), jnp.bfloat16)] + v_relay + relay,
        compiler_params=pltpu.CompilerParams(dimension_semantics=("arbitrary",), vmem_limit_bytes=VMEM_LIMIT),
        name="mixer_sample",
    )(x, cbuf_all, h0_all, *[p[k] for k in _SAMPLE_MIXER_PARAMS], *cast)
    return y, v, hlast, convnew, cast_bf16


CAST_STEPS = 4


def _cast_kernel(*refs):
    n = len(refs) // 2
    for w_f32, w_bf16 in zip(refs[:n], refs[n:]):
        w_bf16[...] = w_f32[...].astype(jnp.bfloat16)


def _cast_layer(weights, l):
    in_specs, out_specs, out_shapes = [], [], []
    for w in weights:
        k_rows, n_cols = w.shape[1:]
        blk = k_rows // CAST_STEPS
        assert blk * CAST_STEPS == k_rows and blk % (2 * SUBLANES) == 0
        in_specs.append(pl.BlockSpec((None, blk, n_cols), lambda j: (l, j, 0)))
        out_specs.append(pl.BlockSpec((None, blk, n_cols), lambda j: (0, j, 0)))
        out_shapes.append(jax.ShapeDtypeStruct((1, k_rows, n_cols), jnp.bfloat16))
    return pl.pallas_call(
        _cast_kernel,
        grid=(CAST_STEPS,),
        in_specs=in_specs,
        out_specs=out_specs,
        out_shape=out_shapes,
        compiler_params=pltpu.CompilerParams(dimension_semantics=("arbitrary",), vmem_limit_bytes=VMEM_LIMIT),
        name="cast_weights",
    )(*weights)


def _prepare_params(steps, nseq, norm_mix_g, w_in, conv_w, conv_b, gate_r_w, gate_r_b, gate_i_w, gate_i_b,
                    lru_lambda, sgu_norm_g, sgu_norm_b, sgu_w, sgu_b, w_out, norm_mlp_g, mlp_w1, mlp_w2, final_norm_g):
    rows = [None] * N_ROWS
    rows[ROW_G_MIX], rows[ROW_G_MLP] = norm_mix_g, norm_mlp_g
    rows[ROW_GATE_RB], rows[ROW_GATE_IB], rows[ROW_LAM], rows[ROW_CONV_B] = gate_r_b, gate_i_b, lru_lambda, conv_b
    rows[ROW_SGU] = jnp.concatenate([sgu_norm_g, sgu_norm_b], axis=-1)
    rows[ROW_G_FINAL] = jnp.broadcast_to(final_norm_g, (DEPTH, D_MODEL))
    return {
        "rows": jnp.stack(rows, axis=1),
        "conv_w": conv_w,
        "gate_rw": gate_r_w,
        "gate_iw": gate_i_w,
        "sgu_w": sgu_w,
        "sgu_bias": jnp.repeat(jnp.transpose(sgu_b, (0, 2, 1)), SGU_HEAD_DIM, axis=2),
        "sgu_w8": jnp.repeat(jnp.transpose(sgu_w[:, :, :steps, :steps], (0, 2, 3, 1)), SGU_HEAD_DIM, axis=3),
        "sgu_b8": jnp.repeat(jnp.transpose(sgu_b[:, :, :steps], (0, 2, 1)), SGU_HEAD_DIM, axis=2),
    }


def kernel(x_prompt, x_sample, state_lru_h, state_conv, norm_mix_g, w_in, conv_w, conv_b, gate_r_w, gate_r_b, gate_i_w, gate_i_b, lru_lambda, sgu_norm_g, sgu_norm_b, sgu_w, sgu_b, w_out, norm_mlp_g, mlp_w1, mlp_w2, final_norm_g):
    nseq, seq, _ = x_prompt.shape
    nb, steps, _ = x_sample.shape
    assert seq % CHUNK == 0 and nseq == SUBLANES and steps == nseq and nb == 2 * HALF
    p = _prepare_params(steps, nseq, norm_mix_g, w_in, conv_w, conv_b, gate_r_w, gate_r_b, gate_i_w, gate_i_b,
                        lru_lambda, sgu_norm_g, sgu_norm_b, sgu_w, sgu_b, w_out, norm_mlp_g, mlp_w1, mlp_w2,
                        final_norm_g)

    xp = x_prompt
    xs = x_sample.reshape(nb * steps, D_MODEL)
    cbuf_tm = jnp.transpose(state_conv, (0, 2, 1, 3))
    hs_p, convs_p, hs_s, convs_s, vs_s = [], [], [], [], []
    big_f32 = (w_in, w_out, mlp_w1, mlp_w2)
    big = _cast_layer(big_f32[:2], 0)
    for l in range(DEPTH):
        last = l == DEPTH - 1
        first = l == 0
        p.update(zip(_BIG_WEIGHTS, big))
        xs, v_s, h_s, conv_s, mlp_bf16 = _mixer_sample(
            xs, state_lru_h, cbuf_tm, p, l, seq_major=first,
            groups=SAMPLE_GROUPS_CAST if first else SAMPLE_GROUPS, cast=big_f32[2:] if first else ())
        p.update(zip(_BIG_WEIGHTS[2:], mlp_bf16))
        xp, xs, h_p, conv_p, big = _layer_prompt(xp, xs.reshape(steps * nb, D_MODEL), p, l, final_norm=last,
                                                 sample_seq_major=last, cast_next=() if last else big_f32)
        if not last:
            xs = xs.reshape(nb // HALF, steps, HALF, D_MODEL)
        hs_p.append(h_p)
        convs_p.append(conv_p)
        hs_s.append(h_s)
        convs_s.append(conv_s)
        vs_s.append(v_s)

    y_sample = xs.reshape(nb, steps, D_MODEL)
    return (xp, y_sample, jnp.stack(hs_p), jnp.stack(convs_p), jnp.stack(hs_s),
            jnp.transpose(jnp.stack(convs_s), (0, 2, 1, 3)), jnp.stack(vs_s))
```

```python
import functools

import jax
import jax.numpy as jnp
from jax import lax
from jax.experimental import pallas as pl
from jax.experimental.pallas import tpu as pltpu

D_MODEL = 1024
DEPTH = 2
SGU_HEADS = 8
SGU_HEAD_DIM = 64
D_SGU = SGU_HEADS * SGU_HEAD_DIM
CHUNK = 128
D_LRU = 1024
LRU_BLOCKS = 16
LRU_BLOCK_DIM = 64
CONV_W = 4
LRU_C = 8.0
D_FF = 4 * D_MODEL
EPS = 1e-6

SUBLANES = 8
LANES = 128
GATE_TILE = 256
N_GATE_TILES = D_LRU // GATE_TILE
HEADS_PER_LANE_GROUP = LANES // SGU_HEAD_DIM
N_LANE_GROUPS = D_SGU // LANES
N_SLABS = D_LRU // LANES
FF_CHUNK = D_MODEL
N_FF_CHUNKS = D_FF // FF_CHUNK

SAMPLE_GROUPS = 2
SAMPLE_GROUPS_CAST = 4
HALF = CHUNK // 2
PITCH = HALF + SUBLANES
V7X_VMEM_BYTES = 64 * 1024 * 1024
VMEM_HEADROOM = 2 * 1024 * 1024
VMEM_LIMIT = V7X_VMEM_BYTES - VMEM_HEADROOM
F32_TINY = 1.1754944e-38


def _rms_norm(x, g):
    return x * lax.rsqrt(jnp.mean(x * x, axis=-1, keepdims=True) + EPS) * g


def _layer_norm(x, g, b):
    mu = jnp.mean(x, axis=-1, keepdims=True)
    xc = x - mu
    return xc * lax.rsqrt(jnp.mean(xc * xc, axis=-1, keepdims=True) + EPS) * g + b


GELU_C0 = 0.7978845608028654
GELU_C1 = 0.044715


def _gelu(x):
    inner = x * (GELU_C0 + (GELU_C0 * GELU_C1) * (x * x))
    hx = 0.5 * x
    return hx * jnp.tanh(inner) + hx


def _dot(a, b):
    return jnp.dot(a, b, preferred_element_type=jnp.float32)


def _lru_elementwise(r_pre, i_pre, xc, half_rb, half_ib, half_c):
    th_r = jnp.tanh(r_pre + half_rb)
    th_i = jnp.tanh(i_pre + half_ib)
    log_a = half_c * th_r + half_c
    i = 0.5 * th_i + 0.5
    a = jnp.exp(log_a)
    y = jnp.tanh(log_a) * (-1.0 - a * a)
    b = (y * lax.rsqrt(jnp.maximum(y, F32_TINY))) * (i * xc)
    return a, b


def _lru_consts(gate_rb, gate_ib, lam):
    return 0.5 * gate_rb, 0.5 * gate_ib, (0.5 * LRU_C) * jax.nn.log_sigmoid(lam)


def _build_gate_tiles(gate_rw_ref, gate_iw_ref, gate_sc):
    per_tile = GATE_TILE // LRU_BLOCK_DIM
    k_idx = lax.broadcasted_iota(jnp.int32, (LRU_BLOCK_DIM, GATE_TILE), 0)
    n_idx = lax.broadcasted_iota(jnp.int32, (LRU_BLOCK_DIM, GATE_TILE), 1)
    replicate = (n_idx % LRU_BLOCK_DIM == k_idx).astype(jnp.bfloat16)
    row_blk = lax.broadcasted_iota(jnp.int32, (GATE_TILE, GATE_TILE), 0) // LRU_BLOCK_DIM
    col_blk = lax.broadcasted_iota(jnp.int32, (GATE_TILE, GATE_TILE), 1) // LRU_BLOCK_DIM
    for j in range(N_GATE_TILES):
        for k, ref in enumerate((gate_rw_ref, gate_iw_ref)):
            stacked = ref[per_tile * j:per_tile * (j + 1)].reshape(GATE_TILE, LRU_BLOCK_DIM)
            rep = _dot(stacked.astype(jnp.bfloat16), replicate)
            gate_sc[j, :, k * GATE_TILE:(k + 1) * GATE_TILE] = jnp.where(
                row_blk == col_blk, 0.5 * rep, 0.0).astype(jnp.bfloat16)


def _lru_coeffs(xc, gate_w_ref, gate_rb, gate_ib, lam):
    xcb = xc.astype(jnp.bfloat16)
    r_parts, i_parts = [], []
    for j in range(N_GATE_TILES):
        ri = _dot(xcb[:, j * GATE_TILE:(j + 1) * GATE_TILE], gate_w_ref[j])
        r_parts.append(ri[:, :GATE_TILE])
        i_parts.append(ri[:, GATE_TILE:])
    return _lru_elementwise(jnp.concatenate(r_parts, axis=1), jnp.concatenate(i_parts, axis=1), xc,
                            *_lru_consts(gate_rb, gate_ib, lam))


ROW_G_MIX, ROW_G_MLP, ROW_GATE_RB, ROW_GATE_IB, ROW_LAM, ROW_CONV_B, ROW_SGU, ROW_G_FINAL = range(8)
N_ROWS = 8


def _row_views(rows_ref):
    one = lambda k, lo=0, hi=D_MODEL: rows_ref.at[k:k + 1, lo:hi]
    return (one(ROW_G_MIX), one(ROW_G_MLP), one(ROW_GATE_RB), one(ROW_GATE_IB), one(ROW_LAM), one(ROW_CONV_B),
            one(ROW_SGU, 0, D_SGU), one(ROW_SGU, D_SGU, 2 * D_SGU), one(ROW_G_FINAL))


def _mlp_up(xn, w1_ref, c):
    cols = slice(c * FF_CHUNK, (c + 1) * FF_CHUNK)
    return jnp.square(jnp.maximum(_dot(xn, w1_ref[:, cols]), 0.0)).astype(jnp.bfloat16)


def _mlp_down(hid, w2_ref, c):
    return _dot(hid, w2_ref[c * FF_CHUNK:(c + 1) * FF_CHUNK, :])


def _layer_prompt_kernel(*refs, nseq, n_tiles, final_norm, n_cast, sample_seq_major):
    n_in = 2 + len(_PROMPT_PARAMS)
    (x_ref, xs_ref, rows_ref, w_in_ref, conv_w_ref, gate_rw_ref, gate_iw_ref, sgu_w_ref, sgu_bias_ref, w_out_ref,
     w1_ref, w2_ref) = refs[:n_in]
    (g_mix_ref, g_mlp_ref, gate_rb_ref, gate_ib_ref, lam_ref, conv_b_ref, sgu_g_ref, sgu_bn_ref,
     gf_ref) = _row_views(rows_ref)
    cast_in, refs = refs[n_in:n_in + n_cast], refs[n_in + n_cast:]
    y_ref, ys_ref, hlast_ref, convnew_ref = refs[:4]
    cast_out, refs = refs[4:4 + n_cast], refs[4 + n_cast:]
    vprev_sc, sgu_lhs_sc, ymix_sc, gate_sc = refs[:4]
    slab_scratch = refs[4:]
    rows = nseq * HALF
    j = pl.program_id(0)
    slab_sc = slab_scratch[0:N_SLABS]
    tail_sc = slab_scratch[N_SLABS:2 * N_SLABS]
    h_sc = slab_scratch[2 * N_SLABS:3 * N_SLABS]

    @pl.when(j == 0)
    def _():
        for c in range(N_SLABS):
            tail_sc[c][...] = jnp.zeros_like(tail_sc[c])
            h_sc[c][...] = jnp.zeros_like(h_sc[c])
        vprev_sc[...] = jnp.zeros_like(vprev_sc)
        ymix_sc[1] = xs_ref[...]
        _build_gate_tiles(gate_rw_ref, gate_iw_ref, gate_sc)
        t_idx = lax.broadcasted_iota(jnp.int32, (CHUNK, CHUNK), 0)
        s_idx = lax.broadcasted_iota(jnp.int32, (CHUNK, CHUNK), 1)
        for g in range(N_LANE_GROUPS):
            first, second = [], []
            for hh in range(HEADS_PER_LANE_GROUP):
                wm = jnp.where(s_idx <= t_idx, sgu_w_ref[HEADS_PER_LANE_GROUP * g + hh], 0.0)
                first.append(pltpu.roll(wm[:HALF], HALF, axis=1))
                second.append(wm[HALF:])
            sgu_lhs_sc[0, g] = jnp.concatenate(first, axis=0).astype(jnp.bfloat16)
            sgu_lhs_sc[1, g] = jnp.concatenate(second, axis=0).astype(jnp.bfloat16)

    parity = j % 2

    @pl.when(j < n_tiles)
    def _():
        _fused_step(x_ref, g_mix_ref, w_in_ref, conv_w_ref, conv_b_ref, gate_rb_ref, gate_ib_ref, lam_ref, sgu_g_ref, sgu_bn_ref,
                    sgu_bias_ref, w_out_ref, g_mlp_ref, w1_ref, w2_ref, gf_ref, y_ref, vprev_sc, sgu_lhs_sc, ymix_sc,
                    gate_sc, slab_sc, tail_sc, h_sc, parity=parity, nseq=nseq, final_norm=final_norm)
        for w_f32, w_bf16 in zip(cast_in, cast_out):
            w_bf16[...] = w_f32[...].astype(jnp.bfloat16)

    def store_sample(tile):
        if not sample_seq_major:
            ys_ref[...] = tile
            return
        for c in range(N_SLABS):
            for t in range(nseq):
                slab_sc[c][t * PITCH:t * PITCH + HALF, :] = tile[t * HALF:(t + 1) * HALF, c * LANES:(c + 1) * LANES]
        for s in range(HALF):
            for c in range(N_SLABS):
                ys_ref[s * nseq:(s + 1) * nseq, c * LANES:(c + 1) * LANES] = (
                    slab_sc[c][pl.ds(s, nseq, stride=PITCH), :])

    @pl.when(j == 0)
    def _():
        store_sample(y_ref[...].reshape(rows, D_MODEL))

    @pl.when(j == n_tiles - 1)
    def _():
        for c in range(N_SLABS):
            cols = slice(c * LANES, (c + 1) * LANES)
            hlast_ref[:, cols] = h_sc[c][...]
            for k in range(CONV_W - 1):
                convnew_ref[CONV_W - 2 - k, :, cols] = tail_sc[c][parity, k]

    @pl.when(j == n_tiles + 1)
    def _():
        ymix_sc[1 - parity] = xs_ref[...]

    @pl.when(j >= n_tiles)
    def _():
        xm = ymix_sc[1 - parity]
        xn = _rms_norm(xm, g_mlp_ref[...]).astype(jnp.bfloat16)
        acc = xm
        for c in range(N_FF_CHUNKS):
            acc = acc + _mlp_down(_mlp_up(xn, w1_ref, c), w2_ref, c)
        ymix_sc[parity] = _rms_norm(acc, gf_ref[...]) if final_norm else acc

    @pl.when(j == n_tiles)
    def _():
        y_ref[...] = ymix_sc[parity].reshape(nseq, HALF, D_MODEL)

    @pl.when(j == n_tiles + 1)
    def _():
        store_sample(ymix_sc[parity])


def _fused_step(x_ref, g_mix_ref, w_in_ref, conv_w_ref, conv_b_ref, gate_rb_ref, gate_ib_ref, lam_ref, sgu_g_ref, sgu_bn_ref,
                sgu_bias_ref, w_out_ref, g_mlp_ref, w1_ref, w2_ref, gf_ref, y_ref, vprev_sc, sgu_lhs_sc, ymix_sc,
                gate_sc, slab_sc, tail_sc, h_sc, *, parity, nseq, final_norm):
    rows = nseq * HALF
    half_rb, half_ib, half_c = _lru_consts(gate_rb_ref[...], gate_ib_ref[...], lam_ref[...])
    xb_col0 = 2 * D_SGU
    yb_col0 = 2 * D_SGU + D_LRU
    slabs_per_tile = GATE_TILE // LANES
    st = [{} for _ in range(N_GATE_TILES)]
    sg = {}
    ml = {}

    def mlp_norm():
        xm = ymix_sc[1 - parity]
        ml["acc"] = xm
        ml["xn"] = _rms_norm(xm, g_mlp_ref[...]).astype(jnp.bfloat16)

    def mlp_up(c):
        ml["hid", c] = _mlp_up(ml["xn"], w1_ref, c)

    def mlp_down(c):
        ml["acc"] = ml["acc"] + _mlp_down(ml.pop(("hid", c)), w2_ref, c)

    def mlp_store():
        out = _rms_norm(ml["acc"], gf_ref[...]) if final_norm else ml["acc"]
        y_ref[...] = out.reshape(nseq, HALF, D_MODEL)

    def mix_norm():
        sg["x"] = x_ref[...].reshape(rows, D_MODEL)
        sg["xn"] = _rms_norm(sg["x"], g_mix_ref[...]).astype(jnp.bfloat16)

    def lru_proj(q):
        xb = _dot(sg["xn"], w_in_ref[:, xb_col0 + q * GATE_TILE:xb_col0 + (q + 1) * GATE_TILE])
        for i in range(slabs_per_tile):
            for s in range(nseq):
                slab_sc[slabs_per_tile * q + i][s * PITCH:s * PITCH + HALF, :] = (
                    xb[s * HALF:(s + 1) * HALF, i * LANES:(i + 1) * LANES])

    def lru_conv(q):
        xc_slabs = []
        for c in range(slabs_per_tile * q, slabs_per_tile * (q + 1)):
            lanes = slice(c * LANES, (c + 1) * LANES)
            taps = [jnp.broadcast_to(conv_w_ref[k:k + 1, lanes], (nseq, LANES)) for k in range(CONV_W)]
            taps.append(jnp.broadcast_to(conv_b_ref[:, lanes], (nseq, LANES)))
            p1, p2, p3 = (tail_sc[c][1 - parity, k] for k in range(CONV_W - 1))
            steps_out = []
            for t in range(HALF):
                cur = slab_sc[c][pl.ds(t, nseq, stride=PITCH), :]
                steps_out.append(taps[CONV_W] + taps[3] * cur + taps[2] * p1 + taps[1] * p2 + taps[0] * p3)
                p1, p2, p3 = cur, p1, p2
            for k, pk in enumerate((p1, p2, p3)):
                tail_sc[c][parity, k] = pk
            xc_slabs.append(jnp.concatenate(steps_out, axis=0))
        st[q]["xc"] = jnp.concatenate(xc_slabs, axis=1)

    def lru_gates(q):
        st[q]["ri"] = _dot(st[q]["xc"].astype(jnp.bfloat16), gate_sc[q])

    def lru_coeffs(q):
        cols = slice(q * GATE_TILE, (q + 1) * GATE_TILE)
        ri = st[q].pop("ri")
        st[q]["ab"] = _lru_elementwise(ri[:, :GATE_TILE], ri[:, GATE_TILE:], st[q].pop("xc"),
                                       half_rb[:, cols], half_ib[:, cols], half_c[:, cols])

    def lru_scan(q):
        a, b = st[q].pop("ab")
        for i in range(slabs_per_tile):
            c = slabs_per_tile * q + i
            h = h_sc[c][...]
            for t in range(HALF):
                h = (a[t * nseq:(t + 1) * nseq, i * LANES:(i + 1) * LANES] * h
                     + b[t * nseq:(t + 1) * nseq, i * LANES:(i + 1) * LANES])
                slab_sc[c][pl.ds(t, nseq, stride=PITCH), :] = h
            h_sc[c][...] = h

    def lru_out(q):
        hs = jnp.concatenate(
            [jnp.concatenate([slab_sc[c][s * PITCH:s * PITCH + HALF, :]
                              for c in range(slabs_per_tile * q, slabs_per_tile * (q + 1))], axis=1)
             for s in range(nseq)], axis=0)
        yb = _dot(sg["xn"], w_in_ref[:, yb_col0 + q * GATE_TILE:yb_col0 + (q + 1) * GATE_TILE])
        st[q]["out_b"] = (hs * _gelu(yb)).astype(jnp.bfloat16)

    def lru_mix(q):
        r0 = D_SGU + q * GATE_TILE
        sg["acc"] = sg["acc"] + _dot(st[q].pop("out_b"), w_out_ref[r0:r0 + GATE_TILE, :])

    def sgu_u():
        sg["u"] = _gelu(_dot(sg["xn"], w_in_ref[:, 0:D_SGU]))

    def sgu_v():
        v = _layer_norm(_gelu(_dot(sg["xn"], w_in_ref[:, D_SGU:2 * D_SGU])), sgu_g_ref[...], sgu_bn_ref[...])
        sg["vb"] = v.astype(jnp.bfloat16)

    def sgu_gate():
        vb = sg.pop("vb")
        lane = lax.broadcasted_iota(jnp.int32, (HALF, LANES), 1)
        bias = sgu_bias_ref[pl.ds(pl.multiple_of(parity * HALF, HALF), HALF), :]
        v_full = [jnp.concatenate([vprev_sc[1 - parity, s], vb[s * HALF:(s + 1) * HALF]], axis=0)
                  for s in range(nseq)]
        vprev_sc[parity] = vb.reshape(nseq, HALF, D_SGU)
        gate_rows = [[] for _ in range(nseq)]
        for s in range(0, nseq, 2):
            for g in range(N_LANE_GROUPS):
                lanes = slice(g * LANES, (g + 1) * LANES)
                rhs = jnp.concatenate([v_full[s][:, lanes], v_full[s + 1][:, lanes]], axis=1)
                res = _dot(sgu_lhs_sc[parity, g], rhs)
                for i in range(2):
                    part = res[:, i * LANES:(i + 1) * LANES]
                    gate_rows[s + i].append(jnp.where(lane < SGU_HEAD_DIM, part[:HALF], part[HALF:]))
        gate = jnp.concatenate([jnp.concatenate(r, axis=1) + bias for r in gate_rows], axis=0)
        sg["out_a"] = (sg.pop("u") * gate).astype(jnp.bfloat16)

    def sgu_mix():
        sg["acc"] = sg["x"] + _dot(sg.pop("out_a"), w_out_ref[0:D_SGU, :])

    def mix_store():
        ymix_sc[parity] = sg["acc"]

    tiles = range(N_GATE_TILES)
    mix_norm()
    for q in tiles:
        lru_proj(q)
    mlp_norm()
    mlp_up(0)
    for q in tiles:
        lru_conv(q)
        lru_gates(q)
    mlp_up(1)
    sgu_u()
    sgu_v()
    for q in tiles:
        lru_coeffs(q)
        lru_scan(q)
    mlp_down(0)
    mlp_down(1)
    for q in tiles:
        lru_out(q)
    mlp_up(2)
    mlp_up(3)
    sgu_gate()
    sgu_mix()
    mlp_down(2)
    mlp_down(3)
    for q in tiles:
        lru_mix(q)
    mix_store()
    mlp_store()


def _const_spec(shape):
    nd = len(shape)
    return pl.BlockSpec(shape, lambda *_: (0,) * nd, pipeline_mode=pl.Buffered(1))


def _layer_spec(arr, l):
    nd = arr.ndim - 1
    return pl.BlockSpec((None,) + arr.shape[1:], lambda *_: (l,) + (0,) * nd, pipeline_mode=pl.Buffered(1))


_BIG_WEIGHTS = ("w_in", "w_out", "w1", "w2")
_PROMPT_PARAMS = ("rows", "w_in", "conv_w", "gate_rw", "gate_iw", "sgu_w", "sgu_bias", "w_out", "w1", "w2")
_SAMPLE_MIXER_PARAMS = ("rows", "w_in", "conv_w", "gate_rw", "gate_iw", "sgu_w8", "sgu_b8", "w_out")


def _param_spec(p, k, l):
    return _layer_spec(p[k], 0 if k in _BIG_WEIGHTS else l)


def _layer_prompt(x, xs, p, l, *, final_norm, sample_seq_major, cast_next=()):
    nseq, seq, _ = x.shape
    n_tiles = seq // HALF
    rows = nseq * HALF
    assert xs.shape == (2 * rows, D_MODEL)
    consts = [p[k] for k in _PROMPT_PARAMS]
    cast_in_specs, cast_out_specs, cast_shapes = [], [], []
    for w in cast_next:
        k_rows, n_cols = w.shape[1:]
        blk = k_rows // n_tiles
        assert blk * n_tiles == k_rows and blk % (2 * SUBLANES) == 0
        cast_in_specs.append(pl.BlockSpec((None, blk, n_cols), lambda j: (l + 1, jnp.minimum(j, n_tiles - 1), 0)))
        cast_out_specs.append(pl.BlockSpec((None, blk, n_cols), lambda j: (0, jnp.minimum(j, n_tiles - 1), 0)))
        cast_shapes.append(jax.ShapeDtypeStruct((1, k_rows, n_cols), jnp.bfloat16))
    x_spec = pl.BlockSpec((nseq, HALF, D_MODEL), lambda j: (0, jnp.minimum(j, n_tiles - 1), 0))
    y_spec = pl.BlockSpec((nseq, HALF, D_MODEL), lambda j: (0, jnp.clip(j - 1, 0, n_tiles - 1), 0))
    xs_spec = pl.BlockSpec((rows, D_MODEL), lambda j: (jnp.where(j > n_tiles, 1, 0), 0),
                           pipeline_mode=pl.Buffered(1))
    y, ys, hlast, convnew, *cast = pl.pallas_call(
        functools.partial(_layer_prompt_kernel, nseq=nseq, n_tiles=n_tiles, final_norm=final_norm,
                          n_cast=len(cast_next), sample_seq_major=sample_seq_major),
        grid=(n_tiles + 2,),
        in_specs=[x_spec, xs_spec] + [_param_spec(p, k, l) for k in _PROMPT_PARAMS] + cast_in_specs,
        out_specs=([y_spec, xs_spec, _const_spec((nseq, D_LRU)), _const_spec((CONV_W - 1, nseq, D_LRU))]
                   + cast_out_specs),
        out_shape=[jax.ShapeDtypeStruct(x.shape, jnp.float32),
                   jax.ShapeDtypeStruct(xs.shape, jnp.float32),
                   jax.ShapeDtypeStruct((nseq, D_LRU), jnp.float32),
                   jax.ShapeDtypeStruct((CONV_W - 1, nseq, D_LRU), jnp.float32)] + cast_shapes,
        scratch_shapes=([pltpu.VMEM((2, nseq, HALF, D_SGU), jnp.bfloat16),
                         pltpu.VMEM((2, N_LANE_GROUPS, CHUNK, CHUNK), jnp.bfloat16),
                         pltpu.VMEM((2, rows, D_MODEL), jnp.float32),
                         pltpu.VMEM((N_GATE_TILES, GATE_TILE, 2 * GATE_TILE), jnp.bfloat16)]
                        + [pltpu.VMEM((nseq * PITCH, LANES), jnp.float32)] * N_SLABS
                        + [pltpu.VMEM((2, CONV_W - 1, nseq, LANES), jnp.float32)] * N_SLABS
                        + [pltpu.VMEM((nseq, LANES), jnp.float32)] * N_SLABS),
        compiler_params=pltpu.CompilerParams(dimension_semantics=("arbitrary",), vmem_limit_bytes=VMEM_LIMIT),
        name="layer_prompt",
    )(x, xs, *consts, *cast_next)
    return y, ys, hlast, jnp.transpose(convnew, (1, 0, 2)), cast


def _mixer_sample_kernel(*refs, steps, nb, seq_major, n_cast):
    n_in = 3 + len(_SAMPLE_MIXER_PARAMS)
    (x_ref, cbuf_ref, h0_ref, rows_ref, w_in_ref, conv_w_ref, gate_rw_ref, gate_iw_ref, sgu_w8_ref, sgu_b8_ref,
     w_out_ref) = refs[:n_in]
    cast_in, refs = refs[n_in:n_in + n_cast], refs[n_in + n_cast:]
    y_ref, v_ref, hlast_ref, convnew_ref = refs[:4]
    cast_out, refs = refs[4:4 + n_cast], refs[4 + n_cast:]
    gate_sc, v_relay_sc, relay_sc = refs[0], refs[1:1 + N_LANE_GROUPS], refs[1 + N_LANE_GROUPS:]
    for w_f32, w_bf16 in zip(cast_in, cast_out):
        w_bf16[...] = w_f32[...].astype(jnp.bfloat16)
    (g_mix_ref, _, gate_rb_ref, gate_ib_ref, lam_ref, conv_b_ref, sgu_g_ref, sgu_bn_ref, _) = _row_views(rows_ref)

    @pl.when(pl.program_id(0) == 0)
    def _():
        _build_gate_tiles(gate_rw_ref, gate_iw_ref, gate_sc)

    if seq_major:
        for c in range(D_MODEL // LANES):
            relay_sc[c][...] = x_ref[:, c * LANES:(c + 1) * LANES]
        x = jnp.concatenate(
            [jnp.concatenate([relay_sc[c][pl.ds(t, nb, stride=steps), :] for c in range(D_MODEL // LANES)], axis=1)
             for t in range(steps)], axis=0)
    else:
        x = x_ref[...].reshape(steps * nb, D_MODEL)
    xn = _rms_norm(x, g_mix_ref[...]).astype(jnp.bfloat16)

    u = _gelu(_dot(xn, w_in_ref[:, 0:D_SGU]))
    v = _layer_norm(_gelu(_dot(xn, w_in_ref[:, D_SGU:2 * D_SGU])), sgu_g_ref[...], sgu_bn_ref[...])
    pitch = nb + SUBLANES
    for g in range(N_LANE_GROUPS):
        for t in range(steps):
            v_relay_sc[g][t * pitch:t * pitch + nb, :] = v[t * nb:(t + 1) * nb, g * LANES:(g + 1) * LANES]
    for s in range(nb):
        for g in range(N_LANE_GROUPS):
            v_ref[s, :, g * LANES:(g + 1) * LANES] = v_relay_sc[g][pl.ds(s, steps, stride=pitch), :]
    gate_rows = []
    for t in range(steps):
        acc = sgu_b8_ref[t:t + 1, :] + sgu_w8_ref[t, 0:1, :] * v[0:nb]
        for s in range(1, t + 1):
            acc = acc + sgu_w8_ref[t, s:s + 1, :] * v[s * nb:(s + 1) * nb]
        gate_rows.append(acc)
    out_a = (u * jnp.concatenate(gate_rows, axis=0)).astype(jnp.bfloat16)

    xb = _dot(xn, w_in_ref[:, 2 * D_SGU:2 * D_SGU + D_LRU])
    xp = jnp.concatenate([cbuf_ref[k] for k in range(CONV_W - 1)] + [xb], axis=0)
    xc = conv_b_ref[...] + conv_w_ref[0:1, :] * xp[0:steps * nb]
    for k in range(1, CONV_W):
        xc = xc + conv_w_ref[k:k + 1, :] * xp[k * nb:(k + steps) * nb]
    convnew_ref[...] = xp[steps * nb:].reshape(CONV_W - 1, nb, D_LRU)

    a, b = _lru_coeffs(xc, gate_sc, gate_rb_ref[...], gate_ib_ref[...], lam_ref[...])
    h = h0_ref[...]
    hs = []
    for t in range(steps):
        h = a[t * nb:(t + 1) * nb] * h + b[t * nb:(t + 1) * nb]
        hs.append(h)
    hlast_ref[...] = h

    yb = _dot(xn, w_in_ref[:, 2 * D_SGU + D_LRU:])
    out_b = (jnp.concatenate(hs, axis=0) * _gelu(yb)).astype(jnp.bfloat16)
    out = _dot(jnp.concatenate([out_a, out_b], axis=1), w_out_ref[...])
    y_ref[...] = (x + out).reshape(steps, nb, D_MODEL)


def _mixer_sample(x, h0_all, cbuf_all, p, l, *, seq_major, groups, cast=()):
    nb_all = h0_all.shape[1]
    steps = x.shape[0] // nb_all if seq_major else x.shape[1]
    nb = nb_all // groups
    cast_in_specs, cast_out_specs, cast_shapes = [], [], []
    for w in cast:
        k_rows, n_cols = w.shape[1:]
        blk = k_rows // groups
        assert blk * groups == k_rows and blk % (2 * SUBLANES) == 0
        cast_in_specs.append(pl.BlockSpec((None, blk, n_cols), lambda i: (l, i, 0)))
        cast_out_specs.append(pl.BlockSpec((None, blk, n_cols), lambda i: (0, i, 0)))
        cast_shapes.append(jax.ShapeDtypeStruct((1, k_rows, n_cols), jnp.bfloat16))
    per_tile = HALF // nb
    assert per_tile * nb == HALF and nb_all % HALF == 0
    act_spec = pl.BlockSpec((None, steps, nb, D_MODEL), lambda i: (i // per_tile, 0, i % per_tile, 0))
    if seq_major:
        x_spec = pl.BlockSpec((nb * steps, D_MODEL), lambda i: (i, 0))
        relay = [pltpu.VMEM((nb * steps, LANES), jnp.float32)] * (D_MODEL // LANES)
    else:
        x_spec = act_spec
        relay = []
    group = lambda lead, width: pl.BlockSpec((lead, nb, width), lambda i: (0, i, 0))
    out_shapes = [(nb_all // HALF, steps, HALF, D_MODEL), (nb_all, steps, D_SGU), (nb_all, D_LRU),
                  (CONV_W - 1, nb_all, D_LRU)]
    v_relay = [pltpu.VMEM((steps * (nb + SUBLANES), LANES), jnp.float32)] * N_LANE_GROUPS
    y, v, hlast, convnew, *cast_bf16 = pl.pallas_call(
        functools.partial(_mixer_sample_kernel, steps=steps, nb=nb, seq_major=seq_major, n_cast=len(cast)),
        grid=(groups,),
        in_specs=([x_spec,
                   pl.BlockSpec((None, CONV_W - 1, nb, D_LRU), lambda i: (l, 0, i, 0)),
                   pl.BlockSpec((None, nb, D_LRU), lambda i: (l, i, 0))]
                  + [_param_spec(p, k, l) for k in _SAMPLE_MIXER_PARAMS] + cast_in_specs),
        out_specs=[act_spec, pl.BlockSpec((nb, steps, D_SGU), lambda i: (i, 0, 0)),
                   pl.BlockSpec((nb, D_LRU), lambda i: (i, 0)), group(CONV_W - 1, D_LRU)] + cast_out_specs,
        out_shape=[jax.ShapeDtypeStruct(s, jnp.float32) for s in out_shapes] + cast_shapes,
        scratch_shapes=[pltpu.VMEM((N_GATE_TILES, GATE_TILE, 2 * GATE_TILE), jnp.bfloat16)] + v_relay + relay,
        compiler_params=pltpu.CompilerParams(dimension_semantics=("arbitrary",), vmem_limit_bytes=VMEM_LIMIT),
        name="mixer_sample",
    )(x, cbuf_all, h0_all, *[p[k] for k in _SAMPLE_MIXER_PARAMS], *cast)
    return y, v, hlast, convnew, cast_bf16


CAST_STEPS = 4


def _cast_kernel(*refs):
    n = len(refs) // 2
    for w_f32, w_bf16 in zip(refs[:n], refs[n:]):
        w_bf16[...] = w_f32[...].astype(jnp.bfloat16)


def _cast_layer(weights, l):
    in_specs, out_specs, out_shapes = [], [], []
    for w in weights:
        k_rows, n_cols = w.shape[1:]
        blk = k_rows // CAST_STEPS
        assert blk * CAST_STEPS == k_rows and blk % (2 * SUBLANES) == 0
        in_specs.append(pl.BlockSpec((None, blk, n_cols), lambda j: (l, j, 0)))
        out_specs.append(pl.BlockSpec((None, blk, n_cols), lambda j: (0, j, 0)))
        out_shapes.append(jax.ShapeDtypeStruct((1, k_rows, n_cols), jnp.bfloat16))
    return pl.pallas_call(
        _cast_kernel,
        grid=(CAST_STEPS,),
        in_specs=in_specs,
        out_specs=out_specs,
        out_shape=out_shapes,
        compiler_params=pltpu.CompilerParams(dimension_semantics=("arbitrary",), vmem_limit_bytes=VMEM_LIMIT),
        name="cast_weights",
    )(*weights)


def _prepare_params(steps, nseq, norm_mix_g, w_in, conv_w, conv_b, gate_r_w, gate_r_b, gate_i_w, gate_i_b,
                    lru_lambda, sgu_norm_g, sgu_norm_b, sgu_w, sgu_b, w_out, norm_mlp_g, mlp_w1, mlp_w2, final_norm_g):
    rows = [None] * N_ROWS
    rows[ROW_G_MIX], rows[ROW_G_MLP] = norm_mix_g, norm_mlp_g
    rows[ROW_GATE_RB], rows[ROW_GATE_IB], rows[ROW_LAM], rows[ROW_CONV_B] = gate_r_b, gate_i_b, lru_lambda, conv_b
    rows[ROW_SGU] = jnp.concatenate([sgu_norm_g, sgu_norm_b], axis=-1)
    rows[ROW_G_FINAL] = jnp.broadcast_to(final_norm_g, (DEPTH, D_MODEL))
    return {
        "rows": jnp.stack(rows, axis=1),
        "conv_w": conv_w,
        "gate_rw": gate_r_w,
        "gate_iw": gate_i_w,
        "sgu_w": sgu_w,
        "sgu_bias": jnp.repeat(jnp.transpose(sgu_b, (0, 2, 1)), SGU_HEAD_DIM, axis=2),
        "sgu_w8": jnp.repeat(jnp.transpose(sgu_w[:, :, :steps, :steps], (0, 2, 3, 1)), SGU_HEAD_DIM, axis=3),
        "sgu_b8": jnp.repeat(jnp.transpose(sgu_b[:, :, :steps], (0, 2, 1)), SGU_HEAD_DIM, axis=2),
    }


def kernel(x_prompt, x_sample, state_lru_h, state_conv, norm_mix_g, w_in, conv_w, conv_b, gate_r_w, gate_r_b, gate_i_w, gate_i_b, lru_lambda, sgu_norm_g, sgu_norm_b, sgu_w, sgu_b, w_out, norm_mlp_g, mlp_w1, mlp_w2, final_norm_g):
    nseq, seq, _ = x_prompt.shape
    nb, steps, _ = x_sample.shape
    assert seq % CHUNK == 0 and nseq == SUBLANES and steps == nseq and nb == 2 * HALF
    p = _prepare_params(steps, nseq, norm_mix_g, w_in, conv_w, conv_b, gate_r_w, gate_r_b, gate_i_w, gate_i_b,
                        lru_lambda, sgu_norm_g, sgu_norm_b, sgu_w, sgu_b, w_out, norm_mlp_g, mlp_w1, mlp_w2,
                        final_norm_g)

    xp = x_prompt
    xs = x_sample.reshape(nb * steps, D_MODEL)
    cbuf_tm = jnp.transpose(state_conv, (0, 2, 1, 3))
    hs_p, convs_p, hs_s, convs_s, vs_s = [], [], [], [], []
    big_f32 = (w_in, w_out, mlp_w1, mlp_w2)
    big = _cast_layer(big_f32[:2], 0)
    for l in range(DEPTH):
        last = l == DEPTH - 1
        first = l == 0
        p.update(zip(_BIG_WEIGHTS, big))
        xs, v_s, h_s, conv_s, mlp_bf16 = _mixer_sample(
            xs, state_lru_h, cbuf_tm, p, l, seq_major=first,
            groups=SAMPLE_GROUPS_CAST if first else SAMPLE_GROUPS, cast=big_f32[2:] if first else ())
        p.update(zip(_BIG_WEIGHTS[2:], mlp_bf16))
        xp, xs, h_p, conv_p, big = _layer_prompt(xp, xs.reshape(steps * nb, D_MODEL), p, l, final_norm=last,
                                                 sample_seq_major=last, cast_next=() if last else big_f32)
        if not last:
            xs = xs.reshape(nb // HALF, steps, HALF, D_MODEL)
        hs_p.append(h_p)
        convs_p.append(conv_p)
        hs_s.append(h_s)
        convs_s.append(conv_s)
        vs_s.append(v_s)

    y_sample = xs.reshape(nb, steps, D_MODEL)
    return (xp, y_sample, jnp.stack(hs_p), jnp.stack(convs_p), jnp.stack(hs_s),
            jnp.transpose(jnp.stack(convs_s), (0, 2, 1, 3)), jnp.stack(vs_s))
```

```python
import functools

import jax
import jax.numpy as jnp
from jax import lax
from jax.experimental import pallas as pl
from jax.experimental.pallas import tpu as pltpu

D_MODEL = 1024
DEPTH = 2
SGU_HEADS = 8
SGU_HEAD_DIM = 64
D_SGU = SGU_HEADS * SGU_HEAD_DIM
CHUNK = 128
D_LRU = 1024
LRU_BLOCKS = 16
LRU_BLOCK_DIM = 64
CONV_W = 4
LRU_C = 8.0
D_FF = 4 * D_MODEL
EPS = 1e-6

SUBLANES = 8
LANES = 128
GATE_TILE = 256
N_GATE_TILES = D_LRU // GATE_TILE
HEADS_PER_LANE_GROUP = LANES // SGU_HEAD_DIM
N_LANE_GROUPS = D_SGU // LANES
N_SLABS = D_LRU // LANES
FF_CHUNK = D_MODEL
N_FF_CHUNKS = D_FF // FF_CHUNK

SAMPLE_GROUPS = 2
SAMPLE_GROUPS_CAST = 4
HALF = CHUNK // 2
PITCH = HALF + SUBLANES
V7X_VMEM_BYTES = 64 * 1024 * 1024
VMEM_HEADROOM = 2 * 1024 * 1024
VMEM_LIMIT = V7X_VMEM_BYTES - VMEM_HEADROOM
F32_TINY = 1.1754944e-38


def _rms_norm(x, g):
    return x * lax.rsqrt(jnp.mean(x * x, axis=-1, keepdims=True) + EPS) * g


def _layer_norm(x, g, b):
    mu = jnp.mean(x, axis=-1, keepdims=True)
    xc = x - mu
    return xc * lax.rsqrt(jnp.mean(xc * xc, axis=-1, keepdims=True) + EPS) * g + b


def _gelu(x):
    return jax.nn.gelu(x, approximate=True)


def _dot(a, b):
    return jnp.dot(a, b, preferred_element_type=jnp.float32)


def _lru_elementwise(r_pre, i_pre, xc, half_rb, half_ib, half_c):
    th_r = jnp.tanh(r_pre + half_rb)
    th_i = jnp.tanh(i_pre + half_ib)
    log_a = half_c * th_r + half_c
    i = 0.5 * th_i + 0.5
    a = jnp.exp(log_a)
    y = jnp.tanh(log_a) * (-1.0 - a * a)
    b = (y * lax.rsqrt(jnp.maximum(y, F32_TINY))) * (i * xc)
    return a, b


def _lru_consts(gate_rb, gate_ib, lam):
    return 0.5 * gate_rb, 0.5 * gate_ib, (0.5 * LRU_C) * jax.nn.log_sigmoid(lam)


def _build_gate_tiles(gate_rw_ref, gate_iw_ref, gate_sc):
    per_tile = GATE_TILE // LRU_BLOCK_DIM
    k_idx = lax.broadcasted_iota(jnp.int32, (LRU_BLOCK_DIM, GATE_TILE), 0)
    n_idx = lax.broadcasted_iota(jnp.int32, (LRU_BLOCK_DIM, GATE_TILE), 1)
    replicate = (n_idx % LRU_BLOCK_DIM == k_idx).astype(jnp.bfloat16)
    row_blk = lax.broadcasted_iota(jnp.int32, (GATE_TILE, GATE_TILE), 0) // LRU_BLOCK_DIM
    col_blk = lax.broadcasted_iota(jnp.int32, (GATE_TILE, GATE_TILE), 1) // LRU_BLOCK_DIM
    for j in range(N_GATE_TILES):
        for k, ref in enumerate((gate_rw_ref, gate_iw_ref)):
            stacked = ref[per_tile * j:per_tile * (j + 1)].reshape(GATE_TILE, LRU_BLOCK_DIM)
            rep = _dot(stacked.astype(jnp.bfloat16), replicate)
            gate_sc[j, :, k * GATE_TILE:(k + 1) * GATE_TILE] = jnp.where(
                row_blk == col_blk, 0.5 * rep, 0.0).astype(jnp.bfloat16)


def _lru_coeffs(xc, gate_w_ref, gate_rb, gate_ib, lam):
    xcb = xc.astype(jnp.bfloat16)
    r_parts, i_parts = [], []
    for j in range(N_GATE_TILES):
        ri = _dot(xcb[:, j * GATE_TILE:(j + 1) * GATE_TILE], gate_w_ref[j])
        r_parts.append(ri[:, :GATE_TILE])
        i_parts.append(ri[:, GATE_TILE:])
    return _lru_elementwise(jnp.concatenate(r_parts, axis=1), jnp.concatenate(i_parts, axis=1), xc,
                            *_lru_consts(gate_rb, gate_ib, lam))


ROW_G_MIX, ROW_G_MLP, ROW_GATE_RB, ROW_GATE_IB, ROW_LAM, ROW_CONV_B, ROW_SGU, ROW_G_FINAL = range(8)
N_ROWS = 8


def _row_views(rows_ref):
    one = lambda k, lo=0, hi=D_MODEL: rows_ref.at[k:k + 1, lo:hi]
    return (one(ROW_G_MIX), one(ROW_G_MLP), one(ROW_GATE_RB), one(ROW_GATE_IB), one(ROW_LAM), one(ROW_CONV_B),
            one(ROW_SGU, 0, D_SGU), one(ROW_SGU, D_SGU, 2 * D_SGU), one(ROW_G_FINAL))


def _mlp_up(xn, w1_ref, c):
    cols = slice(c * FF_CHUNK, (c + 1) * FF_CHUNK)
    return jnp.square(jnp.maximum(_dot(xn, w1_ref[:, cols]), 0.0)).astype(jnp.bfloat16)


def _mlp_down(hid, w2_ref, c):
    return _dot(hid, w2_ref[c * FF_CHUNK:(c + 1) * FF_CHUNK, :])


def _layer_prompt_kernel(*refs, nseq, n_tiles, final_norm, n_cast, sample_seq_major):
    n_in = 2 + len(_PROMPT_PARAMS)
    (x_ref, xs_ref, rows_ref, w_in_ref, conv_w_ref, gate_rw_ref, gate_iw_ref, sgu_w_ref, sgu_bias_ref, w_out_ref,
     w1_ref, w2_ref) = refs[:n_in]
    (g_mix_ref, g_mlp_ref, gate_rb_ref, gate_ib_ref, lam_ref, conv_b_ref, sgu_g_ref, sgu_bn_ref,
     gf_ref) = _row_views(rows_ref)
    cast_in, refs = refs[n_in:n_in + n_cast], refs[n_in + n_cast:]
    y_ref, ys_ref, hlast_ref, convnew_ref = refs[:4]
    cast_out, refs = refs[4:4 + n_cast], refs[4 + n_cast:]
    vprev_sc, sgu_lhs_sc, ymix_sc, gate_sc = refs[:4]
    slab_scratch = refs[4:]
    rows = nseq * HALF
    j = pl.program_id(0)
    slab_sc = slab_scratch[0:N_SLABS]
    tail_sc = slab_scratch[N_SLABS:2 * N_SLABS]
    h_sc = slab_scratch[2 * N_SLABS:3 * N_SLABS]

    @pl.when(j == 0)
    def _():
        for c in range(N_SLABS):
            tail_sc[c][...] = jnp.zeros_like(tail_sc[c])
            h_sc[c][...] = jnp.zeros_like(h_sc[c])
        vprev_sc[...] = jnp.zeros_like(vprev_sc)
        ymix_sc[1] = xs_ref[...]
        _build_gate_tiles(gate_rw_ref, gate_iw_ref, gate_sc)
        t_idx = lax.broadcasted_iota(jnp.int32, (CHUNK, CHUNK), 0)
        s_idx = lax.broadcasted_iota(jnp.int32, (CHUNK, CHUNK), 1)
        for g in range(N_LANE_GROUPS):
            first, second = [], []
            for hh in range(HEADS_PER_LANE_GROUP):
                wm = jnp.where(s_idx <= t_idx, sgu_w_ref[HEADS_PER_LANE_GROUP * g + hh], 0.0)
                first.append(pltpu.roll(wm[:HALF], HALF, axis=1))
                second.append(wm[HALF:])
            sgu_lhs_sc[0, g] = jnp.concatenate(first, axis=0).astype(jnp.bfloat16)
            sgu_lhs_sc[1, g] = jnp.concatenate(second, axis=0).astype(jnp.bfloat16)

    parity = j % 2

    @pl.when(j < n_tiles)
    def _():
        _fused_step(x_ref, g_mix_ref, w_in_ref, conv_w_ref, conv_b_ref, gate_rb_ref, gate_ib_ref, lam_ref, sgu_g_ref, sgu_bn_ref,
                    sgu_bias_ref, w_out_ref, g_mlp_ref, w1_ref, w2_ref, gf_ref, y_ref, vprev_sc, sgu_lhs_sc, ymix_sc,
                    gate_sc, slab_sc, tail_sc, h_sc, parity=parity, nseq=nseq, final_norm=final_norm)
        for w_f32, w_bf16 in zip(cast_in, cast_out):
            w_bf16[...] = w_f32[...].astype(jnp.bfloat16)

    def store_sample(tile):
        if not sample_seq_major:
            ys_ref[...] = tile
            return
        for c in range(N_SLABS):
            for t in range(nseq):
                slab_sc[c][t * PITCH:t * PITCH + HALF, :] = tile[t * HALF:(t + 1) * HALF, c * LANES:(c + 1) * LANES]
        for s in range(HALF):
            for c in range(N_SLABS):
                ys_ref[s * nseq:(s + 1) * nseq, c * LANES:(c + 1) * LANES] = (
                    slab_sc[c][pl.ds(s, nseq, stride=PITCH), :])

    @pl.when(j == 0)
    def _():
        store_sample(y_ref[...].reshape(rows, D_MODEL))

    @pl.when(j == n_tiles - 1)
    def _():
        for c in range(N_SLABS):
            cols = slice(c * LANES, (c + 1) * LANES)
            hlast_ref[:, cols] = h_sc[c][...]
            for k in range(CONV_W - 1):
                convnew_ref[CONV_W - 2 - k, :, cols] = tail_sc[c][parity, k]

    @pl.when(j == n_tiles + 1)
    def _():
        ymix_sc[1 - parity] = xs_ref[...]

    @pl.when(j >= n_tiles)
    def _():
        xm = ymix_sc[1 - parity]
        xn = _rms_norm(xm, g_mlp_ref[...]).astype(jnp.bfloat16)
        acc = xm
        for c in range(N_FF_CHUNKS):
            acc = acc + _mlp_down(_mlp_up(xn, w1_ref, c), w2_ref, c)
        ymix_sc[parity] = _rms_norm(acc, gf_ref[...]) if final_norm else acc

    @pl.when(j == n_tiles)
    def _():
        y_ref[...] = ymix_sc[parity].reshape(nseq, HALF, D_MODEL)

    @pl.when(j == n_tiles + 1)
    def _():
        store_sample(ymix_sc[parity])


def _fused_step(x_ref, g_mix_ref, w_in_ref, conv_w_ref, conv_b_ref, gate_rb_ref, gate_ib_ref, lam_ref, sgu_g_ref, sgu_bn_ref,
                sgu_bias_ref, w_out_ref, g_mlp_ref, w1_ref, w2_ref, gf_ref, y_ref, vprev_sc, sgu_lhs_sc, ymix_sc,
                gate_sc, slab_sc, tail_sc, h_sc, *, parity, nseq, final_norm):
    rows = nseq * HALF
    half_rb, half_ib, half_c = _lru_consts(gate_rb_ref[...], gate_ib_ref[...], lam_ref[...])
    xb_col0 = 2 * D_SGU
    yb_col0 = 2 * D_SGU + D_LRU
    slabs_per_tile = GATE_TILE // LANES
    st = [{} for _ in range(N_GATE_TILES)]
    sg = {}
    ml = {}

    def mlp_norm():
        xm = ymix_sc[1 - parity]
        ml["acc"] = xm
        ml["xn"] = _rms_norm(xm, g_mlp_ref[...]).astype(jnp.bfloat16)

    def mlp_up(c):
        ml["hid", c] = _mlp_up(ml["xn"], w1_ref, c)

    def mlp_down(c):
        ml["acc"] = ml["acc"] + _mlp_down(ml.pop(("hid", c)), w2_ref, c)

    def mlp_store():
        out = _rms_norm(ml["acc"], gf_ref[...]) if final_norm else ml["acc"]
        y_ref[...] = out.reshape(nseq, HALF, D_MODEL)

    def mix_norm():
        sg["x"] = x_ref[...].reshape(rows, D_MODEL)
        sg["xn"] = _rms_norm(sg["x"], g_mix_ref[...]).astype(jnp.bfloat16)

    def lru_proj(q):
        xb = _dot(sg["xn"], w_in_ref[:, xb_col0 + q * GATE_TILE:xb_col0 + (q + 1) * GATE_TILE])
        for i in range(slabs_per_tile):
            for s in range(nseq):
                slab_sc[slabs_per_tile * q + i][s * PITCH:s * PITCH + HALF, :] = (
                    xb[s * HALF:(s + 1) * HALF, i * LANES:(i + 1) * LANES])

    def lru_conv(q):
        xc_slabs = []
        for c in range(slabs_per_tile * q, slabs_per_tile * (q + 1)):
            lanes = slice(c * LANES, (c + 1) * LANES)
            taps = [jnp.broadcast_to(conv_w_ref[k:k + 1, lanes], (nseq, LANES)) for k in range(CONV_W)]
            taps.append(jnp.broadcast_to(conv_b_ref[:, lanes], (nseq, LANES)))
            p1, p2, p3 = (tail_sc[c][1 - parity, k] for k in range(CONV_W - 1))
            steps_out = []
            for t in range(HALF):
                cur = slab_sc[c][pl.ds(t, nseq, stride=PITCH), :]
                steps_out.append(taps[CONV_W] + taps[3] * cur + taps[2] * p1 + taps[1] * p2 + taps[0] * p3)
                p1, p2, p3 = cur, p1, p2
            for k, pk in enumerate((p1, p2, p3)):
                tail_sc[c][parity, k] = pk
            xc_slabs.append(jnp.concatenate(steps_out, axis=0))
        st[q]["xc"] = jnp.concatenate(xc_slabs, axis=1)

    def lru_gates(q):
        st[q]["ri"] = _dot(st[q]["xc"].astype(jnp.bfloat16), gate_sc[q])

    def lru_coeffs(q):
        cols = slice(q * GATE_TILE, (q + 1) * GATE_TILE)
        ri = st[q].pop("ri")
        st[q]["ab"] = _lru_elementwise(ri[:, :GATE_TILE], ri[:, GATE_TILE:], st[q].pop("xc"),
                                       half_rb[:, cols], half_ib[:, cols], half_c[:, cols])

    def lru_scan(q):
        a, b = st[q].pop("ab")
        for i in range(slabs_per_tile):
            c = slabs_per_tile * q + i
            h = h_sc[c][...]
            for t in range(HALF):
                h = (a[t * nseq:(t + 1) * nseq, i * LANES:(i + 1) * LANES] * h
                     + b[t * nseq:(t + 1) * nseq, i * LANES:(i + 1) * LANES])
                slab_sc[c][pl.ds(t, nseq, stride=PITCH), :] = h
            h_sc[c][...] = h

    def lru_out(q):
        hs = jnp.concatenate(
            [jnp.concatenate([slab_sc[c][s * PITCH:s * PITCH + HALF, :]
                              for c in range(slabs_per_tile * q, slabs_per_tile * (q + 1))], axis=1)
             for s in range(nseq)], axis=0)
        yb = _dot(sg["xn"], w_in_ref[:, yb_col0 + q * GATE_TILE:yb_col0 + (q + 1) * GATE_TILE])
        st[q]["out_b"] = (hs * _gelu(yb)).astype(jnp.bfloat16)

    def lru_mix(q):
        r0 = D_SGU + q * GATE_TILE
        sg["acc"] = sg["acc"] + _dot(st[q].pop("out_b"), w_out_ref[r0:r0 + GATE_TILE, :])

    def sgu_u():
        sg["u"] = _gelu(_dot(sg["xn"], w_in_ref[:, 0:D_SGU]))

    def sgu_v():
        v = _layer_norm(_gelu(_dot(sg["xn"], w_in_ref[:, D_SGU:2 * D_SGU])), sgu_g_ref[...], sgu_bn_ref[...])
        sg["vb"] = v.astype(jnp.bfloat16)

    def sgu_gate():
        vb = sg.pop("vb")
        lane = lax.broadcasted_iota(jnp.int32, (HALF, LANES), 1)
        bias = sgu_bias_ref[pl.ds(pl.multiple_of(parity * HALF, HALF), HALF), :]
        v_full = [jnp.concatenate([vprev_sc[1 - parity, s], vb[s * HALF:(s + 1) * HALF]], axis=0)
                  for s in range(nseq)]
        vprev_sc[parity] = vb.reshape(nseq, HALF, D_SGU)
        gate_rows = [[] for _ in range(nseq)]
        for s in range(0, nseq, 2):
            for g in range(N_LANE_GROUPS):
                lanes = slice(g * LANES, (g + 1) * LANES)
                rhs = jnp.concatenate([v_full[s][:, lanes], v_full[s + 1][:, lanes]], axis=1)
                res = _dot(sgu_lhs_sc[parity, g], rhs)
                for i in range(2):
                    part = res[:, i * LANES:(i + 1) * LANES]
                    gate_rows[s + i].append(jnp.where(lane < SGU_HEAD_DIM, part[:HALF], part[HALF:]))
        gate = jnp.concatenate([jnp.concatenate(r, axis=1) + bias for r in gate_rows], axis=0)
        sg["out_a"] = (sg.pop("u") * gate).astype(jnp.bfloat16)

    def sgu_mix():
        sg["acc"] = sg["x"] + _dot(sg.pop("out_a"), w_out_ref[0:D_SGU, :])

    def mix_store():
        ymix_sc[parity] = sg["acc"]

    tiles = range(N_GATE_TILES)
    mlp_norm()
    mix_norm()
    for q in tiles:
        lru_proj(q)
    mlp_up(0)
    for q in tiles:
        lru_conv(q)
        lru_gates(q)
    mlp_up(1)
    sgu_u()
    sgu_v()
    for q in tiles:
        lru_coeffs(q)
        lru_scan(q)
    mlp_down(0)
    mlp_down(1)
    for q in tiles:
        lru_out(q)
    mlp_up(2)
    mlp_up(3)
    sgu_gate()
    sgu_mix()
    mlp_down(2)
    mlp_down(3)
    for q in tiles:
        lru_mix(q)
    mix_store()
    mlp_store()


def _const_spec(shape):
    nd = len(shape)
    return pl.BlockSpec(shape, lambda *_: (0,) * nd, pipeline_mode=pl.Buffered(1))


def _layer_spec(arr, l):
    nd = arr.ndim - 1
    return pl.BlockSpec((None,) + arr.shape[1:], lambda *_: (l,) + (0,) * nd, pipeline_mode=pl.Buffered(1))


_BIG_WEIGHTS = ("w_in", "w_out", "w1", "w2")
_PROMPT_PARAMS = ("rows", "w_in", "conv_w", "gate_rw", "gate_iw", "sgu_w", "sgu_bias", "w_out", "w1", "w2")
_SAMPLE_MIXER_PARAMS = ("rows", "w_in", "conv_w", "gate_rw", "gate_iw", "sgu_w8", "sgu_b8", "w_out")


def _param_spec(p, k, l):
    return _layer_spec(p[k], 0 if k in _BIG_WEIGHTS else l)


def _layer_prompt(x, xs, p, l, *, final_norm, sample_seq_major, cast_next=()):
    nseq, seq, _ = x.shape
    n_tiles = seq // HALF
    rows = nseq * HALF
    assert xs.shape == (2 * rows, D_MODEL)
    consts = [p[k] for k in _PROMPT_PARAMS]
    cast_in_specs, cast_out_specs, cast_shapes = [], [], []
    for w in cast_next:
        k_rows, n_cols = w.shape[1:]
        blk = k_rows // n_tiles
        assert blk * n_tiles == k_rows and blk % (2 * SUBLANES) == 0
        cast_in_specs.append(pl.BlockSpec((None, blk, n_cols), lambda j: (l + 1, jnp.minimum(j, n_tiles - 1), 0)))
        cast_out_specs.append(pl.BlockSpec((None, blk, n_cols), lambda j: (0, jnp.minimum(j, n_tiles - 1), 0)))
        cast_shapes.append(jax.ShapeDtypeStruct((1, k_rows, n_cols), jnp.bfloat16))
    x_spec = pl.BlockSpec((nseq, HALF, D_MODEL), lambda j: (0, jnp.minimum(j, n_tiles - 1), 0))
    y_spec = pl.BlockSpec((nseq, HALF, D_MODEL), lambda j: (0, jnp.clip(j - 1, 0, n_tiles - 1), 0))
    xs_spec = pl.BlockSpec((rows, D_MODEL), lambda j: (jnp.where(j > n_tiles, 1, 0), 0),
                           pipeline_mode=pl.Buffered(1))
    y, ys, hlast, convnew, *cast = pl.pallas_call(
        functools.partial(_layer_prompt_kernel, nseq=nseq, n_tiles=n_tiles, final_norm=final_norm,
                          n_cast=len(cast_next), sample_seq_major=sample_seq_major),
        grid=(n_tiles + 2,),
        in_specs=[x_spec, xs_spec] + [_param_spec(p, k, l) for k in _PROMPT_PARAMS] + cast_in_specs,
        out_specs=([y_spec, xs_spec, _const_spec((nseq, D_LRU)), _const_spec((CONV_W - 1, nseq, D_LRU))]
                   + cast_out_specs),
        out_shape=[jax.ShapeDtypeStruct(x.shape, jnp.float32),
                   jax.ShapeDtypeStruct(xs.shape, jnp.float32),
                   jax.ShapeDtypeStruct((nseq, D_LRU), jnp.float32),
                   jax.ShapeDtypeStruct((CONV_W - 1, nseq, D_LRU), jnp.float32)] + cast_shapes,
        scratch_shapes=([pltpu.VMEM((2, nseq, HALF, D_SGU), jnp.bfloat16),
                         pltpu.VMEM((2, N_LANE_GROUPS, CHUNK, CHUNK), jnp.bfloat16),
                         pltpu.VMEM((2, rows, D_MODEL), jnp.float32),
                         pltpu.VMEM((N_GATE_TILES, GATE_TILE, 2 * GATE_TILE), jnp.bfloat16)]
                        + [pltpu.VMEM((nseq * PITCH, LANES), jnp.float32)] * N_SLABS
                        + [pltpu.VMEM((2, CONV_W - 1, nseq, LANES), jnp.float32)] * N_SLABS
                        + [pltpu.VMEM((nseq, LANES), jnp.float32)] * N_SLABS),
        compiler_params=pltpu.CompilerParams(dimension_semantics=("arbitrary",), vmem_limit_bytes=VMEM_LIMIT),
        name="layer_prompt",
    )(x, xs, *consts, *cast_next)
    return y, ys, hlast, jnp.transpose(convnew, (1, 0, 2)), cast


def _mixer_sample_kernel(*refs, steps, nb, seq_major, n_cast, layer):
    n_in = 3 + len(_SAMPLE_MIXER_PARAMS)
    (x_ref, cbuf_ref, h0_ref, rows_ref, w_in_ref, conv_w_ref, gate_rw_ref, gate_iw_ref, sgu_w8_ref, sgu_b8_ref,
     w_out_ref) = refs[:n_in]
    cast_in, refs = refs[n_in:n_in + n_cast], refs[n_in + n_cast:]
    n_prev = 3 if layer else 0
    prev_refs, refs = refs[:n_prev], refs[n_prev:]
    y_ref, v_ref, hlast_ref, convnew_ref = refs[:4]
    cast_out, refs = refs[4:4 + n_cast], refs[4 + n_cast:]
    gate_sc, v_relay_sc, relay_sc = refs[0], refs[1:1 + N_LANE_GROUPS], refs[1 + N_LANE_GROUPS:]
    for dst, prev in zip((v_ref, hlast_ref, convnew_ref), prev_refs):
        dst[0:layer] = prev[...]
    for w_f32, w_bf16 in zip(cast_in, cast_out):
        w_bf16[...] = w_f32[...].astype(jnp.bfloat16)
    (g_mix_ref, _, gate_rb_ref, gate_ib_ref, lam_ref, conv_b_ref, sgu_g_ref, sgu_bn_ref, _) = _row_views(rows_ref)

    @pl.when(pl.program_id(0) == 0)
    def _():
        _build_gate_tiles(gate_rw_ref, gate_iw_ref, gate_sc)

    if seq_major:
        for c in range(D_MODEL // LANES):
            relay_sc[c][...] = x_ref[:, c * LANES:(c + 1) * LANES]
        x = jnp.concatenate(
            [jnp.concatenate([relay_sc[c][pl.ds(t, nb, stride=steps), :] for c in range(D_MODEL // LANES)], axis=1)
             for t in range(steps)], axis=0)
    else:
        x = x_ref[...].reshape(steps * nb, D_MODEL)
    xn = _rms_norm(x, g_mix_ref[...]).astype(jnp.bfloat16)

    u = _gelu(_dot(xn, w_in_ref[:, 0:D_SGU]))
    v = _layer_norm(_gelu(_dot(xn, w_in_ref[:, D_SGU:2 * D_SGU])), sgu_g_ref[...], sgu_bn_ref[...])
    pitch = nb + SUBLANES
    for g in range(N_LANE_GROUPS):
        for t in range(steps):
            v_relay_sc[g][t * pitch:t * pitch + nb, :] = v[t * nb:(t + 1) * nb, g * LANES:(g + 1) * LANES]
    for s in range(nb):
        for g in range(N_LANE_GROUPS):
            v_ref[layer, s, :, g * LANES:(g + 1) * LANES] = v_relay_sc[g][pl.ds(s, steps, stride=pitch), :]
    gate_rows = []
    for t in range(steps):
        acc = sgu_b8_ref[t:t + 1, :] + sgu_w8_ref[t, 0:1, :] * v[0:nb]
        for s in range(1, t + 1):
            acc = acc + sgu_w8_ref[t, s:s + 1, :] * v[s * nb:(s + 1) * nb]
        gate_rows.append(acc)
    out_a = (u * jnp.concatenate(gate_rows, axis=0)).astype(jnp.bfloat16)

    xb = _dot(xn, w_in_ref[:, 2 * D_SGU:2 * D_SGU + D_LRU])
    xp = jnp.concatenate([cbuf_ref[k] for k in range(CONV_W - 1)] + [xb], axis=0)
    xc = conv_b_ref[...] + conv_w_ref[0:1, :] * xp[0:steps * nb]
    for k in range(1, CONV_W):
        xc = xc + conv_w_ref[k:k + 1, :] * xp[k * nb:(k + steps) * nb]
    convnew_ref[layer] = xp[steps * nb:].reshape(CONV_W - 1, nb, D_LRU)

    a, b = _lru_coeffs(xc, gate_sc, gate_rb_ref[...], gate_ib_ref[...], lam_ref[...])
    h = h0_ref[...]
    hs = []
    for t in range(steps):
        h = a[t * nb:(t + 1) * nb] * h + b[t * nb:(t + 1) * nb]
        hs.append(h)
    hlast_ref[layer] = h

    yb = _dot(xn, w_in_ref[:, 2 * D_SGU + D_LRU:])
    out_b = (jnp.concatenate(hs, axis=0) * _gelu(yb)).astype(jnp.bfloat16)
    out = _dot(jnp.concatenate([out_a, out_b], axis=1), w_out_ref[...])
    y_ref[...] = (x + out).reshape(steps, nb, D_MODEL)


def _mixer_sample(x, h0_all, cbuf_all, p, l, *, seq_major, groups, cast=(), prev=()):
    assert len(prev) == (3 if l else 0)
    nb_all = h0_all.shape[1]
    steps = x.shape[0] // nb_all if seq_major else x.shape[1]
    nb = nb_all // groups
    cast_in_specs, cast_out_specs, cast_shapes = [], [], []
    for w in cast:
        k_rows, n_cols = w.shape[1:]
        blk = k_rows // groups
        assert blk * groups == k_rows and blk % (2 * SUBLANES) == 0
        cast_in_specs.append(pl.BlockSpec((None, blk, n_cols), lambda i: (l, i, 0)))
        cast_out_specs.append(pl.BlockSpec((None, blk, n_cols), lambda i: (0, i, 0)))
        cast_shapes.append(jax.ShapeDtypeStruct((1, k_rows, n_cols), jnp.bfloat16))
    per_tile = HALF // nb
    assert per_tile * nb == HALF and nb_all % HALF == 0
    act_spec = pl.BlockSpec((None, steps, nb, D_MODEL), lambda i: (i // per_tile, 0, i % per_tile, 0))
    if seq_major:
        x_spec = pl.BlockSpec((nb * steps, D_MODEL), lambda i: (i, 0))
        relay = [pltpu.VMEM((nb * steps, LANES), jnp.float32)] * (D_MODEL // LANES)
    else:
        x_spec = act_spec
        relay = []
    state_specs = lambda layers: [pl.BlockSpec((layers, nb, steps, D_SGU), lambda i: (0, i, 0, 0)),
                                  pl.BlockSpec((layers, nb, D_LRU), lambda i: (0, i, 0)),
                                  pl.BlockSpec((layers, CONV_W - 1, nb, D_LRU), lambda i: (0, 0, i, 0))]
    out_shapes = [(nb_all // HALF, steps, HALF, D_MODEL), (l + 1, nb_all, steps, D_SGU), (l + 1, nb_all, D_LRU),
                  (l + 1, CONV_W - 1, nb_all, D_LRU)]
    v_relay = [pltpu.VMEM((steps * (nb + SUBLANES), LANES), jnp.float32)] * N_LANE_GROUPS
    y, v, hlast, convnew, *cast_bf16 = pl.pallas_call(
        functools.partial(_mixer_sample_kernel, steps=steps, nb=nb, seq_major=seq_major, n_cast=len(cast),
                          layer=l),
        grid=(groups,),
        in_specs=([x_spec,
                   pl.BlockSpec((None, CONV_W - 1, nb, D_LRU), lambda i: (l, 0, i, 0)),
                   pl.BlockSpec((None, nb, D_LRU), lambda i: (l, i, 0))]
                  + [_param_spec(p, k, l) for k in _SAMPLE_MIXER_PARAMS] + cast_in_specs
                  + (state_specs(l) if l else [])),
        out_specs=[act_spec] + state_specs(l + 1) + cast_out_specs,
        out_shape=[jax.ShapeDtypeStruct(s, jnp.float32) for s in out_shapes] + cast_shapes,
        scratch_shapes=[pltpu.VMEM((N_GATE_TILES, GATE_TILE, 2 * GATE_TILE), jnp.bfloat16)] + v_relay + relay,
        compiler_params=pltpu.CompilerParams(dimension_semantics=("arbitrary",), vmem_limit_bytes=VMEM_LIMIT),
        name="mixer_sample",
    )(x, cbuf_all, h0_all, *[p[k] for k in _SAMPLE_MIXER_PARAMS], *cast, *prev)
    return y, (v, hlast, convnew), cast_bf16


CAST_STEPS = 4


def _cast_kernel(*refs):
    n = len(refs) // 2
    for w_f32, w_bf16 in zip(refs[:n], refs[n:]):
        w_bf16[...] = w_f32[...].astype(jnp.bfloat16)


def _cast_layer(weights, l):
    in_specs, out_specs, out_shapes = [], [], []
    for w in weights:
        k_rows, n_cols = w.shape[1:]
        blk = k_rows // CAST_STEPS
        assert blk * CAST_STEPS == k_rows and blk % (2 * SUBLANES) == 0
        in_specs.append(pl.BlockSpec((None, blk, n_cols), lambda j: (l, j, 0)))
        out_specs.append(pl.BlockSpec((None, blk, n_cols), lambda j: (0, j, 0)))
        out_shapes.append(jax.ShapeDtypeStruct((1, k_rows, n_cols), jnp.bfloat16))
    return pl.pallas_call(
        _cast_kernel,
        grid=(CAST_STEPS,),
        in_specs=in_specs,
        out_specs=out_specs,
        out_shape=out_shapes,
        compiler_params=pltpu.CompilerParams(dimension_semantics=("arbitrary",), vmem_limit_bytes=VMEM_LIMIT),
        name="cast_weights",
    )(*weights)


def _prepare_params(steps, nseq, norm_mix_g, w_in, conv_w, conv_b, gate_r_w, gate_r_b, gate_i_w, gate_i_b,
                    lru_lambda, sgu_norm_g, sgu_norm_b, sgu_w, sgu_b, w_out, norm_mlp_g, mlp_w1, mlp_w2, final_norm_g):
    rows = [None] * N_ROWS
    rows[ROW_G_MIX], rows[ROW_G_MLP] = norm_mix_g, norm_mlp_g
    rows[ROW_GATE_RB], rows[ROW_GATE_IB], rows[ROW_LAM], rows[ROW_CONV_B] = gate_r_b, gate_i_b, lru_lambda, conv_b
    rows[ROW_SGU] = jnp.concatenate([sgu_norm_g, sgu_norm_b], axis=-1)
    rows[ROW_G_FINAL] = jnp.broadcast_to(final_norm_g, (DEPTH, D_MODEL))
    return {
        "rows": jnp.stack(rows, axis=1),
        "conv_w": conv_w,
        "gate_rw": gate_r_w,
        "gate_iw": gate_i_w,
        "sgu_w": sgu_w,
        "sgu_bias": jnp.repeat(jnp.transpose(sgu_b, (0, 2, 1)), SGU_HEAD_DIM, axis=2),
        "sgu_w8": jnp.repeat(jnp.transpose(sgu_w[:, :, :steps, :steps], (0, 2, 3, 1)), SGU_HEAD_DIM, axis=3),
        "sgu_b8": jnp.repeat(jnp.transpose(sgu_b[:, :, :steps], (0, 2, 1)), SGU_HEAD_DIM, axis=2),
    }


def kernel(x_prompt, x_sample, state_lru_h, state_conv, norm_mix_g, w_in, conv_w, conv_b, gate_r_w, gate_r_b, gate_i_w, gate_i_b, lru_lambda, sgu_norm_g, sgu_norm_b, sgu_w, sgu_b, w_out, norm_mlp_g, mlp_w1, mlp_w2, final_norm_g):
    nseq, seq, _ = x_prompt.shape
    nb, steps, _ = x_sample.shape
    assert seq % CHUNK == 0 and nseq == SUBLANES and steps == nseq and nb == 2 * HALF
    p = _prepare_params(steps, nseq, norm_mix_g, w_in, conv_w, conv_b, gate_r_w, gate_r_b, gate_i_w, gate_i_b,
                        lru_lambda, sgu_norm_g, sgu_norm_b, sgu_w, sgu_b, w_out, norm_mlp_g, mlp_w1, mlp_w2,
                        final_norm_g)

    xp = x_prompt
    xs = x_sample.reshape(nb * steps, D_MODEL)
    cbuf_tm = jnp.transpose(state_conv, (0, 2, 1, 3))
    hs_p, convs_p, sample_state = [], [], ()
    big_f32 = (w_in, w_out, mlp_w1, mlp_w2)
    big = _cast_layer(big_f32[:2], 0)
    for l in range(DEPTH):
        last = l == DEPTH - 1
        first = l == 0
        p.update(zip(_BIG_WEIGHTS, big))
        xs, sample_state, mlp_bf16 = _mixer_sample(
            xs, state_lru_h, cbuf_tm, p, l, seq_major=first, prev=sample_state,
            groups=SAMPLE_GROUPS_CAST if first else SAMPLE_GROUPS, cast=big_f32[2:] if first else ())
        p.update(zip(_BIG_WEIGHTS[2:], mlp_bf16))
        xp, xs, h_p, conv_p, big = _layer_prompt(xp, xs.reshape(steps * nb, D_MODEL), p, l, final_norm=last,
                                                 sample_seq_major=last, cast_next=() if last else big_f32)
        if not last:
            xs = xs.reshape(nb // HALF, steps, HALF, D_MODEL)
        hs_p.append(h_p)
        convs_p.append(conv_p)

    y_sample = xs.reshape(nb, steps, D_MODEL)
    v_s, h_s, conv_s = sample_state
    return (xp, y_sample, jnp.stack(hs_p), jnp.stack(convs_p), h_s, jnp.transpose(conv_s, (0, 2, 1, 3)), v_s)
```

```python
import functools

import jax
import jax.numpy as jnp
from jax import lax
from jax.experimental import pallas as pl
from jax.experimental.pallas import tpu as pltpu

D_MODEL = 1024
DEPTH = 2
SGU_HEADS = 8
SGU_HEAD_DIM = 64
D_SGU = SGU_HEADS * SGU_HEAD_DIM
CHUNK = 128
D_LRU = 1024
LRU_BLOCKS = 16
LRU_BLOCK_DIM = 64
CONV_W = 4
LRU_C = 8.0
D_FF = 4 * D_MODEL
EPS = 1e-6

SUBLANES = 8
LANES = 128
GATE_TILE = 256
N_GATE_TILES = D_LRU // GATE_TILE
HEADS_PER_LANE_GROUP = LANES // SGU_HEAD_DIM
N_LANE_GROUPS = D_SGU // LANES
N_SLABS = D_LRU // LANES
FF_CHUNK = D_MODEL
N_FF_CHUNKS = D_FF // FF_CHUNK

SAMPLE_GROUPS = 2
SAMPLE_GROUPS_CAST = 4
HALF = CHUNK // 2
PITCH = HALF + SUBLANES
V7X_VMEM_BYTES = 64 * 1024 * 1024
VMEM_HEADROOM = 2 * 1024 * 1024
VMEM_LIMIT = V7X_VMEM_BYTES - VMEM_HEADROOM
F32_TINY = 1.1754944e-38


def _rms_norm(x, g):
    return x * lax.rsqrt(jnp.mean(x * x, axis=-1, keepdims=True) + EPS) * g


def _layer_norm(x, g, b):
    mu = jnp.mean(x, axis=-1, keepdims=True)
    xc = x - mu
    return xc * lax.rsqrt(jnp.mean(xc * xc, axis=-1, keepdims=True) + EPS) * g + b


def _gelu(x):
    return jax.nn.gelu(x, approximate=True)


def _dot(a, b):
    return jnp.dot(a, b, preferred_element_type=jnp.float32)


def _lru_elementwise(r_pre, i_pre, xc, half_rb, half_ib, half_c):
    th_r = jnp.tanh(r_pre + half_rb)
    th_i = jnp.tanh(i_pre + half_ib)
    log_a = half_c * th_r + half_c
    i = 0.5 * th_i + 0.5
    a = jnp.exp(log_a)
    y = jnp.tanh(log_a) * (-1.0 - a * a)
    b = (y * lax.rsqrt(jnp.maximum(y, F32_TINY))) * (i * xc)
    return a, b


def _lru_consts(gate_rb, gate_ib, lam):
    return 0.5 * gate_rb, 0.5 * gate_ib, (0.5 * LRU_C) * jax.nn.log_sigmoid(lam)


def _build_gate_tiles(gate_rw_ref, gate_iw_ref, gate_sc):
    per_tile = GATE_TILE // LRU_BLOCK_DIM
    k_idx = lax.broadcasted_iota(jnp.int32, (LRU_BLOCK_DIM, GATE_TILE), 0)
    n_idx = lax.broadcasted_iota(jnp.int32, (LRU_BLOCK_DIM, GATE_TILE), 1)
    replicate = (n_idx % LRU_BLOCK_DIM == k_idx).astype(jnp.bfloat16)
    row_blk = lax.broadcasted_iota(jnp.int32, (GATE_TILE, GATE_TILE), 0) // LRU_BLOCK_DIM
    col_blk = lax.broadcasted_iota(jnp.int32, (GATE_TILE, GATE_TILE), 1) // LRU_BLOCK_DIM
    for j in range(N_GATE_TILES):
        for k, ref in enumerate((gate_rw_ref, gate_iw_ref)):
            stacked = ref[per_tile * j:per_tile * (j + 1)].reshape(GATE_TILE, LRU_BLOCK_DIM)
            rep = _dot(stacked.astype(jnp.bfloat16), replicate)
            gate_sc[j, :, k * GATE_TILE:(k + 1) * GATE_TILE] = jnp.where(
                row_blk == col_blk, 0.5 * rep, 0.0).astype(jnp.bfloat16)


def _lru_coeffs(xc, gate_w_ref, gate_rb, gate_ib, lam):
    xcb = xc.astype(jnp.bfloat16)
    r_parts, i_parts = [], []
    for j in range(N_GATE_TILES):
        ri = _dot(xcb[:, j * GATE_TILE:(j + 1) * GATE_TILE], gate_w_ref[j])
        r_parts.append(ri[:, :GATE_TILE])
        i_parts.append(ri[:, GATE_TILE:])
    return _lru_elementwise(jnp.concatenate(r_parts, axis=1), jnp.concatenate(i_parts, axis=1), xc,
                            *_lru_consts(gate_rb, gate_ib, lam))


ROW_G_MIX, ROW_G_MLP, ROW_GATE_RB, ROW_GATE_IB, ROW_LAM, ROW_CONV_B, ROW_SGU, ROW_G_FINAL = range(8)
N_ROWS = 8


def _row_views(rows_ref):
    one = lambda k, lo=0, hi=D_MODEL: rows_ref.at[k:k + 1, lo:hi]
    return (one(ROW_G_MIX), one(ROW_G_MLP), one(ROW_GATE_RB), one(ROW_GATE_IB), one(ROW_LAM), one(ROW_CONV_B),
            one(ROW_SGU, 0, D_SGU), one(ROW_SGU, D_SGU, 2 * D_SGU), one(ROW_G_FINAL))


def _mlp_up(xn, w1_ref, c):
    cols = slice(c * FF_CHUNK, (c + 1) * FF_CHUNK)
    return jnp.square(jnp.maximum(_dot(xn, w1_ref[:, cols]), 0.0)).astype(jnp.bfloat16)


def _mlp_down(hid, w2_ref, c):
    return _dot(hid, w2_ref[c * FF_CHUNK:(c + 1) * FF_CHUNK, :])


def _layer_prompt_kernel(*refs, nseq, n_tiles, final_norm, n_cast, sample_seq_major, layer):
    n_in = 2 + len(_PROMPT_PARAMS)
    (x_ref, xs_ref, rows_ref, w_in_ref, conv_w_ref, gate_rw_ref, gate_iw_ref, sgu_w_ref, sgu_bias_ref, w_out_ref,
     w1_ref, w2_ref) = refs[:n_in]
    (g_mix_ref, g_mlp_ref, gate_rb_ref, gate_ib_ref, lam_ref, conv_b_ref, sgu_g_ref, sgu_bn_ref,
     gf_ref) = _row_views(rows_ref)
    cast_in, refs = refs[n_in:n_in + n_cast], refs[n_in + n_cast:]
    n_prev = 2 if layer else 0
    prev_refs, refs = refs[:n_prev], refs[n_prev:]
    y_ref, ys_ref, hlast_ref, convnew_ref = refs[:4]
    cast_out, refs = refs[4:4 + n_cast], refs[4 + n_cast:]
    vprev_sc, sgu_lhs_sc, ymix_sc, gate_sc = refs[:4]
    slab_scratch = refs[4:]
    rows = nseq * HALF
    j = pl.program_id(0)
    slab_sc = slab_scratch[0:N_SLABS]
    tail_sc = slab_scratch[N_SLABS:2 * N_SLABS]
    h_sc = slab_scratch[2 * N_SLABS:3 * N_SLABS]

    @pl.when(j == 0)
    def _():
        for c in range(N_SLABS):
            tail_sc[c][...] = jnp.zeros_like(tail_sc[c])
            h_sc[c][...] = jnp.zeros_like(h_sc[c])
        vprev_sc[...] = jnp.zeros_like(vprev_sc)
        ymix_sc[1] = xs_ref[...]
        _build_gate_tiles(gate_rw_ref, gate_iw_ref, gate_sc)
        t_idx = lax.broadcasted_iota(jnp.int32, (CHUNK, CHUNK), 0)
        s_idx = lax.broadcasted_iota(jnp.int32, (CHUNK, CHUNK), 1)
        for g in range(N_LANE_GROUPS):
            first, second = [], []
            for hh in range(HEADS_PER_LANE_GROUP):
                wm = jnp.where(s_idx <= t_idx, sgu_w_ref[HEADS_PER_LANE_GROUP * g + hh], 0.0)
                first.append(pltpu.roll(wm[:HALF], HALF, axis=1))
                second.append(wm[HALF:])
            sgu_lhs_sc[0, g] = jnp.concatenate(first, axis=0).astype(jnp.bfloat16)
            sgu_lhs_sc[1, g] = jnp.concatenate(second, axis=0).astype(jnp.bfloat16)

    parity = j % 2

    @pl.when(j < n_tiles)
    def _():
        _fused_step(x_ref, g_mix_ref, w_in_ref, conv_w_ref, conv_b_ref, gate_rb_ref, gate_ib_ref, lam_ref, sgu_g_ref, sgu_bn_ref,
                    sgu_bias_ref, w_out_ref, g_mlp_ref, w1_ref, w2_ref, gf_ref, y_ref, vprev_sc, sgu_lhs_sc, ymix_sc,
                    gate_sc, slab_sc, tail_sc, h_sc, parity=parity, nseq=nseq, final_norm=final_norm)
        for w_f32, w_bf16 in zip(cast_in, cast_out):
            w_bf16[...] = w_f32[...].astype(jnp.bfloat16)

    def store_sample(tile):
        if not sample_seq_major:
            ys_ref[...] = tile
            return
        for c in range(N_SLABS):
            for t in range(nseq):
                slab_sc[c][t * PITCH:t * PITCH + HALF, :] = tile[t * HALF:(t + 1) * HALF, c * LANES:(c + 1) * LANES]
        for s in range(HALF):
            for c in range(N_SLABS):
                ys_ref[s * nseq:(s + 1) * nseq, c * LANES:(c + 1) * LANES] = (
                    slab_sc[c][pl.ds(s, nseq, stride=PITCH), :])

    @pl.when(j == 0)
    def _():
        store_sample(y_ref[...].reshape(rows, D_MODEL))

    @pl.when(j == n_tiles - 1)
    def _():
        for c in range(N_SLABS):
            cols = slice(c * LANES, (c + 1) * LANES)
            hlast_ref[layer, :, cols] = h_sc[c][...]
            for k in range(CONV_W - 1):
                convnew_ref[layer, CONV_W - 2 - k, :, cols] = tail_sc[c][parity, k]
        for dst, prev in zip((hlast_ref, convnew_ref), prev_refs):
            dst[0:layer] = prev[...]

    @pl.when(j == n_tiles + 1)
    def _():
        ymix_sc[1 - parity] = xs_ref[...]

    @pl.when(j >= n_tiles)
    def _():
        xm = ymix_sc[1 - parity]
        xn = _rms_norm(xm, g_mlp_ref[...]).astype(jnp.bfloat16)
        acc = xm
        for c in range(N_FF_CHUNKS):
            acc = acc + _mlp_down(_mlp_up(xn, w1_ref, c), w2_ref, c)
        ymix_sc[parity] = _rms_norm(acc, gf_ref[...]) if final_norm else acc

    @pl.when(j == n_tiles)
    def _():
        y_ref[...] = ymix_sc[parity].reshape(nseq, HALF, D_MODEL)

    @pl.when(j == n_tiles + 1)
    def _():
        store_sample(ymix_sc[parity])


def _fused_step(x_ref, g_mix_ref, w_in_ref, conv_w_ref, conv_b_ref, gate_rb_ref, gate_ib_ref, lam_ref, sgu_g_ref, sgu_bn_ref,
                sgu_bias_ref, w_out_ref, g_mlp_ref, w1_ref, w2_ref, gf_ref, y_ref, vprev_sc, sgu_lhs_sc, ymix_sc,
                gate_sc, slab_sc, tail_sc, h_sc, *, parity, nseq, final_norm):
    rows = nseq * HALF
    half_rb, half_ib, half_c = _lru_consts(gate_rb_ref[...], gate_ib_ref[...], lam_ref[...])
    xb_col0 = 2 * D_SGU
    yb_col0 = 2 * D_SGU + D_LRU
    slabs_per_tile = GATE_TILE // LANES
    st = [{} for _ in range(N_GATE_TILES)]
    sg = {}
    ml = {}

    def mlp_norm():
        xm = ymix_sc[1 - parity]
        ml["acc"] = xm
        ml["xn"] = _rms_norm(xm, g_mlp_ref[...]).astype(jnp.bfloat16)

    def mlp_up(c):
        ml["hid", c] = _mlp_up(ml["xn"], w1_ref, c)

    def mlp_down(c):
        ml["acc"] = ml["acc"] + _mlp_down(ml.pop(("hid", c)), w2_ref, c)

    def mlp_store():
        out = _rms_norm(ml["acc"], gf_ref[...]) if final_norm else ml["acc"]
        y_ref[...] = out.reshape(nseq, HALF, D_MODEL)

    def mix_norm():
        sg["x"] = x_ref[...].reshape(rows, D_MODEL)
        sg["xn"] = _rms_norm(sg["x"], g_mix_ref[...]).astype(jnp.bfloat16)

    def lru_proj(q):
        xb = _dot(sg["xn"], w_in_ref[:, xb_col0 + q * GATE_TILE:xb_col0 + (q + 1) * GATE_TILE])
        for i in range(slabs_per_tile):
            for s in range(nseq):
                slab_sc[slabs_per_tile * q + i][s * PITCH:s * PITCH + HALF, :] = (
                    xb[s * HALF:(s + 1) * HALF, i * LANES:(i + 1) * LANES])

    def lru_conv(q):
        xc_slabs = []
        for c in range(slabs_per_tile * q, slabs_per_tile * (q + 1)):
            lanes = slice(c * LANES, (c + 1) * LANES)
            taps = [jnp.broadcast_to(conv_w_ref[k:k + 1, lanes], (nseq, LANES)) for k in range(CONV_W)]
            taps.append(jnp.broadcast_to(conv_b_ref[:, lanes], (nseq, LANES)))
            p1, p2, p3 = (tail_sc[c][1 - parity, k] for k in range(CONV_W - 1))
            steps_out = []
            for t in range(HALF):
                cur = slab_sc[c][pl.ds(t, nseq, stride=PITCH), :]
                steps_out.append(taps[CONV_W] + taps[3] * cur + taps[2] * p1 + taps[1] * p2 + taps[0] * p3)
                p1, p2, p3 = cur, p1, p2
            for k, pk in enumerate((p1, p2, p3)):
                tail_sc[c][parity, k] = pk
            xc_slabs.append(jnp.concatenate(steps_out, axis=0))
        st[q]["xc"] = jnp.concatenate(xc_slabs, axis=1)

    def lru_gates(q):
        st[q]["ri"] = _dot(st[q]["xc"].astype(jnp.bfloat16), gate_sc[q])

    def lru_coeffs(q):
        cols = slice(q * GATE_TILE, (q + 1) * GATE_TILE)
        ri = st[q].pop("ri")
        st[q]["ab"] = _lru_elementwise(ri[:, :GATE_TILE], ri[:, GATE_TILE:], st[q].pop("xc"),
                                       half_rb[:, cols], half_ib[:, cols], half_c[:, cols])

    def lru_scan(q):
        a, b = st[q].pop("ab")
        for i in range(slabs_per_tile):
            c = slabs_per_tile * q + i
            h = h_sc[c][...]
            for t in range(HALF):
                h = (a[t * nseq:(t + 1) * nseq, i * LANES:(i + 1) * LANES] * h
                     + b[t * nseq:(t + 1) * nseq, i * LANES:(i + 1) * LANES])
                slab_sc[c][pl.ds(t, nseq, stride=PITCH), :] = h
            h_sc[c][...] = h

    def lru_out(q):
        hs = jnp.concatenate(
            [jnp.concatenate([slab_sc[c][s * PITCH:s * PITCH + HALF, :]
                              for c in range(slabs_per_tile * q, slabs_per_tile * (q + 1))], axis=1)
             for s in range(nseq)], axis=0)
        yb = _dot(sg["xn"], w_in_ref[:, yb_col0 + q * GATE_TILE:yb_col0 + (q + 1) * GATE_TILE])
        st[q]["out_b"] = (hs * _gelu(yb)).astype(jnp.bfloat16)

    def lru_mix(q):
        r0 = D_SGU + q * GATE_TILE
        sg["acc"] = sg["acc"] + _dot(st[q].pop("out_b"), w_out_ref[r0:r0 + GATE_TILE, :])

    def sgu_u():
        sg["u"] = _gelu(_dot(sg["xn"], w_in_ref[:, 0:D_SGU]))

    def sgu_v():
        v = _layer_norm(_gelu(_dot(sg["xn"], w_in_ref[:, D_SGU:2 * D_SGU])), sgu_g_ref[...], sgu_bn_ref[...])
        sg["vb"] = v.astype(jnp.bfloat16)

    def sgu_gate():
        vb = sg.pop("vb")
        lane = lax.broadcasted_iota(jnp.int32, (HALF, LANES), 1)
        bias = sgu_bias_ref[pl.ds(pl.multiple_of(parity * HALF, HALF), HALF), :]
        v_full = [jnp.concatenate([vprev_sc[1 - parity, s], vb[s * HALF:(s + 1) * HALF]], axis=0)
                  for s in range(nseq)]
        vprev_sc[parity] = vb.reshape(nseq, HALF, D_SGU)
        gate_rows = [[] for _ in range(nseq)]
        for s in range(0, nseq, 2):
            for g in range(N_LANE_GROUPS):
                lanes = slice(g * LANES, (g + 1) * LANES)
                rhs = jnp.concatenate([v_full[s][:, lanes], v_full[s + 1][:, lanes]], axis=1)
                res = _dot(sgu_lhs_sc[parity, g], rhs)
                for i in range(2):
                    part = res[:, i * LANES:(i + 1) * LANES]
                    gate_rows[s + i].append(jnp.where(lane < SGU_HEAD_DIM, part[:HALF], part[HALF:]))
        gate = jnp.concatenate([jnp.concatenate(r, axis=1) + bias for r in gate_rows], axis=0)
        sg["out_a"] = (sg.pop("u") * gate).astype(jnp.bfloat16)

    def sgu_mix():
        sg["acc"] = sg["x"] + _dot(sg.pop("out_a"), w_out_ref[0:D_SGU, :])

    def mix_store():
        ymix_sc[parity] = sg["acc"]

    tiles = range(N_GATE_TILES)
    mlp_norm()
    mix_norm()
    for q in tiles:
        lru_proj(q)
    mlp_up(0)
    for q in tiles:
        lru_conv(q)
        lru_gates(q)
    mlp_up(1)
    sgu_u()
    sgu_v()
    for q in tiles:
        lru_coeffs(q)
        lru_scan(q)
    mlp_down(0)
    mlp_down(1)
    for q in tiles:
        lru_out(q)
    mlp_up(2)
    mlp_up(3)
    sgu_gate()
    sgu_mix()
    mlp_down(2)
    mlp_down(3)
    for q in tiles:
        lru_mix(q)
    mix_store()
    mlp_store()


def _const_spec(shape):
    nd = len(shape)
    return pl.BlockSpec(shape, lambda *_: (0,) * nd, pipeline_mode=pl.Buffered(1))


def _layer_spec(arr, l):
    nd = arr.ndim - 1
    return pl.BlockSpec((None,) + arr.shape[1:], lambda *_: (l,) + (0,) * nd, pipeline_mode=pl.Buffered(1))


_BIG_WEIGHTS = ("w_in", "w_out", "w1", "w2")
_PROMPT_PARAMS = ("rows", "w_in", "conv_w", "gate_rw", "gate_iw", "sgu_w", "sgu_bias", "w_out", "w1", "w2")
_SAMPLE_MIXER_PARAMS = ("rows", "w_in", "conv_w", "gate_rw", "gate_iw", "sgu_w8", "sgu_b8", "w_out")


def _param_spec(p, k, l):
    return _layer_spec(p[k], 0 if k in _BIG_WEIGHTS else l)


def _layer_prompt(x, xs, p, l, *, final_norm, sample_seq_major, cast_next=(), prev=()):
    assert len(prev) == (2 if l else 0)
    nseq, seq, _ = x.shape
    n_tiles = seq // HALF
    rows = nseq * HALF
    assert xs.shape == (2 * rows, D_MODEL)
    consts = [p[k] for k in _PROMPT_PARAMS]
    cast_in_specs, cast_out_specs, cast_shapes = [], [], []
    for w in cast_next:
        k_rows, n_cols = w.shape[1:]
        blk = k_rows // n_tiles
        assert blk * n_tiles == k_rows and blk % (2 * SUBLANES) == 0
        cast_in_specs.append(pl.BlockSpec((None, blk, n_cols), lambda j: (l + 1, jnp.minimum(j, n_tiles - 1), 0)))
        cast_out_specs.append(pl.BlockSpec((None, blk, n_cols), lambda j: (0, jnp.minimum(j, n_tiles - 1), 0)))
        cast_shapes.append(jax.ShapeDtypeStruct((1, k_rows, n_cols), jnp.bfloat16))
    x_spec = pl.BlockSpec((nseq, HALF, D_MODEL), lambda j: (0, jnp.minimum(j, n_tiles - 1), 0))
    y_spec = pl.BlockSpec((nseq, HALF, D_MODEL), lambda j: (0, jnp.clip(j - 1, 0, n_tiles - 1), 0))
    xs_spec = pl.BlockSpec((rows, D_MODEL), lambda j: (jnp.where(j > n_tiles, 1, 0), 0),
                           pipeline_mode=pl.Buffered(1))
    y, ys, hlast, convnew, *cast = pl.pallas_call(
        functools.partial(_layer_prompt_kernel, nseq=nseq, n_tiles=n_tiles, final_norm=final_norm,
                          n_cast=len(cast_next), sample_seq_major=sample_seq_major, layer=l),
        grid=(n_tiles + 2,),
        in_specs=([x_spec, xs_spec] + [_param_spec(p, k, l) for k in _PROMPT_PARAMS] + cast_in_specs
                  + [_const_spec(a.shape) for a in prev]),
        out_specs=([y_spec, xs_spec, _const_spec((l + 1, nseq, D_LRU)),
                    _const_spec((l + 1, CONV_W - 1, nseq, D_LRU))] + cast_out_specs),
        out_shape=[jax.ShapeDtypeStruct(x.shape, jnp.float32),
                   jax.ShapeDtypeStruct(xs.shape, jnp.float32),
                   jax.ShapeDtypeStruct((l + 1, nseq, D_LRU), jnp.float32),
                   jax.ShapeDtypeStruct((l + 1, CONV_W - 1, nseq, D_LRU), jnp.float32)] + cast_shapes,
        scratch_shapes=([pltpu.VMEM((2, nseq, HALF, D_SGU), jnp.bfloat16),
                         pltpu.VMEM((2, N_LANE_GROUPS, CHUNK, CHUNK), jnp.bfloat16),
                         pltpu.VMEM((2, rows, D_MODEL), jnp.float32),
                         pltpu.VMEM((N_GATE_TILES, GATE_TILE, 2 * GATE_TILE), jnp.bfloat16)]
                        + [pltpu.VMEM((nseq * PITCH, LANES), jnp.float32)] * N_SLABS
                        + [pltpu.VMEM((2, CONV_W - 1, nseq, LANES), jnp.float32)] * N_SLABS
                        + [pltpu.VMEM((nseq, LANES), jnp.float32)] * N_SLABS),
        compiler_params=pltpu.CompilerParams(dimension_semantics=("arbitrary",), vmem_limit_bytes=VMEM_LIMIT),
        name="layer_prompt",
    )(x, xs, *consts, *cast_next, *prev)
    return y, ys, (hlast, convnew), cast


def _mixer_sample_kernel(*refs, steps, nb, seq_major, n_cast, layer):
    n_in = 3 + len(_SAMPLE_MIXER_PARAMS)
    (x_ref, cbuf_ref, h0_ref, rows_ref, w_in_ref, conv_w_ref, gate_rw_ref, gate_iw_ref, sgu_w8_ref, sgu_b8_ref,
     w_out_ref) = refs[:n_in]
    cast_in, refs = refs[n_in:n_in + n_cast], refs[n_in + n_cast:]
    n_prev = 3 if layer else 0
    prev_refs, refs = refs[:n_prev], refs[n_prev:]
    y_ref, v_ref, hlast_ref, convnew_ref = refs[:4]
    cast_out, refs = refs[4:4 + n_cast], refs[4 + n_cast:]
    gate_sc, v_relay_sc, relay_sc = refs[0], refs[1:1 + N_LANE_GROUPS], refs[1 + N_LANE_GROUPS:]
    for dst, prev in zip((v_ref, hlast_ref, convnew_ref), prev_refs):
        dst[0:layer] = prev[...]
    for w_f32, w_bf16 in zip(cast_in, cast_out):
        w_bf16[...] = w_f32[...].astype(jnp.bfloat16)
    (g_mix_ref, _, gate_rb_ref, gate_ib_ref, lam_ref, conv_b_ref, sgu_g_ref, sgu_bn_ref, _) = _row_views(rows_ref)

    @pl.when(pl.program_id(0) == 0)
    def _():
        _build_gate_tiles(gate_rw_ref, gate_iw_ref, gate_sc)

    if seq_major:
        for c in range(D_MODEL // LANES):
            relay_sc[c][...] = x_ref[:, c * LANES:(c + 1) * LANES]
        x = jnp.concatenate(
            [jnp.concatenate([relay_sc[c][pl.ds(t, nb, stride=steps), :] for c in range(D_MODEL // LANES)], axis=1)
             for t in range(steps)], axis=0)
    else:
        x = x_ref[...].reshape(steps * nb, D_MODEL)
    xn = _rms_norm(x, g_mix_ref[...]).astype(jnp.bfloat16)

    u = _gelu(_dot(xn, w_in_ref[:, 0:D_SGU]))
    v = _layer_norm(_gelu(_dot(xn, w_in_ref[:, D_SGU:2 * D_SGU])), sgu_g_ref[...], sgu_bn_ref[...])
    pitch = nb + SUBLANES
    for g in range(N_LANE_GROUPS):
        for t in range(steps):
            v_relay_sc[g][t * pitch:t * pitch + nb, :] = v[t * nb:(t + 1) * nb, g * LANES:(g + 1) * LANES]
    for s in range(nb):
        for g in range(N_LANE_GROUPS):
            v_ref[layer, s, :, g * LANES:(g + 1) * LANES] = v_relay_sc[g][pl.ds(s, steps, stride=pitch), :]
    gate_rows = []
    for t in range(steps):
        acc = sgu_b8_ref[t:t + 1, :] + sgu_w8_ref[t, 0:1, :] * v[0:nb]
        for s in range(1, t + 1):
            acc = acc + sgu_w8_ref[t, s:s + 1, :] * v[s * nb:(s + 1) * nb]
        gate_rows.append(acc)
    out_a = (u * jnp.concatenate(gate_rows, axis=0)).astype(jnp.bfloat16)

    xb = _dot(xn, w_in_ref[:, 2 * D_SGU:2 * D_SGU + D_LRU])
    xp = jnp.concatenate([cbuf_ref[k] for k in range(CONV_W - 1)] + [xb], axis=0)
    xc = conv_b_ref[...] + conv_w_ref[0:1, :] * xp[0:steps * nb]
    for k in range(1, CONV_W):
        xc = xc + conv_w_ref[k:k + 1, :] * xp[k * nb:(k + steps) * nb]
    convnew_ref[layer] = xp[steps * nb:].reshape(CONV_W - 1, nb, D_LRU)

    a, b = _lru_coeffs(xc, gate_sc, gate_rb_ref[...], gate_ib_ref[...], lam_ref[...])
    h = h0_ref[...]
    hs = []
    for t in range(steps):
        h = a[t * nb:(t + 1) * nb] * h + b[t * nb:(t + 1) * nb]
        hs.append(h)
    hlast_ref[layer] = h

    yb = _dot(xn, w_in_ref[:, 2 * D_SGU + D_LRU:])
    out_b = (jnp.concatenate(hs, axis=0) * _gelu(yb)).astype(jnp.bfloat16)
    out = _dot(jnp.concatenate([out_a, out_b], axis=1), w_out_ref[...])
    y_ref[...] = (x + out).reshape(steps, nb, D_MODEL)


def _mixer_sample(x, h0_all, cbuf_all, p, l, *, seq_major, groups, cast=(), prev=()):
    assert len(prev) == (3 if l else 0)
    nb_all = h0_all.shape[1]
    steps = x.shape[0] // nb_all if seq_major else x.shape[1]
    nb = nb_all // groups
    cast_in_specs, cast_out_specs, cast_shapes = [], [], []
    for w in cast:
        k_rows, n_cols = w.shape[1:]
        blk = k_rows // groups
        assert blk * groups == k_rows and blk % (2 * SUBLANES) == 0
        cast_in_specs.append(pl.BlockSpec((None, blk, n_cols), lambda i: (l, i, 0)))
        cast_out_specs.append(pl.BlockSpec((None, blk, n_cols), lambda i: (0, i, 0)))
        cast_shapes.append(jax.ShapeDtypeStruct((1, k_rows, n_cols), jnp.bfloat16))
    per_tile = HALF // nb
    assert per_tile * nb == HALF and nb_all % HALF == 0
    act_spec = pl.BlockSpec((None, steps, nb, D_MODEL), lambda i: (i // per_tile, 0, i % per_tile, 0))
    if seq_major:
        x_spec = pl.BlockSpec((nb * steps, D_MODEL), lambda i: (i, 0))
        relay = [pltpu.VMEM((nb * steps, LANES), jnp.float32)] * (D_MODEL // LANES)
    else:
        x_spec = act_spec
        relay = []
    state_specs = lambda layers: [pl.BlockSpec((layers, nb, steps, D_SGU), lambda i: (0, i, 0, 0)),
                                  pl.BlockSpec((layers, nb, D_LRU), lambda i: (0, i, 0)),
                                  pl.BlockSpec((layers, CONV_W - 1, nb, D_LRU), lambda i: (0, 0, i, 0))]
    out_shapes = [(nb_all // HALF, steps, HALF, D_MODEL), (l + 1, nb_all, steps, D_SGU), (l + 1, nb_all, D_LRU),
                  (l + 1, CONV_W - 1, nb_all, D_LRU)]
    v_relay = [pltpu.VMEM((steps * (nb + SUBLANES), LANES), jnp.float32)] * N_LANE_GROUPS
    y, v, hlast, convnew, *cast_bf16 = pl.pallas_call(
        functools.partial(_mixer_sample_kernel, steps=steps, nb=nb, seq_major=seq_major, n_cast=len(cast),
                          layer=l),
        grid=(groups,),
        in_specs=([x_spec,
                   pl.BlockSpec((None, CONV_W - 1, nb, D_LRU), lambda i: (l, 0, i, 0)),
                   pl.BlockSpec((None, nb, D_LRU), lambda i: (l, i, 0))]
                  + [_param_spec(p, k, l) for k in _SAMPLE_MIXER_PARAMS] + cast_in_specs
                  + (state_specs(l) if l else [])),
        out_specs=[act_spec] + state_specs(l + 1) + cast_out_specs,
        out_shape=[jax.ShapeDtypeStruct(s, jnp.float32) for s in out_shapes] + cast_shapes,
        scratch_shapes=[pltpu.VMEM((N_GATE_TILES, GATE_TILE, 2 * GATE_TILE), jnp.bfloat16)] + v_relay + relay,
        compiler_params=pltpu.CompilerParams(dimension_semantics=("arbitrary",), vmem_limit_bytes=VMEM_LIMIT),
        name="mixer_sample",
    )(x, cbuf_all, h0_all, *[p[k] for k in _SAMPLE_MIXER_PARAMS], *cast, *prev)
    return y, (v, hlast, convnew), cast_bf16


CAST_STEPS = 4


def _cast_kernel(*refs):
    n = len(refs) // 2
    for w_f32, w_bf16 in zip(refs[:n], refs[n:]):
        w_bf16[...] = w_f32[...].astype(jnp.bfloat16)


def _cast_layer(weights, l):
    in_specs, out_specs, out_shapes = [], [], []
    for w in weights:
        k_rows, n_cols = w.shape[1:]
        blk = k_rows // CAST_STEPS
        assert blk * CAST_STEPS == k_rows and blk % (2 * SUBLANES) == 0
        in_specs.append(pl.BlockSpec((None, blk, n_cols), lambda j: (l, j, 0)))
        out_specs.append(pl.BlockSpec((None, blk, n_cols), lambda j: (0, j, 0)))
        out_shapes.append(jax.ShapeDtypeStruct((1, k_rows, n_cols), jnp.bfloat16))
    return pl.pallas_call(
        _cast_kernel,
        grid=(CAST_STEPS,),
        in_specs=in_specs,
        out_specs=out_specs,
        out_shape=out_shapes,
        compiler_params=pltpu.CompilerParams(dimension_semantics=("arbitrary",), vmem_limit_bytes=VMEM_LIMIT),
        name="cast_weights",
    )(*weights)


def _prepare_params(steps, nseq, norm_mix_g, w_in, conv_w, conv_b, gate_r_w, gate_r_b, gate_i_w, gate_i_b,
                    lru_lambda, sgu_norm_g, sgu_norm_b, sgu_w, sgu_b, w_out, norm_mlp_g, mlp_w1, mlp_w2, final_norm_g):
    rows = [None] * N_ROWS
    rows[ROW_G_MIX], rows[ROW_G_MLP] = norm_mix_g, norm_mlp_g
    rows[ROW_GATE_RB], rows[ROW_GATE_IB], rows[ROW_LAM], rows[ROW_CONV_B] = gate_r_b, gate_i_b, lru_lambda, conv_b
    rows[ROW_SGU] = jnp.concatenate([sgu_norm_g, sgu_norm_b], axis=-1)
    rows[ROW_G_FINAL] = jnp.broadcast_to(final_norm_g, (DEPTH, D_MODEL))
    k_idx = lax.broadcasted_iota(jnp.int32, (DEPTH, N_ROWS, D_MODEL), 1)
    packed = jnp.zeros((DEPTH, N_ROWS, D_MODEL), jnp.float32)
    for k, r in enumerate(rows):
        packed = jnp.where(k_idx == k, r[:, None, :], packed)
    return {
        "rows": packed,
        "conv_w": conv_w,
        "gate_rw": gate_r_w,
        "gate_iw": gate_i_w,
        "sgu_w": sgu_w,
        "sgu_bias": jnp.repeat(jnp.transpose(sgu_b, (0, 2, 1)), SGU_HEAD_DIM, axis=2),
        "sgu_w8": jnp.repeat(jnp.transpose(sgu_w[:, :, :steps, :steps], (0, 2, 3, 1)), SGU_HEAD_DIM, axis=3),
        "sgu_b8": jnp.repeat(jnp.transpose(sgu_b[:, :, :steps], (0, 2, 1)), SGU_HEAD_DIM, axis=2),
    }


def kernel(x_prompt, x_sample, state_lru_h, state_conv, norm_mix_g, w_in, conv_w, conv_b, gate_r_w, gate_r_b, gate_i_w, gate_i_b, lru_lambda, sgu_norm_g, sgu_norm_b, sgu_w, sgu_b, w_out, norm_mlp_g, mlp_w1, mlp_w2, final_norm_g):
    nseq, seq, _ = x_prompt.shape
    nb, steps, _ = x_sample.shape
    assert seq % CHUNK == 0 and nseq == SUBLANES and steps == nseq and nb == 2 * HALF
    p = _prepare_params(steps, nseq, norm_mix_g, w_in, conv_w, conv_b, gate_r_w, gate_r_b, gate_i_w, gate_i_b,
                        lru_lambda, sgu_norm_g, sgu_norm_b, sgu_w, sgu_b, w_out, norm_mlp_g, mlp_w1, mlp_w2,
                        final_norm_g)

    xp = x_prompt
    xs = x_sample.reshape(nb * steps, D_MODEL)
    cbuf_tm = jnp.transpose(state_conv, (0, 2, 1, 3))
    prompt_state, sample_state = (), ()
    big_f32 = (w_in, w_out, mlp_w1, mlp_w2)
    big = _cast_layer(big_f32[:2], 0)
    for l in range(DEPTH):
        last = l == DEPTH - 1
        first = l == 0
        p.update(zip(_BIG_WEIGHTS, big))
        xs, sample_state, mlp_bf16 = _mixer_sample(
            xs, state_lru_h, cbuf_tm, p, l, seq_major=first, prev=sample_state,
            groups=SAMPLE_GROUPS_CAST if first else SAMPLE_GROUPS, cast=big_f32[2:] if first else ())
        p.update(zip(_BIG_WEIGHTS[2:], mlp_bf16))
        xp, xs, prompt_state, big = _layer_prompt(xp, xs.reshape(steps * nb, D_MODEL), p, l, final_norm=last,
                                                  sample_seq_major=last, cast_next=() if last else big_f32,
                                                  prev=prompt_state)
        if not last:
            xs = xs.reshape(nb // HALF, steps, HALF, D_MODEL)

    y_sample = xs.reshape(nb, steps, D_MODEL)
    h_p, conv_p = prompt_state
    v_s, h_s, conv_s = sample_state
    return (xp, y_sample, h_p, jnp.transpose(conv_p, (0, 2, 1, 3)), h_s, jnp.transpose(conv_s, (0, 2, 1, 3)), v_s)
```

```python
import functools

import jax
import jax.numpy as jnp
from jax import lax
from jax.experimental import pallas as pl
from jax.experimental.pallas import tpu as pltpu

D_MODEL = 1024
DEPTH = 2
SGU_HEADS = 8
SGU_HEAD_DIM = 64
D_SGU = SGU_HEADS * SGU_HEAD_DIM
CHUNK = 128
D_LRU = 1024
LRU_BLOCKS = 16
LRU_BLOCK_DIM = 64
CONV_W = 4
LRU_C = 8.0
D_FF = 4 * D_MODEL
EPS = 1e-6

SUBLANES = 8
LANES = 128
GATE_TILE = 256
N_GATE_TILES = D_LRU // GATE_TILE
HEADS_PER_LANE_GROUP = LANES // SGU_HEAD_DIM
N_LANE_GROUPS = D_SGU // LANES
N_SLABS = D_LRU // LANES
FF_CHUNK = D_MODEL
N_FF_CHUNKS = D_FF // FF_CHUNK

SAMPLE_GROUPS = 2
SAMPLE_GROUPS_CAST = 4
HALF = CHUNK // 2
PITCH = HALF + SUBLANES
V7X_VMEM_BYTES = 64 * 1024 * 1024
VMEM_HEADROOM = 2 * 1024 * 1024
VMEM_LIMIT = V7X_VMEM_BYTES - VMEM_HEADROOM
F32_TINY = 1.1754944e-38


def _rms_norm(x, g):
    return x * lax.rsqrt(jnp.mean(x * x, axis=-1, keepdims=True) + EPS) * g


def _layer_norm(x, g, b):
    mu = jnp.mean(x, axis=-1, keepdims=True)
    xc = x - mu
    return xc * lax.rsqrt(jnp.mean(xc * xc, axis=-1, keepdims=True) + EPS) * g + b


def _gelu(x):
    return jax.nn.gelu(x, approximate=True)


def _dot(a, b):
    return jnp.dot(a, b, preferred_element_type=jnp.float32)


def _lru_elementwise(r_pre, i_pre, xc, half_rb, half_ib, half_c):
    th_r = jnp.tanh(r_pre + half_rb)
    th_i = jnp.tanh(i_pre + half_ib)
    log_a = half_c * th_r + half_c
    i = 0.5 * th_i + 0.5
    a = jnp.exp(log_a)
    y = jnp.tanh(log_a) * (-1.0 - a * a)
    b = (y * lax.rsqrt(jnp.maximum(y, F32_TINY))) * (i * xc)
    return a, b


def _lru_consts(gate_rb, gate_ib, lam):
    return 0.5 * gate_rb, 0.5 * gate_ib, (0.5 * LRU_C) * jax.nn.log_sigmoid(lam)


def _build_gate_tiles(gate_rw_ref, gate_iw_ref, gate_sc):
    per_tile = GATE_TILE // LRU_BLOCK_DIM
    k_idx = lax.broadcasted_iota(jnp.int32, (LRU_BLOCK_DIM, GATE_TILE), 0)
    n_idx = lax.broadcasted_iota(jnp.int32, (LRU_BLOCK_DIM, GATE_TILE), 1)
    replicate = (n_idx % LRU_BLOCK_DIM == k_idx).astype(jnp.bfloat16)
    row_blk = lax.broadcasted_iota(jnp.int32, (GATE_TILE, GATE_TILE), 0) // LRU_BLOCK_DIM
    col_blk = lax.broadcasted_iota(jnp.int32, (GATE_TILE, GATE_TILE), 1) // LRU_BLOCK_DIM
    for j in range(N_GATE_TILES):
        for k, ref in enumerate((gate_rw_ref, gate_iw_ref)):
            stacked = ref[per_tile * j:per_tile * (j + 1)].reshape(GATE_TILE, LRU_BLOCK_DIM)
            rep = _dot(stacked.astype(jnp.bfloat16), replicate)
            gate_sc[j, :, k * GATE_TILE:(k + 1) * GATE_TILE] = jnp.where(
                row_blk == col_blk, 0.5 * rep, 0.0).astype(jnp.bfloat16)


def _lru_coeffs(xc, gate_w_ref, gate_rb, gate_ib, lam):
    xcb = xc.astype(jnp.bfloat16)
    r_parts, i_parts = [], []
    for j in range(N_GATE_TILES):
        ri = _dot(xcb[:, j * GATE_TILE:(j + 1) * GATE_TILE], gate_w_ref[j])
        r_parts.append(ri[:, :GATE_TILE])
        i_parts.append(ri[:, GATE_TILE:])
    return _lru_elementwise(jnp.concatenate(r_parts, axis=1), jnp.concatenate(i_parts, axis=1), xc,
                            *_lru_consts(gate_rb, gate_ib, lam))


ROW_G_MIX, ROW_G_MLP, ROW_GATE_RB, ROW_GATE_IB, ROW_LAM, ROW_CONV_B, ROW_SGU, ROW_G_FINAL = range(8)
N_ROWS = 8


def _row_views(rows_ref):
    one = lambda k, lo=0, hi=D_MODEL: rows_ref.at[k:k + 1, lo:hi]
    return (one(ROW_G_MIX), one(ROW_G_MLP), one(ROW_GATE_RB), one(ROW_GATE_IB), one(ROW_LAM), one(ROW_CONV_B),
            one(ROW_SGU, 0, D_SGU), one(ROW_SGU, D_SGU, 2 * D_SGU), one(ROW_G_FINAL))


def _mlp_up(xn, w1_ref, c):
    cols = slice(c * FF_CHUNK, (c + 1) * FF_CHUNK)
    return jnp.square(jnp.maximum(_dot(xn, w1_ref[:, cols]), 0.0)).astype(jnp.bfloat16)


def _mlp_down(hid, w2_ref, c):
    return _dot(hid, w2_ref[c * FF_CHUNK:(c + 1) * FF_CHUNK, :])


def _layer_prompt_kernel(*refs, nseq, n_tiles, final_norm, n_cast, sample_seq_major, layer):
    n_in = 2 + len(_PROMPT_PARAMS)
    (x_ref, xs_ref, rows_ref, w_in_ref, conv_w_ref, gate_rw_ref, gate_iw_ref, sgu_w_ref, sgu_bias_ref, w_out_ref,
     w1_ref, w2_ref) = refs[:n_in]
    (g_mix_ref, g_mlp_ref, gate_rb_ref, gate_ib_ref, lam_ref, conv_b_ref, sgu_g_ref, sgu_bn_ref,
     gf_ref) = _row_views(rows_ref)
    cast_in, refs = refs[n_in:n_in + n_cast], refs[n_in + n_cast:]
    n_prev = 2 if layer else 0
    prev_refs, refs = refs[:n_prev], refs[n_prev:]
    y_ref, ys_ref, hlast_ref, convnew_ref = refs[:4]
    cast_out, refs = refs[4:4 + n_cast], refs[4 + n_cast:]
    vprev_sc, sgu_lhs_sc, ymix_sc, gate_sc = refs[:4]
    slab_scratch = refs[4:]
    rows = nseq * HALF
    j = pl.program_id(0)
    slab_sc = slab_scratch[0:N_SLABS]
    tail_sc = slab_scratch[N_SLABS:2 * N_SLABS]
    h_sc = slab_scratch[2 * N_SLABS:3 * N_SLABS]

    @pl.when(j == 0)
    def _():
        for c in range(N_SLABS):
            tail_sc[c][...] = jnp.zeros_like(tail_sc[c])
            h_sc[c][...] = jnp.zeros_like(h_sc[c])
        vprev_sc[...] = jnp.zeros_like(vprev_sc)
        ymix_sc[1] = xs_ref[...]
        _build_gate_tiles(gate_rw_ref, gate_iw_ref, gate_sc)
        t_idx = lax.broadcasted_iota(jnp.int32, (CHUNK, CHUNK), 0)
        s_idx = lax.broadcasted_iota(jnp.int32, (CHUNK, CHUNK), 1)
        for g in range(N_LANE_GROUPS):
            first, second = [], []
            for hh in range(HEADS_PER_LANE_GROUP):
                wm = jnp.where(s_idx <= t_idx, sgu_w_ref[HEADS_PER_LANE_GROUP * g + hh], 0.0)
                first.append(pltpu.roll(wm[:HALF], HALF, axis=1))
                second.append(wm[HALF:])
            sgu_lhs_sc[0, g] = jnp.concatenate(first, axis=0).astype(jnp.bfloat16)
            sgu_lhs_sc[1, g] = jnp.concatenate(second, axis=0).astype(jnp.bfloat16)

    parity = j % 2

    @pl.when(j < n_tiles)
    def _():
        _fused_step(x_ref, g_mix_ref, w_in_ref, conv_w_ref, conv_b_ref, gate_rb_ref, gate_ib_ref, lam_ref, sgu_g_ref, sgu_bn_ref,
                    sgu_bias_ref, w_out_ref, g_mlp_ref, w1_ref, w2_ref, gf_ref, y_ref, vprev_sc, sgu_lhs_sc, ymix_sc,
                    gate_sc, slab_sc, tail_sc, h_sc, parity=parity, nseq=nseq, final_norm=final_norm)
        for w_f32, w_bf16 in zip(cast_in, cast_out):
            w_bf16[...] = w_f32[...].astype(jnp.bfloat16)

    def store_sample(tile):
        if not sample_seq_major:
            ys_ref[...] = tile
            return
        for c in range(N_SLABS):
            for t in range(nseq):
                slab_sc[c][t * PITCH:t * PITCH + HALF, :] = tile[t * HALF:(t + 1) * HALF, c * LANES:(c + 1) * LANES]
        for s in range(HALF):
            for c in range(N_SLABS):
                ys_ref[s * nseq:(s + 1) * nseq, c * LANES:(c + 1) * LANES] = (
                    slab_sc[c][pl.ds(s, nseq, stride=PITCH), :])

    @pl.when(j == 0)
    def _():
        store_sample(y_ref[...].reshape(rows, D_MODEL))

    @pl.when(j == n_tiles - 1)
    def _():
        for c in range(N_SLABS):
            cols = slice(c * LANES, (c + 1) * LANES)
            hlast_ref[layer, :, cols] = h_sc[c][...]
            for k in range(CONV_W - 1):
                convnew_ref[layer, CONV_W - 2 - k, :, cols] = tail_sc[c][parity, k]
        for dst, prev in zip((hlast_ref, convnew_ref), prev_refs):
            dst[0:layer] = prev[...]

    @pl.when(j == n_tiles + 1)
    def _():
        ymix_sc[1 - parity] = xs_ref[...]

    @pl.when(j >= n_tiles)
    def _():
        xm = ymix_sc[1 - parity]
        xn = _rms_norm(xm, g_mlp_ref[...]).astype(jnp.bfloat16)
        acc = xm
        for c in range(N_FF_CHUNKS):
            acc = acc + _mlp_down(_mlp_up(xn, w1_ref, c), w2_ref, c)
        ymix_sc[parity] = _rms_norm(acc, gf_ref[...]) if final_norm else acc

    @pl.when(j == n_tiles)
    def _():
        y_ref[...] = ymix_sc[parity].reshape(nseq, HALF, D_MODEL)

    @pl.when(j == n_tiles + 1)
    def _():
        store_sample(ymix_sc[parity])


def _fused_step(x_ref, g_mix_ref, w_in_ref, conv_w_ref, conv_b_ref, gate_rb_ref, gate_ib_ref, lam_ref, sgu_g_ref, sgu_bn_ref,
                sgu_bias_ref, w_out_ref, g_mlp_ref, w1_ref, w2_ref, gf_ref, y_ref, vprev_sc, sgu_lhs_sc, ymix_sc,
                gate_sc, slab_sc, tail_sc, h_sc, *, parity, nseq, final_norm):
    rows = nseq * HALF
    half_rb, half_ib, half_c = _lru_consts(gate_rb_ref[...], gate_ib_ref[...], lam_ref[...])
    xb_col0 = 2 * D_SGU
    yb_col0 = 2 * D_SGU + D_LRU
    slabs_per_tile = GATE_TILE // LANES
    st = [{} for _ in range(N_GATE_TILES)]
    sg = {}
    ml = {}

    def mlp_norm():
        xm = ymix_sc[1 - parity]
        ml["acc"] = xm
        ml["xn"] = _rms_norm(xm, g_mlp_ref[...]).astype(jnp.bfloat16)

    def mlp_up(c):
        ml["hid", c] = _mlp_up(ml["xn"], w1_ref, c)

    def mlp_down(c):
        ml["acc"] = ml["acc"] + _mlp_down(ml.pop(("hid", c)), w2_ref, c)

    def mlp_store():
        out = _rms_norm(ml["acc"], gf_ref[...]) if final_norm else ml["acc"]
        y_ref[...] = out.reshape(nseq, HALF, D_MODEL)

    def mix_norm():
        sg["x"] = x_ref[...].reshape(rows, D_MODEL)
        sg["xn"] = _rms_norm(sg["x"], g_mix_ref[...]).astype(jnp.bfloat16)

    def lru_proj(q):
        xb = _dot(sg["xn"], w_in_ref[:, xb_col0 + q * GATE_TILE:xb_col0 + (q + 1) * GATE_TILE])
        for i in range(slabs_per_tile):
            for s in range(nseq):
                slab_sc[slabs_per_tile * q + i][s * PITCH:s * PITCH + HALF, :] = (
                    xb[s * HALF:(s + 1) * HALF, i * LANES:(i + 1) * LANES])

    def lru_conv(q):
        xc_slabs = []
        for c in range(slabs_per_tile * q, slabs_per_tile * (q + 1)):
            lanes = slice(c * LANES, (c + 1) * LANES)
            taps = [jnp.broadcast_to(conv_w_ref[k:k + 1, lanes], (nseq, LANES)) for k in range(CONV_W)]
            taps.append(jnp.broadcast_to(conv_b_ref[:, lanes], (nseq, LANES)))
            p1, p2, p3 = (tail_sc[c][1 - parity, k] for k in range(CONV_W - 1))
            steps_out = []
            for t in range(HALF):
                cur = slab_sc[c][pl.ds(t, nseq, stride=PITCH), :]
                steps_out.append(taps[CONV_W] + taps[3] * cur + taps[2] * p1 + taps[1] * p2 + taps[0] * p3)
                p1, p2, p3 = cur, p1, p2
            for k, pk in enumerate((p1, p2, p3)):
                tail_sc[c][parity, k] = pk
            xc_slabs.append(jnp.concatenate(steps_out, axis=0))
        st[q]["xc"] = jnp.concatenate(xc_slabs, axis=1)

    def lru_gates(q):
        st[q]["ri"] = _dot(st[q]["xc"].astype(jnp.bfloat16), gate_sc[q])

    def lru_coeffs(q):
        cols = slice(q * GATE_TILE, (q + 1) * GATE_TILE)
        ri = st[q].pop("ri")
        st[q]["ab"] = _lru_elementwise(ri[:, :GATE_TILE], ri[:, GATE_TILE:], st[q].pop("xc"),
                                       half_rb[:, cols], half_ib[:, cols], half_c[:, cols])

    def lru_scan(q):
        a, b = st[q].pop("ab")
        for i in range(slabs_per_tile):
            c = slabs_per_tile * q + i
            h = h_sc[c][...]
            for t in range(HALF):
                h = (a[t * nseq:(t + 1) * nseq, i * LANES:(i + 1) * LANES] * h
                     + b[t * nseq:(t + 1) * nseq, i * LANES:(i + 1) * LANES])
                slab_sc[c][pl.ds(t, nseq, stride=PITCH), :] = h
            h_sc[c][...] = h

    def lru_out(q):
        hs = jnp.concatenate(
            [jnp.concatenate([slab_sc[c][s * PITCH:s * PITCH + HALF, :]
                              for c in range(slabs_per_tile * q, slabs_per_tile * (q + 1))], axis=1)
             for s in range(nseq)], axis=0)
        yb = _dot(sg["xn"], w_in_ref[:, yb_col0 + q * GATE_TILE:yb_col0 + (q + 1) * GATE_TILE])
        st[q]["out_b"] = (hs * _gelu(yb)).astype(jnp.bfloat16)

    def lru_mix(q):
        r0 = D_SGU + q * GATE_TILE
        sg["acc"] = sg["acc"] + _dot(st[q].pop("out_b"), w_out_ref[r0:r0 + GATE_TILE, :])

    def sgu_u():
        sg["u"] = _gelu(_dot(sg["xn"], w_in_ref[:, 0:D_SGU]))

    def sgu_v():
        v = _layer_norm(_gelu(_dot(sg["xn"], w_in_ref[:, D_SGU:2 * D_SGU])), sgu_g_ref[...], sgu_bn_ref[...])
        sg["vb"] = v.astype(jnp.bfloat16)

    def sgu_gate():
        vb = sg.pop("vb")
        lane = lax.broadcasted_iota(jnp.int32, (HALF, LANES), 1)
        bias = sgu_bias_ref[pl.ds(pl.multiple_of(parity * HALF, HALF), HALF), :]
        v_full = [jnp.concatenate([vprev_sc[1 - parity, s], vb[s * HALF:(s + 1) * HALF]], axis=0)
                  for s in range(nseq)]
        vprev_sc[parity] = vb.reshape(nseq, HALF, D_SGU)
        gate_rows = [[] for _ in range(nseq)]
        for s in range(0, nseq, 2):
            for g in range(N_LANE_GROUPS):
                lanes = slice(g * LANES, (g + 1) * LANES)
                rhs = jnp.concatenate([v_full[s][:, lanes], v_full[s + 1][:, lanes]], axis=1)
                res = _dot(sgu_lhs_sc[parity, g], rhs)
                for i in range(2):
                    part = res[:, i * LANES:(i + 1) * LANES]
                    gate_rows[s + i].append(jnp.where(lane < SGU_HEAD_DIM, part[:HALF], part[HALF:]))
        gate = jnp.concatenate([jnp.concatenate(r, axis=1) + bias for r in gate_rows], axis=0)
        sg["out_a"] = (sg.pop("u") * gate).astype(jnp.bfloat16)

    def sgu_mix():
        sg["acc"] = sg["x"] + _dot(sg.pop("out_a"), w_out_ref[0:D_SGU, :])

    def mix_store():
        ymix_sc[parity] = sg["acc"]

    tiles = range(N_GATE_TILES)
    mix_norm()
    for q in tiles:
        lru_proj(q)
    mlp_norm()
    mlp_up(0)
    mlp_up(1)
    sgu_u()
    sgu_v()
    for q in tiles:
        lru_conv(q)
    for q in tiles:
        lru_gates(q)
    mlp_down(0)
    mlp_down(1)
    for q in tiles:
        lru_coeffs(q)
        lru_scan(q)
    for q in tiles:
        lru_out(q)
    mlp_up(2)
    mlp_up(3)
    sgu_gate()
    sgu_mix()
    mlp_down(2)
    mlp_down(3)
    for q in tiles:
        lru_mix(q)
    mix_store()
    mlp_store()


def _const_spec(shape):
    nd = len(shape)
    return pl.BlockSpec(shape, lambda *_: (0,) * nd, pipeline_mode=pl.Buffered(1))


def _layer_spec(arr, l):
    nd = arr.ndim - 1
    return pl.BlockSpec((None,) + arr.shape[1:], lambda *_: (l,) + (0,) * nd, pipeline_mode=pl.Buffered(1))


_BIG_WEIGHTS = ("w_in", "w_out", "w1", "w2")
_PROMPT_PARAMS = ("rows", "w_in", "conv_w", "gate_rw", "gate_iw", "sgu_w", "sgu_bias", "w_out", "w1", "w2")
_SAMPLE_MIXER_PARAMS = ("rows", "w_in", "conv_w", "gate_rw", "gate_iw", "sgu_w8", "sgu_b8", "w_out")


def _param_spec(p, k, l):
    return _layer_spec(p[k], 0 if k in _BIG_WEIGHTS else l)


def _layer_prompt(x, xs, p, l, *, final_norm, sample_seq_major, cast_next=(), prev=()):
    assert len(prev) == (2 if l else 0)
    nseq, seq, _ = x.shape
    n_tiles = seq // HALF
    rows = nseq * HALF
    assert xs.shape == (2 * rows, D_MODEL)
    consts = [p[k] for k in _PROMPT_PARAMS]
    cast_in_specs, cast_out_specs, cast_shapes = [], [], []
    for w in cast_next:
        k_rows, n_cols = w.shape[1:]
        blk = k_rows // n_tiles
        assert blk * n_tiles == k_rows and blk % (2 * SUBLANES) == 0
        cast_in_specs.append(pl.BlockSpec((None, blk, n_cols), lambda j: (l + 1, jnp.minimum(j, n_tiles - 1), 0)))
        cast_out_specs.append(pl.BlockSpec((None, blk, n_cols), lambda j: (0, jnp.minimum(j, n_tiles - 1), 0)))
        cast_shapes.append(jax.ShapeDtypeStruct((1, k_rows, n_cols), jnp.bfloat16))
    x_spec = pl.BlockSpec((nseq, HALF, D_MODEL), lambda j: (0, jnp.minimum(j, n_tiles - 1), 0))
    y_spec = pl.BlockSpec((nseq, HALF, D_MODEL), lambda j: (0, jnp.clip(j - 1, 0, n_tiles - 1), 0))
    xs_spec = pl.BlockSpec((rows, D_MODEL), lambda j: (jnp.where(j > n_tiles, 1, 0), 0),
                           pipeline_mode=pl.Buffered(1))
    y, ys, hlast, convnew, *cast = pl.pallas_call(
        functools.partial(_layer_prompt_kernel, nseq=nseq, n_tiles=n_tiles, final_norm=final_norm,
                          n_cast=len(cast_next), sample_seq_major=sample_seq_major, layer=l),
        grid=(n_tiles + 2,),
        in_specs=([x_spec, xs_spec] + [_param_spec(p, k, l) for k in _PROMPT_PARAMS] + cast_in_specs
                  + [_const_spec(a.shape) for a in prev]),
        out_specs=([y_spec, xs_spec, _const_spec((l + 1, nseq, D_LRU)),
                    _const_spec((l + 1, CONV_W - 1, nseq, D_LRU))] + cast_out_specs),
        out_shape=[jax.ShapeDtypeStruct(x.shape, jnp.float32),
                   jax.ShapeDtypeStruct(xs.shape, jnp.float32),
                   jax.ShapeDtypeStruct((l + 1, nseq, D_LRU), jnp.float32),
                   jax.ShapeDtypeStruct((l + 1, CONV_W - 1, nseq, D_LRU), jnp.float32)] + cast_shapes,
        scratch_shapes=([pltpu.VMEM((2, nseq, HALF, D_SGU), jnp.bfloat16),
                         pltpu.VMEM((2, N_LANE_GROUPS, CHUNK, CHUNK), jnp.bfloat16),
                         pltpu.VMEM((2, rows, D_MODEL), jnp.float32),
                         pltpu.VMEM((N_GATE_TILES, GATE_TILE, 2 * GATE_TILE), jnp.bfloat16)]
                        + [pltpu.VMEM((nseq * PITCH, LANES), jnp.float32)] * N_SLABS
                        + [pltpu.VMEM((2, CONV_W - 1, nseq, LANES), jnp.float32)] * N_SLABS
                        + [pltpu.VMEM((nseq, LANES), jnp.float32)] * N_SLABS),
        compiler_params=pltpu.CompilerParams(dimension_semantics=("arbitrary",), vmem_limit_bytes=VMEM_LIMIT),
        name="layer_prompt",
    )(x, xs, *consts, *cast_next, *prev)
    return y, ys, (hlast, convnew), cast


def _mixer_sample_kernel(*refs, steps, nb, seq_major, n_cast, layer):
    n_in = 3 + len(_SAMPLE_MIXER_PARAMS)
    (x_ref, cbuf_ref, h0_ref, rows_ref, w_in_ref, conv_w_ref, gate_rw_ref, gate_iw_ref, sgu_w8_ref, sgu_b8_ref,
     w_out_ref) = refs[:n_in]
    cast_in, refs = refs[n_in:n_in + n_cast], refs[n_in + n_cast:]
    n_prev = 3 if layer else 0
    prev_refs, refs = refs[:n_prev], refs[n_prev:]
    y_ref, v_ref, hlast_ref, convnew_ref = refs[:4]
    cast_out, refs = refs[4:4 + n_cast], refs[4 + n_cast:]
    gate_sc, v_relay_sc, relay_sc = refs[0], refs[1:1 + N_LANE_GROUPS], refs[1 + N_LANE_GROUPS:]
    for dst, prev in zip((v_ref, hlast_ref, convnew_ref), prev_refs):
        dst[0:layer] = prev[...]
    for w_f32, w_bf16 in zip(cast_in, cast_out):
        w_bf16[...] = w_f32[...].astype(jnp.bfloat16)
    (g_mix_ref, _, gate_rb_ref, gate_ib_ref, lam_ref, conv_b_ref, sgu_g_ref, sgu_bn_ref, _) = _row_views(rows_ref)

    @pl.when(pl.program_id(0) == 0)
    def _():
        _build_gate_tiles(gate_rw_ref, gate_iw_ref, gate_sc)

    if seq_major:
        for c in range(D_MODEL // LANES):
            relay_sc[c][...] = x_ref[:, c * LANES:(c + 1) * LANES]
        x = jnp.concatenate(
            [jnp.concatenate([relay_sc[c][pl.ds(t, nb, stride=steps), :] for c in range(D_MODEL // LANES)], axis=1)
             for t in range(steps)], axis=0)
    else:
        x = x_ref[...].reshape(steps * nb, D_MODEL)
    xn = _rms_norm(x, g_mix_ref[...]).astype(jnp.bfloat16)

    u = _gelu(_dot(xn, w_in_ref[:, 0:D_SGU]))
    v = _layer_norm(_gelu(_dot(xn, w_in_ref[:, D_SGU:2 * D_SGU])), sgu_g_ref[...], sgu_bn_ref[...])
    pitch = nb + SUBLANES
    for g in range(N_LANE_GROUPS):
        for t in range(steps):
            v_relay_sc[g][t * pitch:t * pitch + nb, :] = v[t * nb:(t + 1) * nb, g * LANES:(g + 1) * LANES]
    for s in range(nb):
        for g in range(N_LANE_GROUPS):
            v_ref[layer, s, :, g * LANES:(g + 1) * LANES] = v_relay_sc[g][pl.ds(s, steps, stride=pitch), :]
    gate_rows = []
    for t in range(steps):
        acc = sgu_b8_ref[t:t + 1, :] + sgu_w8_ref[t, 0:1, :] * v[0:nb]
        for s in range(1, t + 1):
            acc = acc + sgu_w8_ref[t, s:s + 1, :] * v[s * nb:(s + 1) * nb]
        gate_rows.append(acc)
    out_a = (u * jnp.concatenate(gate_rows, axis=0)).astype(jnp.bfloat16)

    xb = _dot(xn, w_in_ref[:, 2 * D_SGU:2 * D_SGU + D_LRU])
    xp = jnp.concatenate([cbuf_ref[k] for k in range(CONV_W - 1)] + [xb], axis=0)
    xc = conv_b_ref[...] + conv_w_ref[0:1, :] * xp[0:steps * nb]
    for k in range(1, CONV_W):
        xc = xc + conv_w_ref[k:k + 1, :] * xp[k * nb:(k + steps) * nb]
    convnew_ref[layer] = xp[steps * nb:].reshape(CONV_W - 1, nb, D_LRU)

    a, b = _lru_coeffs(xc, gate_sc, gate_rb_ref[...], gate_ib_ref[...], lam_ref[...])
    h = h0_ref[...]
    hs = []
    for t in range(steps):
        h = a[t * nb:(t + 1) * nb] * h + b[t * nb:(t + 1) * nb]
        hs.append(h)
    hlast_ref[layer] = h

    yb = _dot(xn, w_in_ref[:, 2 * D_SGU + D_LRU:])
    out_b = (jnp.concatenate(hs, axis=0) * _gelu(yb)).astype(jnp.bfloat16)
    out = _dot(jnp.concatenate([out_a, out_b], axis=1), w_out_ref[...])
    y_ref[...] = (x + out).reshape(steps, nb, D_MODEL)


def _mixer_sample(x, h0_all, cbuf_all, p, l, *, seq_major, groups, cast=(), prev=()):
    assert len(prev) == (3 if l else 0)
    nb_all = h0_all.shape[1]
    steps = x.shape[0] // nb_all if seq_major else x.shape[1]
    nb = nb_all // groups
    cast_in_specs, cast_out_specs, cast_shapes = [], [], []
    for w in cast:
        k_rows, n_cols = w.shape[1:]
        blk = k_rows // groups
        assert blk * groups == k_rows and blk % (2 * SUBLANES) == 0
        cast_in_specs.append(pl.BlockSpec((None, blk, n_cols), lambda i: (l, i, 0)))
        cast_out_specs.append(pl.BlockSpec((None, blk, n_cols), lambda i: (0, i, 0)))
        cast_shapes.append(jax.ShapeDtypeStruct((1, k_rows, n_cols), jnp.bfloat16))
    per_tile = HALF // nb
    assert per_tile * nb == HALF and nb_all % HALF == 0
    act_spec = pl.BlockSpec((None, steps, nb, D_MODEL), lambda i: (i // per_tile, 0, i % per_tile, 0))
    if seq_major:
        x_spec = pl.BlockSpec((nb * steps, D_MODEL), lambda i: (i, 0))
        relay = [pltpu.VMEM((nb * steps, LANES), jnp.float32)] * (D_MODEL // LANES)
    else:
        x_spec = act_spec
        relay = []
    state_specs = lambda layers: [pl.BlockSpec((layers, nb, steps, D_SGU), lambda i: (0, i, 0, 0)),
                                  pl.BlockSpec((layers, nb, D_LRU), lambda i: (0, i, 0)),
                                  pl.BlockSpec((layers, CONV_W - 1, nb, D_LRU), lambda i: (0, 0, i, 0))]
    out_shapes = [(nb_all // HALF, steps, HALF, D_MODEL), (l + 1, nb_all, steps, D_SGU), (l + 1, nb_all, D_LRU),
                  (l + 1, CONV_W - 1, nb_all, D_LRU)]
    v_relay = [pltpu.VMEM((steps * (nb + SUBLANES), LANES), jnp.float32)] * N_LANE_GROUPS
    y, v, hlast, convnew, *cast_bf16 = pl.pallas_call(
        functools.partial(_mixer_sample_kernel, steps=steps, nb=nb, seq_major=seq_major, n_cast=len(cast),
                          layer=l),
        grid=(groups,),
        in_specs=([x_spec,
                   pl.BlockSpec((None, CONV_W - 1, nb, D_LRU), lambda i: (l, 0, i, 0)),
                   pl.BlockSpec((None, nb, D_LRU), lambda i: (l, i, 0))]
                  + [_param_spec(p, k, l) for k in _SAMPLE_MIXER_PARAMS] + cast_in_specs
                  + (state_specs(l) if l else [])),
        out_specs=[act_spec] + state_specs(l + 1) + cast_out_specs,
        out_shape=[jax.ShapeDtypeStruct(s, jnp.float32) for s in out_shapes] + cast_shapes,
        scratch_shapes=[pltpu.VMEM((N_GATE_TILES, GATE_TILE, 2 * GATE_TILE), jnp.bfloat16)] + v_relay + relay,
        compiler_params=pltpu.CompilerParams(dimension_semantics=("arbitrary",), vmem_limit_bytes=VMEM_LIMIT),
        name="mixer_sample",
    )(x, cbuf_all, h0_all, *[p[k] for k in _SAMPLE_MIXER_PARAMS], *cast, *prev)
    return y, (v, hlast, convnew), cast_bf16


CAST_STEPS = 4


def _cast_kernel(*refs):
    n = len(refs) // 2
    for w_f32, w_bf16 in zip(refs[:n], refs[n:]):
        w_bf16[...] = w_f32[...].astype(jnp.bfloat16)


def _cast_layer(weights, l):
    in_specs, out_specs, out_shapes = [], [], []
    for w in weights:
        k_rows, n_cols = w.shape[1:]
        blk = k_rows // CAST_STEPS
        assert blk * CAST_STEPS == k_rows and blk % (2 * SUBLANES) == 0
        in_specs.append(pl.BlockSpec((None, blk, n_cols), lambda j: (l, j, 0)))
        out_specs.append(pl.BlockSpec((None, blk, n_cols), lambda j: (0, j, 0)))
        out_shapes.append(jax.ShapeDtypeStruct((1, k_rows, n_cols), jnp.bfloat16))
    return pl.pallas_call(
        _cast_kernel,
        grid=(CAST_STEPS,),
        in_specs=in_specs,
        out_specs=out_specs,
        out_shape=out_shapes,
        compiler_params=pltpu.CompilerParams(dimension_semantics=("arbitrary",), vmem_limit_bytes=VMEM_LIMIT),
        name="cast_weights",
    )(*weights)


def _prepare_params(steps, nseq, norm_mix_g, w_in, conv_w, conv_b, gate_r_w, gate_r_b, gate_i_w, gate_i_b,
                    lru_lambda, sgu_norm_g, sgu_norm_b, sgu_w, sgu_b, w_out, norm_mlp_g, mlp_w1, mlp_w2, final_norm_g):
    rows = [None] * N_ROWS
    rows[ROW_G_MIX], rows[ROW_G_MLP] = norm_mix_g, norm_mlp_g
    rows[ROW_GATE_RB], rows[ROW_GATE_IB], rows[ROW_LAM], rows[ROW_CONV_B] = gate_r_b, gate_i_b, lru_lambda, conv_b
    rows[ROW_SGU] = jnp.concatenate([sgu_norm_g, sgu_norm_b], axis=-1)
    rows[ROW_G_FINAL] = jnp.broadcast_to(final_norm_g, (DEPTH, D_MODEL))
    k_idx = lax.broadcasted_iota(jnp.int32, (DEPTH, N_ROWS, D_MODEL), 1)
    packed = jnp.zeros((DEPTH, N_ROWS, D_MODEL), jnp.float32)
    for k, r in enumerate(rows):
        packed = jnp.where(k_idx == k, r[:, None, :], packed)
    return {
        "rows": packed,
        "conv_w": conv_w,
        "gate_rw": gate_r_w,
        "gate_iw": gate_i_w,
        "sgu_w": sgu_w,
        "sgu_bias": jnp.repeat(jnp.transpose(sgu_b, (0, 2, 1)), SGU_HEAD_DIM, axis=2),
        "sgu_w8": jnp.repeat(jnp.transpose(sgu_w[:, :, :steps, :steps], (0, 2, 3, 1)), SGU_HEAD_DIM, axis=3),
        "sgu_b8": jnp.repeat(jnp.transpose(sgu_b[:, :, :steps], (0, 2, 1)), SGU_HEAD_DIM, axis=2),
    }


def kernel(x_prompt, x_sample, state_lru_h, state_conv, norm_mix_g, w_in, conv_w, conv_b, gate_r_w, gate_r_b, gate_i_w, gate_i_b, lru_lambda, sgu_norm_g, sgu_norm_b, sgu_w, sgu_b, w_out, norm_mlp_g, mlp_w1, mlp_w2, final_norm_g):
    nseq, seq, _ = x_prompt.shape
    nb, steps, _ = x_sample.shape
    assert seq % CHUNK == 0 and nseq == SUBLANES and steps == nseq and nb == 2 * HALF
    p = _prepare_params(steps, nseq, norm_mix_g, w_in, conv_w, conv_b, gate_r_w, gate_r_b, gate_i_w, gate_i_b,
                        lru_lambda, sgu_norm_g, sgu_norm_b, sgu_w, sgu_b, w_out, norm_mlp_g, mlp_w1, mlp_w2,
                        final_norm_g)

    xp = x_prompt
    xs = x_sample.reshape(nb * steps, D_MODEL)
    cbuf_tm = jnp.transpose(state_conv, (0, 2, 1, 3))
    prompt_state, sample_state = (), ()
    big_f32 = (w_in, w_out, mlp_w1, mlp_w2)
    big = _cast_layer(big_f32[:2], 0)
    for l in range(DEPTH):
        last = l == DEPTH - 1
        first = l == 0
        p.update(zip(_BIG_WEIGHTS, big))
        xs, sample_state, mlp_bf16 = _mixer_sample(
            xs, state_lru_h, cbuf_tm, p, l, seq_major=first, prev=sample_state,
            groups=SAMPLE_GROUPS_CAST if first else SAMPLE_GROUPS, cast=big_f32[2:] if first else ())
        p.update(zip(_BIG_WEIGHTS[2:], mlp_bf16))
        xp, xs, prompt_state, big = _layer_prompt(xp, xs.reshape(steps * nb, D_MODEL), p, l, final_norm=last,
                                                  sample_seq_major=last, cast_next=() if last else big_f32,
                                                  prev=prompt_state)
        if not last:
            xs = xs.reshape(nb // HALF, steps, HALF, D_MODEL)

    y_sample = xs.reshape(nb, steps, D_MODEL)
    h_p, conv_p = prompt_state
    v_s, h_s, conv_s = sample_state
    return (xp, y_sample, h_p, jnp.transpose(conv_p, (0, 2, 1, 3)), h_s, jnp.transpose(conv_s, (0, 2, 1, 3)), v_s)
```

```python
import functools

import jax
import jax.numpy as jnp
from jax import lax
from jax.experimental import pallas as pl
from jax.experimental.pallas import tpu as pltpu

D_MODEL = 1024
DEPTH = 2
SGU_HEADS = 8
SGU_HEAD_DIM = 64
D_SGU = SGU_HEADS * SGU_HEAD_DIM
CHUNK = 128
D_LRU = 1024
LRU_BLOCKS = 16
LRU_BLOCK_DIM = 64
CONV_W = 4
LRU_C = 8.0
D_FF = 4 * D_MODEL
EPS = 1e-6

SUBLANES = 8
LANES = 128
GATE_TILE = 256
N_GATE_TILES = D_LRU // GATE_TILE
HEADS_PER_LANE_GROUP = LANES // SGU_HEAD_DIM
N_LANE_GROUPS = D_SGU // LANES
N_SLABS = D_LRU // LANES
FF_CHUNK = 2 * D_MODEL
N_FF_CHUNKS = D_FF // FF_CHUNK

SAMPLE_GROUPS = 2
SAMPLE_GROUPS_CAST = 4
HALF = CHUNK // 2
PITCH = HALF + SUBLANES
V7X_VMEM_BYTES = 64 * 1024 * 1024
VMEM_HEADROOM = 2 * 1024 * 1024
VMEM_LIMIT = V7X_VMEM_BYTES - VMEM_HEADROOM
F32_TINY = 1.1754944e-38


def _rms_norm(x, g):
    return x * lax.rsqrt(jnp.mean(x * x, axis=-1, keepdims=True) + EPS) * g


def _layer_norm(x, g, b):
    mu = jnp.mean(x, axis=-1, keepdims=True)
    xc = x - mu
    return xc * lax.rsqrt(jnp.mean(xc * xc, axis=-1, keepdims=True) + EPS) * g + b


def _gelu(x):
    return jax.nn.gelu(x, approximate=True)


def _dot(a, b):
    return jnp.dot(a, b, preferred_element_type=jnp.float32)


def _lru_elementwise(r_pre, i_pre, xc, half_rb, half_ib, half_c):
    th_r = jnp.tanh(r_pre + half_rb)
    th_i = jnp.tanh(i_pre + half_ib)
    log_a = half_c * th_r + half_c
    i = 0.5 * th_i + 0.5
    a = jnp.exp(log_a)
    y = jnp.tanh(log_a) * (-1.0 - a * a)
    b = (y * lax.rsqrt(jnp.maximum(y, F32_TINY))) * (i * xc)
    return a, b


def _lru_consts(gate_rb, gate_ib, lam):
    return 0.5 * gate_rb, 0.5 * gate_ib, (0.5 * LRU_C) * jax.nn.log_sigmoid(lam)


def _build_gate_tiles(gate_rw_ref, gate_iw_ref, gate_sc):
    per_tile = GATE_TILE // LRU_BLOCK_DIM
    k_idx = lax.broadcasted_iota(jnp.int32, (LRU_BLOCK_DIM, GATE_TILE), 0)
    n_idx = lax.broadcasted_iota(jnp.int32, (LRU_BLOCK_DIM, GATE_TILE), 1)
    replicate = (n_idx % LRU_BLOCK_DIM == k_idx).astype(jnp.bfloat16)
    row_blk = lax.broadcasted_iota(jnp.int32, (GATE_TILE, GATE_TILE), 0) // LRU_BLOCK_DIM
    col_blk = lax.broadcasted_iota(jnp.int32, (GATE_TILE, GATE_TILE), 1) // LRU_BLOCK_DIM
    for j in range(N_GATE_TILES):
        for k, ref in enumerate((gate_rw_ref, gate_iw_ref)):
            stacked = ref[per_tile * j:per_tile * (j + 1)].reshape(GATE_TILE, LRU_BLOCK_DIM)
            rep = _dot(stacked.astype(jnp.bfloat16), replicate)
            gate_sc[j, :, k * GATE_TILE:(k + 1) * GATE_TILE] = jnp.where(
                row_blk == col_blk, 0.5 * rep, 0.0).astype(jnp.bfloat16)


def _lru_coeffs(xc, gate_w_ref, gate_rb, gate_ib, lam):
    xcb = xc.astype(jnp.bfloat16)
    r_parts, i_parts = [], []
    for j in range(N_GATE_TILES):
        ri = _dot(xcb[:, j * GATE_TILE:(j + 1) * GATE_TILE], gate_w_ref[j])
        r_parts.append(ri[:, :GATE_TILE])
        i_parts.append(ri[:, GATE_TILE:])
    return _lru_elementwise(jnp.concatenate(r_parts, axis=1), jnp.concatenate(i_parts, axis=1), xc,
                            *_lru_consts(gate_rb, gate_ib, lam))


ROW_G_MIX, ROW_G_MLP, ROW_GATE_RB, ROW_GATE_IB, ROW_LAM, ROW_CONV_B, ROW_SGU, ROW_G_FINAL = range(8)
N_ROWS = 8


def _row_views(rows_ref):
    one = lambda k, lo=0, hi=D_MODEL: rows_ref.at[k:k + 1, lo:hi]
    return (one(ROW_G_MIX), one(ROW_G_MLP), one(ROW_GATE_RB), one(ROW_GATE_IB), one(ROW_LAM), one(ROW_CONV_B),
            one(ROW_SGU, 0, D_SGU), one(ROW_SGU, D_SGU, 2 * D_SGU), one(ROW_G_FINAL))


def _mlp_up(xn, w1_ref, c):
    cols = slice(c * FF_CHUNK, (c + 1) * FF_CHUNK)
    return jnp.square(jnp.maximum(_dot(xn, w1_ref[:, cols]), 0.0)).astype(jnp.bfloat16)


def _mlp_down(hid, w2_ref, c):
    return _dot(hid, w2_ref[c * FF_CHUNK:(c + 1) * FF_CHUNK, :])


def _layer_prompt_kernel(*refs, nseq, n_tiles, final_norm, n_cast, sample_seq_major, layer):
    n_in = 2 + len(_PROMPT_PARAMS)
    (x_ref, xs_ref, rows_ref, w_in_ref, conv_w_ref, gate_rw_ref, gate_iw_ref, sgu_w_ref, sgu_bias_ref, w_out_ref,
     w1_ref, w2_ref) = refs[:n_in]
    (g_mix_ref, g_mlp_ref, gate_rb_ref, gate_ib_ref, lam_ref, conv_b_ref, sgu_g_ref, sgu_bn_ref,
     gf_ref) = _row_views(rows_ref)
    cast_in, refs = refs[n_in:n_in + n_cast], refs[n_in + n_cast:]
    n_prev = 2 if layer else 0
    prev_refs, refs = refs[:n_prev], refs[n_prev:]
    y_ref, ys_ref, hlast_ref, convnew_ref = refs[:4]
    cast_out, refs = refs[4:4 + n_cast], refs[4 + n_cast:]
    vprev_sc, sgu_lhs_sc, ymix_sc, gate_sc = refs[:4]
    slab_scratch = refs[4:]
    rows = nseq * HALF
    j = pl.program_id(0)
    slab_sc = slab_scratch[0:N_SLABS]
    tail_sc = slab_scratch[N_SLABS:2 * N_SLABS]
    h_sc = slab_scratch[2 * N_SLABS:3 * N_SLABS]

    @pl.when(j == 0)
    def _():
        for c in range(N_SLABS):
            tail_sc[c][...] = jnp.zeros_like(tail_sc[c])
            h_sc[c][...] = jnp.zeros_like(h_sc[c])
        vprev_sc[...] = jnp.zeros_like(vprev_sc)
        ymix_sc[1] = xs_ref[...]
        _build_gate_tiles(gate_rw_ref, gate_iw_ref, gate_sc)
        t_idx = lax.broadcasted_iota(jnp.int32, (CHUNK, CHUNK), 0)
        s_idx = lax.broadcasted_iota(jnp.int32, (CHUNK, CHUNK), 1)
        for g in range(N_LANE_GROUPS):
            first, second = [], []
            for hh in range(HEADS_PER_LANE_GROUP):
                wm = jnp.where(s_idx <= t_idx, sgu_w_ref[HEADS_PER_LANE_GROUP * g + hh], 0.0)
                first.append(pltpu.roll(wm[:HALF], HALF, axis=1))
                second.append(wm[HALF:])
            sgu_lhs_sc[0, g] = jnp.concatenate(first, axis=0).astype(jnp.bfloat16)
            sgu_lhs_sc[1, g] = jnp.concatenate(second, axis=0).astype(jnp.bfloat16)

    parity = j % 2

    @pl.when(j < n_tiles)
    def _():
        _fused_step(x_ref, g_mix_ref, w_in_ref, conv_w_ref, conv_b_ref, gate_rb_ref, gate_ib_ref, lam_ref, sgu_g_ref, sgu_bn_ref,
                    sgu_bias_ref, w_out_ref, g_mlp_ref, w1_ref, w2_ref, gf_ref, y_ref, vprev_sc, sgu_lhs_sc, ymix_sc,
                    gate_sc, slab_sc, tail_sc, h_sc, parity=parity, nseq=nseq, final_norm=final_norm)
        for w_f32, w_bf16 in zip(cast_in, cast_out):
            w_bf16[...] = w_f32[...].astype(jnp.bfloat16)

    def store_sample(tile):
        if not sample_seq_major:
            ys_ref[...] = tile
            return
        for c in range(N_SLABS):
            for t in range(nseq):
                slab_sc[c][t * PITCH:t * PITCH + HALF, :] = tile[t * HALF:(t + 1) * HALF, c * LANES:(c + 1) * LANES]
        for s in range(HALF):
            for c in range(N_SLABS):
                ys_ref[s * nseq:(s + 1) * nseq, c * LANES:(c + 1) * LANES] = (
                    slab_sc[c][pl.ds(s, nseq, stride=PITCH), :])

    @pl.when(j == 0)
    def _():
        store_sample(y_ref[...].reshape(rows, D_MODEL))

    @pl.when(j == n_tiles - 1)
    def _():
        for c in range(N_SLABS):
            cols = slice(c * LANES, (c + 1) * LANES)
            hlast_ref[layer, :, cols] = h_sc[c][...]
            for k in range(CONV_W - 1):
                convnew_ref[layer, CONV_W - 2 - k, :, cols] = tail_sc[c][parity, k]
        for dst, prev in zip((hlast_ref, convnew_ref), prev_refs):
            dst[0:layer] = prev[...]

    @pl.when(j == n_tiles + 1)
    def _():
        ymix_sc[1 - parity] = xs_ref[...]

    @pl.when(j >= n_tiles)
    def _():
        xm = ymix_sc[1 - parity]
        xn = _rms_norm(xm, g_mlp_ref[...]).astype(jnp.bfloat16)
        acc = xm
        for c in range(N_FF_CHUNKS):
            acc = acc + _mlp_down(_mlp_up(xn, w1_ref, c), w2_ref, c)
        ymix_sc[parity] = _rms_norm(acc, gf_ref[...]) if final_norm else acc

    @pl.when(j == n_tiles)
    def _():
        y_ref[...] = ymix_sc[parity].reshape(nseq, HALF, D_MODEL)

    @pl.when(j == n_tiles + 1)
    def _():
        store_sample(ymix_sc[parity])


def _fused_step(x_ref, g_mix_ref, w_in_ref, conv_w_ref, conv_b_ref, gate_rb_ref, gate_ib_ref, lam_ref, sgu_g_ref, sgu_bn_ref,
                sgu_bias_ref, w_out_ref, g_mlp_ref, w1_ref, w2_ref, gf_ref, y_ref, vprev_sc, sgu_lhs_sc, ymix_sc,
                gate_sc, slab_sc, tail_sc, h_sc, *, parity, nseq, final_norm):
    rows = nseq * HALF
    half_rb, half_ib, half_c = _lru_consts(gate_rb_ref[...], gate_ib_ref[...], lam_ref[...])
    xb_col0 = 2 * D_SGU
    yb_col0 = 2 * D_SGU + D_LRU
    slabs_per_tile = GATE_TILE // LANES
    st = [{} for _ in range(N_GATE_TILES)]
    sg = {}
    ml = {}

    def mlp_norm():
        xm = ymix_sc[1 - parity]
        ml["acc"] = xm
        ml["xn"] = _rms_norm(xm, g_mlp_ref[...]).astype(jnp.bfloat16)

    def mlp_up(c):
        ml["hid", c] = _mlp_up(ml["xn"], w1_ref, c)

    def mlp_down(c):
        ml["acc"] = ml["acc"] + _mlp_down(ml.pop(("hid", c)), w2_ref, c)

    def mlp_store():
        out = _rms_norm(ml["acc"], gf_ref[...]) if final_norm else ml["acc"]
        y_ref[...] = out.reshape(nseq, HALF, D_MODEL)

    def mix_norm():
        sg["x"] = x_ref[...].reshape(rows, D_MODEL)
        sg["xn"] = _rms_norm(sg["x"], g_mix_ref[...]).astype(jnp.bfloat16)

    def lru_proj(q):
        xb = _dot(sg["xn"], w_in_ref[:, xb_col0 + q * GATE_TILE:xb_col0 + (q + 1) * GATE_TILE])
        for i in range(slabs_per_tile):
            for s in range(nseq):
                slab_sc[slabs_per_tile * q + i][s * PITCH:s * PITCH + HALF, :] = (
                    xb[s * HALF:(s + 1) * HALF, i * LANES:(i + 1) * LANES])

    def lru_conv(q):
        xc_slabs = []
        for c in range(slabs_per_tile * q, slabs_per_tile * (q + 1)):
            lanes = slice(c * LANES, (c + 1) * LANES)
            taps = [jnp.broadcast_to(conv_w_ref[k:k + 1, lanes], (nseq, LANES)) for k in range(CONV_W)]
            taps.append(jnp.broadcast_to(conv_b_ref[:, lanes], (nseq, LANES)))
            p1, p2, p3 = (tail_sc[c][1 - parity, k] for k in range(CONV_W - 1))
            steps_out = []
            for t in range(HALF):
                cur = slab_sc[c][pl.ds(t, nseq, stride=PITCH), :]
                steps_out.append(taps[CONV_W] + taps[3] * cur + taps[2] * p1 + taps[1] * p2 + taps[0] * p3)
                p1, p2, p3 = cur, p1, p2
            for k, pk in enumerate((p1, p2, p3)):
                tail_sc[c][parity, k] = pk
            xc_slabs.append(jnp.concatenate(steps_out, axis=0))
        st[q]["xc"] = jnp.concatenate(xc_slabs, axis=1)

    def lru_gates(q):
        st[q]["ri"] = _dot(st[q]["xc"].astype(jnp.bfloat16), gate_sc[q])

    def lru_coeffs(q):
        cols = slice(q * GATE_TILE, (q + 1) * GATE_TILE)
        ri = st[q].pop("ri")
        st[q]["ab"] = _lru_elementwise(ri[:, :GATE_TILE], ri[:, GATE_TILE:], st[q].pop("xc"),
                                       half_rb[:, cols], half_ib[:, cols], half_c[:, cols])

    def lru_scan(q):
        a, b = st[q].pop("ab")
        for i in range(slabs_per_tile):
            c = slabs_per_tile * q + i
            h = h_sc[c][...]
            for t in range(HALF):
                h = (a[t * nseq:(t + 1) * nseq, i * LANES:(i + 1) * LANES] * h
                     + b[t * nseq:(t + 1) * nseq, i * LANES:(i + 1) * LANES])
                slab_sc[c][pl.ds(t, nseq, stride=PITCH), :] = h
            h_sc[c][...] = h

    def lru_out(q):
        hs = jnp.concatenate(
            [jnp.concatenate([slab_sc[c][s * PITCH:s * PITCH + HALF, :]
                              for c in range(slabs_per_tile * q, slabs_per_tile * (q + 1))], axis=1)
             for s in range(nseq)], axis=0)
        yb = _dot(sg["xn"], w_in_ref[:, yb_col0 + q * GATE_TILE:yb_col0 + (q + 1) * GATE_TILE])
        st[q]["out_b"] = (hs * _gelu(yb)).astype(jnp.bfloat16)

    def lru_mix(q):
        r0 = D_SGU + q * GATE_TILE
        sg["acc"] = sg["acc"] + _dot(st[q].pop("out_b"), w_out_ref[r0:r0 + GATE_TILE, :])

    def sgu_u():
        sg["u"] = _gelu(_dot(sg["xn"], w_in_ref[:, 0:D_SGU]))

    def sgu_v():
        v = _layer_norm(_gelu(_dot(sg["xn"], w_in_ref[:, D_SGU:2 * D_SGU])), sgu_g_ref[...], sgu_bn_ref[...])
        sg["vb"] = v.astype(jnp.bfloat16)

    def sgu_gate():
        vb = sg.pop("vb")
        lane = lax.broadcasted_iota(jnp.int32, (HALF, LANES), 1)
        bias = sgu_bias_ref[pl.ds(pl.multiple_of(parity * HALF, HALF), HALF), :]
        v_full = [jnp.concatenate([vprev_sc[1 - parity, s], vb[s * HALF:(s + 1) * HALF]], axis=0)
                  for s in range(nseq)]
        vprev_sc[parity] = vb.reshape(nseq, HALF, D_SGU)
        gate_rows = [[] for _ in range(nseq)]
        for s in range(0, nseq, 2):
            for g in range(N_LANE_GROUPS):
                lanes = slice(g * LANES, (g + 1) * LANES)
                rhs = jnp.concatenate([v_full[s][:, lanes], v_full[s + 1][:, lanes]], axis=1)
                res = _dot(sgu_lhs_sc[parity, g], rhs)
                for i in range(2):
                    part = res[:, i * LANES:(i + 1) * LANES]
                    gate_rows[s + i].append(jnp.where(lane < SGU_HEAD_DIM, part[:HALF], part[HALF:]))
        gate = jnp.concatenate([jnp.concatenate(r, axis=1) + bias for r in gate_rows], axis=0)
        sg["out_a"] = (sg.pop("u") * gate).astype(jnp.bfloat16)

    def sgu_mix():
        sg["acc"] = sg["x"] + _dot(sg.pop("out_a"), w_out_ref[0:D_SGU, :])

    def mix_store():
        ymix_sc[parity] = sg["acc"]

    tiles = range(N_GATE_TILES)
    mlp_norm()
    mix_norm()
    for q in tiles:
        lru_proj(q)
    first, second = range(N_FF_CHUNKS // 2), range(N_FF_CHUNKS // 2, N_FF_CHUNKS)
    mlp_up(first[0])
    for q in tiles:
        lru_conv(q)
        lru_gates(q)
    for c in first[1:]:
        mlp_up(c)
    sgu_u()
    sgu_v()
    for q in tiles:
        lru_coeffs(q)
        lru_scan(q)
    for c in first:
        mlp_down(c)
    for q in tiles:
        lru_out(q)
    for c in second:
        mlp_up(c)
    sgu_gate()
    sgu_mix()
    for c in second:
        mlp_down(c)
    for q in tiles:
        lru_mix(q)
    mix_store()
    mlp_store()


def _const_spec(shape):
    nd = len(shape)
    return pl.BlockSpec(shape, lambda *_: (0,) * nd, pipeline_mode=pl.Buffered(1))


def _layer_spec(arr, l):
    nd = arr.ndim - 1
    return pl.BlockSpec((None,) + arr.shape[1:], lambda *_: (l,) + (0,) * nd, pipeline_mode=pl.Buffered(1))


_BIG_WEIGHTS = ("w_in", "w_out", "w1", "w2")
_PROMPT_PARAMS = ("rows", "w_in", "conv_w", "gate_rw", "gate_iw", "sgu_w", "sgu_bias", "w_out", "w1", "w2")
_SAMPLE_MIXER_PARAMS = ("rows", "w_in", "conv_w", "gate_rw", "gate_iw", "sgu_w8", "sgu_b8", "w_out")


def _param_spec(p, k, l):
    return _layer_spec(p[k], 0 if k in _BIG_WEIGHTS else l)


def _layer_prompt(x, xs, p, l, *, final_norm, sample_seq_major, cast_next=(), prev=()):
    assert len(prev) == (2 if l else 0)
    nseq, seq, _ = x.shape
    n_tiles = seq // HALF
    rows = nseq * HALF
    assert xs.shape == (2 * rows, D_MODEL)
    consts = [p[k] for k in _PROMPT_PARAMS]
    cast_in_specs, cast_out_specs, cast_shapes = [], [], []
    for w in cast_next:
        k_rows, n_cols = w.shape[1:]
        blk = k_rows // n_tiles
        assert blk * n_tiles == k_rows and blk % (2 * SUBLANES) == 0
        cast_in_specs.append(pl.BlockSpec((None, blk, n_cols), lambda j: (l + 1, jnp.minimum(j, n_tiles - 1), 0)))
        cast_out_specs.append(pl.BlockSpec((None, blk, n_cols), lambda j: (0, jnp.minimum(j, n_tiles - 1), 0)))
        cast_shapes.append(jax.ShapeDtypeStruct((1, k_rows, n_cols), jnp.bfloat16))
    x_spec = pl.BlockSpec((nseq, HALF, D_MODEL), lambda j: (0, jnp.minimum(j, n_tiles - 1), 0))
    y_spec = pl.BlockSpec((nseq, HALF, D_MODEL), lambda j: (0, jnp.clip(j - 1, 0, n_tiles - 1), 0))
    xs_spec = pl.BlockSpec((rows, D_MODEL), lambda j: (jnp.where(j > n_tiles, 1, 0), 0),
                           pipeline_mode=pl.Buffered(1))
    y, ys, hlast, convnew, *cast = pl.pallas_call(
        functools.partial(_layer_prompt_kernel, nseq=nseq, n_tiles=n_tiles, final_norm=final_norm,
                          n_cast=len(cast_next), sample_seq_major=sample_seq_major, layer=l),
        grid=(n_tiles + 2,),
        in_specs=([x_spec, xs_spec] + [_param_spec(p, k, l) for k in _PROMPT_PARAMS] + cast_in_specs
                  + [_const_spec(a.shape) for a in prev]),
        out_specs=([y_spec, xs_spec, _const_spec((l + 1, nseq, D_LRU)),
                    _const_spec((l + 1, CONV_W - 1, nseq, D_LRU))] + cast_out_specs),
        out_shape=[jax.ShapeDtypeStruct(x.shape, jnp.float32),
                   jax.ShapeDtypeStruct(xs.shape, jnp.float32),
                   jax.ShapeDtypeStruct((l + 1, nseq, D_LRU), jnp.float32),
                   jax.ShapeDtypeStruct((l + 1, CONV_W - 1, nseq, D_LRU), jnp.float32)] + cast_shapes,
        scratch_shapes=([pltpu.VMEM((2, nseq, HALF, D_SGU), jnp.bfloat16),
                         pltpu.VMEM((2, N_LANE_GROUPS, CHUNK, CHUNK), jnp.bfloat16),
                         pltpu.VMEM((2, rows, D_MODEL), jnp.float32),
                         pltpu.VMEM((N_GATE_TILES, GATE_TILE, 2 * GATE_TILE), jnp.bfloat16)]
                        + [pltpu.VMEM((nseq * PITCH, LANES), jnp.float32)] * N_SLABS
                        + [pltpu.VMEM((2, CONV_W - 1, nseq, LANES), jnp.float32)] * N_SLABS
                        + [pltpu.VMEM((nseq, LANES), jnp.float32)] * N_SLABS),
        compiler_params=pltpu.CompilerParams(dimension_semantics=("arbitrary",), vmem_limit_bytes=VMEM_LIMIT),
        name="layer_prompt",
    )(x, xs, *consts, *cast_next, *prev)
    return y, ys, (hlast, convnew), cast


def _mixer_sample_kernel(*refs, steps, nb, seq_major, n_cast, layer):
    n_in = 3 + len(_SAMPLE_MIXER_PARAMS)
    (x_ref, cbuf_ref, h0_ref, rows_ref, w_in_ref, conv_w_ref, gate_rw_ref, gate_iw_ref, sgu_w8_ref, sgu_b8_ref,
     w_out_ref) = refs[:n_in]
    cast_in, refs = refs[n_in:n_in + n_cast], refs[n_in + n_cast:]
    n_prev = 3 if layer else 0
    prev_refs, refs = refs[:n_prev], refs[n_prev:]
    y_ref, v_ref, hlast_ref, convnew_ref = refs[:4]
    cast_out, refs = refs[4:4 + n_cast], refs[4 + n_cast:]
    gate_sc, v_relay_sc, relay_sc = refs[0], refs[1:1 + N_LANE_GROUPS], refs[1 + N_LANE_GROUPS:]
    for dst, prev in zip((v_ref, hlast_ref, convnew_ref), prev_refs):
        dst[0:layer] = prev[...]
    for w_f32, w_bf16 in zip(cast_in, cast_out):
        w_bf16[...] = w_f32[...].astype(jnp.bfloat16)
    (g_mix_ref, _, gate_rb_ref, gate_ib_ref, lam_ref, conv_b_ref, sgu_g_ref, sgu_bn_ref, _) = _row_views(rows_ref)

    @pl.when(pl.program_id(0) == 0)
    def _():
        _build_gate_tiles(gate_rw_ref, gate_iw_ref, gate_sc)

    if seq_major:
        for c in range(D_MODEL // LANES):
            relay_sc[c][...] = x_ref[:, c * LANES:(c + 1) * LANES]
        x = jnp.concatenate(
            [jnp.concatenate([relay_sc[c][pl.ds(t, nb, stride=steps), :] for c in range(D_MODEL // LANES)], axis=1)
             for t in range(steps)], axis=0)
    else:
        x = x_ref[...].reshape(steps * nb, D_MODEL)
    xn = _rms_norm(x, g_mix_ref[...]).astype(jnp.bfloat16)

    u = _gelu(_dot(xn, w_in_ref[:, 0:D_SGU]))
    v = _layer_norm(_gelu(_dot(xn, w_in_ref[:, D_SGU:2 * D_SGU])), sgu_g_ref[...], sgu_bn_ref[...])
    pitch = nb + SUBLANES
    for g in range(N_LANE_GROUPS):
        for t in range(steps):
            v_relay_sc[g][t * pitch:t * pitch + nb, :] = v[t * nb:(t + 1) * nb, g * LANES:(g + 1) * LANES]
    for s in range(nb):
        for g in range(N_LANE_GROUPS):
            v_ref[layer, s, :, g * LANES:(g + 1) * LANES] = v_relay_sc[g][pl.ds(s, steps, stride=pitch), :]
    gate_rows = []
    for t in range(steps):
        acc = sgu_b8_ref[t:t + 1, :] + sgu_w8_ref[t, 0:1, :] * v[0:nb]
        for s in range(1, t + 1):
            acc = acc + sgu_w8_ref[t, s:s + 1, :] * v[s * nb:(s + 1) * nb]
        gate_rows.append(acc)
    out_a = (u * jnp.concatenate(gate_rows, axis=0)).astype(jnp.bfloat16)

    xb = _dot(xn, w_in_ref[:, 2 * D_SGU:2 * D_SGU + D_LRU])
    xp = jnp.concatenate([cbuf_ref[k] for k in range(CONV_W - 1)] + [xb], axis=0)
    xc = conv_b_ref[...] + conv_w_ref[0:1, :] * xp[0:steps * nb]
    for k in range(1, CONV_W):
        xc = xc + conv_w_ref[k:k + 1, :] * xp[k * nb:(k + steps) * nb]
    convnew_ref[layer] = xp[steps * nb:].reshape(CONV_W - 1, nb, D_LRU)

    a, b = _lru_coeffs(xc, gate_sc, gate_rb_ref[...], gate_ib_ref[...], lam_ref[...])
    h = h0_ref[...]
    hs = []
    for t in range(steps):
        h = a[t * nb:(t + 1) * nb] * h + b[t * nb:(t + 1) * nb]
        hs.append(h)
    hlast_ref[layer] = h

    yb = _dot(xn, w_in_ref[:, 2 * D_SGU + D_LRU:])
    out_b = (jnp.concatenate(hs, axis=0) * _gelu(yb)).astype(jnp.bfloat16)
    out = _dot(jnp.concatenate([out_a, out_b], axis=1), w_out_ref[...])
    y_ref[...] = (x + out).reshape(steps, nb, D_MODEL)


def _mixer_sample(x, h0_all, cbuf_all, p, l, *, seq_major, groups, cast=(), prev=()):
    assert len(prev) == (3 if l else 0)
    nb_all = h0_all.shape[1]
    steps = x.shape[0] // nb_all if seq_major else x.shape[1]
    nb = nb_all // groups
    cast_in_specs, cast_out_specs, cast_shapes = [], [], []
    for w in cast:
        k_rows, n_cols = w.shape[1:]
        blk = k_rows // groups
        assert blk * groups == k_rows and blk % (2 * SUBLANES) == 0
        cast_in_specs.append(pl.BlockSpec((None, blk, n_cols), lambda i: (l, i, 0)))
        cast_out_specs.append(pl.BlockSpec((None, blk, n_cols), lambda i: (0, i, 0)))
        cast_shapes.append(jax.ShapeDtypeStruct((1, k_rows, n_cols), jnp.bfloat16))
    per_tile = HALF // nb
    assert per_tile * nb == HALF and nb_all % HALF == 0
    act_spec = pl.BlockSpec((None, steps, nb, D_MODEL), lambda i: (i // per_tile, 0, i % per_tile, 0))
    if seq_major:
        x_spec = pl.BlockSpec((nb * steps, D_MODEL), lambda i: (i, 0))
        relay = [pltpu.VMEM((nb * steps, LANES), jnp.float32)] * (D_MODEL // LANES)
    else:
        x_spec = act_spec
        relay = []
    state_specs = lambda layers: [pl.BlockSpec((layers, nb, steps, D_SGU), lambda i: (0, i, 0, 0)),
                                  pl.BlockSpec((layers, nb, D_LRU), lambda i: (0, i, 0)),
                                  pl.BlockSpec((layers, CONV_W - 1, nb, D_LRU), lambda i: (0, 0, i, 0))]
    out_shapes = [(nb_all // HALF, steps, HALF, D_MODEL), (l + 1, nb_all, steps, D_SGU), (l + 1, nb_all, D_LRU),
                  (l + 1, CONV_W - 1, nb_all, D_LRU)]
    v_relay = [pltpu.VMEM((steps * (nb + SUBLANES), LANES), jnp.float32)] * N_LANE_GROUPS
    y, v, hlast, convnew, *cast_bf16 = pl.pallas_call(
        functools.partial(_mixer_sample_kernel, steps=steps, nb=nb, seq_major=seq_major, n_cast=len(cast),
                          layer=l),
        grid=(groups,),
        in_specs=([x_spec,
                   pl.BlockSpec((None, CONV_W - 1, nb, D_LRU), lambda i: (l, 0, i, 0)),
                   pl.BlockSpec((None, nb, D_LRU), lambda i: (l, i, 0))]
                  + [_param_spec(p, k, l) for k in _SAMPLE_MIXER_PARAMS] + cast_in_specs
                  + (state_specs(l) if l else [])),
        out_specs=[act_spec] + state_specs(l + 1) + cast_out_specs,
        out_shape=[jax.ShapeDtypeStruct(s, jnp.float32) for s in out_shapes] + cast_shapes,
        scratch_shapes=[pltpu.VMEM((N_GATE_TILES, GATE_TILE, 2 * GATE_TILE), jnp.bfloat16)] + v_relay + relay,
        compiler_params=pltpu.CompilerParams(dimension_semantics=("arbitrary",), vmem_limit_bytes=VMEM_LIMIT),
        name="mixer_sample",
    )(x, cbuf_all, h0_all, *[p[k] for k in _SAMPLE_MIXER_PARAMS], *cast, *prev)
    return y, (v, hlast, convnew), cast_bf16


CAST_STEPS = 4


def _cast_kernel(*refs):
    n = len(refs) // 2
    for w_f32, w_bf16 in zip(refs[:n], refs[n:]):
        w_bf16[...] = w_f32[...].astype(jnp.bfloat16)


def _cast_layer(weights, l):
    in_specs, out_specs, out_shapes = [], [], []
    for w in weights:
        k_rows, n_cols = w.shape[1:]
        blk = k_rows // CAST_STEPS
        assert blk * CAST_STEPS == k_rows and blk % (2 * SUBLANES) == 0
        in_specs.append(pl.BlockSpec((None, blk, n_cols), lambda j: (l, j, 0)))
        out_specs.append(pl.BlockSpec((None, blk, n_cols), lambda j: (0, j, 0)))
        out_shapes.append(jax.ShapeDtypeStruct((1, k_rows, n_cols), jnp.bfloat16))
    return pl.pallas_call(
        _cast_kernel,
        grid=(CAST_STEPS,),
        in_specs=in_specs,
        out_specs=out_specs,
        out_shape=out_shapes,
        compiler_params=pltpu.CompilerParams(dimension_semantics=("arbitrary",), vmem_limit_bytes=VMEM_LIMIT),
        name="cast_weights",
    )(*weights)


def _prepare_params(steps, nseq, norm_mix_g, w_in, conv_w, conv_b, gate_r_w, gate_r_b, gate_i_w, gate_i_b,
                    lru_lambda, sgu_norm_g, sgu_norm_b, sgu_w, sgu_b, w_out, norm_mlp_g, mlp_w1, mlp_w2, final_norm_g):
    rows = [None] * N_ROWS
    rows[ROW_G_MIX], rows[ROW_G_MLP] = norm_mix_g, norm_mlp_g
    rows[ROW_GATE_RB], rows[ROW_GATE_IB], rows[ROW_LAM], rows[ROW_CONV_B] = gate_r_b, gate_i_b, lru_lambda, conv_b
    rows[ROW_SGU] = jnp.concatenate([sgu_norm_g, sgu_norm_b], axis=-1)
    rows[ROW_G_FINAL] = jnp.broadcast_to(final_norm_g, (DEPTH, D_MODEL))
    k_idx = lax.broadcasted_iota(jnp.int32, (DEPTH, N_ROWS, D_MODEL), 1)
    packed = jnp.zeros((DEPTH, N_ROWS, D_MODEL), jnp.float32)
    for k, r in enumerate(rows):
        packed = jnp.where(k_idx == k, r[:, None, :], packed)
    return {
        "rows": packed,
        "conv_w": conv_w,
        "gate_rw": gate_r_w,
        "gate_iw": gate_i_w,
        "sgu_w": sgu_w,
        "sgu_bias": jnp.repeat(jnp.transpose(sgu_b, (0, 2, 1)), SGU_HEAD_DIM, axis=2),
        "sgu_w8": jnp.repeat(jnp.transpose(sgu_w[:, :, :steps, :steps], (0, 2, 3, 1)), SGU_HEAD_DIM, axis=3),
        "sgu_b8": jnp.repeat(jnp.transpose(sgu_b[:, :, :steps], (0, 2, 1)), SGU_HEAD_DIM, axis=2),
    }


def kernel(x_prompt, x_sample, state_lru_h, state_conv, norm_mix_g, w_in, conv_w, conv_b, gate_r_w, gate_r_b, gate_i_w, gate_i_b, lru_lambda, sgu_norm_g, sgu_norm_b, sgu_w, sgu_b, w_out, norm_mlp_g, mlp_w1, mlp_w2, final_norm_g):
    nseq, seq, _ = x_prompt.shape
    nb, steps, _ = x_sample.shape
    assert seq % CHUNK == 0 and nseq == SUBLANES and steps == nseq and nb == 2 * HALF
    p = _prepare_params(steps, nseq, norm_mix_g, w_in, conv_w, conv_b, gate_r_w, gate_r_b, gate_i_w, gate_i_b,
                        lru_lambda, sgu_norm_g, sgu_norm_b, sgu_w, sgu_b, w_out, norm_mlp_g, mlp_w1, mlp_w2,
                        final_norm_g)

    xp = x_prompt
    xs = x_sample.reshape(nb * steps, D_MODEL)
    cbuf_tm = jnp.transpose(state_conv, (0, 2, 1, 3))
    prompt_state, sample_state = (), ()
    big_f32 = (w_in, w_out, mlp_w1, mlp_w2)
    big = _cast_layer(big_f32[:2], 0)
    for l in range(DEPTH):
        last = l == DEPTH - 1
        first = l == 0
        p.update(zip(_BIG_WEIGHTS, big))
        xs, sample_state, mlp_bf16 = _mixer_sample(
            xs, state_lru_h, cbuf_tm, p, l, seq_major=first, prev=sample_state,
            groups=SAMPLE_GROUPS_CAST if first else SAMPLE_GROUPS, cast=big_f32[2:] if first else ())
        p.update(zip(_BIG_WEIGHTS[2:], mlp_bf16))
        xp, xs, prompt_state, big = _layer_prompt(xp, xs.reshape(steps * nb, D_MODEL), p, l, final_norm=last,
                                                  sample_seq_major=last, cast_next=() if last else big_f32,
                                                  prev=prompt_state)
        if not last:
            xs = xs.reshape(nb // HALF, steps, HALF, D_MODEL)

    y_sample = xs.reshape(nb, steps, D_MODEL)
    h_p, conv_p = prompt_state
    v_s, h_s, conv_s = sample_state
    return (xp, y_sample, h_p, jnp.transpose(conv_p, (0, 2, 1, 3)), h_s, jnp.transpose(conv_s, (0, 2, 1, 3)), v_s)
```

```python
import functools

import jax
import jax.numpy as jnp
from jax import lax
from jax.experimental import pallas as pl
from jax.experimental.pallas import tpu as pltpu

D_MODEL = 1024
DEPTH = 2
SGU_HEADS = 8
SGU_HEAD_DIM = 64
D_SGU = SGU_HEADS * SGU_HEAD_DIM
CHUNK = 128
D_LRU = 1024
LRU_BLOCKS = 16
LRU_BLOCK_DIM = 64
CONV_W = 4
LRU_C = 8.0
D_FF = 4 * D_MODEL
EPS = 1e-6

SUBLANES = 8
LANES = 128
GATE_TILE = 256
N_GATE_TILES = D_LRU // GATE_TILE
HEADS_PER_LANE_GROUP = LANES // SGU_HEAD_DIM
N_LANE_GROUPS = D_SGU // LANES
N_SLABS = D_LRU // LANES
FF_CHUNK = D_FF
N_FF_CHUNKS = D_FF // FF_CHUNK

SAMPLE_GROUPS = 2
SAMPLE_GROUPS_CAST = 4
HALF = CHUNK // 2
PITCH = HALF + SUBLANES
V7X_VMEM_BYTES = 64 * 1024 * 1024
VMEM_HEADROOM = 2 * 1024 * 1024
VMEM_LIMIT = V7X_VMEM_BYTES - VMEM_HEADROOM
F32_TINY = 1.1754944e-38


def _rms_norm(x, g):
    return x * lax.rsqrt(jnp.mean(x * x, axis=-1, keepdims=True) + EPS) * g


def _layer_norm(x, g, b):
    mu = jnp.mean(x, axis=-1, keepdims=True)
    xc = x - mu
    return xc * lax.rsqrt(jnp.mean(xc * xc, axis=-1, keepdims=True) + EPS) * g + b


def _gelu(x):
    return jax.nn.gelu(x, approximate=True)


def _dot(a, b):
    return jnp.dot(a, b, preferred_element_type=jnp.float32)


def _lru_elementwise(r_pre, i_pre, xc, half_rb, half_ib, half_c):
    th_r = jnp.tanh(r_pre + half_rb)
    th_i = jnp.tanh(i_pre + half_ib)
    log_a = half_c * th_r + half_c
    i = 0.5 * th_i + 0.5
    a = jnp.exp(log_a)
    y = jnp.tanh(log_a) * (-1.0 - a * a)
    b = (y * lax.rsqrt(jnp.maximum(y, F32_TINY))) * (i * xc)
    return a, b


def _lru_consts(gate_rb, gate_ib, lam):
    return 0.5 * gate_rb, 0.5 * gate_ib, (0.5 * LRU_C) * jax.nn.log_sigmoid(lam)


def _build_gate_tiles(gate_rw_ref, gate_iw_ref, gate_sc):
    per_tile = GATE_TILE // LRU_BLOCK_DIM
    k_idx = lax.broadcasted_iota(jnp.int32, (LRU_BLOCK_DIM, GATE_TILE), 0)
    n_idx = lax.broadcasted_iota(jnp.int32, (LRU_BLOCK_DIM, GATE_TILE), 1)
    replicate = (n_idx % LRU_BLOCK_DIM == k_idx).astype(jnp.bfloat16)
    row_blk = lax.broadcasted_iota(jnp.int32, (GATE_TILE, GATE_TILE), 0) // LRU_BLOCK_DIM
    col_blk = lax.broadcasted_iota(jnp.int32, (GATE_TILE, GATE_TILE), 1) // LRU_BLOCK_DIM
    for j in range(N_GATE_TILES):
        for k, ref in enumerate((gate_rw_ref, gate_iw_ref)):
            stacked = ref[per_tile * j:per_tile * (j + 1)].reshape(GATE_TILE, LRU_BLOCK_DIM)
            rep = _dot(stacked.astype(jnp.bfloat16), replicate)
            gate_sc[j, :, k * GATE_TILE:(k + 1) * GATE_TILE] = jnp.where(
                row_blk == col_blk, 0.5 * rep, 0.0).astype(jnp.bfloat16)


def _lru_coeffs(xc, gate_w_ref, gate_rb, gate_ib, lam):
    xcb = xc.astype(jnp.bfloat16)
    r_parts, i_parts = [], []
    for j in range(N_GATE_TILES):
        ri = _dot(xcb[:, j * GATE_TILE:(j + 1) * GATE_TILE], gate_w_ref[j])
        r_parts.append(ri[:, :GATE_TILE])
        i_parts.append(ri[:, GATE_TILE:])
    return _lru_elementwise(jnp.concatenate(r_parts, axis=1), jnp.concatenate(i_parts, axis=1), xc,
                            *_lru_consts(gate_rb, gate_ib, lam))


ROW_G_MIX, ROW_G_MLP, ROW_GATE_RB, ROW_GATE_IB, ROW_LAM, ROW_CONV_B, ROW_SGU, ROW_G_FINAL = range(8)
N_ROWS = 8


def _row_views(rows_ref):
    one = lambda k, lo=0, hi=D_MODEL: rows_ref.at[k:k + 1, lo:hi]
    return (one(ROW_G_MIX), one(ROW_G_MLP), one(ROW_GATE_RB), one(ROW_GATE_IB), one(ROW_LAM), one(ROW_CONV_B),
            one(ROW_SGU, 0, D_SGU), one(ROW_SGU, D_SGU, 2 * D_SGU), one(ROW_G_FINAL))


def _mlp_up(xn, w1_ref, c):
    cols = slice(c * FF_CHUNK, (c + 1) * FF_CHUNK)
    return jnp.square(jnp.maximum(_dot(xn, w1_ref[:, cols]), 0.0)).astype(jnp.bfloat16)


def _mlp_down(hid, w2_ref, c):
    return _dot(hid, w2_ref[c * FF_CHUNK:(c + 1) * FF_CHUNK, :])


def _layer_prompt_kernel(*refs, nseq, n_tiles, final_norm, n_cast, sample_seq_major, layer):
    n_in = 2 + len(_PROMPT_PARAMS)
    (x_ref, xs_ref, rows_ref, w_in_ref, conv_w_ref, gate_rw_ref, gate_iw_ref, sgu_w_ref, sgu_bias_ref, w_out_ref,
     w1_ref, w2_ref) = refs[:n_in]
    (g_mix_ref, g_mlp_ref, gate_rb_ref, gate_ib_ref, lam_ref, conv_b_ref, sgu_g_ref, sgu_bn_ref,
     gf_ref) = _row_views(rows_ref)
    cast_in, refs = refs[n_in:n_in + n_cast], refs[n_in + n_cast:]
    n_prev = 2 if layer else 0
    prev_refs, refs = refs[:n_prev], refs[n_prev:]
    y_ref, ys_ref, hlast_ref, convnew_ref = refs[:4]
    cast_out, refs = refs[4:4 + n_cast], refs[4 + n_cast:]
    vprev_sc, sgu_lhs_sc, ymix_sc, gate_sc = refs[:4]
    slab_scratch = refs[4:]
    rows = nseq * HALF
    j = pl.program_id(0)
    slab_sc = slab_scratch[0:N_SLABS]
    tail_sc = slab_scratch[N_SLABS:2 * N_SLABS]
    h_sc = slab_scratch[2 * N_SLABS:3 * N_SLABS]

    @pl.when(j == 0)
    def _():
        for c in range(N_SLABS):
            tail_sc[c][...] = jnp.zeros_like(tail_sc[c])
            h_sc[c][...] = jnp.zeros_like(h_sc[c])
        vprev_sc[...] = jnp.zeros_like(vprev_sc)
        ymix_sc[1] = xs_ref[...]
        _build_gate_tiles(gate_rw_ref, gate_iw_ref, gate_sc)
        t_idx = lax.broadcasted_iota(jnp.int32, (CHUNK, CHUNK), 0)
        s_idx = lax.broadcasted_iota(jnp.int32, (CHUNK, CHUNK), 1)
        for g in range(N_LANE_GROUPS):
            first, second = [], []
            for hh in range(HEADS_PER_LANE_GROUP):
                wm = jnp.where(s_idx <= t_idx, sgu_w_ref[HEADS_PER_LANE_GROUP * g + hh], 0.0)
                first.append(pltpu.roll(wm[:HALF], HALF, axis=1))
                second.append(wm[HALF:])
            sgu_lhs_sc[0, g] = jnp.concatenate(first, axis=0).astype(jnp.bfloat16)
            sgu_lhs_sc[1, g] = jnp.concatenate(second, axis=0).astype(jnp.bfloat16)

    parity = j % 2

    @pl.when(j < n_tiles)
    def _():
        _fused_step(x_ref, g_mix_ref, w_in_ref, conv_w_ref, conv_b_ref, gate_rb_ref, gate_ib_ref, lam_ref, sgu_g_ref, sgu_bn_ref,
                    sgu_bias_ref, w_out_ref, g_mlp_ref, w1_ref, w2_ref, gf_ref, y_ref, vprev_sc, sgu_lhs_sc, ymix_sc,
                    gate_sc, slab_sc, tail_sc, h_sc, parity=parity, nseq=nseq, final_norm=final_norm)
        for w_f32, w_bf16 in zip(cast_in, cast_out):
            w_bf16[...] = w_f32[...].astype(jnp.bfloat16)

    def store_sample(tile):
        if not sample_seq_major:
            ys_ref[...] = tile
            return
        for c in range(N_SLABS):
            for t in range(nseq):
                slab_sc[c][t * PITCH:t * PITCH + HALF, :] = tile[t * HALF:(t + 1) * HALF, c * LANES:(c + 1) * LANES]
        for s in range(HALF):
            for c in range(N_SLABS):
                ys_ref[s * nseq:(s + 1) * nseq, c * LANES:(c + 1) * LANES] = (
                    slab_sc[c][pl.ds(s, nseq, stride=PITCH), :])

    @pl.when(j == 0)
    def _():
        store_sample(y_ref[...].reshape(rows, D_MODEL))

    @pl.when(j == n_tiles - 1)
    def _():
        for c in range(N_SLABS):
            cols = slice(c * LANES, (c + 1) * LANES)
            hlast_ref[layer, :, cols] = h_sc[c][...]
            for k in range(CONV_W - 1):
                convnew_ref[layer, CONV_W - 2 - k, :, cols] = tail_sc[c][parity, k]
        for dst, prev in zip((hlast_ref, convnew_ref), prev_refs):
            dst[0:layer] = prev[...]

    @pl.when(j == n_tiles + 1)
    def _():
        ymix_sc[1 - parity] = xs_ref[...]

    @pl.when(j >= n_tiles)
    def _():
        xm = ymix_sc[1 - parity]
        xn = _rms_norm(xm, g_mlp_ref[...]).astype(jnp.bfloat16)
        acc = xm
        for c in range(N_FF_CHUNKS):
            acc = acc + _mlp_down(_mlp_up(xn, w1_ref, c), w2_ref, c)
        ymix_sc[parity] = _rms_norm(acc, gf_ref[...]) if final_norm else acc

    @pl.when(j == n_tiles)
    def _():
        y_ref[...] = ymix_sc[parity].reshape(nseq, HALF, D_MODEL)

    @pl.when(j == n_tiles + 1)
    def _():
        store_sample(ymix_sc[parity])


def _fused_step(x_ref, g_mix_ref, w_in_ref, conv_w_ref, conv_b_ref, gate_rb_ref, gate_ib_ref, lam_ref, sgu_g_ref, sgu_bn_ref,
                sgu_bias_ref, w_out_ref, g_mlp_ref, w1_ref, w2_ref, gf_ref, y_ref, vprev_sc, sgu_lhs_sc, ymix_sc,
                gate_sc, slab_sc, tail_sc, h_sc, *, parity, nseq, final_norm):
    rows = nseq * HALF
    half_rb, half_ib, half_c = _lru_consts(gate_rb_ref[...], gate_ib_ref[...], lam_ref[...])
    xb_col0 = 2 * D_SGU
    yb_col0 = 2 * D_SGU + D_LRU
    slabs_per_tile = GATE_TILE // LANES
    st = [{} for _ in range(N_GATE_TILES)]
    sg = {}
    ml = {}

    def mlp_norm():
        xm = ymix_sc[1 - parity]
        ml["acc"] = xm
        ml["xn"] = _rms_norm(xm, g_mlp_ref[...]).astype(jnp.bfloat16)

    def mlp_up(c):
        ml["hid", c] = _mlp_up(ml["xn"], w1_ref, c)

    def mlp_down(c):
        ml["acc"] = ml["acc"] + _mlp_down(ml.pop(("hid", c)), w2_ref, c)

    def mlp_store():
        out = _rms_norm(ml["acc"], gf_ref[...]) if final_norm else ml["acc"]
        y_ref[...] = out.reshape(nseq, HALF, D_MODEL)

    def mix_norm():
        sg["x"] = x_ref[...].reshape(rows, D_MODEL)
        sg["xn"] = _rms_norm(sg["x"], g_mix_ref[...]).astype(jnp.bfloat16)

    def lru_proj(q):
        xb = _dot(sg["xn"], w_in_ref[:, xb_col0 + q * GATE_TILE:xb_col0 + (q + 1) * GATE_TILE])
        for i in range(slabs_per_tile):
            for s in range(nseq):
                slab_sc[slabs_per_tile * q + i][s * PITCH:s * PITCH + HALF, :] = (
                    xb[s * HALF:(s + 1) * HALF, i * LANES:(i + 1) * LANES])

    def lru_conv(q):
        xc_slabs = []
        for c in range(slabs_per_tile * q, slabs_per_tile * (q + 1)):
            lanes = slice(c * LANES, (c + 1) * LANES)
            taps = [jnp.broadcast_to(conv_w_ref[k:k + 1, lanes], (nseq, LANES)) for k in range(CONV_W)]
            taps.append(jnp.broadcast_to(conv_b_ref[:, lanes], (nseq, LANES)))
            p1, p2, p3 = (tail_sc[c][1 - parity, k] for k in range(CONV_W - 1))
            steps_out = []
            for t in range(HALF):
                cur = slab_sc[c][pl.ds(t, nseq, stride=PITCH), :]
                steps_out.append(taps[CONV_W] + taps[3] * cur + taps[2] * p1 + taps[1] * p2 + taps[0] * p3)
                p1, p2, p3 = cur, p1, p2
            for k, pk in enumerate((p1, p2, p3)):
                tail_sc[c][parity, k] = pk
            xc_slabs.append(jnp.concatenate(steps_out, axis=0))
        st[q]["xc"] = jnp.concatenate(xc_slabs, axis=1)

    def lru_gates(q):
        st[q]["ri"] = _dot(st[q]["xc"].astype(jnp.bfloat16), gate_sc[q])

    def lru_coeffs(q):
        cols = slice(q * GATE_TILE, (q + 1) * GATE_TILE)
        ri = st[q].pop("ri")
        st[q]["ab"] = _lru_elementwise(ri[:, :GATE_TILE], ri[:, GATE_TILE:], st[q].pop("xc"),
                                       half_rb[:, cols], half_ib[:, cols], half_c[:, cols])

    def lru_scan(q):
        a, b = st[q].pop("ab")
        for i in range(slabs_per_tile):
            c = slabs_per_tile * q + i
            h = h_sc[c][...]
            for t in range(HALF):
                h = (a[t * nseq:(t + 1) * nseq, i * LANES:(i + 1) * LANES] * h
                     + b[t * nseq:(t + 1) * nseq, i * LANES:(i + 1) * LANES])
                slab_sc[c][pl.ds(t, nseq, stride=PITCH), :] = h
            h_sc[c][...] = h

    def lru_out(q):
        hs = jnp.concatenate(
            [jnp.concatenate([slab_sc[c][s * PITCH:s * PITCH + HALF, :]
                              for c in range(slabs_per_tile * q, slabs_per_tile * (q + 1))], axis=1)
             for s in range(nseq)], axis=0)
        yb = _dot(sg["xn"], w_in_ref[:, yb_col0 + q * GATE_TILE:yb_col0 + (q + 1) * GATE_TILE])
        st[q]["out_b"] = (hs * _gelu(yb)).astype(jnp.bfloat16)

    def lru_mix(q):
        r0 = D_SGU + q * GATE_TILE
        sg["acc"] = sg["acc"] + _dot(st[q].pop("out_b"), w_out_ref[r0:r0 + GATE_TILE, :])

    def sgu_u():
        sg["u"] = _gelu(_dot(sg["xn"], w_in_ref[:, 0:D_SGU]))

    def sgu_v():
        v = _layer_norm(_gelu(_dot(sg["xn"], w_in_ref[:, D_SGU:2 * D_SGU])), sgu_g_ref[...], sgu_bn_ref[...])
        sg["vb"] = v.astype(jnp.bfloat16)

    def sgu_gate():
        vb = sg.pop("vb")
        lane = lax.broadcasted_iota(jnp.int32, (HALF, LANES), 1)
        bias = sgu_bias_ref[pl.ds(pl.multiple_of(parity * HALF, HALF), HALF), :]
        v_full = [jnp.concatenate([vprev_sc[1 - parity, s], vb[s * HALF:(s + 1) * HALF]], axis=0)
                  for s in range(nseq)]
        vprev_sc[parity] = vb.reshape(nseq, HALF, D_SGU)
        gate_rows = [[] for _ in range(nseq)]
        for s in range(0, nseq, 2):
            for g in range(N_LANE_GROUPS):
                lanes = slice(g * LANES, (g + 1) * LANES)
                rhs = jnp.concatenate([v_full[s][:, lanes], v_full[s + 1][:, lanes]], axis=1)
                res = _dot(sgu_lhs_sc[parity, g], rhs)
                for i in range(2):
                    part = res[:, i * LANES:(i + 1) * LANES]
                    gate_rows[s + i].append(jnp.where(lane < SGU_HEAD_DIM, part[:HALF], part[HALF:]))
        gate = jnp.concatenate([jnp.concatenate(r, axis=1) + bias for r in gate_rows], axis=0)
        sg["out_a"] = (sg.pop("u") * gate).astype(jnp.bfloat16)

    def sgu_mix():
        sg["acc"] = sg["x"] + _dot(sg.pop("out_a"), w_out_ref[0:D_SGU, :])

    def mix_store():
        ymix_sc[parity] = sg["acc"]

    tiles = range(N_GATE_TILES)
    mlp_norm()
    mix_norm()
    for q in tiles:
        lru_proj(q)
    n_first = max(1, N_FF_CHUNKS // 2)
    first, second = range(n_first), range(n_first, N_FF_CHUNKS)
    mlp_up(first[0])
    for q in tiles:
        lru_conv(q)
        lru_gates(q)
    for c in first[1:]:
        mlp_up(c)
    sgu_u()
    sgu_v()
    for q in tiles:
        lru_coeffs(q)
        lru_scan(q)
    for c in first:
        mlp_down(c)
    for q in tiles:
        lru_out(q)
    for c in second:
        mlp_up(c)
    sgu_gate()
    sgu_mix()
    for c in second:
        mlp_down(c)
    for q in tiles:
        lru_mix(q)
    mix_store()
    mlp_store()


def _const_spec(shape):
    nd = len(shape)
    return pl.BlockSpec(shape, lambda *_: (0,) * nd, pipeline_mode=pl.Buffered(1))


def _layer_spec(arr, l):
    nd = arr.ndim - 1
    return pl.BlockSpec((None,) + arr.shape[1:], lambda *_: (l,) + (0,) * nd, pipeline_mode=pl.Buffered(1))


_BIG_WEIGHTS = ("w_in", "w_out", "w1", "w2")
_PROMPT_PARAMS = ("rows", "w_in", "conv_w", "gate_rw", "gate_iw", "sgu_w", "sgu_bias", "w_out", "w1", "w2")
_SAMPLE_MIXER_PARAMS = ("rows", "w_in", "conv_w", "gate_rw", "gate_iw", "sgu_w8", "sgu_b8", "w_out")


def _param_spec(p, k, l):
    return _layer_spec(p[k], 0 if k in _BIG_WEIGHTS else l)


def _layer_prompt(x, xs, p, l, *, final_norm, sample_seq_major, cast_next=(), prev=()):
    assert len(prev) == (2 if l else 0)
    nseq, seq, _ = x.shape
    n_tiles = seq // HALF
    rows = nseq * HALF
    assert xs.shape == (2 * rows, D_MODEL)
    consts = [p[k] for k in _PROMPT_PARAMS]
    cast_in_specs, cast_out_specs, cast_shapes = [], [], []
    for w in cast_next:
        k_rows, n_cols = w.shape[1:]
        blk = k_rows // n_tiles
        assert blk * n_tiles == k_rows and blk % (2 * SUBLANES) == 0
        cast_in_specs.append(pl.BlockSpec((None, blk, n_cols), lambda j: (l + 1, jnp.minimum(j, n_tiles - 1), 0)))
        cast_out_specs.append(pl.BlockSpec((None, blk, n_cols), lambda j: (0, jnp.minimum(j, n_tiles - 1), 0)))
        cast_shapes.append(jax.ShapeDtypeStruct((1, k_rows, n_cols), jnp.bfloat16))
    x_spec = pl.BlockSpec((nseq, HALF, D_MODEL), lambda j: (0, jnp.minimum(j, n_tiles - 1), 0))
    y_spec = pl.BlockSpec((nseq, HALF, D_MODEL), lambda j: (0, jnp.clip(j - 1, 0, n_tiles - 1), 0))
    xs_spec = pl.BlockSpec((rows, D_MODEL), lambda j: (jnp.where(j > n_tiles, 1, 0), 0),
                           pipeline_mode=pl.Buffered(1))
    y, ys, hlast, convnew, *cast = pl.pallas_call(
        functools.partial(_layer_prompt_kernel, nseq=nseq, n_tiles=n_tiles, final_norm=final_norm,
                          n_cast=len(cast_next), sample_seq_major=sample_seq_major, layer=l),
        grid=(n_tiles + 2,),
        in_specs=([x_spec, xs_spec] + [_param_spec(p, k, l) for k in _PROMPT_PARAMS] + cast_in_specs
                  + [_const_spec(a.shape) for a in prev]),
        out_specs=([y_spec, xs_spec, _const_spec((l + 1, nseq, D_LRU)),
                    _const_spec((l + 1, CONV_W - 1, nseq, D_LRU))] + cast_out_specs),
        out_shape=[jax.ShapeDtypeStruct(x.shape, jnp.float32),
                   jax.ShapeDtypeStruct(xs.shape, jnp.float32),
                   jax.ShapeDtypeStruct((l + 1, nseq, D_LRU), jnp.float32),
                   jax.ShapeDtypeStruct((l + 1, CONV_W - 1, nseq, D_LRU), jnp.float32)] + cast_shapes,
        scratch_shapes=([pltpu.VMEM((2, nseq, HALF, D_SGU), jnp.bfloat16),
                         pltpu.VMEM((2, N_LANE_GROUPS, CHUNK, CHUNK), jnp.bfloat16),
                         pltpu.VMEM((2, rows, D_MODEL), jnp.float32),
                         pltpu.VMEM((N_GATE_TILES, GATE_TILE, 2 * GATE_TILE), jnp.bfloat16)]
                        + [pltpu.VMEM((nseq * PITCH, LANES), jnp.float32)] * N_SLABS
                        + [pltpu.VMEM((2, CONV_W - 1, nseq, LANES), jnp.float32)] * N_SLABS
                        + [pltpu.VMEM((nseq, LANES), jnp.float32)] * N_SLABS),
        compiler_params=pltpu.CompilerParams(dimension_semantics=("arbitrary",), vmem_limit_bytes=VMEM_LIMIT),
        name="layer_prompt",
    )(x, xs, *consts, *cast_next, *prev)
    return y, ys, (hlast, convnew), cast


def _mixer_sample_kernel(*refs, steps, nb, seq_major, n_cast, layer):
    n_in = 3 + len(_SAMPLE_MIXER_PARAMS)
    (x_ref, cbuf_ref, h0_ref, rows_ref, w_in_ref, conv_w_ref, gate_rw_ref, gate_iw_ref, sgu_w8_ref, sgu_b8_ref,
     w_out_ref) = refs[:n_in]
    cast_in, refs = refs[n_in:n_in + n_cast], refs[n_in + n_cast:]
    n_prev = 3 if layer else 0
    prev_refs, refs = refs[:n_prev], refs[n_prev:]
    y_ref, v_ref, hlast_ref, convnew_ref = refs[:4]
    cast_out, refs = refs[4:4 + n_cast], refs[4 + n_cast:]
    gate_sc, v_relay_sc, relay_sc = refs[0], refs[1:1 + N_LANE_GROUPS], refs[1 + N_LANE_GROUPS:]
    for dst, prev in zip((v_ref, hlast_ref, convnew_ref), prev_refs):
        dst[0:layer] = prev[...]
    for w_f32, w_bf16 in zip(cast_in, cast_out):
        w_bf16[...] = w_f32[...].astype(jnp.bfloat16)
    (g_mix_ref, _, gate_rb_ref, gate_ib_ref, lam_ref, conv_b_ref, sgu_g_ref, sgu_bn_ref, _) = _row_views(rows_ref)

    @pl.when(pl.program_id(0) == 0)
    def _():
        _build_gate_tiles(gate_rw_ref, gate_iw_ref, gate_sc)

    if seq_major:
        for c in range(D_MODEL // LANES):
            relay_sc[c][...] = x_ref[:, c * LANES:(c + 1) * LANES]
        x = jnp.concatenate(
            [jnp.concatenate([relay_sc[c][pl.ds(t, nb, stride=steps), :] for c in range(D_MODEL // LANES)], axis=1)
             for t in range(steps)], axis=0)
    else:
        x = x_ref[...].reshape(steps * nb, D_MODEL)
    xn = _rms_norm(x, g_mix_ref[...]).astype(jnp.bfloat16)

    u = _gelu(_dot(xn, w_in_ref[:, 0:D_SGU]))
    v = _layer_norm(_gelu(_dot(xn, w_in_ref[:, D_SGU:2 * D_SGU])), sgu_g_ref[...], sgu_bn_ref[...])
    pitch = nb + SUBLANES
    for g in range(N_LANE_GROUPS):
        for t in range(steps):
            v_relay_sc[g][t * pitch:t * pitch + nb, :] = v[t * nb:(t + 1) * nb, g * LANES:(g + 1) * LANES]
    for s in range(nb):
        for g in range(N_LANE_GROUPS):
            v_ref[layer, s, :, g * LANES:(g + 1) * LANES] = v_relay_sc[g][pl.ds(s, steps, stride=pitch), :]
    gate_rows = []
    for t in range(steps):
        acc = sgu_b8_ref[t:t + 1, :] + sgu_w8_ref[t, 0:1, :] * v[0:nb]
        for s in range(1, t + 1):
            acc = acc + sgu_w8_ref[t, s:s + 1, :] * v[s * nb:(s + 1) * nb]
        gate_rows.append(acc)
    out_a = (u * jnp.concatenate(gate_rows, axis=0)).astype(jnp.bfloat16)

    xb = _dot(xn, w_in_ref[:, 2 * D_SGU:2 * D_SGU + D_LRU])
    xp = jnp.concatenate([cbuf_ref[k] for k in range(CONV_W - 1)] + [xb], axis=0)
    xc = conv_b_ref[...] + conv_w_ref[0:1, :] * xp[0:steps * nb]
    for k in range(1, CONV_W):
        xc = xc + conv_w_ref[k:k + 1, :] * xp[k * nb:(k + steps) * nb]
    convnew_ref[layer] = xp[steps * nb:].reshape(CONV_W - 1, nb, D_LRU)

    a, b = _lru_coeffs(xc, gate_sc, gate_rb_ref[...], gate_ib_ref[...], lam_ref[...])
    h = h0_ref[...]
    hs = []
    for t in range(steps):
        h = a[t * nb:(t + 1) * nb] * h + b[t * nb:(t + 1) * nb]
        hs.append(h)
    hlast_ref[layer] = h

    yb = _dot(xn, w_in_ref[:, 2 * D_SGU + D_LRU:])
    out_b = (jnp.concatenate(hs, axis=0) * _gelu(yb)).astype(jnp.bfloat16)
    out = _dot(jnp.concatenate([out_a, out_b], axis=1), w_out_ref[...])
    y_ref[...] = (x + out).reshape(steps, nb, D_MODEL)


def _mixer_sample(x, h0_all, cbuf_all, p, l, *, seq_major, groups, cast=(), prev=()):
    assert len(prev) == (3 if l else 0)
    nb_all = h0_all.shape[1]
    steps = x.shape[0] // nb_all if seq_major else x.shape[1]
    nb = nb_all // groups
    cast_in_specs, cast_out_specs, cast_shapes = [], [], []
    for w in cast:
        k_rows, n_cols = w.shape[1:]
        blk = k_rows // groups
        assert blk * groups == k_rows and blk % (2 * SUBLANES) == 0
        cast_in_specs.append(pl.BlockSpec((None, blk, n_cols), lambda i: (l, i, 0)))
        cast_out_specs.append(pl.BlockSpec((None, blk, n_cols), lambda i: (0, i, 0)))
        cast_shapes.append(jax.ShapeDtypeStruct((1, k_rows, n_cols), jnp.bfloat16))
    per_tile = HALF // nb
    assert per_tile * nb == HALF and nb_all % HALF == 0
    act_spec = pl.BlockSpec((None, steps, nb, D_MODEL), lambda i: (i // per_tile, 0, i % per_tile, 0))
    if seq_major:
        x_spec = pl.BlockSpec((nb * steps, D_MODEL), lambda i: (i, 0))
        relay = [pltpu.VMEM((nb * steps, LANES), jnp.float32)] * (D_MODEL // LANES)
    else:
        x_spec = act_spec
        relay = []
    state_specs = lambda layers: [pl.BlockSpec((layers, nb, steps, D_SGU), lambda i: (0, i, 0, 0)),
                                  pl.BlockSpec((layers, nb, D_LRU), lambda i: (0, i, 0)),
                                  pl.BlockSpec((layers, CONV_W - 1, nb, D_LRU), lambda i: (0, 0, i, 0))]
    out_shapes = [(nb_all // HALF, steps, HALF, D_MODEL), (l + 1, nb_all, steps, D_SGU), (l + 1, nb_all, D_LRU),
                  (l + 1, CONV_W - 1, nb_all, D_LRU)]
    v_relay = [pltpu.VMEM((steps * (nb + SUBLANES), LANES), jnp.float32)] * N_LANE_GROUPS
    y, v, hlast, convnew, *cast_bf16 = pl.pallas_call(
        functools.partial(_mixer_sample_kernel, steps=steps, nb=nb, seq_major=seq_major, n_cast=len(cast),
                          layer=l),
        grid=(groups,),
        in_specs=([x_spec,
                   pl.BlockSpec((None, CONV_W - 1, nb, D_LRU), lambda i: (l, 0, i, 0)),
                   pl.BlockSpec((None, nb, D_LRU), lambda i: (l, i, 0))]
                  + [_param_spec(p, k, l) for k in _SAMPLE_MIXER_PARAMS] + cast_in_specs
                  + (state_specs(l) if l else [])),
        out_specs=[act_spec] + state_specs(l + 1) + cast_out_specs,
        out_shape=[jax.ShapeDtypeStruct(s, jnp.float32) for s in out_shapes] + cast_shapes,
        scratch_shapes=[pltpu.VMEM((N_GATE_TILES, GATE_TILE, 2 * GATE_TILE), jnp.bfloat16)] + v_relay + relay,
        compiler_params=pltpu.CompilerParams(dimension_semantics=("arbitrary",), vmem_limit_bytes=VMEM_LIMIT),
        name="mixer_sample",
    )(x, cbuf_all, h0_all, *[p[k] for k in _SAMPLE_MIXER_PARAMS], *cast, *prev)
    return y, (v, hlast, convnew), cast_bf16


CAST_STEPS = 4


def _cast_kernel(*refs):
    n = len(refs) // 2
    for w_f32, w_bf16 in zip(refs[:n], refs[n:]):
        w_bf16[...] = w_f32[...].astype(jnp.bfloat16)


def _cast_layer(weights, l):
    in_specs, out_specs, out_shapes = [], [], []
    for w in weights:
        k_rows, n_cols = w.shape[1:]
        blk = k_rows // CAST_STEPS
        assert blk * CAST_STEPS == k_rows and blk % (2 * SUBLANES) == 0
        in_specs.append(pl.BlockSpec((None, blk, n_cols), lambda j: (l, j, 0)))
        out_specs.append(pl.BlockSpec((None, blk, n_cols), lambda j: (0, j, 0)))
        out_shapes.append(jax.ShapeDtypeStruct((1, k_rows, n_cols), jnp.bfloat16))
    return pl.pallas_call(
        _cast_kernel,
        grid=(CAST_STEPS,),
        in_specs=in_specs,
        out_specs=out_specs,
        out_shape=out_shapes,
        compiler_params=pltpu.CompilerParams(dimension_semantics=("arbitrary",), vmem_limit_bytes=VMEM_LIMIT),
        name="cast_weights",
    )(*weights)


def _prepare_params(steps, nseq, norm_mix_g, w_in, conv_w, conv_b, gate_r_w, gate_r_b, gate_i_w, gate_i_b,
                    lru_lambda, sgu_norm_g, sgu_norm_b, sgu_w, sgu_b, w_out, norm_mlp_g, mlp_w1, mlp_w2, final_norm_g):
    rows = [None] * N_ROWS
    rows[ROW_G_MIX], rows[ROW_G_MLP] = norm_mix_g, norm_mlp_g
    rows[ROW_GATE_RB], rows[ROW_GATE_IB], rows[ROW_LAM], rows[ROW_CONV_B] = gate_r_b, gate_i_b, lru_lambda, conv_b
    rows[ROW_SGU] = jnp.concatenate([sgu_norm_g, sgu_norm_b], axis=-1)
    rows[ROW_G_FINAL] = jnp.broadcast_to(final_norm_g, (DEPTH, D_MODEL))
    k_idx = lax.broadcasted_iota(jnp.int32, (DEPTH, N_ROWS, D_MODEL), 1)
    packed = jnp.zeros((DEPTH, N_ROWS, D_MODEL), jnp.float32)
    for k, r in enumerate(rows):
        packed = jnp.where(k_idx == k, r[:, None, :], packed)
    return {
        "rows": packed,
        "conv_w": conv_w,
        "gate_rw": gate_r_w,
        "gate_iw": gate_i_w,
        "sgu_w": sgu_w,
        "sgu_bias": jnp.repeat(jnp.transpose(sgu_b, (0, 2, 1)), SGU_HEAD_DIM, axis=2),
        "sgu_w8": jnp.repeat(jnp.transpose(sgu_w[:, :, :steps, :steps], (0, 2, 3, 1)), SGU_HEAD_DIM, axis=3),
        "sgu_b8": jnp.repeat(jnp.transpose(sgu_b[:, :, :steps], (0, 2, 1)), SGU_HEAD_DIM, axis=2),
    }


def kernel(x_prompt, x_sample, state_lru_h, state_conv, norm_mix_g, w_in, conv_w, conv_b, gate_r_w, gate_r_b, gate_i_w, gate_i_b, lru_lambda, sgu_norm_g, sgu_norm_b, sgu_w, sgu_b, w_out, norm_mlp_g, mlp_w1, mlp_w2, final_norm_g):
    nseq, seq, _ = x_prompt.shape
    nb, steps, _ = x_sample.shape
    assert seq % CHUNK == 0 and nseq == SUBLANES and steps == nseq and nb == 2 * HALF
    p = _prepare_params(steps, nseq, norm_mix_g, w_in, conv_w, conv_b, gate_r_w, gate_r_b, gate_i_w, gate_i_b,
                        lru_lambda, sgu_norm_g, sgu_norm_b, sgu_w, sgu_b, w_out, norm_mlp_g, mlp_w1, mlp_w2,
                        final_norm_g)

    xp = x_prompt
    xs = x_sample.reshape(nb * steps, D_MODEL)
    cbuf_tm = jnp.transpose(state_conv, (0, 2, 1, 3))
    prompt_state, sample_state = (), ()
    big_f32 = (w_in, w_out, mlp_w1, mlp_w2)
    big = _cast_layer(big_f32[:2], 0)
    for l in range(DEPTH):
        last = l == DEPTH - 1
        first = l == 0
        p.update(zip(_BIG_WEIGHTS, big))
        xs, sample_state, mlp_bf16 = _mixer_sample(
            xs, state_lru_h, cbuf_tm, p, l, seq_major=first, prev=sample_state,
            groups=SAMPLE_GROUPS_CAST if first else SAMPLE_GROUPS, cast=big_f32[2:] if first else ())
        p.update(zip(_BIG_WEIGHTS[2:], mlp_bf16))
        xp, xs, prompt_state, big = _layer_prompt(xp, xs.reshape(steps * nb, D_MODEL), p, l, final_norm=last,
                                                  sample_seq_major=last, cast_next=() if last else big_f32,
                                                  prev=prompt_state)
        if not last:
            xs = xs.reshape(nb // HALF, steps, HALF, D_MODEL)

    y_sample = xs.reshape(nb, steps, D_MODEL)
    h_p, conv_p = prompt_state
    v_s, h_s, conv_s = sample_state
    return (xp, y_sample, h_p, jnp.transpose(conv_p, (0, 2, 1, 3)), h_s, jnp.transpose(conv_s, (0, 2, 1, 3)), v_s)
```

```python
import functools

import jax
import jax.numpy as jnp
from jax import lax
from jax.experimental import pallas as pl
from jax.experimental.pallas import tpu as pltpu

D_MODEL = 1024
DEPTH = 2
SGU_HEADS = 8
SGU_HEAD_DIM = 64
D_SGU = SGU_HEADS * SGU_HEAD_DIM
CHUNK = 128
D_LRU = 1024
LRU_BLOCKS = 16
LRU_BLOCK_DIM = 64
CONV_W = 4
LRU_C = 8.0
D_FF = 4 * D_MODEL
EPS = 1e-6

SUBLANES = 8
LANES = 128
GATE_TILE = 256
N_GATE_TILES = D_LRU // GATE_TILE
HEADS_PER_LANE_GROUP = LANES // SGU_HEAD_DIM
N_LANE_GROUPS = D_SGU // LANES
N_SLABS = D_LRU // LANES
FF_CHUNK = 2 * D_MODEL
N_FF_CHUNKS = D_FF // FF_CHUNK

SAMPLE_GROUPS = 2
SAMPLE_GROUPS_CAST = 4
HALF = CHUNK // 2
PITCH = HALF + SUBLANES
V7X_VMEM_BYTES = 64 * 1024 * 1024
VMEM_HEADROOM = 2 * 1024 * 1024
VMEM_LIMIT = V7X_VMEM_BYTES - VMEM_HEADROOM
F32_TINY = 1.1754944e-38


def _rms_norm(x, g):
    return x * lax.rsqrt(jnp.mean(x * x, axis=-1, keepdims=True) + EPS) * g


def _layer_norm(x, g, b):
    mu = jnp.mean(x, axis=-1, keepdims=True)
    xc = x - mu
    return xc * lax.rsqrt(jnp.mean(xc * xc, axis=-1, keepdims=True) + EPS) * g + b


def _gelu(x):
    return jax.nn.gelu(x, approximate=True)


def _dot(a, b):
    return jnp.dot(a, b, preferred_element_type=jnp.float32)


def _lru_elementwise(r_pre, i_pre, xc, half_rb, half_ib, half_c):
    th_r = jnp.tanh(r_pre + half_rb)
    th_i = jnp.tanh(i_pre + half_ib)
    log_a = half_c * th_r + half_c
    i = 0.5 * th_i + 0.5
    a = jnp.exp(log_a)
    y = jnp.tanh(log_a) * (-1.0 - a * a)
    b = (y * lax.rsqrt(jnp.maximum(y, F32_TINY))) * (i * xc)
    return a, b


def _lru_consts(gate_rb, gate_ib, lam):
    return 0.5 * gate_rb, 0.5 * gate_ib, (0.5 * LRU_C) * jax.nn.log_sigmoid(lam)


def _build_gate_tiles(gate_rw_ref, gate_iw_ref, gate_sc):
    per_tile = GATE_TILE // LRU_BLOCK_DIM
    k_idx = lax.broadcasted_iota(jnp.int32, (LRU_BLOCK_DIM, GATE_TILE), 0)
    n_idx = lax.broadcasted_iota(jnp.int32, (LRU_BLOCK_DIM, GATE_TILE), 1)
    replicate = (n_idx % LRU_BLOCK_DIM == k_idx).astype(jnp.bfloat16)
    row_blk = lax.broadcasted_iota(jnp.int32, (GATE_TILE, GATE_TILE), 0) // LRU_BLOCK_DIM
    col_blk = lax.broadcasted_iota(jnp.int32, (GATE_TILE, GATE_TILE), 1) // LRU_BLOCK_DIM
    for j in range(N_GATE_TILES):
        for k, ref in enumerate((gate_rw_ref, gate_iw_ref)):
            stacked = ref[per_tile * j:per_tile * (j + 1)].reshape(GATE_TILE, LRU_BLOCK_DIM)
            rep = _dot(stacked.astype(jnp.bfloat16), replicate)
            gate_sc[j, :, k * GATE_TILE:(k + 1) * GATE_TILE] = jnp.where(
                row_blk == col_blk, 0.5 * rep, 0.0).astype(jnp.bfloat16)


def _lru_coeffs(xc, gate_w_ref, gate_rb, gate_ib, lam):
    xcb = xc.astype(jnp.bfloat16)
    r_parts, i_parts = [], []
    for j in range(N_GATE_TILES):
        ri = _dot(xcb[:, j * GATE_TILE:(j + 1) * GATE_TILE], gate_w_ref[j])
        r_parts.append(ri[:, :GATE_TILE])
        i_parts.append(ri[:, GATE_TILE:])
    return _lru_elementwise(jnp.concatenate(r_parts, axis=1), jnp.concatenate(i_parts, axis=1), xc,
                            *_lru_consts(gate_rb, gate_ib, lam))


ROW_G_MIX, ROW_G_MLP, ROW_GATE_RB, ROW_GATE_IB, ROW_LAM, ROW_CONV_B, ROW_SGU, ROW_G_FINAL = range(8)
N_ROWS = 8


def _row_views(rows_ref):
    one = lambda k, lo=0, hi=D_MODEL: rows_ref.at[k:k + 1, lo:hi]
    return (one(ROW_G_MIX), one(ROW_G_MLP), one(ROW_GATE_RB), one(ROW_GATE_IB), one(ROW_LAM), one(ROW_CONV_B),
            one(ROW_SGU, 0, D_SGU), one(ROW_SGU, D_SGU, 2 * D_SGU), one(ROW_G_FINAL))


def _mlp_up(xn, w1_ref, c):
    cols = slice(c * FF_CHUNK, (c + 1) * FF_CHUNK)
    return jnp.square(jnp.maximum(_dot(xn, w1_ref[:, cols]), 0.0)).astype(jnp.bfloat16)


def _mlp_down(hid, w2_ref, c):
    return _dot(hid, w2_ref[c * FF_CHUNK:(c + 1) * FF_CHUNK, :])


def _layer_prompt_kernel(*refs, nseq, n_tiles, final_norm, n_cast, sample_seq_major, layer):
    n_in = 2 + len(_PROMPT_PARAMS)
    (x_ref, xs_ref, rows_ref, w_in_ref, conv_w_ref, gate_rw_ref, gate_iw_ref, sgu_w_ref, sgu_bias_ref, w_out_ref,
     w1_ref, w2_ref) = refs[:n_in]
    (g_mix_ref, g_mlp_ref, gate_rb_ref, gate_ib_ref, lam_ref, conv_b_ref, sgu_g_ref, sgu_bn_ref,
     gf_ref) = _row_views(rows_ref)
    cast_in, refs = refs[n_in:n_in + n_cast], refs[n_in + n_cast:]
    n_prev = 2 if layer else 0
    prev_refs, refs = refs[:n_prev], refs[n_prev:]
    y_ref, ys_ref, hlast_ref, convnew_ref = refs[:4]
    cast_out, refs = refs[4:4 + n_cast], refs[4 + n_cast:]
    vprev_sc, sgu_lhs_sc, ymix_sc, gate_sc = refs[:4]
    slab_scratch = refs[4:]
    rows = nseq * HALF
    j = pl.program_id(0)
    slab_sc = slab_scratch[0:N_SLABS]
    tail_sc = slab_scratch[N_SLABS:2 * N_SLABS]
    h_sc = slab_scratch[2 * N_SLABS:3 * N_SLABS]

    @pl.when(j == 0)
    def _():
        for c in range(N_SLABS):
            tail_sc[c][...] = jnp.zeros_like(tail_sc[c])
            h_sc[c][...] = jnp.zeros_like(h_sc[c])
        vprev_sc[...] = jnp.zeros_like(vprev_sc)
        ymix_sc[1] = xs_ref[...]
        _build_gate_tiles(gate_rw_ref, gate_iw_ref, gate_sc)
        t_idx = lax.broadcasted_iota(jnp.int32, (CHUNK, CHUNK), 0)
        s_idx = lax.broadcasted_iota(jnp.int32, (CHUNK, CHUNK), 1)
        for g in range(N_LANE_GROUPS):
            first, second = [], []
            for hh in range(HEADS_PER_LANE_GROUP):
                wm = jnp.where(s_idx <= t_idx, sgu_w_ref[HEADS_PER_LANE_GROUP * g + hh], 0.0)
                first.append(pltpu.roll(wm[:HALF], HALF, axis=1))
                second.append(wm[HALF:])
            sgu_lhs_sc[0, g] = jnp.concatenate(first, axis=0).astype(jnp.bfloat16)
            sgu_lhs_sc[1, g] = jnp.concatenate(second, axis=0).astype(jnp.bfloat16)

    parity = j % 2

    @pl.when(j < n_tiles)
    def _():
        _fused_step(x_ref, g_mix_ref, w_in_ref, conv_w_ref, conv_b_ref, gate_rb_ref, gate_ib_ref, lam_ref, sgu_g_ref, sgu_bn_ref,
                    sgu_bias_ref, w_out_ref, g_mlp_ref, w1_ref, w2_ref, gf_ref, y_ref, vprev_sc, sgu_lhs_sc, ymix_sc,
                    gate_sc, slab_sc, tail_sc, h_sc, parity=parity, nseq=nseq, final_norm=final_norm)
        for w_f32, w_bf16 in zip(cast_in, cast_out):
            w_bf16[...] = w_f32[...].astype(jnp.bfloat16)

    def store_sample(tile):
        if not sample_seq_major:
            ys_ref[...] = tile
            return
        for c in range(N_SLABS):
            for t in range(nseq):
                slab_sc[c][t * PITCH:t * PITCH + HALF, :] = tile[t * HALF:(t + 1) * HALF, c * LANES:(c + 1) * LANES]
        for s in range(HALF):
            for c in range(N_SLABS):
                ys_ref[s * nseq:(s + 1) * nseq, c * LANES:(c + 1) * LANES] = (
                    slab_sc[c][pl.ds(s, nseq, stride=PITCH), :])

    @pl.when(j == 0)
    def _():
        store_sample(y_ref[...].reshape(rows, D_MODEL))

    @pl.when(j == n_tiles - 1)
    def _():
        for c in range(N_SLABS):
            cols = slice(c * LANES, (c + 1) * LANES)
            hlast_ref[layer, :, cols] = h_sc[c][...]
            for k in range(CONV_W - 1):
                convnew_ref[layer, CONV_W - 2 - k, :, cols] = tail_sc[c][parity, k]
        for dst, prev in zip((hlast_ref, convnew_ref), prev_refs):
            dst[0:layer] = prev[...]

    @pl.when(j == n_tiles + 1)
    def _():
        ymix_sc[1 - parity] = xs_ref[...]

    @pl.when(j >= n_tiles)
    def _():
        xm = ymix_sc[1 - parity]
        xn = _rms_norm(xm, g_mlp_ref[...]).astype(jnp.bfloat16)
        acc = xm
        for c in range(N_FF_CHUNKS):
            acc = acc + _mlp_down(_mlp_up(xn, w1_ref, c), w2_ref, c)
        ymix_sc[parity] = _rms_norm(acc, gf_ref[...]) if final_norm else acc

    @pl.when(j == n_tiles)
    def _():
        y_ref[...] = ymix_sc[parity].reshape(nseq, HALF, D_MODEL)

    @pl.when(j == n_tiles + 1)
    def _():
        store_sample(ymix_sc[parity])


def _fused_step(x_ref, g_mix_ref, w_in_ref, conv_w_ref, conv_b_ref, gate_rb_ref, gate_ib_ref, lam_ref, sgu_g_ref, sgu_bn_ref,
                sgu_bias_ref, w_out_ref, g_mlp_ref, w1_ref, w2_ref, gf_ref, y_ref, vprev_sc, sgu_lhs_sc, ymix_sc,
                gate_sc, slab_sc, tail_sc, h_sc, *, parity, nseq, final_norm):
    rows = nseq * HALF
    half_rb, half_ib, half_c = _lru_consts(gate_rb_ref[...], gate_ib_ref[...], lam_ref[...])
    xb_col0 = 2 * D_SGU
    yb_col0 = 2 * D_SGU + D_LRU
    slabs_per_tile = GATE_TILE // LANES
    st = [{} for _ in range(N_GATE_TILES)]
    sg = {}
    ml = {}

    def mlp_norm():
        xm = ymix_sc[1 - parity]
        ml["acc"] = xm
        ml["xn"] = _rms_norm(xm, g_mlp_ref[...]).astype(jnp.bfloat16)

    def mlp_up(c):
        ml["hid", c] = _mlp_up(ml["xn"], w1_ref, c)

    def mlp_down(c):
        ml["acc"] = ml["acc"] + _mlp_down(ml.pop(("hid", c)), w2_ref, c)

    def mlp_store():
        out = _rms_norm(ml["acc"], gf_ref[...]) if final_norm else ml["acc"]
        y_ref[...] = out.reshape(nseq, HALF, D_MODEL)

    def mix_norm():
        sg["x"] = x_ref[...].reshape(rows, D_MODEL)
        sg["xn"] = _rms_norm(sg["x"], g_mix_ref[...]).astype(jnp.bfloat16)

    def lru_proj(q):
        xb = _dot(sg["xn"], w_in_ref[:, xb_col0 + q * GATE_TILE:xb_col0 + (q + 1) * GATE_TILE])
        for i in range(slabs_per_tile):
            for s in range(nseq):
                slab_sc[slabs_per_tile * q + i][s * PITCH:s * PITCH + HALF, :] = (
                    xb[s * HALF:(s + 1) * HALF, i * LANES:(i + 1) * LANES])

    def lru_conv(q):
        xc_slabs = []
        for c in range(slabs_per_tile * q, slabs_per_tile * (q + 1)):
            lanes = slice(c * LANES, (c + 1) * LANES)
            taps = [jnp.broadcast_to(conv_w_ref[k:k + 1, lanes], (nseq, LANES)) for k in range(CONV_W)]
            taps.append(jnp.broadcast_to(conv_b_ref[:, lanes], (nseq, LANES)))
            p1, p2, p3 = (tail_sc[c][1 - parity, k] for k in range(CONV_W - 1))
            steps_out = []
            for t in range(HALF):
                cur = slab_sc[c][pl.ds(t, nseq, stride=PITCH), :]
                steps_out.append(taps[CONV_W] + taps[3] * cur + taps[2] * p1 + taps[1] * p2 + taps[0] * p3)
                p1, p2, p3 = cur, p1, p2
            for k, pk in enumerate((p1, p2, p3)):
                tail_sc[c][parity, k] = pk
            xc_slabs.append(jnp.concatenate(steps_out, axis=0))
        st[q]["xc"] = jnp.concatenate(xc_slabs, axis=1)

    def lru_gates(q):
        st[q]["ri"] = _dot(st[q]["xc"].astype(jnp.bfloat16), gate_sc[q])

    def lru_coeffs(q):
        cols = slice(q * GATE_TILE, (q + 1) * GATE_TILE)
        ri = st[q].pop("ri")
        st[q]["ab"] = _lru_elementwise(ri[:, :GATE_TILE], ri[:, GATE_TILE:], st[q].pop("xc"),
                                       half_rb[:, cols], half_ib[:, cols], half_c[:, cols])

    def lru_scan(q):
        a, b = st[q].pop("ab")
        for i in range(slabs_per_tile):
            c = slabs_per_tile * q + i
            h = h_sc[c][...]
            for t in range(HALF):
                h = (a[t * nseq:(t + 1) * nseq, i * LANES:(i + 1) * LANES] * h
                     + b[t * nseq:(t + 1) * nseq, i * LANES:(i + 1) * LANES])
                slab_sc[c][pl.ds(t, nseq, stride=PITCH), :] = h
            h_sc[c][...] = h

    def lru_out(q):
        hs = jnp.concatenate(
            [jnp.concatenate([slab_sc[c][s * PITCH:s * PITCH + HALF, :]
                              for c in range(slabs_per_tile * q, slabs_per_tile * (q + 1))], axis=1)
             for s in range(nseq)], axis=0)
        yb = _dot(sg["xn"], w_in_ref[:, yb_col0 + q * GATE_TILE:yb_col0 + (q + 1) * GATE_TILE])
        st[q]["out_b"] = (hs * _gelu(yb)).astype(jnp.bfloat16)

    def lru_mix(q):
        r0 = D_SGU + q * GATE_TILE
        out_b = jnp.concatenate([st[q].pop("out_b"), st[q + 1].pop("out_b")], axis=1)
        sg["acc"] = sg["acc"] + _dot(out_b, w_out_ref[r0:r0 + 2 * GATE_TILE, :])

    def sgu_u():
        sg["u"] = _gelu(_dot(sg["xn"], w_in_ref[:, 0:D_SGU]))

    def sgu_v():
        v = _layer_norm(_gelu(_dot(sg["xn"], w_in_ref[:, D_SGU:2 * D_SGU])), sgu_g_ref[...], sgu_bn_ref[...])
        sg["vb"] = v.astype(jnp.bfloat16)

    def sgu_gate():
        vb = sg.pop("vb")
        lane = lax.broadcasted_iota(jnp.int32, (HALF, LANES), 1)
        bias = sgu_bias_ref[pl.ds(pl.multiple_of(parity * HALF, HALF), HALF), :]
        v_full = [jnp.concatenate([vprev_sc[1 - parity, s], vb[s * HALF:(s + 1) * HALF]], axis=0)
                  for s in range(nseq)]
        vprev_sc[parity] = vb.reshape(nseq, HALF, D_SGU)
        gate_rows = [[] for _ in range(nseq)]
        for s in range(0, nseq, 2):
            for g in range(N_LANE_GROUPS):
                lanes = slice(g * LANES, (g + 1) * LANES)
                rhs = jnp.concatenate([v_full[s][:, lanes], v_full[s + 1][:, lanes]], axis=1)
                res = _dot(sgu_lhs_sc[parity, g], rhs)
                for i in range(2):
                    part = res[:, i * LANES:(i + 1) * LANES]
                    gate_rows[s + i].append(jnp.where(lane < SGU_HEAD_DIM, part[:HALF], part[HALF:]))
        gate = jnp.concatenate([jnp.concatenate(r, axis=1) + bias for r in gate_rows], axis=0)
        sg["out_a"] = (sg.pop("u") * gate).astype(jnp.bfloat16)

    def sgu_mix():
        sg["acc"] = sg["x"] + _dot(sg.pop("out_a"), w_out_ref[0:D_SGU, :])

    def mix_store():
        ymix_sc[parity] = sg["acc"]

    tiles = range(N_GATE_TILES)
    mlp_norm()
    mix_norm()
    for q in tiles:
        lru_proj(q)
    first, second = range(N_FF_CHUNKS // 2), range(N_FF_CHUNKS // 2, N_FF_CHUNKS)
    mlp_up(first[0])
    for q in tiles:
        lru_conv(q)
        lru_gates(q)
    for c in first[1:]:
        mlp_up(c)
    sgu_u()
    sgu_v()
    for q in tiles:
        lru_coeffs(q)
        lru_scan(q)
    for c in first:
        mlp_down(c)
    for q in tiles:
        lru_out(q)
    for c in second:
        mlp_up(c)
    sgu_gate()
    sgu_mix()
    for c in second:
        mlp_down(c)
    for q in tiles[::2]:
        lru_mix(q)
    mix_store()
    mlp_store()


def _const_spec(shape):
    nd = len(shape)
    return pl.BlockSpec(shape, lambda *_: (0,) * nd, pipeline_mode=pl.Buffered(1))


def _layer_spec(arr, l):
    nd = arr.ndim - 1
    return pl.BlockSpec((None,) + arr.shape[1:], lambda *_: (l,) + (0,) * nd, pipeline_mode=pl.Buffered(1))


_BIG_WEIGHTS = ("w_in", "w_out", "w1", "w2")
_PROMPT_PARAMS = ("rows", "w_in", "conv_w", "gate_rw", "gate_iw", "sgu_w", "sgu_bias", "w_out", "w1", "w2")
_SAMPLE_MIXER_PARAMS = ("rows", "w_in", "conv_w", "gate_rw", "gate_iw", "sgu_w8", "sgu_b8", "w_out")


def _param_spec(p, k, l):
    return _layer_spec(p[k], 0 if k in _BIG_WEIGHTS else l)


def _layer_prompt(x, xs, p, l, *, final_norm, sample_seq_major, cast_next=(), prev=()):
    assert len(prev) == (2 if l else 0)
    nseq, seq, _ = x.shape
    n_tiles = seq // HALF
    rows = nseq * HALF
    assert xs.shape == (2 * rows, D_MODEL)
    consts = [p[k] for k in _PROMPT_PARAMS]
    cast_in_specs, cast_out_specs, cast_shapes = [], [], []
    for w in cast_next:
        k_rows, n_cols = w.shape[1:]
        blk = k_rows // n_tiles
        assert blk * n_tiles == k_rows and blk % (2 * SUBLANES) == 0
        cast_in_specs.append(pl.BlockSpec((None, blk, n_cols), lambda j: (l + 1, jnp.minimum(j, n_tiles - 1), 0)))
        cast_out_specs.append(pl.BlockSpec((None, blk, n_cols), lambda j: (0, jnp.minimum(j, n_tiles - 1), 0)))
        cast_shapes.append(jax.ShapeDtypeStruct((1, k_rows, n_cols), jnp.bfloat16))
    x_spec = pl.BlockSpec((nseq, HALF, D_MODEL), lambda j: (0, jnp.minimum(j, n_tiles - 1), 0))
    y_spec = pl.BlockSpec((nseq, HALF, D_MODEL), lambda j: (0, jnp.clip(j - 1, 0, n_tiles - 1), 0))
    xs_spec = pl.BlockSpec((rows, D_MODEL), lambda j: (jnp.where(j > n_tiles, 1, 0), 0),
                           pipeline_mode=pl.Buffered(1))
    y, ys, hlast, convnew, *cast = pl.pallas_call(
        functools.partial(_layer_prompt_kernel, nseq=nseq, n_tiles=n_tiles, final_norm=final_norm,
                          n_cast=len(cast_next), sample_seq_major=sample_seq_major, layer=l),
        grid=(n_tiles + 2,),
        in_specs=([x_spec, xs_spec] + [_param_spec(p, k, l) for k in _PROMPT_PARAMS] + cast_in_specs
                  + [_const_spec(a.shape) for a in prev]),
        out_specs=([y_spec, xs_spec, _const_spec((l + 1, nseq, D_LRU)),
                    _const_spec((l + 1, CONV_W - 1, nseq, D_LRU))] + cast_out_specs),
        out_shape=[jax.ShapeDtypeStruct(x.shape, jnp.float32),
                   jax.ShapeDtypeStruct(xs.shape, jnp.float32),
                   jax.ShapeDtypeStruct((l + 1, nseq, D_LRU), jnp.float32),
                   jax.ShapeDtypeStruct((l + 1, CONV_W - 1, nseq, D_LRU), jnp.float32)] + cast_shapes,
        scratch_shapes=([pltpu.VMEM((2, nseq, HALF, D_SGU), jnp.bfloat16),
                         pltpu.VMEM((2, N_LANE_GROUPS, CHUNK, CHUNK), jnp.bfloat16),
                         pltpu.VMEM((2, rows, D_MODEL), jnp.float32),
                         pltpu.VMEM((N_GATE_TILES, GATE_TILE, 2 * GATE_TILE), jnp.bfloat16)]
                        + [pltpu.VMEM((nseq * PITCH, LANES), jnp.float32)] * N_SLABS
                        + [pltpu.VMEM((2, CONV_W - 1, nseq, LANES), jnp.float32)] * N_SLABS
                        + [pltpu.VMEM((nseq, LANES), jnp.float32)] * N_SLABS),
        compiler_params=pltpu.CompilerParams(dimension_semantics=("arbitrary",), vmem_limit_bytes=VMEM_LIMIT),
        name="layer_prompt",
    )(x, xs, *consts, *cast_next, *prev)
    return y, ys, (hlast, convnew), cast


def _mixer_sample_kernel(*refs, steps, nb, seq_major, n_cast, layer):
    n_in = 3 + len(_SAMPLE_MIXER_PARAMS)
    (x_ref, cbuf_ref, h0_ref, rows_ref, w_in_ref, conv_w_ref, gate_rw_ref, gate_iw_ref, sgu_w8_ref, sgu_b8_ref,
     w_out_ref) = refs[:n_in]
    cast_in, refs = refs[n_in:n_in + n_cast], refs[n_in + n_cast:]
    n_prev = 3 if layer else 0
    prev_refs, refs = refs[:n_prev], refs[n_prev:]
    y_ref, v_ref, hlast_ref, convnew_ref = refs[:4]
    cast_out, refs = refs[4:4 + n_cast], refs[4 + n_cast:]
    gate_sc, v_relay_sc, relay_sc = refs[0], refs[1:1 + N_LANE_GROUPS], refs[1 + N_LANE_GROUPS:]
    for dst, prev in zip((v_ref, hlast_ref, convnew_ref), prev_refs):
        dst[0:layer] = prev[...]
    for w_f32, w_bf16 in zip(cast_in, cast_out):
        w_bf16[...] = w_f32[...].astype(jnp.bfloat16)
    (g_mix_ref, _, gate_rb_ref, gate_ib_ref, lam_ref, conv_b_ref, sgu_g_ref, sgu_bn_ref, _) = _row_views(rows_ref)

    @pl.when(pl.program_id(0) == 0)
    def _():
        _build_gate_tiles(gate_rw_ref, gate_iw_ref, gate_sc)

    if seq_major:
        for c in range(D_MODEL // LANES):
            relay_sc[c][...] = x_ref[:, c * LANES:(c + 1) * LANES]
        x = jnp.concatenate(
            [jnp.concatenate([relay_sc[c][pl.ds(t, nb, stride=steps), :] for c in range(D_MODEL // LANES)], axis=1)
             for t in range(steps)], axis=0)
    else:
        x = x_ref[...].reshape(steps * nb, D_MODEL)
    xn = _rms_norm(x, g_mix_ref[...]).astype(jnp.bfloat16)

    u = _gelu(_dot(xn, w_in_ref[:, 0:D_SGU]))
    v = _layer_norm(_gelu(_dot(xn, w_in_ref[:, D_SGU:2 * D_SGU])), sgu_g_ref[...], sgu_bn_ref[...])
    pitch = nb + SUBLANES
    for g in range(N_LANE_GROUPS):
        for t in range(steps):
            v_relay_sc[g][t * pitch:t * pitch + nb, :] = v[t * nb:(t + 1) * nb, g * LANES:(g + 1) * LANES]
    for s in range(nb):
        for g in range(N_LANE_GROUPS):
            v_ref[layer, s, :, g * LANES:(g + 1) * LANES] = v_relay_sc[g][pl.ds(s, steps, stride=pitch), :]
    gate_rows = []
    for t in range(steps):
        acc = sgu_b8_ref[t:t + 1, :] + sgu_w8_ref[t, 0:1, :] * v[0:nb]
        for s in range(1, t + 1):
            acc = acc + sgu_w8_ref[t, s:s + 1, :] * v[s * nb:(s + 1) * nb]
        gate_rows.append(acc)
    out_a = (u * jnp.concatenate(gate_rows, axis=0)).astype(jnp.bfloat16)

    xb = _dot(xn, w_in_ref[:, 2 * D_SGU:2 * D_SGU + D_LRU])
    xp = jnp.concatenate([cbuf_ref[k] for k in range(CONV_W - 1)] + [xb], axis=0)
    xc = conv_b_ref[...] + conv_w_ref[0:1, :] * xp[0:steps * nb]
    for k in range(1, CONV_W):
        xc = xc + conv_w_ref[k:k + 1, :] * xp[k * nb:(k + steps) * nb]
    convnew_ref[layer] = xp[steps * nb:].reshape(CONV_W - 1, nb, D_LRU)

    a, b = _lru_coeffs(xc, gate_sc, gate_rb_ref[...], gate_ib_ref[...], lam_ref[...])
    h = h0_ref[...]
    hs = []
    for t in range(steps):
        h = a[t * nb:(t + 1) * nb] * h + b[t * nb:(t + 1) * nb]
        hs.append(h)
    hlast_ref[layer] = h

    yb = _dot(xn, w_in_ref[:, 2 * D_SGU + D_LRU:])
    out_b = (jnp.concatenate(hs, axis=0) * _gelu(yb)).astype(jnp.bfloat16)
    out = _dot(jnp.concatenate([out_a, out_b], axis=1), w_out_ref[...])
    y_ref[...] = (x + out).reshape(steps, nb, D_MODEL)


def _mixer_sample(x, h0_all, cbuf_all, p, l, *, seq_major, groups, cast=(), prev=()):
    assert len(prev) == (3 if l else 0)
    nb_all = h0_all.shape[1]
    steps = x.shape[0] // nb_all if seq_major else x.shape[1]
    nb = nb_all // groups
    cast_in_specs, cast_out_specs, cast_shapes = [], [], []
    for w in cast:
        k_rows, n_cols = w.shape[1:]
        blk = k_rows // groups
        assert blk * groups == k_rows and blk % (2 * SUBLANES) == 0
        cast_in_specs.append(pl.BlockSpec((None, blk, n_cols), lambda i: (l, i, 0)))
        cast_out_specs.append(pl.BlockSpec((None, blk, n_cols), lambda i: (0, i, 0)))
        cast_shapes.append(jax.ShapeDtypeStruct((1, k_rows, n_cols), jnp.bfloat16))
    per_tile = HALF // nb
    assert per_tile * nb == HALF and nb_all % HALF == 0
    act_spec = pl.BlockSpec((None, steps, nb, D_MODEL), lambda i: (i // per_tile, 0, i % per_tile, 0))
    if seq_major:
        x_spec = pl.BlockSpec((nb * steps, D_MODEL), lambda i: (i, 0))
        relay = [pltpu.VMEM((nb * steps, LANES), jnp.float32)] * (D_MODEL // LANES)
    else:
        x_spec = act_spec
        relay = []
    state_specs = lambda layers: [pl.BlockSpec((layers, nb, steps, D_SGU), lambda i: (0, i, 0, 0)),
                                  pl.BlockSpec((layers, nb, D_LRU), lambda i: (0, i, 0)),
                                  pl.BlockSpec((layers, CONV_W - 1, nb, D_LRU), lambda i: (0, 0, i, 0))]
    out_shapes = [(nb_all // HALF, steps, HALF, D_MODEL), (l + 1, nb_all, steps, D_SGU), (l + 1, nb_all, D_LRU),
                  (l + 1, CONV_W - 1, nb_all, D_LRU)]
    v_relay = [pltpu.VMEM((steps * (nb + SUBLANES), LANES), jnp.float32)] * N_LANE_GROUPS
    y, v, hlast, convnew, *cast_bf16 = pl.pallas_call(
        functools.partial(_mixer_sample_kernel, steps=steps, nb=nb, seq_major=seq_major, n_cast=len(cast),
                          layer=l),
        grid=(groups,),
        in_specs=([x_spec,
                   pl.BlockSpec((None, CONV_W - 1, nb, D_LRU), lambda i: (l, 0, i, 0)),
                   pl.BlockSpec((None, nb, D_LRU), lambda i: (l, i, 0))]
                  + [_param_spec(p, k, l) for k in _SAMPLE_MIXER_PARAMS] + cast_in_specs
                  + (state_specs(l) if l else [])),
        out_specs=[act_spec] + state_specs(l + 1) + cast_out_specs,
        out_shape=[jax.ShapeDtypeStruct(s, jnp.float32) for s in out_shapes] + cast_shapes,
        scratch_shapes=[pltpu.VMEM((N_GATE_TILES, GATE_TILE, 2 * GATE_TILE), jnp.bfloat16)] + v_relay + relay,
        compiler_params=pltpu.CompilerParams(dimension_semantics=("arbitrary",), vmem_limit_bytes=VMEM_LIMIT),
        name="mixer_sample",
    )(x, cbuf_all, h0_all, *[p[k] for k in _SAMPLE_MIXER_PARAMS], *cast, *prev)
    return y, (v, hlast, convnew), cast_bf16


CAST_STEPS = 4


def _cast_kernel(*refs):
    n = len(refs) // 2
    for w_f32, w_bf16 in zip(refs[:n], refs[n:]):
        w_bf16[...] = w_f32[...].astype(jnp.bfloat16)


def _cast_layer(weights, l):
    in_specs, out_specs, out_shapes = [], [], []
    for w in weights:
        k_rows, n_cols = w.shape[1:]
        blk = k_rows // CAST_STEPS
        assert blk * CAST_STEPS == k_rows and blk % (2 * SUBLANES) == 0
        in_specs.append(pl.BlockSpec((None, blk, n_cols), lambda j: (l, j, 0)))
        out_specs.append(pl.BlockSpec((None, blk, n_cols), lambda j: (0, j, 0)))
        out_shapes.append(jax.ShapeDtypeStruct((1, k_rows, n_cols), jnp.bfloat16))
    return pl.pallas_call(
        _cast_kernel,
        grid=(CAST_STEPS,),
        in_specs=in_specs,
        out_specs=out_specs,
        out_shape=out_shapes,
        compiler_params=pltpu.CompilerParams(dimension_semantics=("arbitrary",), vmem_limit_bytes=VMEM_LIMIT),
        name="cast_weights",
    )(*weights)


def _prepare_params(steps, nseq, norm_mix_g, w_in, conv_w, conv_b, gate_r_w, gate_r_b, gate_i_w, gate_i_b,
                    lru_lambda, sgu_norm_g, sgu_norm_b, sgu_w, sgu_b, w_out, norm_mlp_g, mlp_w1, mlp_w2, final_norm_g):
    rows = [None] * N_ROWS
    rows[ROW_G_MIX], rows[ROW_G_MLP] = norm_mix_g, norm_mlp_g
    rows[ROW_GATE_RB], rows[ROW_GATE_IB], rows[ROW_LAM], rows[ROW_CONV_B] = gate_r_b, gate_i_b, lru_lambda, conv_b
    rows[ROW_SGU] = jnp.concatenate([sgu_norm_g, sgu_norm_b], axis=-1)
    rows[ROW_G_FINAL] = jnp.broadcast_to(final_norm_g, (DEPTH, D_MODEL))
    k_idx = lax.broadcasted_iota(jnp.int32, (DEPTH, N_ROWS, D_MODEL), 1)
    packed = jnp.zeros((DEPTH, N_ROWS, D_MODEL), jnp.float32)
    for k, r in enumerate(rows):
        packed = jnp.where(k_idx == k, r[:, None, :], packed)
    return {
        "rows": packed,
        "conv_w": conv_w,
        "gate_rw": gate_r_w,
        "gate_iw": gate_i_w,
        "sgu_w": sgu_w,
        "sgu_bias": jnp.repeat(jnp.transpose(sgu_b, (0, 2, 1)), SGU_HEAD_DIM, axis=2),
        "sgu_w8": jnp.repeat(jnp.transpose(sgu_w[:, :, :steps, :steps], (0, 2, 3, 1)), SGU_HEAD_DIM, axis=3),
        "sgu_b8": jnp.repeat(jnp.transpose(sgu_b[:, :, :steps], (0, 2, 1)), SGU_HEAD_DIM, axis=2),
    }


def kernel(x_prompt, x_sample, state_lru_h, state_conv, norm_mix_g, w_in, conv_w, conv_b, gate_r_w, gate_r_b, gate_i_w, gate_i_b, lru_lambda, sgu_norm_g, sgu_norm_b, sgu_w, sgu_b, w_out, norm_mlp_g, mlp_w1, mlp_w2, final_norm_g):
    nseq, seq, _ = x_prompt.shape
    nb, steps, _ = x_sample.shape
    assert seq % CHUNK == 0 and nseq == SUBLANES and steps == nseq and nb == 2 * HALF
    p = _prepare_params(steps, nseq, norm_mix_g, w_in, conv_w, conv_b, gate_r_w, gate_r_b, gate_i_w, gate_i_b,
                        lru_lambda, sgu_norm_g, sgu_norm_b, sgu_w, sgu_b, w_out, norm_mlp_g, mlp_w1, mlp_w2,
                        final_norm_g)

    xp = x_prompt
    xs = x_sample.reshape(nb * steps, D_MODEL)
    cbuf_tm = jnp.transpose(state_conv, (0, 2, 1, 3))
    prompt_state, sample_state = (), ()
    big_f32 = (w_in, w_out, mlp_w1, mlp_w2)
    big = _cast_layer(big_f32[:2], 0)
    for l in range(DEPTH):
        last = l == DEPTH - 1
        first = l == 0
        p.update(zip(_BIG_WEIGHTS, big))
        xs, sample_state, mlp_bf16 = _mixer_sample(
            xs, state_lru_h, cbuf_tm, p, l, seq_major=first, prev=sample_state,
            groups=SAMPLE_GROUPS_CAST if first else SAMPLE_GROUPS, cast=big_f32[2:] if first else ())
        p.update(zip(_BIG_WEIGHTS[2:], mlp_bf16))
        xp, xs, prompt_state, big = _layer_prompt(xp, xs.reshape(steps * nb, D_MODEL), p, l, final_norm=last,
                                                  sample_seq_major=last, cast_next=() if last else big_f32,
                                                  prev=prompt_state)
        if not last:
            xs = xs.reshape(nb // HALF, steps, HALF, D_MODEL)

    y_sample = xs.reshape(nb, steps, D_MODEL)
    h_p, conv_p = prompt_state
    v_s, h_s, conv_s = sample_state
    return (xp, y_sample, h_p, jnp.transpose(conv_p, (0, 2, 1, 3)), h_s, jnp.transpose(conv_s, (0, 2, 1, 3)), v_s)
```

```python
import functools

import jax
import jax.numpy as jnp
from jax import lax
from jax.experimental import pallas as pl
from jax.experimental.pallas import tpu as pltpu

D_MODEL = 1024
DEPTH = 2
SGU_HEADS = 8
SGU_HEAD_DIM = 64
D_SGU = SGU_HEADS * SGU_HEAD_DIM
CHUNK = 128
D_LRU = 1024
LRU_BLOCKS = 16
LRU_BLOCK_DIM = 64
CONV_W = 4
LRU_C = 8.0
D_FF = 4 * D_MODEL
EPS = 1e-6

SUBLANES = 8
LANES = 128
GATE_TILE = 256
N_GATE_TILES = D_LRU // GATE_TILE
HEADS_PER_LANE_GROUP = LANES // SGU_HEAD_DIM
N_LANE_GROUPS = D_SGU // LANES
N_SLABS = D_LRU // LANES
FF_CHUNK = 2 * D_MODEL
N_FF_CHUNKS = D_FF // FF_CHUNK

SAMPLE_GROUPS = 2
SAMPLE_GROUPS_CAST = 4
HALF = CHUNK // 2
PITCH = HALF + SUBLANES
V7X_VMEM_BYTES = 64 * 1024 * 1024
VMEM_HEADROOM = 2 * 1024 * 1024
VMEM_LIMIT = V7X_VMEM_BYTES - VMEM_HEADROOM
F32_TINY = 1.1754944e-38


def _rms_norm(x, g):
    return x * lax.rsqrt(jnp.mean(x * x, axis=-1, keepdims=True) + EPS) * g


def _layer_norm(x, g, b):
    mu = jnp.mean(x, axis=-1, keepdims=True)
    xc = x - mu
    return xc * lax.rsqrt(jnp.mean(xc * xc, axis=-1, keepdims=True) + EPS) * g + b


def _gelu(x):
    return jax.nn.gelu(x, approximate=True)


def _dot(a, b):
    return jnp.dot(a, b, preferred_element_type=jnp.float32)


def _lru_elementwise(r_pre, i_pre, xc, half_rb, half_ib, half_c):
    th_r = jnp.tanh(r_pre + half_rb)
    th_i = jnp.tanh(i_pre + half_ib)
    log_a = half_c * th_r + half_c
    i = 0.5 * th_i + 0.5
    a = jnp.exp(log_a)
    y = jnp.tanh(log_a) * (-1.0 - a * a)
    b = (y * lax.rsqrt(jnp.maximum(y, F32_TINY))) * (i * xc)
    return a, b


def _lru_consts(gate_rb, gate_ib, lam):
    return 0.5 * gate_rb, 0.5 * gate_ib, (0.5 * LRU_C) * jax.nn.log_sigmoid(lam)


def _build_gate_tiles(gate_rw_ref, gate_iw_ref, gate_sc):
    per_tile = GATE_TILE // LRU_BLOCK_DIM
    k_idx = lax.broadcasted_iota(jnp.int32, (LRU_BLOCK_DIM, GATE_TILE), 0)
    n_idx = lax.broadcasted_iota(jnp.int32, (LRU_BLOCK_DIM, GATE_TILE), 1)
    replicate = (n_idx % LRU_BLOCK_DIM == k_idx).astype(jnp.bfloat16)
    row_blk = lax.broadcasted_iota(jnp.int32, (GATE_TILE, GATE_TILE), 0) // LRU_BLOCK_DIM
    col_blk = lax.broadcasted_iota(jnp.int32, (GATE_TILE, GATE_TILE), 1) // LRU_BLOCK_DIM
    for j in range(N_GATE_TILES):
        for k, ref in enumerate((gate_rw_ref, gate_iw_ref)):
            stacked = ref[per_tile * j:per_tile * (j + 1)].reshape(GATE_TILE, LRU_BLOCK_DIM)
            rep = _dot(stacked.astype(jnp.bfloat16), replicate)
            gate_sc[j, :, k * GATE_TILE:(k + 1) * GATE_TILE] = jnp.where(
                row_blk == col_blk, 0.5 * rep, 0.0).astype(jnp.bfloat16)


def _lru_coeffs(xc, gate_w_ref, gate_rb, gate_ib, lam):
    xcb = xc.astype(jnp.bfloat16)
    r_parts, i_parts = [], []
    for j in range(N_GATE_TILES):
        ri = _dot(xcb[:, j * GATE_TILE:(j + 1) * GATE_TILE], gate_w_ref[j])
        r_parts.append(ri[:, :GATE_TILE])
        i_parts.append(ri[:, GATE_TILE:])
    return _lru_elementwise(jnp.concatenate(r_parts, axis=1), jnp.concatenate(i_parts, axis=1), xc,
                            *_lru_consts(gate_rb, gate_ib, lam))


ROW_G_MIX, ROW_G_MLP, ROW_GATE_RB, ROW_GATE_IB, ROW_LAM, ROW_CONV_B, ROW_SGU, ROW_G_FINAL = range(8)
N_ROWS = 8


def _row_views(rows_ref):
    one = lambda k, lo=0, hi=D_MODEL: rows_ref.at[k:k + 1, lo:hi]
    return (one(ROW_G_MIX), one(ROW_G_MLP), one(ROW_GATE_RB), one(ROW_GATE_IB), one(ROW_LAM), one(ROW_CONV_B),
            one(ROW_SGU, 0, D_SGU), one(ROW_SGU, D_SGU, 2 * D_SGU), one(ROW_G_FINAL))


def _mlp_up(xn, w1_ref, c):
    cols = slice(c * FF_CHUNK, (c + 1) * FF_CHUNK)
    return jnp.square(jnp.maximum(_dot(xn, w1_ref[:, cols]), 0.0)).astype(jnp.bfloat16)


def _mlp_down(hid, w2_ref, c):
    return _dot(hid, w2_ref[c * FF_CHUNK:(c + 1) * FF_CHUNK, :])


def _layer_prompt_kernel(*refs, nseq, n_tiles, final_norm, n_cast, sample_seq_major, layer):
    n_in = 2 + len(_PROMPT_PARAMS)
    (x_ref, xs_ref, rows_ref, w_in_ref, conv_w_ref, gate_rw_ref, gate_iw_ref, sgu_w_ref, sgu_bias_ref, w_out_ref,
     w1_ref, w2_ref) = refs[:n_in]
    (g_mix_ref, g_mlp_ref, gate_rb_ref, gate_ib_ref, lam_ref, conv_b_ref, sgu_g_ref, sgu_bn_ref,
     gf_ref) = _row_views(rows_ref)
    cast_in, refs = refs[n_in:n_in + n_cast], refs[n_in + n_cast:]
    n_prev = 2 if layer else 0
    prev_refs, refs = refs[:n_prev], refs[n_prev:]
    y_ref, ys_ref, hlast_ref, convnew_ref = refs[:4]
    cast_out, refs = refs[4:4 + n_cast], refs[4 + n_cast:]
    vprev_sc, sgu_lhs_sc, ymix_sc, gate_sc = refs[:4]
    slab_scratch = refs[4:]
    rows = nseq * HALF
    j = pl.program_id(0)
    slab_sc = slab_scratch[0:N_SLABS]
    tail_sc = slab_scratch[N_SLABS:2 * N_SLABS]
    h_sc = slab_scratch[2 * N_SLABS:3 * N_SLABS]

    @pl.when(j == 0)
    def _():
        for c in range(N_SLABS):
            tail_sc[c][...] = jnp.zeros_like(tail_sc[c])
            h_sc[c][...] = jnp.zeros_like(h_sc[c])
        vprev_sc[...] = jnp.zeros_like(vprev_sc)
        ymix_sc[1] = xs_ref[...]
        _build_gate_tiles(gate_rw_ref, gate_iw_ref, gate_sc)
        t_idx = lax.broadcasted_iota(jnp.int32, (CHUNK, CHUNK), 0)
        s_idx = lax.broadcasted_iota(jnp.int32, (CHUNK, CHUNK), 1)
        for g in range(N_LANE_GROUPS):
            first, second = [], []
            for hh in range(HEADS_PER_LANE_GROUP):
                wm = jnp.where(s_idx <= t_idx, sgu_w_ref[HEADS_PER_LANE_GROUP * g + hh], 0.0)
                first.append(pltpu.roll(wm[:HALF], HALF, axis=1))
                second.append(wm[HALF:])
            sgu_lhs_sc[0, g] = jnp.concatenate(first, axis=0).astype(jnp.bfloat16)
            sgu_lhs_sc[1, g] = jnp.concatenate(second, axis=0).astype(jnp.bfloat16)

    parity = j % 2

    @pl.when(j < n_tiles)
    def _():
        _fused_step(x_ref, g_mix_ref, w_in_ref, conv_w_ref, conv_b_ref, gate_rb_ref, gate_ib_ref, lam_ref, sgu_g_ref, sgu_bn_ref,
                    sgu_bias_ref, w_out_ref, g_mlp_ref, w1_ref, w2_ref, gf_ref, y_ref, vprev_sc, sgu_lhs_sc, ymix_sc,
                    gate_sc, slab_sc, tail_sc, h_sc, parity=parity, nseq=nseq, final_norm=final_norm)
        for w_f32, w_bf16 in zip(cast_in, cast_out):
            w_bf16[...] = w_f32[...].astype(jnp.bfloat16)

    def store_sample(tile):
        if not sample_seq_major:
            ys_ref[...] = tile
            return
        for c in range(N_SLABS):
            for t in range(nseq):
                slab_sc[c][t * PITCH:t * PITCH + HALF, :] = tile[t * HALF:(t + 1) * HALF, c * LANES:(c + 1) * LANES]
        for s in range(HALF):
            for c in range(N_SLABS):
                ys_ref[s * nseq:(s + 1) * nseq, c * LANES:(c + 1) * LANES] = (
                    slab_sc[c][pl.ds(s, nseq, stride=PITCH), :])

    @pl.when(j == 0)
    def _():
        store_sample(y_ref[...].reshape(rows, D_MODEL))

    @pl.when(j == n_tiles - 1)
    def _():
        for c in range(N_SLABS):
            cols = slice(c * LANES, (c + 1) * LANES)
            hlast_ref[layer, :, cols] = h_sc[c][...]
            for k in range(CONV_W - 1):
                convnew_ref[layer, CONV_W - 2 - k, :, cols] = tail_sc[c][parity, k]
        for dst, prev in zip((hlast_ref, convnew_ref), prev_refs):
            dst[0:layer] = prev[...]

    @pl.when(j == n_tiles + 1)
    def _():
        ymix_sc[1 - parity] = xs_ref[...]

    @pl.when(j >= n_tiles)
    def _():
        xm = ymix_sc[1 - parity]
        xn = _rms_norm(xm, g_mlp_ref[...]).astype(jnp.bfloat16)
        acc = xm
        for c in range(N_FF_CHUNKS):
            acc = acc + _mlp_down(_mlp_up(xn, w1_ref, c), w2_ref, c)
        ymix_sc[parity] = _rms_norm(acc, gf_ref[...]) if final_norm else acc

    @pl.when(j == n_tiles)
    def _():
        y_ref[...] = ymix_sc[parity].reshape(nseq, HALF, D_MODEL)

    @pl.when(j == n_tiles + 1)
    def _():
        store_sample(ymix_sc[parity])


def _fused_step(x_ref, g_mix_ref, w_in_ref, conv_w_ref, conv_b_ref, gate_rb_ref, gate_ib_ref, lam_ref, sgu_g_ref, sgu_bn_ref,
                sgu_bias_ref, w_out_ref, g_mlp_ref, w1_ref, w2_ref, gf_ref, y_ref, vprev_sc, sgu_lhs_sc, ymix_sc,
                gate_sc, slab_sc, tail_sc, h_sc, *, parity, nseq, final_norm):
    rows = nseq * HALF
    half_rb, half_ib, half_c = _lru_consts(gate_rb_ref[...], gate_ib_ref[...], lam_ref[...])
    xb_col0 = 2 * D_SGU
    yb_col0 = 2 * D_SGU + D_LRU
    slabs_per_tile = GATE_TILE // LANES
    st = [{} for _ in range(N_GATE_TILES)]
    sg = {}
    ml = {}

    def mlp_norm():
        xm = ymix_sc[1 - parity]
        ml["acc"] = xm
        ml["xn"] = _rms_norm(xm, g_mlp_ref[...]).astype(jnp.bfloat16)

    def mlp_up(c):
        ml["hid", c] = _mlp_up(ml["xn"], w1_ref, c)

    def mlp_down(c):
        ml["acc"] = ml["acc"] + _mlp_down(ml.pop(("hid", c)), w2_ref, c)

    def mlp_store():
        out = _rms_norm(ml["acc"], gf_ref[...]) if final_norm else ml["acc"]
        y_ref[...] = out.reshape(nseq, HALF, D_MODEL)

    def mix_norm():
        sg["x"] = x_ref[...].reshape(rows, D_MODEL)
        sg["xn"] = _rms_norm(sg["x"], g_mix_ref[...]).astype(jnp.bfloat16)

    def lru_proj(q):
        xb = _dot(sg["xn"], w_in_ref[:, xb_col0 + q * GATE_TILE:xb_col0 + (q + 1) * GATE_TILE])
        for i in range(slabs_per_tile):
            for s in range(nseq):
                slab_sc[slabs_per_tile * q + i][s * PITCH:s * PITCH + HALF, :] = (
                    xb[s * HALF:(s + 1) * HALF, i * LANES:(i + 1) * LANES])

    def lru_conv(q):
        xc_slabs = []
        for c in range(slabs_per_tile * q, slabs_per_tile * (q + 1)):
            lanes = slice(c * LANES, (c + 1) * LANES)
            taps = [jnp.broadcast_to(conv_w_ref[k:k + 1, lanes], (nseq, LANES)) for k in range(CONV_W)]
            taps.append(jnp.broadcast_to(conv_b_ref[:, lanes], (nseq, LANES)))
            p1, p2, p3 = (tail_sc[c][1 - parity, k] for k in range(CONV_W - 1))
            steps_out = []
            for t in range(HALF):
                cur = slab_sc[c][pl.ds(t, nseq, stride=PITCH), :]
                steps_out.append(taps[CONV_W] + taps[3] * cur + taps[2] * p1 + taps[1] * p2 + taps[0] * p3)
                p1, p2, p3 = cur, p1, p2
            for k, pk in enumerate((p1, p2, p3)):
                tail_sc[c][parity, k] = pk
            xc_slabs.append(jnp.concatenate(steps_out, axis=0))
        st[q]["xc"] = jnp.concatenate(xc_slabs, axis=1)

    def lru_gates(q):
        st[q]["ri"] = _dot(st[q]["xc"].astype(jnp.bfloat16), gate_sc[q])

    def lru_coeffs(q):
        cols = slice(q * GATE_TILE, (q + 1) * GATE_TILE)
        ri = st[q].pop("ri")
        st[q]["ab"] = _lru_elementwise(ri[:, :GATE_TILE], ri[:, GATE_TILE:], st[q].pop("xc"),
                                       half_rb[:, cols], half_ib[:, cols], half_c[:, cols])

    def lru_scan(q):
        a, b = st[q].pop("ab")
        for i in range(slabs_per_tile):
            c = slabs_per_tile * q + i
            h = h_sc[c][...]
            for t in range(HALF):
                h = (a[t * nseq:(t + 1) * nseq, i * LANES:(i + 1) * LANES] * h
                     + b[t * nseq:(t + 1) * nseq, i * LANES:(i + 1) * LANES])
                slab_sc[c][pl.ds(t, nseq, stride=PITCH), :] = h
            h_sc[c][...] = h

    def lru_out(q):
        hs = jnp.concatenate(
            [jnp.concatenate([slab_sc[c][s * PITCH:s * PITCH + HALF, :]
                              for c in range(slabs_per_tile * q, slabs_per_tile * (q + 1))], axis=1)
             for s in range(nseq)], axis=0)
        yb = _dot(sg["xn"], w_in_ref[:, yb_col0 + q * GATE_TILE:yb_col0 + (q + 1) * GATE_TILE])
        st[q]["out_b"] = (hs * _gelu(yb)).astype(jnp.bfloat16)

    def lru_mix(q):
        r0 = D_SGU + q * GATE_TILE
        sg["acc"] = sg["acc"] + _dot(st[q].pop("out_b"), w_out_ref[r0:r0 + GATE_TILE, :])

    def sgu_u():
        uv = _gelu(_dot(sg["xn"], w_in_ref[:, 0:2 * D_SGU]))
        sg["u"], sg["gv"] = uv[:, :D_SGU], uv[:, D_SGU:]

    def sgu_v():
        v = _layer_norm(sg.pop("gv"), sgu_g_ref[...], sgu_bn_ref[...])
        sg["vb"] = v.astype(jnp.bfloat16)

    def sgu_gate():
        vb = sg.pop("vb")
        lane = lax.broadcasted_iota(jnp.int32, (HALF, LANES), 1)
        bias = sgu_bias_ref[pl.ds(pl.multiple_of(parity * HALF, HALF), HALF), :]
        v_full = [jnp.concatenate([vprev_sc[1 - parity, s], vb[s * HALF:(s + 1) * HALF]], axis=0)
                  for s in range(nseq)]
        vprev_sc[parity] = vb.reshape(nseq, HALF, D_SGU)
        gate_rows = [[] for _ in range(nseq)]
        for s in range(0, nseq, 2):
            for g in range(N_LANE_GROUPS):
                lanes = slice(g * LANES, (g + 1) * LANES)
                rhs = jnp.concatenate([v_full[s][:, lanes], v_full[s + 1][:, lanes]], axis=1)
                res = _dot(sgu_lhs_sc[parity, g], rhs)
                for i in range(2):
                    part = res[:, i * LANES:(i + 1) * LANES]
                    gate_rows[s + i].append(jnp.where(lane < SGU_HEAD_DIM, part[:HALF], part[HALF:]))
        gate = jnp.concatenate([jnp.concatenate(r, axis=1) + bias for r in gate_rows], axis=0)
        sg["out_a"] = (sg.pop("u") * gate).astype(jnp.bfloat16)

    def sgu_mix():
        sg["acc"] = sg["x"] + _dot(sg.pop("out_a"), w_out_ref[0:D_SGU, :])

    def mix_store():
        ymix_sc[parity] = sg["acc"]

    tiles = range(N_GATE_TILES)
    mlp_norm()
    mix_norm()
    for q in tiles:
        lru_proj(q)
    first, second = range(N_FF_CHUNKS // 2), range(N_FF_CHUNKS // 2, N_FF_CHUNKS)
    mlp_up(first[0])
    for q in tiles:
        lru_conv(q)
        lru_gates(q)
    for c in first[1:]:
        mlp_up(c)
    sgu_u()
    sgu_v()
    for q in tiles:
        lru_coeffs(q)
        lru_scan(q)
    for c in first:
        mlp_down(c)
    for q in tiles:
        lru_out(q)
    for c in second:
        mlp_up(c)
    sgu_gate()
    sgu_mix()
    for c in second:
        mlp_down(c)
    for q in tiles:
        lru_mix(q)
    mix_store()
    mlp_store()


def _const_spec(shape):
    nd = len(shape)
    return pl.BlockSpec(shape, lambda *_: (0,) * nd, pipeline_mode=pl.Buffered(1))


def _layer_spec(arr, l):
    nd = arr.ndim - 1
    return pl.BlockSpec((None,) + arr.shape[1:], lambda *_: (l,) + (0,) * nd, pipeline_mode=pl.Buffered(1))


_BIG_WEIGHTS = ("w_in", "w_out", "w1", "w2")
_PROMPT_PARAMS = ("rows", "w_in", "conv_w", "gate_rw", "gate_iw", "sgu_w", "sgu_bias", "w_out", "w1", "w2")
_SAMPLE_MIXER_PARAMS = ("rows", "w_in", "conv_w", "gate_rw", "gate_iw", "sgu_w8", "sgu_b8", "w_out")


def _param_spec(p, k, l):
    return _layer_spec(p[k], 0 if k in _BIG_WEIGHTS else l)


def _layer_prompt(x, xs, p, l, *, final_norm, sample_seq_major, cast_next=(), prev=()):
    assert len(prev) == (2 if l else 0)
    nseq, seq, _ = x.shape
    n_tiles = seq // HALF
    rows = nseq * HALF
    assert xs.shape == (2 * rows, D_MODEL)
    consts = [p[k] for k in _PROMPT_PARAMS]
    cast_in_specs, cast_out_specs, cast_shapes = [], [], []
    for w in cast_next:
        k_rows, n_cols = w.shape[1:]
        blk = k_rows // n_tiles
        assert blk * n_tiles == k_rows and blk % (2 * SUBLANES) == 0
        cast_in_specs.append(pl.BlockSpec((None, blk, n_cols), lambda j: (l + 1, jnp.minimum(j, n_tiles - 1), 0)))
        cast_out_specs.append(pl.BlockSpec((None, blk, n_cols), lambda j: (0, jnp.minimum(j, n_tiles - 1), 0)))
        cast_shapes.append(jax.ShapeDtypeStruct((1, k_rows, n_cols), jnp.bfloat16))
    x_spec = pl.BlockSpec((nseq, HALF, D_MODEL), lambda j: (0, jnp.minimum(j, n_tiles - 1), 0))
    y_spec = pl.BlockSpec((nseq, HALF, D_MODEL), lambda j: (0, jnp.clip(j - 1, 0, n_tiles - 1), 0))
    xs_spec = pl.BlockSpec((rows, D_MODEL), lambda j: (jnp.where(j > n_tiles, 1, 0), 0),
                           pipeline_mode=pl.Buffered(1))
    y, ys, hlast, convnew, *cast = pl.pallas_call(
        functools.partial(_layer_prompt_kernel, nseq=nseq, n_tiles=n_tiles, final_norm=final_norm,
                          n_cast=len(cast_next), sample_seq_major=sample_seq_major, layer=l),
        grid=(n_tiles + 2,),
        in_specs=([x_spec, xs_spec] + [_param_spec(p, k, l) for k in _PROMPT_PARAMS] + cast_in_specs
                  + [_const_spec(a.shape) for a in prev]),
        out_specs=([y_spec, xs_spec, _const_spec((l + 1, nseq, D_LRU)),
                    _const_spec((l + 1, CONV_W - 1, nseq, D_LRU))] + cast_out_specs),
        out_shape=[jax.ShapeDtypeStruct(x.shape, jnp.float32),
                   jax.ShapeDtypeStruct(xs.shape, jnp.float32),
                   jax.ShapeDtypeStruct((l + 1, nseq, D_LRU), jnp.float32),
                   jax.ShapeDtypeStruct((l + 1, CONV_W - 1, nseq, D_LRU), jnp.float32)] + cast_shapes,
        scratch_shapes=([pltpu.VMEM((2, nseq, HALF, D_SGU), jnp.bfloat16),
                         pltpu.VMEM((2, N_LANE_GROUPS, CHUNK, CHUNK), jnp.bfloat16),
                         pltpu.VMEM((2, rows, D_MODEL), jnp.float32),
                         pltpu.VMEM((N_GATE_TILES, GATE_TILE, 2 * GATE_TILE), jnp.bfloat16)]
                        + [pltpu.VMEM((nseq * PITCH, LANES), jnp.float32)] * N_SLABS
                        + [pltpu.VMEM((2, CONV_W - 1, nseq, LANES), jnp.float32)] * N_SLABS
                        + [pltpu.VMEM((nseq, LANES), jnp.float32)] * N_SLABS),
        compiler_params=pltpu.CompilerParams(dimension_semantics=("arbitrary",), vmem_limit_bytes=VMEM_LIMIT),
        name="layer_prompt",
    )(x, xs, *consts, *cast_next, *prev)
    return y, ys, (hlast, convnew), cast


def _mixer_sample_kernel(*refs, steps, nb, seq_major, n_cast, layer):
    n_in = 3 + len(_SAMPLE_MIXER_PARAMS)
    (x_ref, cbuf_ref, h0_ref, rows_ref, w_in_ref, conv_w_ref, gate_rw_ref, gate_iw_ref, sgu_w8_ref, sgu_b8_ref,
     w_out_ref) = refs[:n_in]
    cast_in, refs = refs[n_in:n_in + n_cast], refs[n_in + n_cast:]
    n_prev = 3 if layer else 0
    prev_refs, refs = refs[:n_prev], refs[n_prev:]
    y_ref, v_ref, hlast_ref, convnew_ref = refs[:4]
    cast_out, refs = refs[4:4 + n_cast], refs[4 + n_cast:]
    gate_sc, v_relay_sc, relay_sc = refs[0], refs[1:1 + N_LANE_GROUPS], refs[1 + N_LANE_GROUPS:]
    for dst, prev in zip((v_ref, hlast_ref, convnew_ref), prev_refs):
        dst[0:layer] = prev[...]
    for w_f32, w_bf16 in zip(cast_in, cast_out):
        w_bf16[...] = w_f32[...].astype(jnp.bfloat16)
    (g_mix_ref, _, gate_rb_ref, gate_ib_ref, lam_ref, conv_b_ref, sgu_g_ref, sgu_bn_ref, _) = _row_views(rows_ref)

    @pl.when(pl.program_id(0) == 0)
    def _():
        _build_gate_tiles(gate_rw_ref, gate_iw_ref, gate_sc)

    if seq_major:
        for c in range(D_MODEL // LANES):
            relay_sc[c][...] = x_ref[:, c * LANES:(c + 1) * LANES]
        x = jnp.concatenate(
            [jnp.concatenate([relay_sc[c][pl.ds(t, nb, stride=steps), :] for c in range(D_MODEL // LANES)], axis=1)
             for t in range(steps)], axis=0)
    else:
        x = x_ref[...].reshape(steps * nb, D_MODEL)
    xn = _rms_norm(x, g_mix_ref[...]).astype(jnp.bfloat16)

    u = _gelu(_dot(xn, w_in_ref[:, 0:D_SGU]))
    v = _layer_norm(_gelu(_dot(xn, w_in_ref[:, D_SGU:2 * D_SGU])), sgu_g_ref[...], sgu_bn_ref[...])
    pitch = nb + SUBLANES
    for g in range(N_LANE_GROUPS):
        for t in range(steps):
            v_relay_sc[g][t * pitch:t * pitch + nb, :] = v[t * nb:(t + 1) * nb, g * LANES:(g + 1) * LANES]
    for s in range(nb):
        for g in range(N_LANE_GROUPS):
            v_ref[layer, s, :, g * LANES:(g + 1) * LANES] = v_relay_sc[g][pl.ds(s, steps, stride=pitch), :]
    gate_rows = []
    for t in range(steps):
        acc = sgu_b8_ref[t:t + 1, :] + sgu_w8_ref[t, 0:1, :] * v[0:nb]
        for s in range(1, t + 1):
            acc = acc + sgu_w8_ref[t, s:s + 1, :] * v[s * nb:(s + 1) * nb]
        gate_rows.append(acc)
    out_a = (u * jnp.concatenate(gate_rows, axis=0)).astype(jnp.bfloat16)

    xb = _dot(xn, w_in_ref[:, 2 * D_SGU:2 * D_SGU + D_LRU])
    xp = jnp.concatenate([cbuf_ref[k] for k in range(CONV_W - 1)] + [xb], axis=0)
    xc = conv_b_ref[...] + conv_w_ref[0:1, :] * xp[0:steps * nb]
    for k in range(1, CONV_W):
        xc = xc + conv_w_ref[k:k + 1, :] * xp[k * nb:(k + steps) * nb]
    convnew_ref[layer] = xp[steps * nb:].reshape(CONV_W - 1, nb, D_LRU)

    a, b = _lru_coeffs(xc, gate_sc, gate_rb_ref[...], gate_ib_ref[...], lam_ref[...])
    h = h0_ref[...]
    hs = []
    for t in range(steps):
        h = a[t * nb:(t + 1) * nb] * h + b[t * nb:(t + 1) * nb]
        hs.append(h)
    hlast_ref[layer] = h

    yb = _dot(xn, w_in_ref[:, 2 * D_SGU + D_LRU:])
    out_b = (jnp.concatenate(hs, axis=0) * _gelu(yb)).astype(jnp.bfloat16)
    out = _dot(jnp.concatenate([out_a, out_b], axis=1), w_out_ref[...])
    y_ref[...] = (x + out).reshape(steps, nb, D_MODEL)


def _mixer_sample(x, h0_all, cbuf_all, p, l, *, seq_major, groups, cast=(), prev=()):
    assert len(prev) == (3 if l else 0)
    nb_all = h0_all.shape[1]
    steps = x.shape[0] // nb_all if seq_major else x.shape[1]
    nb = nb_all // groups
    cast_in_specs, cast_out_specs, cast_shapes = [], [], []
    for w in cast:
        k_rows, n_cols = w.shape[1:]
        blk = k_rows // groups
        assert blk * groups == k_rows and blk % (2 * SUBLANES) == 0
        cast_in_specs.append(pl.BlockSpec((None, blk, n_cols), lambda i: (l, i, 0)))
        cast_out_specs.append(pl.BlockSpec((None, blk, n_cols), lambda i: (0, i, 0)))
        cast_shapes.append(jax.ShapeDtypeStruct((1, k_rows, n_cols), jnp.bfloat16))
    per_tile = HALF // nb
    assert per_tile * nb == HALF and nb_all % HALF == 0
    act_spec = pl.BlockSpec((None, steps, nb, D_MODEL), lambda i: (i // per_tile, 0, i % per_tile, 0))
    if seq_major:
        x_spec = pl.BlockSpec((nb * steps, D_MODEL), lambda i: (i, 0))
        relay = [pltpu.VMEM((nb * steps, LANES), jnp.float32)] * (D_MODEL // LANES)
    else:
        x_spec = act_spec
        relay = []
    state_specs = lambda layers: [pl.BlockSpec((layers, nb, steps, D_SGU), lambda i: (0, i, 0, 0)),
                                  pl.BlockSpec((layers, nb, D_LRU), lambda i: (0, i, 0)),
                                  pl.BlockSpec((layers, CONV_W - 1, nb, D_LRU), lambda i: (0, 0, i, 0))]
    out_shapes = [(nb_all // HALF, steps, HALF, D_MODEL), (l + 1, nb_all, steps, D_SGU), (l + 1, nb_all, D_LRU),
                  (l + 1, CONV_W - 1, nb_all, D_LRU)]
    v_relay = [pltpu.VMEM((steps * (nb + SUBLANES), LANES), jnp.float32)] * N_LANE_GROUPS
    y, v, hlast, convnew, *cast_bf16 = pl.pallas_call(
        functools.partial(_mixer_sample_kernel, steps=steps, nb=nb, seq_major=seq_major, n_cast=len(cast),
                          layer=l),
        grid=(groups,),
        in_specs=([x_spec,
                   pl.BlockSpec((None, CONV_W - 1, nb, D_LRU), lambda i: (l, 0, i, 0)),
                   pl.BlockSpec((None, nb, D_LRU), lambda i: (l, i, 0))]
                  + [_param_spec(p, k, l) for k in _SAMPLE_MIXER_PARAMS] + cast_in_specs
                  + (state_specs(l) if l else [])),
        out_specs=[act_spec] + state_specs(l + 1) + cast_out_specs,
        out_shape=[jax.ShapeDtypeStruct(s, jnp.float32) for s in out_shapes] + cast_shapes,
        scratch_shapes=[pltpu.VMEM((N_GATE_TILES, GATE_TILE, 2 * GATE_TILE), jnp.bfloat16)] + v_relay + relay,
        compiler_params=pltpu.CompilerParams(dimension_semantics=("arbitrary",), vmem_limit_bytes=VMEM_LIMIT),
        name="mixer_sample",
    )(x, cbuf_all, h0_all, *[p[k] for k in _SAMPLE_MIXER_PARAMS], *cast, *prev)
    return y, (v, hlast, convnew), cast_bf16


CAST_STEPS = 4


def _cast_kernel(*refs):
    n = len(refs) // 2
    for w_f32, w_bf16 in zip(refs[:n], refs[n:]):
        w_bf16[...] = w_f32[...].astype(jnp.bfloat16)


def _cast_layer(weights, l):
    in_specs, out_specs, out_shapes = [], [], []
    for w in weights:
        k_rows, n_cols = w.shape[1:]
        blk = k_rows // CAST_STEPS
        assert blk * CAST_STEPS == k_rows and blk % (2 * SUBLANES) == 0
        in_specs.append(pl.BlockSpec((None, blk, n_cols), lambda j: (l, j, 0)))
        out_specs.append(pl.BlockSpec((None, blk, n_cols), lambda j: (0, j, 0)))
        out_shapes.append(jax.ShapeDtypeStruct((1, k_rows, n_cols), jnp.bfloat16))
    return pl.pallas_call(
        _cast_kernel,
        grid=(CAST_STEPS,),
        in_specs=in_specs,
        out_specs=out_specs,
        out_shape=out_shapes,
        compiler_params=pltpu.CompilerParams(dimension_semantics=("arbitrary",), vmem_limit_bytes=VMEM_LIMIT),
        name="cast_weights",
    )(*weights)


def _prepare_params(steps, nseq, norm_mix_g, w_in, conv_w, conv_b, gate_r_w, gate_r_b, gate_i_w, gate_i_b,
                    lru_lambda, sgu_norm_g, sgu_norm_b, sgu_w, sgu_b, w_out, norm_mlp_g, mlp_w1, mlp_w2, final_norm_g):
    rows = [None] * N_ROWS
    rows[ROW_G_MIX], rows[ROW_G_MLP] = norm_mix_g, norm_mlp_g
    rows[ROW_GATE_RB], rows[ROW_GATE_IB], rows[ROW_LAM], rows[ROW_CONV_B] = gate_r_b, gate_i_b, lru_lambda, conv_b
    rows[ROW_SGU] = jnp.concatenate([sgu_norm_g, sgu_norm_b], axis=-1)
    rows[ROW_G_FINAL] = jnp.broadcast_to(final_norm_g, (DEPTH, D_MODEL))
    k_idx = lax.broadcasted_iota(jnp.int32, (DEPTH, N_ROWS, D_MODEL), 1)
    packed = jnp.zeros((DEPTH, N_ROWS, D_MODEL), jnp.float32)
    for k, r in enumerate(rows):
        packed = jnp.where(k_idx == k, r[:, None, :], packed)
    return {
        "rows": packed,
        "conv_w": conv_w,
        "gate_rw": gate_r_w,
        "gate_iw": gate_i_w,
        "sgu_w": sgu_w,
        "sgu_bias": jnp.repeat(jnp.transpose(sgu_b, (0, 2, 1)), SGU_HEAD_DIM, axis=2),
        "sgu_w8": jnp.repeat(jnp.transpose(sgu_w[:, :, :steps, :steps], (0, 2, 3, 1)), SGU_HEAD_DIM, axis=3),
        "sgu_b8": jnp.repeat(jnp.transpose(sgu_b[:, :, :steps], (0, 2, 1)), SGU_HEAD_DIM, axis=2),
    }


def kernel(x_prompt, x_sample, state_lru_h, state_conv, norm_mix_g, w_in, conv_w, conv_b, gate_r_w, gate_r_b, gate_i_w, gate_i_b, lru_lambda, sgu_norm_g, sgu_norm_b, sgu_w, sgu_b, w_out, norm_mlp_g, mlp_w1, mlp_w2, final_norm_g):
    nseq, seq, _ = x_prompt.shape
    nb, steps, _ = x_sample.shape
    assert seq % CHUNK == 0 and nseq == SUBLANES and steps == nseq and nb == 2 * HALF
    p = _prepare_params(steps, nseq, norm_mix_g, w_in, conv_w, conv_b, gate_r_w, gate_r_b, gate_i_w, gate_i_b,
                        lru_lambda, sgu_norm_g, sgu_norm_b, sgu_w, sgu_b, w_out, norm_mlp_g, mlp_w1, mlp_w2,
                        final_norm_g)

    xp = x_prompt
    xs = x_sample.reshape(nb * steps, D_MODEL)
    cbuf_tm = jnp.transpose(state_conv, (0, 2, 1, 3))
    prompt_state, sample_state = (), ()
    big_f32 = (w_in, w_out, mlp_w1, mlp_w2)
    big = _cast_layer(big_f32[:2], 0)
    for l in range(DEPTH):
        last = l == DEPTH - 1
        first = l == 0
        p.update(zip(_BIG_WEIGHTS, big))
        xs, sample_state, mlp_bf16 = _mixer_sample(
            xs, state_lru_h, cbuf_tm, p, l, seq_major=first, prev=sample_state,
            groups=SAMPLE_GROUPS_CAST if first else SAMPLE_GROUPS, cast=big_f32[2:] if first else ())
        p.update(zip(_BIG_WEIGHTS[2:], mlp_bf16))
        xp, xs, prompt_state, big = _layer_prompt(xp, xs.reshape(steps * nb, D_MODEL), p, l, final_norm=last,
                                                  sample_seq_major=last, cast_next=() if last else big_f32,
                                                  prev=prompt_state)
        if not last:
            xs = xs.reshape(nb // HALF, steps, HALF, D_MODEL)

    y_sample = xs.reshape(nb, steps, D_MODEL)
    h_p, conv_p = prompt_state
    v_s, h_s, conv_s = sample_state
    return (xp, y_sample, h_p, jnp.transpose(conv_p, (0, 2, 1, 3)), h_s, jnp.transpose(conv_s, (0, 2, 1, 3)), v_s)
```
